```python
import math
import jax
import jax.numpy as jnp
from jax import lax
import numpy as np


D_MODEL = 1024
BATCH = 8
SEQ = 8192
DEPTH = 1

CTX_LEN = 256
GRID_W = 64
EPS = 1e-6
CHUNK = 64
D_MIX = D_MODEL
ML_HEADS = 4
ML_QK = D_MIX // 16
ML_V = D_MIX // 8
GD_HEADS = 4
GD_QK = D_MIX // 8
GD_V = D_MIX // 8
CONV_W = 5
ML_SIZES = (ML_HEADS * ML_QK, ML_HEADS * ML_QK, ML_HEADS * ML_V, ML_HEADS * ML_V, 4 * ML_HEADS)
GD_SIZES = (GD_HEADS * (2 * GD_QK + GD_V), GD_HEADS * GD_V, 4 * GD_HEADS)
ML_COLS = sum(ML_SIZES)
GD_COLS = sum(GD_SIZES)
D_IN = ML_COLS + GD_COLS
N_EXPERTS = 256
TOP_K = 8
N_GROUPS = 8
TOPK_GROUPS = 4
D_EXPERT = D_MODEL // 4
D_SHARED = D_MODEL // 4
ROUTED_SCALE = 2.5
EXPERT_BLOCK = 128

kernel_name = 'hybrid_mlstm_gdn_moe_dit_layer'


def rms_norm(t, w):
    tf = t.astype(jnp.float32)
    tf = tf * lax.rsqrt(jnp.mean(tf * tf, axis=-1, keepdims=True) + EPS)
    return (tf * w.astype(jnp.float32)).astype(t.dtype)


def l2norm(t):
    tf = t.astype(jnp.float32)
    return tf * lax.rsqrt(jnp.sum(tf * tf, axis=-1, keepdims=True) + EPS)


def modulate(t, shift, scale):
    return t * (1 + scale) + shift


def split_cols(t, sizes):
    return jnp.split(t, np.cumsum(sizes)[:-1].tolist(), axis=-1)


def split_heads(t, n_heads):
    b, l, hd = t.shape
    return t.reshape(b, l, n_heads, hd // n_heads).transpose(0, 2, 1, 3)


def head_rms_norm(t, w):
    t = t * lax.rsqrt(jnp.mean(t * t, axis=-1, keepdims=True) + EPS)
    t = t.transpose(0, 2, 1, 3) * w.astype(jnp.float32)
    b, l, h, d = t.shape
    return t.reshape(b, l, h * d)


def to_chunks(t):
    b, h, l = t.shape[:3]
    return jnp.moveaxis(t.reshape(b, h, l // CHUNK, CHUNK, *t.shape[3:]), 2, 0)


def from_chunks(t):
    t = jnp.moveaxis(t, 0, 2)
    return t.reshape(t.shape[0], t.shape[1], -1, *t.shape[4:])


def raster_to_colmajor(t, rows):
    b, l, ch = t.shape
    return t.reshape(b, rows, GRID_W, ch).transpose(0, 2, 1, 3).reshape(b, l, ch)


def colmajor_to_raster(t, rows):
    b, l, ch = t.shape
    return t.reshape(b, GRID_W, rows, ch).transpose(0, 2, 1, 3).reshape(b, l, ch)


def centred_dwconv(t, w):
    pad = CONV_W // 2
    return lax.conv_general_dilated(t, w.astype(t.dtype)[:, None, :], window_strides=(1,), padding=[(pad, pad)],
                                    dimension_numbers=('NWC', 'WIO', 'NWC'), feature_group_count=t.shape[-1])


def mlstm_scan(q, k, v, i_pre, f_pre, state):
    f32 = jnp.float32
    qc, kc, vc = (to_chunks(t.astype(f32)) for t in (q, k, v))
    ic = to_chunks(i_pre.astype(f32))
    bc = jnp.cumsum(to_chunks(jax.nn.log_sigmoid(f_pre.astype(f32))), axis=-1)
    causal = jnp.tril(jnp.ones((CHUNK, CHUNK), dtype=bool))

    def step(carry, xs):
        C, n, m = carry
        qb, kb, vb, bb, ib = xs
        log_d = jnp.where(causal, bb[..., :, None] - bb[..., None, :] + ib[..., None, :], -jnp.inf)
        log_prev = bb + m[..., None]
        m_t = jnp.maximum(log_prev, jnp.max(log_d, axis=-1))
        s = jnp.einsum('bhtd,bhsd->bhts', qb, kb) * jnp.exp(log_d - m_t[..., None])
        w_prev = jnp.exp(log_prev - m_t)
        num = jnp.einsum('bhts,bhsv->bhtv', s, vb) + w_prev[..., None] * jnp.einsum('bhtd,bhdv->bhtv', qb, C)
        den = jnp.sum(s, axis=-1) + w_prev * jnp.einsum('bhtd,bhd->bht', qb, n)
        h = num / jnp.maximum(jnp.abs(den), jnp.exp(-m_t))[..., None]
        b_end = bb[..., -1]
        log_s = b_end[..., None] - bb + ib
        m_new = jnp.maximum(b_end + m, jnp.max(log_s, axis=-1))
        w_s = jnp.exp(log_s - m_new[..., None])
        w_c = jnp.exp(b_end + m - m_new)
        C = w_c[..., None, None] * C + jnp.einsum('bhs,bhsd,bhsv->bhdv', w_s, kb, vb)
        n = w_c[..., None] * n + jnp.einsum('bhs,bhsd->bhd', w_s, kb)
        return (C, n, m_new), h

    state, h = lax.scan(step, state, (qc, kc, vc, bc, ic))
    return from_chunks(h), state


def gated_delta_scan(q, k, v, g, beta, S):
    f32 = jnp.float32
    qc = to_chunks(q.astype(f32)) * GD_QK ** -0.5
    kc, vc = to_chunks(k.astype(f32)), to_chunks(v.astype(f32))
    bc = to_chunks(beta.astype(f32))
    G = jnp.cumsum(to_chunks(g.astype(f32)), axis=-1)
    incl = jnp.tril(jnp.ones((CHUNK, CHUNK), dtype=bool))
    strict = jnp.tril(jnp.ones((CHUNK, CHUNK), dtype=bool), -1)
    decay = jnp.exp(jnp.where(incl, G[..., :, None] - G[..., None, :], -jnp.inf))
    kk = jnp.einsum('nbhtd,nbhsd->nbhts', kc, kc)
    A = jnp.eye(CHUNK, dtype=f32) + jnp.where(strict, bc[..., :, None] * kk * decay, 0.0)
    rhs = jnp.concatenate([vc * bc[..., None], kc * (bc * jnp.exp(G))[..., None]], axis=-1)
    uw = lax.linalg.triangular_solve(A, rhs, left_side=True, lower=True, unit_diagonal=True)
    u, w = uw[..., :GD_V], uw[..., GD_V:]
    qk = jnp.einsum('nbhtd,nbhsd->nbhts', qc, kc) * decay

    def step(S, xs):
        qb, kb, ub, wb, Gb, qkb = xs
        v_new = ub - jnp.einsum('bhtd,bhdv->bhtv', wb, S)
        o = jnp.einsum('bhtd,bhdv->bhtv', qb * jnp.exp(Gb)[..., None], S) + jnp.einsum('bhts,bhsv->bhtv', qkb, v_new)
        g_end = Gb[..., -1]
        S = S * jnp.exp(g_end)[..., None, None] + jnp.einsum('bhsd,bhs,bhsv->bhdv', kb, jnp.exp(g_end[..., None] - Gb), v_new)
        return S, o

    S, o = lax.scan(step, S, (qc, kc, u, w, G, qk))
    return from_chunks(o), S


def bidir_scan(scan_fn, ctx_in, lat_in, init_state, with_ctx_out):
    ctx_out, lat_out = None, None
    for d in range(2):
        c_in, l_in = ctx_in[d], lat_in[d]
        if d == 1:
            c_in = tuple(jnp.flip(t, axis=2) for t in c_in)
            l_in = tuple(jnp.flip(t, axis=2) for t in l_in)
        c_h, c_state = scan_fn(*c_in, init_state)
        l_h, _ = scan_fn(*l_in, c_state)
        if d == 1:
            c_h, l_h = jnp.flip(c_h, axis=2), jnp.flip(l_h, axis=2)
        lat_out = l_h if lat_out is None else lat_out + l_h
        if with_ctx_out:
            ctx_out = c_h if ctx_out is None else ctx_out + c_h
    return ctx_out, lat_out


def mlstm_group(p_ctx, p_lat, i_bias, f_bias, norm_w, with_ctx_out):
    def prep(p):
        b, l, _ = p.shape
        q, k, v, o, gates = split_cols(p, ML_SIZES)
        gates = gates.reshape(b, l, 2, 2, ML_HEADS).transpose(2, 3, 0, 4, 1)
        q = split_heads(q, ML_HEADS)
        k = split_heads(k, ML_HEADS) * ML_QK ** -0.5
        v = split_heads(v, ML_HEADS)
        per_dir = tuple((q, k, v, gates[d, 0] + i_bias[d][:, None], gates[d, 1] + f_bias[d][:, None]) for d in range(2))
        return per_dir, o

    ctx_in, o_ctx = prep(p_ctx)
    lat_in, o_lat = prep(p_lat)
    b = p_lat.shape[0]
    init = (jnp.zeros((b, ML_HEADS, ML_QK, ML_V), jnp.float32), jnp.zeros((b, ML_HEADS, ML_QK), jnp.float32),
            jnp.zeros((b, ML_HEADS), jnp.float32))
    h_ctx, h_lat = bidir_scan(mlstm_scan, ctx_in, lat_in, init, with_ctx_out)
    gain = norm_w.reshape(ML_HEADS, ML_V)

    def out(h, o):
        return (head_rms_norm(h, gain) * jax.nn.sigmoid(o.astype(jnp.float32))).astype(o.dtype)

    return (out(h_ctx, o_ctx) if with_ctx_out else None), out(h_lat, o_lat)


def gdn_group(p_ctx, p_lat, conv_w, a_log, dt_bias, norm_w, with_ctx_out):
    def prep(p):
        b, l, _ = p.shape
        qkv, z, gates = split_cols(p, GD_SIZES)
        qkv = jax.nn.silu(centred_dwconv(qkv, conv_w))
        q, k, v = split_cols(qkv, (GD_HEADS * GD_QK, GD_HEADS * GD_QK, GD_HEADS * GD_V))
        q, k, v = l2norm(split_heads(q, GD_HEADS)), l2norm(split_heads(k, GD_HEADS)), split_heads(v, GD_HEADS)
        gates = gates.reshape(b, l, 2, 2, GD_HEADS).transpose(2, 3, 0, 4, 1).astype(jnp.float32)
        per_dir = tuple((q, k, v,
                         -jnp.exp(a_log[d].astype(jnp.float32))[:, None] * jax.nn.softplus(gates[d, 0] + dt_bias[d][:, None]),
                         jax.nn.sigmoid(gates[d, 1])) for d in range(2))
        return per_dir, z

    ctx_in, z_ctx = prep(p_ctx)
    lat_in, z_lat = prep(p_lat)
    init = jnp.zeros((p_lat.shape[0], GD_HEADS, GD_QK, GD_V), jnp.float32)
    o_ctx, o_lat = bidir_scan(gated_delta_scan, ctx_in, lat_in, init, with_ctx_out)

    def out(o, z):
        return (head_rms_norm(o, norm_w) * jax.nn.silu(z.astype(jnp.float32))).astype(z.dtype)

    return (out(o_ctx, z_ctx) if with_ctx_out else None), out(o_lat, z_lat)


def hybrid_mixer(h_ctx, h_lat, w_in, w_out, ml_i_bias, ml_f_bias, ml_norm_w, gd_conv_w, gd_a_log, gd_dt_bias,
                 gd_norm_w, with_ctx_out):
    rows = h_lat.shape[1] // GRID_W
    p_ctx = h_ctx @ w_in
    p_lat = h_lat @ w_in
    gd_lat = raster_to_colmajor(p_lat[..., ML_COLS:], rows)
    ml_y_ctx, ml_y_lat = mlstm_group(p_ctx[..., :ML_COLS], p_lat[..., :ML_COLS], ml_i_bias, ml_f_bias, ml_norm_w,
                                     with_ctx_out)
    gd_y_ctx, gd_y_lat = gdn_group(p_ctx[..., ML_COLS:], gd_lat, gd_conv_w, gd_a_log, gd_dt_bias, gd_norm_w,
                                   with_ctx_out)
    y_lat = jnp.concatenate([ml_y_lat, colmajor_to_raster(gd_y_lat, rows)], axis=-1) @ w_out
    y_ctx = jnp.concatenate([ml_y_ctx, gd_y_ctx], axis=-1) @ w_out if with_ctx_out else None
    return y_ctx, y_lat


def moe_ffn(h, router_w, router_bias, w_gate, w_up, w_down, ws_gate, ws_up, ws_down):
    b, l, d = h.shape
    n_tok = b * l
    xf = h.reshape(n_tok, d)
    scores = jax.nn.sigmoid((xf @ router_w).astype(jnp.float32))
    sel = scores + router_bias.astype(jnp.float32)
    grp_score = lax.top_k(sel.reshape(n_tok, N_GROUPS, N_EXPERTS // N_GROUPS), 2)[0].sum(-1)
    _, top_grp = lax.top_k(grp_score, TOPK_GROUPS)
    grp_mask = jnp.any(top_grp[:, :, None] == jnp.arange(N_GROUPS)[None, None, :], axis=1)
    sel = jnp.where(jnp.repeat(grp_mask, N_EXPERTS // N_GROUPS, axis=1), sel, -jnp.inf)
    _, idx = lax.top_k(sel, TOP_K)
    gate = jnp.take_along_axis(scores, idx, axis=1)
    gate = gate / jnp.sum(gate, axis=-1, keepdims=True) * ROUTED_SCALE
    n_asg = n_tok * TOP_K
    flat_e = idx.reshape(-1)
    order = jnp.argsort(flat_e)
    se = flat_e[order]
    st = (order // TOP_K).astype(jnp.int32)
    sw = gate.reshape(-1)[order]
    counts = jnp.bincount(flat_e, length=N_EXPERTS)
    padded = (counts + EXPERT_BLOCK - 1) // EXPERT_BLOCK * EXPERT_BLOCK
    start = jnp.cumsum(counts) - counts
    pend = jnp.cumsum(padded)
    dest = (pend - padded)[se] + jnp.arange(n_asg) - start[se]
    n_blocks = -(-n_asg // EXPERT_BLOCK) + N_EXPERTS
    row_tok = jnp.zeros((n_blocks * EXPERT_BLOCK,), jnp.int32).at[dest].set(st)
    row_w = jnp.zeros((n_blocks * EXPERT_BLOCK,), jnp.float32).at[dest].set(sw)
    block_e = jnp.minimum(jnp.searchsorted(pend, jnp.arange(n_blocks) * EXPERT_BLOCK, side='right'), N_EXPERTS - 1)

    def expert_block(y, xs):
        e, toks, wts = xs
        xb = xf[toks]
        hb = jax.nn.silu(xb @ w_gate[e]) * (xb @ w_up[e])
        return y.at[toks].add((hb @ w_down[e]) * wts[:, None].astype(y.dtype)), None

    y_shared = (jax.nn.silu(xf @ ws_gate) * (xf @ ws_up)) @ ws_down
    y, _ = lax.scan(expert_block, y_shared,
                    (block_e, row_tok.reshape(n_blocks, EXPERT_BLOCK), row_w.reshape(n_blocks, EXPERT_BLOCK)))
    return y.reshape(b, l, d)


def setup_inputs(seed: int = 0) -> dict:
    key = jax.random.key(seed)
    ks = jax.random.split(key, 27)
    f32 = jnp.float32

    def nrm(k, shape, scale):
        return jax.random.normal(k, shape, f32) * scale

    def gain(k, shape):
        return 1.0 + 0.05 * jax.random.normal(k, shape, f32)

    L = DEPTH
    dt = jnp.exp(jax.random.uniform(ks[16], (L, 2, GD_HEADS), f32, math.log(1e-3), math.log(1e-1)))
    return {
        'x': nrm(ks[0], (BATCH, SEQ, D_MODEL), 1.0),
        'c': nrm(ks[1], (BATCH, D_MODEL), 1.0),
        'ctx': nrm(ks[2], (BATCH, CTX_LEN, D_MODEL), 1.0),
        'c_ctx': nrm(ks[3], (D_MODEL,), 1.0),
        'w_ada': nrm(ks[4], (L, D_MODEL, 6 * D_MODEL), 0.5 * D_MODEL ** -0.5),
        'b_ada': nrm(ks[5], (L, 6 * D_MODEL), 0.02),
        'norm_pre_mix': gain(ks[6], (L, D_MODEL)),
        'norm_post_mix': gain(ks[7], (L, D_MODEL)),
        'norm_pre_ffn': gain(ks[8], (L, D_MODEL)),
        'norm_post_ffn': gain(ks[9], (L, D_MODEL)),
        'w_in': nrm(ks[10], (L, D_MODEL, D_IN), D_MODEL ** -0.5),
        'ml_i_bias': nrm(ks[11], (L, 2, ML_HEADS), 0.1),
        'ml_f_bias': 3.0 + 3.0 * jax.random.uniform(ks[12], (L, 2, ML_HEADS), f32),
        'ml_norm_w': gain(ks[13], (L, ML_HEADS * ML_V)),
        'gd_conv_w': nrm(ks[14], (L, CONV_W, GD_SIZES[0]), CONV_W ** -0.5),
        'gd_a_log': jnp.log(jax.random.uniform(ks[15], (L, 2, GD_HEADS), f32, 1.0, 16.0)),
        'gd_dt_bias': dt + jnp.log(-jnp.expm1(-dt)),
        'gd_norm_w': gain(ks[17], (L, GD_V)),
        'w_out': nrm(ks[18], (L, D_MIX, D_MODEL), D_MIX ** -0.5),
        'router_w': nrm(ks[19], (L, D_MODEL, N_EXPERTS), D_MODEL ** -0.5),
        'router_bias': nrm(ks[20], (L, N_EXPERTS), 0.01),
        'w_gate': nrm(ks[21], (L, N_EXPERTS, D_MODEL, D_EXPERT), D_MODEL ** -0.5),
        'w_up': nrm(ks[22], (L, N_EXPERTS, D_MODEL, D_EXPERT), D_MODEL ** -0.5),
        'w_down': nrm(ks[23], (L, N_EXPERTS, D_EXPERT, D_MODEL), D_EXPERT ** -0.5),
        'ws_gate': nrm(ks[24], (L, D_MODEL, D_SHARED), D_MODEL ** -0.5),
        'ws_up': nrm(ks[25], (L, D_MODEL, D_SHARED), D_MODEL ** -0.5),
        'ws_down': nrm(ks[26], (L, D_SHARED, D_MODEL), D_SHARED ** -0.5),
    }


def reference(x, c, ctx, c_ctx, w_ada, b_ada, norm_pre_mix, norm_post_mix, norm_pre_ffn, norm_post_ffn, w_in,
              ml_i_bias, ml_f_bias, ml_norm_w, gd_conv_w, gd_a_log, gd_dt_bias, gd_norm_w, w_out, router_w,
              router_bias, w_gate, w_up, w_down, ws_gate, ws_up, ws_down):
    ctx_s = ctx
    for layer in range(DEPTH):
        last = layer == DEPTH - 1
        mod = jnp.split((jax.nn.silu(c) @ w_ada[layer] + b_ada[layer])[:, None, :], 6, axis=-1)
        mod_ctx = jnp.split(jax.nn.silu(c_ctx) @ w_ada[layer] + b_ada[layer], 6, axis=-1)
        h_lat = modulate(rms_norm(x, norm_pre_mix[layer]), mod[0], mod[1])
        h_ctx = modulate(rms_norm(ctx_s, norm_pre_mix[layer]), mod_ctx[0], mod_ctx[1])
        y_ctx, y_lat = hybrid_mixer(h_ctx, h_lat, w_in[layer], w_out[layer], ml_i_bias[layer], ml_f_bias[layer],
                                    ml_norm_w[layer], gd_conv_w[layer], gd_a_log[layer], gd_dt_bias[layer],
                                    gd_norm_w[layer], not last)
        x = x + mod[2] * rms_norm(y_lat, norm_post_mix[layer])
        h_ffn = modulate(rms_norm(x, norm_pre_ffn[layer]), mod[3], mod[4])
        y_ffn = moe_ffn(h_ffn, router_w[layer], router_bias[layer], w_gate[layer], w_up[layer], w_down[layer],
                        ws_gate[layer], ws_up[layer], ws_down[layer])
        x = x + mod[5] * rms_norm(y_ffn, norm_post_ffn[layer])
        if not last:
            ctx_s = ctx_s + mod_ctx[2] * rms_norm(y_ctx, norm_post_mix[layer])
            h_c = modulate(rms_norm(ctx_s, norm_pre_ffn[layer]), mod_ctx[3], mod_ctx[4])
            y_c = moe_ffn(h_c, router_w[layer], router_bias[layer], w_gate[layer], w_up[layer], w_down[layer],
                          ws_gate[layer], ws_up[layer], ws_down[layer])
            ctx_s = ctx_s + mod_ctx[5] * rms_norm(y_c, norm_post_ffn[layer])
    return x
```

```python
import functools

import jax
import jax.numpy as jnp
from jax import lax
from jax.experimental import pallas as pl
from jax.experimental.pallas import tpu as pltpu

EPS = 1e-6
CHUNK = 64
GRID_W = 64
ML_HEADS, ML_QK, ML_V = 4, 64, 128
GD_HEADS, GD_QK, GD_V = 4, 128, 128
CONV_W = 5
N_EXPERTS, TOP_K, N_GROUPS, TOPK_GROUPS = 256, 8, 8, 4
ROUTED_SCALE = 2.5
EXPERT_BLOCK = 128
N_CHAINS = 8
LANES = 128
GATE_LANES = LANES
ML_GATE0, GD_GATE0 = 0, 16

F32 = jnp.float32
BF16 = jnp.bfloat16
HI = lax.Precision.HIGHEST
VMEM_LIMIT = 56 * 1024 * 1024


def _cparams(*sem):
    return pltpu.CompilerParams(dimension_semantics=sem, vmem_limit_bytes=VMEM_LIMIT)


def _dot(a, b):
    return jnp.dot(a.astype(BF16), b.astype(BF16), preferred_element_type=F32)


def _dot_nt(a, b):
    return lax.dot_general(a.astype(BF16), b.astype(BF16), (((1,), (1,)), ((), ())), preferred_element_type=F32)


def _dot_tn(a, b):
    return lax.dot_general(a.astype(BF16), b.astype(BF16), (((0,), (0,)), ((), ())), preferred_element_type=F32)


def _dot_hi(a, b):
    return jnp.dot(a, b, precision=HI, preferred_element_type=F32)


def _dot_nt_hi(a, b):
    return lax.dot_general(a, b, (((1,), (1,)), ((), ())), precision=HI, preferred_element_type=F32)


def _transpose_hi(x):
    n = x.shape[1]
    eye = (lax.broadcasted_iota(jnp.int32, (n, n), 0) == lax.broadcasted_iota(jnp.int32, (n, n), 1)).astype(F32)
    return _dot_nt_hi(eye, x)


def _silu(x):
    return x * jax.nn.sigmoid(x)


def _past_mask(reverse):
    t = lax.broadcasted_iota(jnp.int32, (CHUNK, CHUNK), 0)
    s = lax.broadcasted_iota(jnp.int32, (CHUNK, CHUNK), 1)
    return (s >= t, s > t) if reverse else (s <= t, s < t)


def _ada_kernel(c_ref, w_ref, b_ref, o_ref):
    o_ref[...] = _dot(_silu(c_ref[...]), w_ref[...]) + b_ref[...]


def _ada(cc, w_ada, b_ada):
    rows, d = cc.shape
    n = w_ada.shape[1]
    tn = 1536
    return pl.pallas_call(
        _ada_kernel,
        grid=(n // tn,),
        in_specs=[pl.BlockSpec((rows, d), lambda j: (0, 0)),
                  pl.BlockSpec((d, tn), lambda j: (0, j)),
                  pl.BlockSpec((1, tn), lambda j: (0, j))],
        out_specs=pl.BlockSpec((rows, tn), lambda j: (0, j)),
        out_shape=jax.ShapeDtypeStruct((rows, n), F32),
        compiler_params=_cparams("arbitrary"),
        name="ada",
    )(cc, w_ada, b_ada.reshape(1, n))


def _proj_kernel(x_ref, nw_ref, sh_ref, sc_ref, wml_ref, wgq_ref, wgz_ref, wg_ref, gb_ref,
                 ml_ref, gq_ref, gz_ref, g_ref):
    x = x_ref[0]
    xn = x * lax.rsqrt(jnp.mean(x * x, axis=-1, keepdims=True) + EPS) * nw_ref[...]
    h = (xn * (1.0 + sc_ref[0]) + sh_ref[0]).astype(BF16)
    ml_ref[0] = jnp.dot(h, wml_ref[...], preferred_element_type=F32)
    gq_ref[0] = jnp.dot(h, wgq_ref[...], preferred_element_type=F32)
    gz_ref[0] = jnp.dot(h, wgz_ref[...], preferred_element_type=F32)
    g_ref[0] = jnp.dot(h, wg_ref[...], preferred_element_type=F32) + gb_ref[...]


def _proj(x, norm_w, shift, scale, wml, wgq, wgz, wg, gbias, tm):
    b, s, d = x.shape
    nml, ngq, ngz = wml.shape[1], wgq.shape[1], wgz.shape[1]
    full = lambda shp: pl.BlockSpec(shp, lambda bi, i: (0,) * len(shp))
    tok = lambda n: pl.BlockSpec((1, tm, n), lambda bi, i: (bi, i, 0))
    mod = pl.BlockSpec((1, 1, d), lambda bi, i: (bi, 0, 0))
    return pl.pallas_call(
        _proj_kernel,
        grid=(b, s // tm),
        in_specs=[tok(d), full((1, d)), mod, mod, full((d, nml)), full((d, ngq)), full((d, ngz)),
                  full((d, GATE_LANES)), full((1, GATE_LANES))],
        out_specs=[tok(nml), tok(ngq), tok(ngz), tok(GATE_LANES)],
        out_shape=[jax.ShapeDtypeStruct((b, s, n), F32) for n in (nml, ngq, ngz, GATE_LANES)],
        compiler_params=_cparams("parallel", "arbitrary"),
        name="proj",
    )(x, norm_w, shift, scale, wml, wgq, wgz, wg, gbias)


def _mlstm_kernel(mlf_ref, mlb_ref, gf_ref, gb_ref, c0_ref, n0_ref, m0_ref,
                  hf_ref, hb_ref, cn_ref, nn_ref, mn_ref, c_scr, n_scr, m_scr):
    i = pl.program_id(1)

    @pl.when(i == 0)
    def _():
        c_scr[...] = c0_ref[0]
        n_scr[...] = n0_ref[0]
        m_scr[...] = m0_ref[0]

    for d in range(2):
        ml_ref, g_ref, h_ref = (mlf_ref, gf_ref, hf_ref) if d == 0 else (mlb_ref, gb_ref, hb_ref)
        past, _ = _past_mask(d == 1)
        pastf = past.astype(F32)
        g = g_ref[0]
        ls = jax.nn.log_sigmoid(g)
        bcol = _dot_hi(pastf, ls)
        tot = jnp.sum(ls, axis=0, keepdims=True)
        g_t = _transpose_hi(g)
        b_t = _transpose_hi(bcol)
        for hd in range(ML_HEADS):
            ci = ML_GATE0 + d * 8 + hd
            cf = ci + ML_HEADS
            ch = d * ML_HEADS + hd
            q = ml_ref[0, :, hd * ML_QK:(hd + 1) * ML_QK]
            k = ml_ref[0, :, ML_HEADS * ML_QK + hd * ML_QK:ML_HEADS * ML_QK + (hd + 1) * ML_QK] * (ML_QK ** -0.5)
            v = ml_ref[0, :, 2 * ML_HEADS * ML_QK + hd * ML_V:2 * ML_HEADS * ML_QK + (hd + 1) * ML_V]
            i_col, b_col = g[:, ci:ci + 1], bcol[:, cf:cf + 1]
            i_row, b_row = g_t[ci:ci + 1, :], b_t[cf:cf + 1, :]
            b_end = tot[:, cf:cf + 1]
            c_st, n_st, m_st = c_scr[ch], n_scr[ch], m_scr[ch]

            log_d = jnp.where(past, b_col - b_row + i_row, -jnp.inf)
            log_prev = b_col + m_st
            m_t = jnp.maximum(log_prev, jnp.max(log_d, axis=-1, keepdims=True))
            s = _dot_nt(q, k) * jnp.exp(log_d - m_t)
            w_prev = jnp.exp(log_prev - m_t)
            num = _dot(s, v) + w_prev * _dot(q, c_st)
            den = jnp.sum(s, axis=-1, keepdims=True) + w_prev * jnp.sum(q * n_st, axis=-1, keepdims=True)
            h_ref[0, :, hd * ML_V:(hd + 1) * ML_V] = num / jnp.maximum(jnp.abs(den), jnp.exp(-m_t))

            log_s = b_end - b_col + i_col
            m_new = jnp.maximum(b_end + m_st, jnp.max(log_s, axis=0, keepdims=True))
            kw = k * jnp.exp(log_s - m_new)
            w_c = jnp.exp(b_end + m_st - m_new)
            c_scr[ch] = w_c * c_st + _dot_tn(kw, v)
            n_scr[ch] = w_c * n_st + jnp.sum(kw, axis=0, keepdims=True)
            m_scr[ch] = m_new

    @pl.when(i == pl.num_programs(1) - 1)
    def _():
        cn_ref[0] = c_scr[...]
        nn_ref[0] = n_scr[...]
        mn_ref[0] = m_scr[...]


def _mlstm(ml, gates, c0, n0, m0):
    b, s, nml = ml.shape
    nc = s // CHUNK
    fwd = lambda n: pl.BlockSpec((1, CHUNK, n), lambda bi, i: (bi, i, 0))
    bwd = lambda n: pl.BlockSpec((1, CHUNK, n), lambda bi, i: (bi, nc - 1 - i, 0))
    st = lambda shp: pl.BlockSpec((1,) + shp, lambda bi, i: (bi,) + (0,) * len(shp))
    hdim = ML_HEADS * ML_V
    return pl.pallas_call(
        _mlstm_kernel,
        grid=(b, nc),
        in_specs=[fwd(nml), bwd(nml), fwd(GATE_LANES), bwd(GATE_LANES),
                  st((N_CHAINS, ML_QK, ML_V)), st((N_CHAINS, 1, ML_QK)), st((N_CHAINS, 1, 1))],
        out_specs=[fwd(hdim), bwd(hdim),
                   st((N_CHAINS, ML_QK, ML_V)), st((N_CHAINS, 1, ML_QK)), st((N_CHAINS, 1, 1))],
        out_shape=[jax.ShapeDtypeStruct((b, s, hdim), F32), jax.ShapeDtypeStruct((b, s, hdim), F32),
                   jax.ShapeDtypeStruct(c0.shape, F32), jax.ShapeDtypeStruct(n0.shape, F32),
                   jax.ShapeDtypeStruct(m0.shape, F32)],
        scratch_shapes=[pltpu.VMEM((N_CHAINS, ML_QK, ML_V), F32), pltpu.VMEM((N_CHAINS, 1, ML_QK), F32),
                        pltpu.VMEM((N_CHAINS, 1, 1), F32)],
        compiler_params=_cparams("parallel", "arbitrary"),
        name="mlstm",
    )(ml, ml, gates, gates, c0, n0, m0)


def _gdconv_kernel(has_halo, *refs):
    if has_halo:
        x_ref, prev_ref, next_ref, w_ref, o_ref, xp_ref = refs
    else:
        x_ref, w_ref, o_ref, xp_ref = refs
    rows = x_ref.shape[1]
    nch = x_ref.shape[2]
    pad = 8
    zero = jnp.zeros((pad, nch), F32)
    if has_halo:
        c = pl.program_id(1)
        xp_ref[0:pad, :] = jnp.where(c > 0, prev_ref[0], zero)
        xp_ref[pad + rows:, :] = jnp.where(c < pl.num_programs(1) - 1, next_ref[0], zero)
    else:
        xp_ref[0:pad, :] = zero
        xp_ref[pad + rows:, :] = zero
    xp_ref[pad:pad + rows, :] = x_ref[0]
    half = CONV_W // 2
    for lc in range(nch // 128):
        sl = slice(lc * 128, (lc + 1) * 128)
        acc = None
        for j in range(CONV_W):
            term = xp_ref[pad - half + j:pad - half + j + rows, sl] * w_ref[j:j + 1, sl]
            acc = term if acc is None else acc + term
        y = _silu(acc)
        if lc < 2 * GD_HEADS:
            y = y * lax.rsqrt(jnp.sum(y * y, axis=-1, keepdims=True) + EPS)
        if lc < GD_HEADS:
            y = y * (GD_QK ** -0.5)
        o_ref[0, :, sl] = y


def _gdconv_ctx(qkv, conv_w):
    b, s, nch = qkv.shape
    return pl.pallas_call(
        functools.partial(_gdconv_kernel, False),
        grid=(b,),
        in_specs=[pl.BlockSpec((1, s, nch), lambda bi: (bi, 0, 0)), pl.BlockSpec((CONV_W, nch), lambda bi: (0, 0))],
        out_specs=pl.BlockSpec((1, s, nch), lambda bi: (bi, 0, 0)),
        out_shape=jax.ShapeDtypeStruct((b, s, nch), F32),
        scratch_shapes=[pltpu.VMEM((s + 16, nch), F32)],
        compiler_params=_cparams("parallel"),
        name="gdconv_ctx",
    )(qkv, conv_w)


def _gdconv_lat(qkv, conv_w):
    b, s, nch = qkv.shape
    rows = s // GRID_W
    view = qkv.reshape(b, rows, GRID_W * nch)
    rb = rows // 8
    out = pl.pallas_call(
        functools.partial(_gdconv_kernel, True),
        grid=(b, GRID_W),
        in_specs=[pl.BlockSpec((1, rows, nch), lambda bi, c: (bi, 0, c)),
                  pl.BlockSpec((1, 8, nch), lambda bi, c: (bi, rb - 1, jnp.maximum(c - 1, 0))),
                  pl.BlockSpec((1, 8, nch), lambda bi, c: (bi, 0, jnp.minimum(c + 1, GRID_W - 1))),
                  pl.BlockSpec((CONV_W, nch), lambda bi, c: (0, 0))],
        out_specs=pl.BlockSpec((1, rows, nch), lambda bi, c: (bi, 0, c)),
        out_shape=jax.ShapeDtypeStruct(view.shape, F32),
        scratch_shapes=[pltpu.VMEM((rows + 16, nch), F32)],
        compiler_params=_cparams("parallel", "arbitrary"),
        name="gdconv_lat",
    )(view, view, view, conv_w)
    return out.reshape(b, s, nch)


def _unit_lower_inverse(n_strict, eye):
    x = n_strict
    p = eye - n_strict
    for _ in range(5):
        x = _dot_hi(x, x)
        p = p + _dot_hi(p, x)
    return p


def _gdn_kernel(qf_ref, qb_ref, gf_ref, gb_ref, na_ref, s0_ref, of_ref, ob_ref, sn_ref, s_scr):
    i = pl.program_id(1)

    @pl.when(i == 0)
    def _():
        s_scr[...] = s0_ref[0]

    eye = (lax.broadcasted_iota(jnp.int32, (CHUNK, CHUNK), 0)
           == lax.broadcasted_iota(jnp.int32, (CHUNK, CHUNK), 1)).astype(F32)
    nqk = GD_HEADS * GD_QK
    for d in range(2):
        x_ref, g_ref, o_ref = (qf_ref, gf_ref, of_ref) if d == 0 else (qb_ref, gb_ref, ob_ref)
        incl, strict = _past_mask(d == 1)
        gates = g_ref[0]
        glog = na_ref[...] * jax.nn.softplus(gates)
        beta_all = jax.nn.sigmoid(gates)
        gcum = _dot_hi(incl.astype(F32), glog)
        gtot = jnp.sum(glog, axis=0, keepdims=True)
        gcum_t = _transpose_hi(gcum)
        for hd in range(GD_HEADS):
            ca = GD_GATE0 + d * 8 + hd
            cb = ca + GD_HEADS
            ch = d * GD_HEADS + hd
            q = x_ref[0, :, hd * GD_QK:(hd + 1) * GD_QK]
            k = x_ref[0, :, nqk + hd * GD_QK:nqk + (hd + 1) * GD_QK]
            v = x_ref[0, :, 2 * nqk + hd * GD_V:2 * nqk + (hd + 1) * GD_V]
            g_col, g_row = gcum[:, ca:ca + 1], gcum_t[ca:ca + 1, :]
            beta = beta_all[:, cb:cb + 1]
            g_end = gtot[:, ca:ca + 1]
            s_st = s_scr[ch]

            decay = jnp.exp(jnp.where(incl, g_col - g_row, -jnp.inf))
            n_strict = jnp.where(strict, beta * _dot_nt(k, k) * decay, 0.0)
            t_inv = _unit_lower_inverse(n_strict, eye)
            rhs = jnp.concatenate([v * beta, k * (beta * jnp.exp(g_col))], axis=-1)
            uw = _dot_hi(t_inv, rhs)
            u, w = uw[:, :GD_V], uw[:, GD_V:]
            qk = _dot_nt(q, k) * decay
            v_new = u - _dot(w, s_st)
            o_ref[0, :, hd * GD_V:(hd + 1) * GD_V] = _dot(q * jnp.exp(g_col), s_st) + _dot(qk, v_new)
            s_scr[ch] = s_st * jnp.exp(g_end) + _dot_tn(k * jnp.exp(g_end - g_col), v_new)

    @pl.when(i == pl.num_programs(1) - 1)
    def _():
        sn_ref[0] = s_scr[...]


def _gdn(qkv_view, gates_view, neg_a, s0, nc, idx_fn, out_view_shape):
    b = qkv_view.shape[0]
    nqkv = 2 * GD_HEADS * GD_QK + GD_HEADS * GD_V
    hdim = GD_HEADS * GD_V
    fwd = lambda n: pl.BlockSpec((1, CHUNK, n), lambda bi, i: idx_fn(bi, i))
    bwd = lambda n: pl.BlockSpec((1, CHUNK, n), lambda bi, i: idx_fn(bi, nc - 1 - i))
    st = pl.BlockSpec((1, N_CHAINS, GD_QK, GD_V), lambda bi, i: (bi, 0, 0, 0))
    return pl.pallas_call(
        _gdn_kernel,
        grid=(b, nc),
        in_specs=[fwd(nqkv), bwd(nqkv), fwd(GATE_LANES), bwd(GATE_LANES),
                  pl.BlockSpec((1, GATE_LANES), lambda bi, i: (0, 0)), st],
        out_specs=[fwd(hdim), bwd(hdim), st],
        out_shape=[jax.ShapeDtypeStruct(out_view_shape, F32), jax.ShapeDtypeStruct(out_view_shape, F32),
                   jax.ShapeDtypeStruct(s0.shape, F32)],
        scratch_shapes=[pltpu.VMEM((N_CHAINS, GD_QK, GD_V), F32)],
        compiler_params=_cparams("parallel", "arbitrary"),
        name="gdn",
    )(qkv_view, qkv_view, gates_view, gates_view, neg_a, s0)


def _head_rms(t, nheads, width):
    outs = []
    for hd in range(nheads):
        th = t[:, hd * width:(hd + 1) * width]
        outs.append(th * lax.rsqrt(jnp.mean(th * th, axis=-1, keepdims=True) + EPS))
    return jnp.concatenate(outs, axis=-1)


def _rms(t, w):
    return t * lax.rsqrt(jnp.mean(t * t, axis=-1, keepdims=True) + EPS) * w


def _post_kernel(x_ref, hf_ref, hb_ref, og_ref, of_ref, ob_ref, z_ref, mlw_ref, gdw_ref, wout_ref,
                 npost_ref, g2_ref, npre_ref, sh_ref, sc_ref, rwt_ref, wsg_ref, wsu_ref, wsd_ref,
                 x1_ref, hffn_ref, lt_ref, ys_ref):
    ml_y = _head_rms(hf_ref[0] + hb_ref[0], ML_HEADS, ML_V) * mlw_ref[...] * jax.nn.sigmoid(og_ref[0])
    gd_y = _head_rms(of_ref[0] + ob_ref[0], GD_HEADS, GD_V) * gdw_ref[...] * _silu(z_ref[0])
    y = _dot(jnp.concatenate([ml_y, gd_y], axis=-1), wout_ref[...])
    x1 = x_ref[0] + g2_ref[0] * _rms(y, npost_ref[...])
    x1_ref[0] = x1
    hffn = _rms(x1, npre_ref[...]) * (1.0 + sc_ref[0]) + sh_ref[0]
    for cidx in range(hffn.shape[1] // LANES):
        hffn_ref[0, :, cidx, :] = hffn[:, cidx * LANES:(cidx + 1) * LANES]
    hb = hffn.astype(BF16)
    lt_ref[...] = lax.dot_general(rwt_ref[...], hb, (((1,), (1,)), ((), ())), preferred_element_type=F32)
    hs = _silu(jnp.dot(hb, wsg_ref[...], preferred_element_type=F32)) * jnp.dot(hb, wsu_ref[...],
                                                                                preferred_element_type=F32)
    ys_ref[0] = _dot(hs, wsd_ref[...])


def _post(x, hf, hb, ml, of, ob, gz, mlw, gdw, wout, npost, g2, npre, sh, sc, rwt, wsg, wsu, wsd, tm):
    b, s, d = x.shape
    nt = s // tm
    hw = ML_HEADS * ML_V
    og_blk = (2 * ML_HEADS * ML_QK + ML_HEADS * ML_V) // hw
    tok = lambda n: pl.BlockSpec((1, tm, n), lambda bi, i: (bi, i, 0))
    full = lambda shp: pl.BlockSpec(shp, lambda bi, i: (0,) * len(shp))
    mod = pl.BlockSpec((1, 1, d), lambda bi, i: (bi, 0, 0))
    ne = rwt.shape[0]
    ds = wsg.shape[1]
    return pl.pallas_call(
        _post_kernel,
        grid=(b, nt),
        in_specs=[tok(d), tok(hw), tok(hw), pl.BlockSpec((1, tm, hw), lambda bi, i: (bi, i, og_blk)),
                  tok(hw), tok(hw), tok(hw), full((1, hw)), full((1, hw)), full((d, d)),
                  full((1, d)), mod, full((1, d)), mod, mod, full((ne, d)), full((d, ds)), full((d, ds)),
                  full((ds, d))],
        out_specs=[tok(d), pl.BlockSpec((1, tm, d // LANES, LANES), lambda bi, i: (bi, i, 0, 0)),
                   pl.BlockSpec((ne, tm), lambda bi, i: (0, bi * nt + i)), tok(d)],
        out_shape=[jax.ShapeDtypeStruct((b, s, d), F32), jax.ShapeDtypeStruct((b, s, d // LANES, LANES), F32),
                   jax.ShapeDtypeStruct((ne, b * s), F32), jax.ShapeDtypeStruct((b, s, d), F32)],
        compiler_params=_cparams("parallel", "arbitrary"),
        name="post",
    )(x, hf, hb, ml, of, ob, gz, mlw, gdw, wout, npost, g2, npre, sh, sc, rwt, wsg, wsu, wsd)


def _route_kernel(lt_ref, bias_ref, idx_ref, gate_ref):
    ne, tn = lt_ref.shape
    gsz = ne // N_GROUPS
    scores = jax.nn.sigmoid(lt_ref[...])
    sel = scores + bias_ref[...]
    neg = -jnp.inf
    sel3 = sel.reshape(N_GROUPS, gsz, tn)
    io3 = lax.broadcasted_iota(jnp.int32, sel3.shape, 1)
    top1 = jnp.max(sel3, axis=1, keepdims=True)
    first = jnp.min(jnp.where(sel3 == top1, io3, gsz), axis=1, keepdims=True)
    top2 = jnp.max(jnp.where(io3 == first, neg, sel3), axis=1, keepdims=True)
    grp = (top1 + top2).reshape(N_GROUPS, tn)
    iog = lax.broadcasted_iota(jnp.int32, grp.shape, 0)
    keep = jnp.zeros(grp.shape, jnp.bool_)
    for _ in range(TOPK_GROUPS):
        m = jnp.max(grp, axis=0, keepdims=True)
        pick = iog == jnp.min(jnp.where(grp == m, iog, N_GROUPS), axis=0, keepdims=True)
        keep = keep | pick
        grp = jnp.where(pick, neg, grp)
    cand = jnp.where(keep.reshape(N_GROUPS, 1, tn), sel3, neg).reshape(ne, tn)
    ioe = lax.broadcasted_iota(jnp.int32, cand.shape, 0)
    idxs, gates = [], []
    for _ in range(TOP_K):
        m = jnp.max(cand, axis=0, keepdims=True)
        e = jnp.min(jnp.where(cand == m, ioe, ne), axis=0, keepdims=True)
        pick = ioe == e
        idxs.append(e)
        gates.append(jnp.sum(jnp.where(pick, scores, 0.0), axis=0, keepdims=True))
        cand = jnp.where(pick, neg, cand)
    gate = jnp.concatenate(gates, axis=0)
    idx_ref[...] = jnp.concatenate(idxs, axis=0)
    gate_ref[...] = gate / jnp.sum(gate, axis=0, keepdims=True) * ROUTED_SCALE


def _route(logits_t, bias_col, tn):
    ne, t = logits_t.shape
    return pl.pallas_call(
        _route_kernel,
        grid=(t // tn,),
        in_specs=[pl.BlockSpec((ne, tn), lambda i: (0, i)), pl.BlockSpec((ne, 1), lambda i: (0, 0))],
        out_specs=[pl.BlockSpec((TOP_K, tn), lambda i: (0, i)), pl.BlockSpec((TOP_K, tn), lambda i: (0, i))],
        out_shape=[jax.ShapeDtypeStruct((TOP_K, t), jnp.int32), jax.ShapeDtypeStruct((TOP_K, t), F32)],
        compiler_params=_cparams("parallel"),
        name="route",
    )(logits_t, bias_col)


def _experts_kernel(be_ref, nv_ref, nu_ref,
                    tok_ref, tokn_ref, slot_ref, w_ref, wg_ref, wu_ref, wd_ref, h_hbm,
                    slots_hbm, xg, ob, gsem, ssem):
    i = pl.program_id(0)
    n_used = nu_ref[0]
    cur = i % 2
    nxt = 1 - cur

    def gather_copy(tref, j, buf):
        return pltpu.make_async_copy(h_hbm.at[tref[0, 0, j]], xg.at[buf, j], gsem.at[buf])

    def scatter_copy(j, buf):
        return pltpu.make_async_copy(ob.at[buf, j], slots_hbm.at[slot_ref[0, 0, j]], ssem.at[buf])

    def scatter_wait(n, buf):
        @pl.when(n > 0)
        def _():
            pltpu.make_async_copy(ob.at[buf, pl.ds(0, n)], slots_hbm.at[pl.ds(0, n)], ssem.at[buf]).wait()

    @pl.when(i < n_used)
    def _():
        @pl.when(i == 0)
        def _():
            for j in range(EXPERT_BLOCK):
                gather_copy(tok_ref, j, cur).start()

        @pl.when(i + 1 < n_used)
        def _():
            for j in range(EXPERT_BLOCK):
                gather_copy(tokn_ref, j, nxt).start()

        pltpu.make_async_copy(h_hbm.at[pl.ds(0, EXPERT_BLOCK)], xg.at[cur], gsem.at[cur]).wait()

        @pl.when(i >= 2)
        def _():
            scatter_wait(nv_ref[jnp.maximum(i - 2, 0)], cur)

        nct = xg.shape[2]
        xb = jnp.concatenate([xg[cur, :, cidx, :] for cidx in range(nct)], axis=-1).astype(BF16)
        hmid = _silu(jnp.dot(xb, wg_ref[0], preferred_element_type=F32)) * jnp.dot(xb, wu_ref[0],
                                                                                   preferred_element_type=F32)
        out = _dot(hmid, wd_ref[0])
        eye = (lax.broadcasted_iota(jnp.int32, (EXPERT_BLOCK, EXPERT_BLOCK), 0)
               == lax.broadcasted_iota(jnp.int32, (EXPERT_BLOCK, EXPERT_BLOCK), 1))
        w_col = jnp.sum(jnp.where(eye, w_ref[0], 0.0), axis=1, keepdims=True)
        out = out * w_col
        for cidx in range(nct):
            ob[cur, :, cidx, :] = out[:, cidx * LANES:(cidx + 1) * LANES]

        nv = nv_ref[i]

        def issue(j, carry):
            scatter_copy(j, cur).start()
            return carry

        lax.fori_loop(0, nv, issue, 0)

        @pl.when(i == n_used - 1)
        def _():
            scatter_wait(nv, cur)

            @pl.when(i >= 1)
            def _():
                scatter_wait(nv_ref[jnp.maximum(i - 1, 0)], nxt)


def _experts(hffn, block_e, block_nv, n_used, row_tok, row_slot, row_w, wg, wu, wd, n_slots):
    t, nct, _ = hffn.shape
    d = nct * LANES
    nb = block_e.shape[0]
    de = wg.shape[2]
    last = nb - 1
    smem_blk = lambda f: pl.BlockSpec((1, 1, EXPERT_BLOCK), f, memory_space=pltpu.SMEM)
    grid_spec = pltpu.PrefetchScalarGridSpec(
        num_scalar_prefetch=3,
        grid=(nb,),
        in_specs=[smem_blk(lambda i, be, nv, nu: (i, 0, 0)),
                  smem_blk(lambda i, be, nv, nu: (jnp.minimum(i + 1, last), 0, 0)),
                  smem_blk(lambda i, be, nv, nu: (i, 0, 0)),
                  pl.BlockSpec((1, 1, EXPERT_BLOCK), lambda i, be, nv, nu: (i, 0, 0)),
                  pl.BlockSpec((1, d, de), lambda i, be, nv, nu: (be[i], 0, 0)),
                  pl.BlockSpec((1, d, de), lambda i, be, nv, nu: (be[i], 0, 0)),
                  pl.BlockSpec((1, de, d), lambda i, be, nv, nu: (be[i], 0, 0)),
                  pl.BlockSpec(memory_space=pl.ANY)],
        out_specs=pl.BlockSpec(memory_space=pl.ANY),
        scratch_shapes=[pltpu.VMEM((2, EXPERT_BLOCK, nct, LANES), F32), pltpu.VMEM((2, EXPERT_BLOCK, nct, LANES), F32),
                        pltpu.SemaphoreType.DMA((2,)), pltpu.SemaphoreType.DMA((2,))],
    )
    return pl.pallas_call(
        _experts_kernel,
        grid_spec=grid_spec,
        out_shape=jax.ShapeDtypeStruct((n_slots, nct, LANES), F32),
        compiler_params=_cparams("arbitrary"),
        name="experts",
    )(block_e, block_nv, n_used, row_tok, row_tok, row_slot, row_w, wg, wu, wd, hffn)


def _combine_kernel(x1_ref, ys_ref, slots_ref, npost_ref, g5_ref, o_ref):
    nct = slots_ref.shape[2]
    routed = []
    for cidx in range(nct):
        acc = slots_ref[0, :, cidx, :]
        for k in range(1, TOP_K):
            acc = acc + slots_ref[k, :, cidx, :]
        routed.append(acc)
    y = ys_ref[0] + jnp.concatenate(routed, axis=-1)
    o_ref[0] = x1_ref[0] + g5_ref[0] * _rms(y, npost_ref[...])


def _combine(x1, ys, slots, npost, g5, tm):
    b, s, d = x1.shape
    nt = s // tm
    tok = pl.BlockSpec((1, tm, d), lambda bi, i: (bi, i, 0))
    return pl.pallas_call(
        _combine_kernel,
        grid=(b, nt),
        in_specs=[tok, tok, pl.BlockSpec((TOP_K, tm, d // LANES, LANES), lambda bi, i: (0, bi * nt + i, 0, 0)),
                  pl.BlockSpec((1, d), lambda bi, i: (0, 0)), pl.BlockSpec((1, 1, d), lambda bi, i: (bi, 0, 0))],
        out_specs=tok,
        out_shape=jax.ShapeDtypeStruct((b, s, d), F32),
        compiler_params=_cparams("parallel", "arbitrary"),
        name="combine",
    )(x1, ys, slots, npost, g5)


def _dispatch_plan(idx_t, gate_t):
    k, t = idx_t.shape
    n_asg = k * t
    nb = n_asg // EXPERT_BLOCK + N_EXPERTS
    flat_e = idx_t.reshape(-1)
    order = jnp.argsort(flat_e).astype(jnp.int32)
    counts = jnp.zeros((N_EXPERTS,), jnp.int32).at[flat_e].add(1)
    padded = (counts + EXPERT_BLOCK - 1) // EXPERT_BLOCK * EXPERT_BLOCK
    start = jnp.cumsum(counts) - counts
    pend = jnp.cumsum(padded)
    pstart = pend - padded
    blk0 = jnp.arange(nb, dtype=jnp.int32) * EXPERT_BLOCK
    block_e = jnp.minimum(jnp.searchsorted(pend, blk0, side='right'), N_EXPERTS - 1).astype(jnp.int32)
    block_nv = jnp.clip(counts[block_e] - (blk0 - pstart[block_e]), 0, EXPERT_BLOCK).astype(jnp.int32)
    n_used = (pend[-1] // EXPERT_BLOCK).astype(jnp.int32).reshape(1)
    pos = blk0[:, None] - pstart[block_e][:, None] + jnp.arange(EXPERT_BLOCK, dtype=jnp.int32)[None, :]
    valid = pos < counts[block_e][:, None]
    src = jnp.clip(start[block_e][:, None] + pos, 0, n_asg - 1)
    asg = order[src]
    row_tok = jnp.where(valid, asg % t, 0).astype(jnp.int32)
    row_slot = jnp.where(valid, asg, 0).astype(jnp.int32)
    row_w = jnp.where(valid, gate_t.reshape(-1)[asg], 0.0).astype(F32)
    shp = (nb, 1, EXPERT_BLOCK)
    return block_e, block_nv, n_used, row_tok.reshape(shp), row_slot.reshape(shp), row_w.reshape(shp)


def _pack_in_weights(w_in, ml_i_bias, ml_f_bias, gd_dt_bias):
    d = w_in.shape[0]
    nml = 2 * ML_HEADS * ML_QK + 2 * ML_HEADS * ML_V
    ml_cols = nml + 4 * ML_HEADS
    ngq = GD_HEADS * (2 * GD_QK + GD_V)
    ngz = GD_HEADS * GD_V
    wml = w_in[:, :nml].astype(BF16)
    wgq = w_in[:, ml_cols:ml_cols + ngq].astype(BF16)
    wgz = w_in[:, ml_cols + ngq:ml_cols + ngq + ngz].astype(BF16)
    wg = jnp.zeros((d, GATE_LANES), F32)
    wg = wg.at[:, ML_GATE0:ML_GATE0 + 16].set(w_in[:, nml:ml_cols])
    wg = wg.at[:, GD_GATE0:GD_GATE0 + 16].set(w_in[:, ml_cols + ngq + ngz:])
    gb = jnp.zeros((GATE_LANES,), F32)
    gb = gb.at[ML_GATE0:ML_GATE0 + 16].set(jnp.stack([ml_i_bias, ml_f_bias], axis=1).reshape(-1))
    gb = gb.at[GD_GATE0:GD_GATE0 + 16].set(jnp.stack([gd_dt_bias, jnp.zeros_like(gd_dt_bias)], axis=1).reshape(-1))
    return wml, wgq, wgz, wg.astype(BF16), gb.reshape(1, GATE_LANES)


def _mixer(x, ctx, mod, mod_ctx, norm_pre_mix, w_in, ml_i_bias, ml_f_bias, gd_conv_w, gd_a_log, gd_dt_bias):
    b, s, d = x.shape
    sc = ctx.shape[1]
    wml, wgq, wgz, wg, gb = _pack_in_weights(w_in, ml_i_bias, ml_f_bias, gd_dt_bias)
    nw = norm_pre_mix.reshape(1, d)
    ctx_mod = lambda j: jnp.broadcast_to(mod_ctx[j].reshape(1, 1, d), (b, 1, d))
    ml_c, gq_c, _, g_c = _proj(ctx, nw, ctx_mod(0), ctx_mod(1), wml, wgq, wgz, wg, gb, tm=sc)
    ml_l, gq_l, gz_l, g_l = _proj(x, nw, mod[0], mod[1], wml, wgq, wgz, wg, gb, tm=512)

    c0 = jnp.zeros((b, N_CHAINS, ML_QK, ML_V), F32)
    n0 = jnp.zeros((b, N_CHAINS, 1, ML_QK), F32)
    m0 = jnp.zeros((b, N_CHAINS, 1, 1), F32)
    _, _, c1, n1, m1 = _mlstm(ml_c, g_c, c0, n0, m0)
    hf, hb, _, _, _ = _mlstm(ml_l, g_l, c1, n1, m1)

    neg_a = jnp.zeros((GATE_LANES,), F32)
    neg_a = neg_a.at[GD_GATE0:GD_GATE0 + 16].set(
        jnp.stack([-jnp.exp(gd_a_log), jnp.zeros_like(gd_a_log)], axis=1).reshape(-1)).reshape(1, GATE_LANES)
    qn_c = _gdconv_ctx(gq_c, gd_conv_w)
    qn_l = _gdconv_lat(gq_l, gd_conv_w)
    s0 = jnp.zeros((b, N_CHAINS, GD_QK, GD_V), F32)
    hdim = GD_HEADS * GD_V
    _, _, s1 = _gdn(qn_c, g_c, neg_a, s0, sc // CHUNK, lambda bi, n: (bi, n, 0), (b, sc, hdim))
    rows = s // GRID_W
    cpc = rows // CHUNK
    col_idx = lambda bi, n: (bi, n % cpc, n // cpc)
    nq = qn_l.shape[2]
    of, ob, _ = _gdn(qn_l.reshape(b, rows, GRID_W * nq), g_l.reshape(b, rows, GRID_W * GATE_LANES), neg_a, s1,
                     s // CHUNK, col_idx, (b, rows, GRID_W * hdim))
    return hf, hb, ml_l, of.reshape(b, s, hdim), ob.reshape(b, s, hdim), gz_l


def kernel(x, c, ctx, c_ctx, w_ada, b_ada, norm_pre_mix, norm_post_mix, norm_pre_ffn, norm_post_ffn, w_in,
           ml_i_bias, ml_f_bias, ml_norm_w, gd_conv_w, gd_a_log, gd_dt_bias, gd_norm_w, w_out, router_w,
           router_bias, w_gate, w_up, w_down, ws_gate, ws_up, ws_down):
    b, s, d = x.shape
    depth = w_ada.shape[0]
    assert depth == 1, "the context stream update of deeper stacks is not implemented"
    ly = 0
    cc = jnp.zeros((16, d), F32).at[:b].set(c).at[b].set(c_ctx)
    mod_all = _ada(cc, w_ada[ly], b_ada[ly])
    mod = [mod_all[:b, j * d:(j + 1) * d].reshape(b, 1, d) for j in range(6)]
    mod_ctx = [mod_all[b, j * d:(j + 1) * d] for j in range(6)]

    hf, hb, ml_l, of, ob, gz_l = _mixer(x, ctx, mod, mod_ctx, norm_pre_mix[ly], w_in[ly], ml_i_bias[ly],
                                        ml_f_bias[ly], gd_conv_w[ly], gd_a_log[ly], gd_dt_bias[ly])

    row = lambda v: v.reshape(1, -1)
    x1, hffn, logits_t, ys = _post(
        x, hf, hb, ml_l, of, ob, gz_l, row(ml_norm_w[ly]), row(jnp.tile(gd_norm_w[ly], GD_HEADS)),
        w_out[ly].astype(BF16), row(norm_post_mix[ly]), mod[2], row(norm_pre_ffn[ly]), mod[3], mod[4],
        router_w[ly].T.astype(BF16), ws_gate[ly].astype(BF16), ws_up[ly].astype(BF16), ws_down[ly].astype(BF16),
        tm=256)

    idx_t, gate_t = _route(logits_t, router_bias[ly].reshape(-1, 1), tn=512)
    block_e, block_nv, n_used, row_tok, row_slot, row_w = _dispatch_plan(idx_t, gate_t)
    t = b * s
    slots = _experts(hffn.reshape(t, d // LANES, LANES), block_e, block_nv, n_used, row_tok, row_slot, row_w,
                     w_gate[ly].astype(BF16), w_up[ly].astype(BF16), w_down[ly].astype(BF16), TOP_K * t)
    return _combine(x1, ys, slots.reshape(TOP_K, t, d // LANES, LANES), row(norm_post_ffn[ly]), mod[5], tm=256)
```

```python
import functools

import jax
import jax.numpy as jnp
from jax import lax
from jax.experimental import pallas as pl
from jax.experimental.pallas import tpu as pltpu

EPS = 1e-6
CHUNK = 64
GRID_W = 64
ML_HEADS, ML_QK, ML_V = 4, 64, 128
GD_HEADS, GD_QK, GD_V = 4, 128, 128
CONV_W = 5
N_EXPERTS, TOP_K, N_GROUPS, TOPK_GROUPS = 256, 8, 8, 4
ROUTED_SCALE = 2.5
EXPERT_BLOCK = 128
N_CHAINS = 8
LANES = 128
GATE_LANES = LANES
ML_GATE0, GD_GATE0 = 0, 16

F32 = jnp.float32
BF16 = jnp.bfloat16
HI = lax.Precision.HIGHEST
VMEM_LIMIT = 56 * 1024 * 1024


def _cparams(*sem):
    return pltpu.CompilerParams(dimension_semantics=sem, vmem_limit_bytes=VMEM_LIMIT)


def _dot(a, b):
    return jnp.dot(a.astype(BF16), b.astype(BF16), preferred_element_type=F32)


def _dot_nt(a, b):
    return lax.dot_general(a.astype(BF16), b.astype(BF16), (((1,), (1,)), ((), ())), preferred_element_type=F32)


def _dot_tn(a, b):
    return lax.dot_general(a.astype(BF16), b.astype(BF16), (((0,), (0,)), ((), ())), preferred_element_type=F32)


def _dot_hi(a, b):
    return jnp.dot(a, b, precision=HI, preferred_element_type=F32)


def _dot_nt_hi(a, b):
    return lax.dot_general(a, b, (((1,), (1,)), ((), ())), precision=HI, preferred_element_type=F32)


def _transpose_hi(x):
    n = x.shape[1]
    eye = (lax.broadcasted_iota(jnp.int32, (n, n), 0) == lax.broadcasted_iota(jnp.int32, (n, n), 1)).astype(F32)
    return _dot_nt_hi(eye, x)


def _silu(x):
    return x * jax.nn.sigmoid(x)


def _past_mask(reverse):
    t = lax.broadcasted_iota(jnp.int32, (CHUNK, CHUNK), 0)
    s = lax.broadcasted_iota(jnp.int32, (CHUNK, CHUNK), 1)
    return (s >= t, s > t) if reverse else (s <= t, s < t)


def _ada_kernel(c_ref, w_ref, b_ref, o_ref):
    o_ref[...] = _dot(_silu(c_ref[...]), w_ref[...]) + b_ref[...]


def _ada(cc, w_ada, b_ada):
    rows, d = cc.shape
    n = w_ada.shape[1]
    tn = 1536
    return pl.pallas_call(
        _ada_kernel,
        grid=(n // tn,),
        in_specs=[pl.BlockSpec((rows, d), lambda j: (0, 0)),
                  pl.BlockSpec((d, tn), lambda j: (0, j)),
                  pl.BlockSpec((1, tn), lambda j: (0, j))],
        out_specs=pl.BlockSpec((rows, tn), lambda j: (0, j)),
        out_shape=jax.ShapeDtypeStruct((rows, n), F32),
        compiler_params=_cparams("arbitrary"),
        name="ada",
    )(cc, w_ada, b_ada.reshape(1, n))


def _proj_kernel(x_ref, nw_ref, sh_ref, sc_ref, wml_ref, wgq_ref, wgz_ref, wg_ref, gb_ref,
                 ml_ref, gq_ref, gz_ref, g_ref):
    x = x_ref[0]
    xn = x * lax.rsqrt(jnp.mean(x * x, axis=-1, keepdims=True) + EPS) * nw_ref[...]
    h = (xn * (1.0 + sc_ref[0]) + sh_ref[0]).astype(BF16)
    ml_ref[0] = jnp.dot(h, wml_ref[...], preferred_element_type=F32)
    gq_ref[0] = jnp.dot(h, wgq_ref[...], preferred_element_type=F32)
    gz_ref[0] = jnp.dot(h, wgz_ref[...], preferred_element_type=F32)
    g_ref[0] = jnp.dot(h, wg_ref[...], preferred_element_type=F32) + gb_ref[...]


def _proj(x, norm_w, shift, scale, wml, wgq, wgz, wg, gbias, tm):
    b, s, d = x.shape
    nml, ngq, ngz = wml.shape[1], wgq.shape[1], wgz.shape[1]
    full = lambda shp: pl.BlockSpec(shp, lambda bi, i: (0,) * len(shp))
    tok = lambda n: pl.BlockSpec((1, tm, n), lambda bi, i: (bi, i, 0))
    mod = pl.BlockSpec((1, 1, d), lambda bi, i: (bi, 0, 0))
    return pl.pallas_call(
        _proj_kernel,
        grid=(b, s // tm),
        in_specs=[tok(d), full((1, d)), mod, mod, full((d, nml)), full((d, ngq)), full((d, ngz)),
                  full((d, GATE_LANES)), full((1, GATE_LANES))],
        out_specs=[tok(nml), tok(ngq), tok(ngz), tok(GATE_LANES)],
        out_shape=[jax.ShapeDtypeStruct((b, s, n), F32) for n in (nml, ngq, ngz, GATE_LANES)],
        compiler_params=_cparams("parallel", "arbitrary"),
        name="proj",
    )(x, norm_w, shift, scale, wml, wgq, wgz, wg, gbias)


def _mlstm_kernel(mlf_ref, mlb_ref, gf_ref, gb_ref, c0_ref, n0_ref, m0_ref,
                  hf_ref, hb_ref, cn_ref, nn_ref, mn_ref, c_scr, n_scr, m_scr):
    i = pl.program_id(1)

    @pl.when(i == 0)
    def _():
        c_scr[...] = c0_ref[0]
        n_scr[...] = n0_ref[0]
        m_scr[...] = m0_ref[0]

    past = [_past_mask(d == 1)[0] for d in range(2)]
    g = [r[0] for r in (gf_ref, gb_ref)]
    ls = [jax.nn.log_sigmoid(x) for x in g]
    bcol = [_dot_hi(past[d].astype(F32), ls[d]) for d in range(2)]
    tot = [jnp.sum(x, axis=0, keepdims=True) for x in ls]
    g_t = [_transpose_hi(x) for x in g]
    b_t = [_transpose_hi(x) for x in bcol]

    chains = [(d, hd) for d in range(2) for hd in range(ML_HEADS)]
    nc = range(len(chains))
    ml_refs, h_refs = (mlf_ref, mlb_ref), (hf_ref, hb_ref)
    k0, v0 = ML_HEADS * ML_QK, 2 * ML_HEADS * ML_QK
    q, k, v, i_col, b_col, b_end, log_d = [], [], [], [], [], [], []
    for d, hd in chains:
        ci = ML_GATE0 + d * 8 + hd
        cf = ci + ML_HEADS
        q.append(ml_refs[d][0, :, hd * ML_QK:(hd + 1) * ML_QK])
        k.append(ml_refs[d][0, :, k0 + hd * ML_QK:k0 + (hd + 1) * ML_QK] * (ML_QK ** -0.5))
        v.append(ml_refs[d][0, :, v0 + hd * ML_V:v0 + (hd + 1) * ML_V])
        i_col.append(g[d][:, ci:ci + 1])
        b_col.append(bcol[d][:, cf:cf + 1])
        b_end.append(tot[d][:, cf:cf + 1])
        log_d.append(jnp.where(past[d], b_col[-1] - b_t[d][cf:cf + 1, :] + g_t[d][ci:ci + 1, :], -jnp.inf))
    c_st = [c_scr[c] for c in nc]
    n_st = [n_scr[c] for c in nc]
    m_st = [m_scr[c] for c in nc]
    log_prev = [b_col[c] + m_st[c] for c in nc]
    m_t = [jnp.maximum(log_prev[c], jnp.max(log_d[c], axis=-1, keepdims=True)) for c in nc]
    qk = [_dot_nt(q[c], k[c]) for c in nc]
    qc = [_dot(q[c], c_st[c]) for c in nc]
    s = [qk[c] * jnp.exp(log_d[c] - m_t[c]) for c in nc]
    w_prev = [jnp.exp(log_prev[c] - m_t[c]) for c in nc]
    sv = [_dot(s[c], v[c]) for c in nc]
    log_s = [b_end[c] - b_col[c] + i_col[c] for c in nc]
    m_new = [jnp.maximum(b_end[c] + m_st[c], jnp.max(log_s[c], axis=0, keepdims=True)) for c in nc]
    kw = [k[c] * jnp.exp(log_s[c] - m_new[c]) for c in nc]
    w_c = [jnp.exp(b_end[c] + m_st[c] - m_new[c]) for c in nc]
    kv = [_dot_tn(kw[c], v[c]) for c in nc]
    for c, (d, hd) in enumerate(chains):
        num = sv[c] + w_prev[c] * qc[c]
        den = jnp.sum(s[c], axis=-1, keepdims=True) + w_prev[c] * jnp.sum(q[c] * n_st[c], axis=-1, keepdims=True)
        h_refs[d][0, :, hd * ML_V:(hd + 1) * ML_V] = num / jnp.maximum(jnp.abs(den), jnp.exp(-m_t[c]))
        c_scr[c] = w_c[c] * c_st[c] + kv[c]
        n_scr[c] = w_c[c] * n_st[c] + jnp.sum(kw[c], axis=0, keepdims=True)
        m_scr[c] = m_new[c]

    @pl.when(i == pl.num_programs(1) - 1)
    def _():
        cn_ref[0] = c_scr[...]
        nn_ref[0] = n_scr[...]
        mn_ref[0] = m_scr[...]


def _mlstm(ml, gates, c0, n0, m0):
    b, s, nml = ml.shape
    nc = s // CHUNK
    fwd = lambda n: pl.BlockSpec((1, CHUNK, n), lambda bi, i: (bi, i, 0))
    bwd = lambda n: pl.BlockSpec((1, CHUNK, n), lambda bi, i: (bi, nc - 1 - i, 0))
    st = lambda shp: pl.BlockSpec((1,) + shp, lambda bi, i: (bi,) + (0,) * len(shp))
    hdim = ML_HEADS * ML_V
    return pl.pallas_call(
        _mlstm_kernel,
        grid=(b, nc),
        in_specs=[fwd(nml), bwd(nml), fwd(GATE_LANES), bwd(GATE_LANES),
                  st((N_CHAINS, ML_QK, ML_V)), st((N_CHAINS, 1, ML_QK)), st((N_CHAINS, 1, 1))],
        out_specs=[fwd(hdim), bwd(hdim),
                   st((N_CHAINS, ML_QK, ML_V)), st((N_CHAINS, 1, ML_QK)), st((N_CHAINS, 1, 1))],
        out_shape=[jax.ShapeDtypeStruct((b, s, hdim), F32), jax.ShapeDtypeStruct((b, s, hdim), F32),
                   jax.ShapeDtypeStruct(c0.shape, F32), jax.ShapeDtypeStruct(n0.shape, F32),
                   jax.ShapeDtypeStruct(m0.shape, F32)],
        scratch_shapes=[pltpu.VMEM((N_CHAINS, ML_QK, ML_V), F32), pltpu.VMEM((N_CHAINS, 1, ML_QK), F32),
                        pltpu.VMEM((N_CHAINS, 1, 1), F32)],
        compiler_params=_cparams("parallel", "arbitrary"),
        name="mlstm",
    )(ml, ml, gates, gates, c0, n0, m0)


def _gdconv_kernel(has_halo, *refs):
    if has_halo:
        x_ref, prev_ref, next_ref, w_ref, o_ref, xp_ref = refs
    else:
        x_ref, w_ref, o_ref, xp_ref = refs
    rows = x_ref.shape[1]
    nch = x_ref.shape[2]
    pad = 8
    zero = jnp.zeros((pad, nch), F32)
    if has_halo:
        c = pl.program_id(1)
        xp_ref[0:pad, :] = jnp.where(c > 0, prev_ref[0], zero)
        xp_ref[pad + rows:, :] = jnp.where(c < pl.num_programs(1) - 1, next_ref[0], zero)
    else:
        xp_ref[0:pad, :] = zero
        xp_ref[pad + rows:, :] = zero
    xp_ref[pad:pad + rows, :] = x_ref[0]
    half = CONV_W // 2
    for lc in range(nch // 128):
        sl = slice(lc * 128, (lc + 1) * 128)
        acc = None
        for j in range(CONV_W):
            term = xp_ref[pad - half + j:pad - half + j + rows, sl] * w_ref[j:j + 1, sl]
            acc = term if acc is None else acc + term
        y = _silu(acc)
        if lc < 2 * GD_HEADS:
            y = y * lax.rsqrt(jnp.sum(y * y, axis=-1, keepdims=True) + EPS)
        if lc < GD_HEADS:
            y = y * (GD_QK ** -0.5)
        o_ref[0, :, sl] = y


def _gdconv_ctx(qkv, conv_w):
    b, s, nch = qkv.shape
    return pl.pallas_call(
        functools.partial(_gdconv_kernel, False),
        grid=(b,),
        in_specs=[pl.BlockSpec((1, s, nch), lambda bi: (bi, 0, 0)), pl.BlockSpec((CONV_W, nch), lambda bi: (0, 0))],
        out_specs=pl.BlockSpec((1, s, nch), lambda bi: (bi, 0, 0)),
        out_shape=jax.ShapeDtypeStruct((b, s, nch), F32),
        scratch_shapes=[pltpu.VMEM((s + 16, nch), F32)],
        compiler_params=_cparams("parallel"),
        name="gdconv_ctx",
    )(qkv, conv_w)


def _gdconv_lat(qkv, conv_w):
    b, s, nch = qkv.shape
    rows = s // GRID_W
    view = qkv.reshape(b, rows, GRID_W * nch)
    rb = rows // 8
    out = pl.pallas_call(
        functools.partial(_gdconv_kernel, True),
        grid=(b, GRID_W),
        in_specs=[pl.BlockSpec((1, rows, nch), lambda bi, c: (bi, 0, c)),
                  pl.BlockSpec((1, 8, nch), lambda bi, c: (bi, rb - 1, jnp.maximum(c - 1, 0))),
                  pl.BlockSpec((1, 8, nch), lambda bi, c: (bi, 0, jnp.minimum(c + 1, GRID_W - 1))),
                  pl.BlockSpec((CONV_W, nch), lambda bi, c: (0, 0))],
        out_specs=pl.BlockSpec((1, rows, nch), lambda bi, c: (bi, 0, c)),
        out_shape=jax.ShapeDtypeStruct(view.shape, F32),
        scratch_shapes=[pltpu.VMEM((rows + 16, nch), F32)],
        compiler_params=_cparams("parallel", "arbitrary"),
        name="gdconv_lat",
    )(view, view, view, conv_w)
    return out.reshape(b, s, nch)


def _unit_lower_inverse(n_strict, eye):
    x = n_strict
    p = eye - n_strict
    for _ in range(5):
        x = _dot_hi(x, x)
        p = p + _dot_hi(p, x)
    return p


def _gdn_kernel(qf_ref, qb_ref, gf_ref, gb_ref, na_ref, s0_ref, of_ref, ob_ref, sn_ref, s_scr):
    i = pl.program_id(1)

    @pl.when(i == 0)
    def _():
        s_scr[...] = s0_ref[0]

    eye = (lax.broadcasted_iota(jnp.int32, (CHUNK, CHUNK), 0)
           == lax.broadcasted_iota(jnp.int32, (CHUNK, CHUNK), 1)).astype(F32)
    nqk = GD_HEADS * GD_QK
    masks = [_past_mask(d == 1) for d in range(2)]
    gates = [r[0] for r in (gf_ref, gb_ref)]
    glog = [na_ref[...] * jax.nn.softplus(g) for g in gates]
    beta_all = [jax.nn.sigmoid(g) for g in gates]
    gcum = [_dot_hi(masks[d][0].astype(F32), glog[d]) for d in range(2)]
    gtot = [jnp.sum(g, axis=0, keepdims=True) for g in glog]
    gcum_t = [_transpose_hi(g) for g in gcum]

    chains = [(d, hd) for d in range(2) for hd in range(GD_HEADS)]
    x_refs, o_refs = (qf_ref, qb_ref), (of_ref, ob_ref)
    q, k, v, g_col, beta, g_end, decay = [], [], [], [], [], [], []
    for d, hd in chains:
        ca = GD_GATE0 + d * 8 + hd
        q.append(x_refs[d][0, :, hd * GD_QK:(hd + 1) * GD_QK])
        k.append(x_refs[d][0, :, nqk + hd * GD_QK:nqk + (hd + 1) * GD_QK])
        v.append(x_refs[d][0, :, 2 * nqk + hd * GD_V:2 * nqk + (hd + 1) * GD_V])
        g_col.append(gcum[d][:, ca:ca + 1])
        beta.append(beta_all[d][:, ca + GD_HEADS:ca + GD_HEADS + 1])
        g_end.append(gtot[d][:, ca:ca + 1])
        decay.append(jnp.exp(jnp.where(masks[d][0], g_col[-1] - gcum_t[d][ca:ca + 1, :], -jnp.inf)))
    nc = range(len(chains))
    kk = [_dot_nt(k[c], k[c]) for c in nc]
    xs = [jnp.where(masks[chains[c][0]][1], beta[c] * kk[c] * decay[c], 0.0) for c in nc]
    ps = [eye - x for x in xs]
    for _ in range(5):
        xs = [_dot_hi(x, x) for x in xs]
        ps = [p + _dot_hi(p, x) for p, x in zip(ps, xs)]
    uw = [_dot_hi(ps[c], jnp.concatenate([v[c] * beta[c], k[c] * (beta[c] * jnp.exp(g_col[c]))], axis=-1)) for c in nc]
    qk = [_dot_nt(q[c], k[c]) * decay[c] for c in nc]
    s_st = [s_scr[c] for c in nc]
    v_new = [uw[c][:, :GD_V] - _dot(uw[c][:, GD_V:], s_st[c]) for c in nc]
    o_loc = [_dot(q[c] * jnp.exp(g_col[c]), s_st[c]) for c in nc]
    o_new = [o_loc[c] + _dot(qk[c], v_new[c]) for c in nc]
    s_new = [s_st[c] * jnp.exp(g_end[c]) + _dot_tn(k[c] * jnp.exp(g_end[c] - g_col[c]), v_new[c]) for c in nc]
    for c, (d, hd) in enumerate(chains):
        o_refs[d][0, :, hd * GD_V:(hd + 1) * GD_V] = o_new[c]
        s_scr[c] = s_new[c]

    @pl.when(i == pl.num_programs(1) - 1)
    def _():
        sn_ref[0] = s_scr[...]


def _gdn(qkv_view, gates_view, neg_a, s0, nc, idx_fn, out_view_shape):
    b = qkv_view.shape[0]
    nqkv = 2 * GD_HEADS * GD_QK + GD_HEADS * GD_V
    hdim = GD_HEADS * GD_V
    fwd = lambda n: pl.BlockSpec((1, CHUNK, n), lambda bi, i: idx_fn(bi, i))
    bwd = lambda n: pl.BlockSpec((1, CHUNK, n), lambda bi, i: idx_fn(bi, nc - 1 - i))
    st = pl.BlockSpec((1, N_CHAINS, GD_QK, GD_V), lambda bi, i: (bi, 0, 0, 0))
    return pl.pallas_call(
        _gdn_kernel,
        grid=(b, nc),
        in_specs=[fwd(nqkv), bwd(nqkv), fwd(GATE_LANES), bwd(GATE_LANES),
                  pl.BlockSpec((1, GATE_LANES), lambda bi, i: (0, 0)), st],
        out_specs=[fwd(hdim), bwd(hdim), st],
        out_shape=[jax.ShapeDtypeStruct(out_view_shape, F32), jax.ShapeDtypeStruct(out_view_shape, F32),
                   jax.ShapeDtypeStruct(s0.shape, F32)],
        scratch_shapes=[pltpu.VMEM((N_CHAINS, GD_QK, GD_V), F32)],
        compiler_params=_cparams("parallel", "arbitrary"),
        name="gdn",
    )(qkv_view, qkv_view, gates_view, gates_view, neg_a, s0)


def _head_rms(t, nheads, width):
    outs = []
    for hd in range(nheads):
        th = t[:, hd * width:(hd + 1) * width]
        outs.append(th * lax.rsqrt(jnp.mean(th * th, axis=-1, keepdims=True) + EPS))
    return jnp.concatenate(outs, axis=-1)


def _rms(t, w):
    return t * lax.rsqrt(jnp.mean(t * t, axis=-1, keepdims=True) + EPS) * w


def _post_kernel(x_ref, hf_ref, hb_ref, og_ref, of_ref, ob_ref, z_ref, mlw_ref, gdw_ref, wout_ref,
                 npost_ref, g2_ref, npre_ref, sh_ref, sc_ref, rwt_ref, wsg_ref, wsu_ref, wsd_ref,
                 x1_ref, hffn_ref, lt_ref, ys_ref):
    ml_y = _head_rms(hf_ref[0] + hb_ref[0], ML_HEADS, ML_V) * mlw_ref[...] * jax.nn.sigmoid(og_ref[0])
    gd_y = _head_rms(of_ref[0] + ob_ref[0], GD_HEADS, GD_V) * gdw_ref[...] * _silu(z_ref[0])
    y = _dot(jnp.concatenate([ml_y, gd_y], axis=-1), wout_ref[...])
    x1 = x_ref[0] + g2_ref[0] * _rms(y, npost_ref[...])
    x1_ref[0] = x1
    hffn = _rms(x1, npre_ref[...]) * (1.0 + sc_ref[0]) + sh_ref[0]
    for cidx in range(hffn.shape[1] // LANES):
        hffn_ref[0, :, cidx, :] = hffn[:, cidx * LANES:(cidx + 1) * LANES]
    hb = hffn.astype(BF16)
    lt_ref[...] = lax.dot_general(rwt_ref[...], hb, (((1,), (1,)), ((), ())), preferred_element_type=F32)
    hs = _silu(jnp.dot(hb, wsg_ref[...], preferred_element_type=F32)) * jnp.dot(hb, wsu_ref[...],
                                                                                preferred_element_type=F32)
    ys_ref[0] = _dot(hs, wsd_ref[...])


def _post(x, hf, hb, ml, of, ob, gz, mlw, gdw, wout, npost, g2, npre, sh, sc, rwt, wsg, wsu, wsd, tm):
    b, s, d = x.shape
    nt = s // tm
    hw = ML_HEADS * ML_V
    og_blk = (2 * ML_HEADS * ML_QK + ML_HEADS * ML_V) // hw
    tok = lambda n: pl.BlockSpec((1, tm, n), lambda bi, i: (bi, i, 0))
    full = lambda shp: pl.BlockSpec(shp, lambda bi, i: (0,) * len(shp))
    mod = pl.BlockSpec((1, 1, d), lambda bi, i: (bi, 0, 0))
    ne = rwt.shape[0]
    ds = wsg.shape[1]
    return pl.pallas_call(
        _post_kernel,
        grid=(b, nt),
        in_specs=[tok(d), tok(hw), tok(hw), pl.BlockSpec((1, tm, hw), lambda bi, i: (bi, i, og_blk)),
                  tok(hw), tok(hw), tok(hw), full((1, hw)), full((1, hw)), full((d, d)),
                  full((1, d)), mod, full((1, d)), mod, mod, full((ne, d)), full((d, ds)), full((d, ds)),
                  full((ds, d))],
        out_specs=[tok(d), pl.BlockSpec((1, tm, d // LANES, LANES), lambda bi, i: (bi, i, 0, 0)),
                   pl.BlockSpec((ne, tm), lambda bi, i: (0, bi * nt + i)), tok(d)],
        out_shape=[jax.ShapeDtypeStruct((b, s, d), F32), jax.ShapeDtypeStruct((b, s, d // LANES, LANES), F32),
                   jax.ShapeDtypeStruct((ne, b * s), F32), jax.ShapeDtypeStruct((b, s, d), F32)],
        compiler_params=_cparams("parallel", "arbitrary"),
        name="post",
    )(x, hf, hb, ml, of, ob, gz, mlw, gdw, wout, npost, g2, npre, sh, sc, rwt, wsg, wsu, wsd)


def _route_kernel(lt_ref, bias_ref, idx_ref, gate_ref):
    ne, tn = lt_ref.shape
    gsz = ne // N_GROUPS
    scores = jax.nn.sigmoid(lt_ref[...])
    sel = scores + bias_ref[...]
    neg = -jnp.inf
    sel3 = sel.reshape(N_GROUPS, gsz, tn)
    io3 = lax.broadcasted_iota(jnp.int32, sel3.shape, 1)
    top1 = jnp.max(sel3, axis=1, keepdims=True)
    first = jnp.min(jnp.where(sel3 == top1, io3, gsz), axis=1, keepdims=True)
    top2 = jnp.max(jnp.where(io3 == first, neg, sel3), axis=1, keepdims=True)
    grp = (top1 + top2).reshape(N_GROUPS, tn)
    iog = lax.broadcasted_iota(jnp.int32, grp.shape, 0)
    keep = jnp.zeros(grp.shape, jnp.bool_)
    for _ in range(TOPK_GROUPS):
        m = jnp.max(grp, axis=0, keepdims=True)
        pick = iog == jnp.min(jnp.where(grp == m, iog, N_GROUPS), axis=0, keepdims=True)
        keep = keep | pick
        grp = jnp.where(pick, neg, grp)
    cand = jnp.where(keep.reshape(N_GROUPS, 1, tn), sel3, neg).reshape(ne, tn)
    ioe = lax.broadcasted_iota(jnp.int32, cand.shape, 0)
    idxs, gates = [], []
    for _ in range(TOP_K):
        m = jnp.max(cand, axis=0, keepdims=True)
        e = jnp.min(jnp.where(cand == m, ioe, ne), axis=0, keepdims=True)
        pick = ioe == e
        idxs.append(e)
        gates.append(jnp.sum(jnp.where(pick, scores, 0.0), axis=0, keepdims=True))
        cand = jnp.where(pick, neg, cand)
    gate = jnp.concatenate(gates, axis=0)
    idx_ref[...] = jnp.concatenate(idxs, axis=0)
    gate_ref[...] = gate / jnp.sum(gate, axis=0, keepdims=True) * ROUTED_SCALE


def _route(logits_t, bias_col, tn):
    ne, t = logits_t.shape
    return pl.pallas_call(
        _route_kernel,
        grid=(t // tn,),
        in_specs=[pl.BlockSpec((ne, tn), lambda i: (0, i)), pl.BlockSpec((ne, 1), lambda i: (0, 0))],
        out_specs=[pl.BlockSpec((TOP_K, tn), lambda i: (0, i)), pl.BlockSpec((TOP_K, tn), lambda i: (0, i))],
        out_shape=[jax.ShapeDtypeStruct((TOP_K, t), jnp.int32), jax.ShapeDtypeStruct((TOP_K, t), F32)],
        compiler_params=_cparams("parallel"),
        name="route",
    )(logits_t, bias_col)


def _experts_kernel(be_ref, nv_ref, nu_ref,
                    tok_ref, tokn_ref, slot_ref, w_ref, wg_ref, wu_ref, wd_ref, h_hbm,
                    slots_hbm, xg, ob, gsem, ssem):
    i = pl.program_id(0)
    n_used = nu_ref[0]
    cur = i % 2
    nxt = 1 - cur

    def gather_copy(tref, j, buf):
        return pltpu.make_async_copy(h_hbm.at[tref[0, 0, j]], xg.at[buf, j], gsem.at[buf])

    def scatter_copy(j, buf):
        return pltpu.make_async_copy(ob.at[buf, j], slots_hbm.at[slot_ref[0, 0, j]], ssem.at[buf])

    def scatter_wait(n, buf):
        @pl.when(n > 0)
        def _():
            pltpu.make_async_copy(ob.at[buf, pl.ds(0, n)], slots_hbm.at[pl.ds(0, n)], ssem.at[buf]).wait()

    @pl.when(i < n_used)
    def _():
        @pl.when(i == 0)
        def _():
            for j in range(EXPERT_BLOCK):
                gather_copy(tok_ref, j, cur).start()

        @pl.when(i + 1 < n_used)
        def _():
            for j in range(EXPERT_BLOCK):
                gather_copy(tokn_ref, j, nxt).start()

        pltpu.make_async_copy(h_hbm.at[pl.ds(0, EXPERT_BLOCK)], xg.at[cur], gsem.at[cur]).wait()

        @pl.when(i >= 2)
        def _():
            scatter_wait(nv_ref[jnp.maximum(i - 2, 0)], cur)

        nct = xg.shape[2]
        xb = jnp.concatenate([xg[cur, :, cidx, :] for cidx in range(nct)], axis=-1).astype(BF16)
        hmid = _silu(jnp.dot(xb, wg_ref[0], preferred_element_type=F32)) * jnp.dot(xb, wu_ref[0],
                                                                                   preferred_element_type=F32)
        out = _dot(hmid, wd_ref[0])
        eye = (lax.broadcasted_iota(jnp.int32, (EXPERT_BLOCK, EXPERT_BLOCK), 0)
               == lax.broadcasted_iota(jnp.int32, (EXPERT_BLOCK, EXPERT_BLOCK), 1))
        w_col = jnp.sum(jnp.where(eye, w_ref[0], 0.0), axis=1, keepdims=True)
        out = out * w_col
        for cidx in range(nct):
            ob[cur, :, cidx, :] = out[:, cidx * LANES:(cidx + 1) * LANES]

        nv = nv_ref[i]

        def issue(j, carry):
            scatter_copy(j, cur).start()
            return carry

        lax.fori_loop(0, nv, issue, 0)

        @pl.when(i == n_used - 1)
        def _():
            scatter_wait(nv, cur)

            @pl.when(i >= 1)
            def _():
                scatter_wait(nv_ref[jnp.maximum(i - 1, 0)], nxt)


def _experts(hffn, block_e, block_nv, n_used, row_tok, row_slot, row_w, wg, wu, wd, n_slots):
    t, nct, _ = hffn.shape
    d = nct * LANES
    nb = block_e.shape[0]
    de = wg.shape[2]
    last = nb - 1
    smem_blk = lambda f: pl.BlockSpec((1, 1, EXPERT_BLOCK), f, memory_space=pltpu.SMEM)
    grid_spec = pltpu.PrefetchScalarGridSpec(
        num_scalar_prefetch=3,
        grid=(nb,),
        in_specs=[smem_blk(lambda i, be, nv, nu: (i, 0, 0)),
                  smem_blk(lambda i, be, nv, nu: (jnp.minimum(i + 1, last), 0, 0)),
                  smem_blk(lambda i, be, nv, nu: (i, 0, 0)),
                  pl.BlockSpec((1, 1, EXPERT_BLOCK), lambda i, be, nv, nu: (i, 0, 0)),
                  pl.BlockSpec((1, d, de), lambda i, be, nv, nu: (be[i], 0, 0)),
                  pl.BlockSpec((1, d, de), lambda i, be, nv, nu: (be[i], 0, 0)),
                  pl.BlockSpec((1, de, d), lambda i, be, nv, nu: (be[i], 0, 0)),
                  pl.BlockSpec(memory_space=pl.ANY)],
        out_specs=pl.BlockSpec(memory_space=pl.ANY),
        scratch_shapes=[pltpu.VMEM((2, EXPERT_BLOCK, nct, LANES), F32), pltpu.VMEM((2, EXPERT_BLOCK, nct, LANES), F32),
                        pltpu.SemaphoreType.DMA((2,)), pltpu.SemaphoreType.DMA((2,))],
    )
    return pl.pallas_call(
        _experts_kernel,
        grid_spec=grid_spec,
        out_shape=jax.ShapeDtypeStruct((n_slots, nct, LANES), F32),
        compiler_params=_cparams("arbitrary"),
        name="experts",
    )(block_e, block_nv, n_used, row_tok, row_tok, row_slot, row_w, wg, wu, wd, hffn)


def _combine_kernel(x1_ref, ys_ref, slots_ref, npost_ref, g5_ref, o_ref):
    nct = slots_ref.shape[2]
    routed = []
    for cidx in range(nct):
        acc = slots_ref[0, :, cidx, :]
        for k in range(1, TOP_K):
            acc = acc + slots_ref[k, :, cidx, :]
        routed.append(acc)
    y = ys_ref[0] + jnp.concatenate(routed, axis=-1)
    o_ref[0] = x1_ref[0] + g5_ref[0] * _rms(y, npost_ref[...])


def _combine(x1, ys, slots, npost, g5, tm):
    b, s, d = x1.shape
    nt = s // tm
    tok = pl.BlockSpec((1, tm, d), lambda bi, i: (bi, i, 0))
    return pl.pallas_call(
        _combine_kernel,
        grid=(b, nt),
        in_specs=[tok, tok, pl.BlockSpec((TOP_K, tm, d // LANES, LANES), lambda bi, i: (0, bi * nt + i, 0, 0)),
                  pl.BlockSpec((1, d), lambda bi, i: (0, 0)), pl.BlockSpec((1, 1, d), lambda bi, i: (bi, 0, 0))],
        out_specs=tok,
        out_shape=jax.ShapeDtypeStruct((b, s, d), F32),
        compiler_params=_cparams("parallel", "arbitrary"),
        name="combine",
    )(x1, ys, slots, npost, g5)


def _dispatch_plan(idx_t, gate_t):
    k, t = idx_t.shape
    n_asg = k * t
    nb = n_asg // EXPERT_BLOCK + N_EXPERTS
    flat_e = idx_t.reshape(-1)
    order = jnp.argsort(flat_e).astype(jnp.int32)
    counts = jnp.zeros((N_EXPERTS,), jnp.int32).at[flat_e].add(1)
    padded = (counts + EXPERT_BLOCK - 1) // EXPERT_BLOCK * EXPERT_BLOCK
    start = jnp.cumsum(counts) - counts
    pend = jnp.cumsum(padded)
    pstart = pend - padded
    blk0 = jnp.arange(nb, dtype=jnp.int32) * EXPERT_BLOCK
    block_e = jnp.minimum(jnp.searchsorted(pend, blk0, side='right'), N_EXPERTS - 1).astype(jnp.int32)
    block_nv = jnp.clip(counts[block_e] - (blk0 - pstart[block_e]), 0, EXPERT_BLOCK).astype(jnp.int32)
    n_used = (pend[-1] // EXPERT_BLOCK).astype(jnp.int32).reshape(1)
    pos = blk0[:, None] - pstart[block_e][:, None] + jnp.arange(EXPERT_BLOCK, dtype=jnp.int32)[None, :]
    valid = pos < counts[block_e][:, None]
    src = jnp.clip(start[block_e][:, None] + pos, 0, n_asg - 1)
    asg = order[src]
    row_tok = jnp.where(valid, asg % t, 0).astype(jnp.int32)
    row_slot = jnp.where(valid, asg, 0).astype(jnp.int32)
    row_w = jnp.where(valid, gate_t.reshape(-1)[asg], 0.0).astype(F32)
    shp = (nb, 1, EXPERT_BLOCK)
    return block_e, block_nv, n_used, row_tok.reshape(shp), row_slot.reshape(shp), row_w.reshape(shp)


def _pack_in_weights(w_in, ml_i_bias, ml_f_bias, gd_dt_bias):
    d = w_in.shape[0]
    nml = 2 * ML_HEADS * ML_QK + 2 * ML_HEADS * ML_V
    ml_cols = nml + 4 * ML_HEADS
    ngq = GD_HEADS * (2 * GD_QK + GD_V)
    ngz = GD_HEADS * GD_V
    wml = w_in[:, :nml].astype(BF16)
    wgq = w_in[:, ml_cols:ml_cols + ngq].astype(BF16)
    wgz = w_in[:, ml_cols + ngq:ml_cols + ngq + ngz].astype(BF16)
    wg = jnp.zeros((d, GATE_LANES), F32)
    wg = wg.at[:, ML_GATE0:ML_GATE0 + 16].set(w_in[:, nml:ml_cols])
    wg = wg.at[:, GD_GATE0:GD_GATE0 + 16].set(w_in[:, ml_cols + ngq + ngz:])
    gb = jnp.zeros((GATE_LANES,), F32)
    gb = gb.at[ML_GATE0:ML_GATE0 + 16].set(jnp.stack([ml_i_bias, ml_f_bias], axis=1).reshape(-1))
    gb = gb.at[GD_GATE0:GD_GATE0 + 16].set(jnp.stack([gd_dt_bias, jnp.zeros_like(gd_dt_bias)], axis=1).reshape(-1))
    return wml, wgq, wgz, wg.astype(BF16), gb.reshape(1, GATE_LANES)


def _mixer(x, ctx, mod, mod_ctx, norm_pre_mix, w_in, ml_i_bias, ml_f_bias, gd_conv_w, gd_a_log, gd_dt_bias):
    b, s, d = x.shape
    sc = ctx.shape[1]
    wml, wgq, wgz, wg, gb = _pack_in_weights(w_in, ml_i_bias, ml_f_bias, gd_dt_bias)
    nw = norm_pre_mix.reshape(1, d)
    ctx_mod = lambda j: jnp.broadcast_to(mod_ctx[j].reshape(1, 1, d), (b, 1, d))
    ml_c, gq_c, _, g_c = _proj(ctx, nw, ctx_mod(0), ctx_mod(1), wml, wgq, wgz, wg, gb, tm=sc)
    ml_l, gq_l, gz_l, g_l = _proj(x, nw, mod[0], mod[1], wml, wgq, wgz, wg, gb, tm=512)

    c0 = jnp.zeros((b, N_CHAINS, ML_QK, ML_V), F32)
    n0 = jnp.zeros((b, N_CHAINS, 1, ML_QK), F32)
    m0 = jnp.zeros((b, N_CHAINS, 1, 1), F32)
    _, _, c1, n1, m1 = _mlstm(ml_c, g_c, c0, n0, m0)
    hf, hb, _, _, _ = _mlstm(ml_l, g_l, c1, n1, m1)

    neg_a = jnp.zeros((GATE_LANES,), F32)
    neg_a = neg_a.at[GD_GATE0:GD_GATE0 + 16].set(
        jnp.stack([-jnp.exp(gd_a_log), jnp.zeros_like(gd_a_log)], axis=1).reshape(-1)).reshape(1, GATE_LANES)
    qn_c = _gdconv_ctx(gq_c, gd_conv_w)
    qn_l = _gdconv_lat(gq_l, gd_conv_w)
    s0 = jnp.zeros((b, N_CHAINS, GD_QK, GD_V), F32)
    hdim = GD_HEADS * GD_V
    _, _, s1 = _gdn(qn_c, g_c, neg_a, s0, sc // CHUNK, lambda bi, n: (bi, n, 0), (b, sc, hdim))
    rows = s // GRID_W
    cpc = rows // CHUNK
    col_idx = lambda bi, n: (bi, n % cpc, n // cpc)
    nq = qn_l.shape[2]
    of, ob, _ = _gdn(qn_l.reshape(b, rows, GRID_W * nq), g_l.reshape(b, rows, GRID_W * GATE_LANES), neg_a, s1,
                     s // CHUNK, col_idx, (b, rows, GRID_W * hdim))
    return hf, hb, ml_l, of.reshape(b, s, hdim), ob.reshape(b, s, hdim), gz_l


def kernel(x, c, ctx, c_ctx, w_ada, b_ada, norm_pre_mix, norm_post_mix, norm_pre_ffn, norm_post_ffn, w_in,
           ml_i_bias, ml_f_bias, ml_norm_w, gd_conv_w, gd_a_log, gd_dt_bias, gd_norm_w, w_out, router_w,
           router_bias, w_gate, w_up, w_down, ws_gate, ws_up, ws_down):
    b, s, d = x.shape
    depth = w_ada.shape[0]
    assert depth == 1, "the context stream update of deeper stacks is not implemented"
    ly = 0
    cc = jnp.zeros((16, d), F32).at[:b].set(c).at[b].set(c_ctx)
    mod_all = _ada(cc, w_ada[ly], b_ada[ly])
    mod = [mod_all[:b, j * d:(j + 1) * d].reshape(b, 1, d) for j in range(6)]
    mod_ctx = [mod_all[b, j * d:(j + 1) * d] for j in range(6)]

    hf, hb, ml_l, of, ob, gz_l = _mixer(x, ctx, mod, mod_ctx, norm_pre_mix[ly], w_in[ly], ml_i_bias[ly],
                                        ml_f_bias[ly], gd_conv_w[ly], gd_a_log[ly], gd_dt_bias[ly])

    row = lambda v: v.reshape(1, -1)
    x1, hffn, logits_t, ys = _post(
        x, hf, hb, ml_l, of, ob, gz_l, row(ml_norm_w[ly]), row(jnp.tile(gd_norm_w[ly], GD_HEADS)),
        w_out[ly].astype(BF16), row(norm_post_mix[ly]), mod[2], row(norm_pre_ffn[ly]), mod[3], mod[4],
        router_w[ly].T.astype(BF16), ws_gate[ly].astype(BF16), ws_up[ly].astype(BF16), ws_down[ly].astype(BF16),
        tm=256)

    idx_t, gate_t = _route(logits_t, router_bias[ly].reshape(-1, 1), tn=512)
    block_e, block_nv, n_used, row_tok, row_slot, row_w = _dispatch_plan(idx_t, gate_t)
    t = b * s
    slots = _experts(hffn.reshape(t, d // LANES, LANES), block_e, block_nv, n_used, row_tok, row_slot, row_w,
                     w_gate[ly].astype(BF16), w_up[ly].astype(BF16), w_down[ly].astype(BF16), TOP_K * t)
    return _combine(x1, ys, slots.reshape(TOP_K, t, d // LANES, LANES), row(norm_post_ffn[ly]), mod[5], tm=256)
```

```python
import functools

import jax
import jax.numpy as jnp
from jax import lax
from jax.experimental import pallas as pl
from jax.experimental.pallas import tpu as pltpu

EPS = 1e-6
CHUNK = 64
GRID_W = 64
ML_HEADS, ML_QK, ML_V = 4, 64, 128
GD_HEADS, GD_QK, GD_V = 4, 128, 128
CONV_W = 5
N_EXPERTS, TOP_K, N_GROUPS, TOPK_GROUPS = 256, 8, 8, 4
ROUTED_SCALE = 2.5
EXPERT_BLOCK = 128
N_CHAINS = 8
LANES = 128
GATE_LANES = LANES
ML_GATE0, GD_GATE0 = 0, 16

F32 = jnp.float32
BF16 = jnp.bfloat16
HI = lax.Precision.HIGHEST
VMEM_LIMIT = 56 * 1024 * 1024


def _cparams(*sem):
    return pltpu.CompilerParams(dimension_semantics=sem, vmem_limit_bytes=VMEM_LIMIT)


def _dot(a, b):
    return jnp.dot(a.astype(BF16), b.astype(BF16), preferred_element_type=F32)


def _dot_nt(a, b):
    return lax.dot_general(a.astype(BF16), b.astype(BF16), (((1,), (1,)), ((), ())), preferred_element_type=F32)


def _dot_tn(a, b):
    return lax.dot_general(a.astype(BF16), b.astype(BF16), (((0,), (0,)), ((), ())), preferred_element_type=F32)


def _dot_hi(a, b):
    return jnp.dot(a, b, precision=HI, preferred_element_type=F32)


def _dot_nt_hi(a, b):
    return lax.dot_general(a, b, (((1,), (1,)), ((), ())), precision=HI, preferred_element_type=F32)


def _transpose_hi(x):
    n = x.shape[1]
    eye = (lax.broadcasted_iota(jnp.int32, (n, n), 0) == lax.broadcasted_iota(jnp.int32, (n, n), 1)).astype(F32)
    return _dot_nt_hi(eye, x)


def _silu(x):
    return x * jax.nn.sigmoid(x)


def _past_mask(reverse):
    t = lax.broadcasted_iota(jnp.int32, (CHUNK, CHUNK), 0)
    s = lax.broadcasted_iota(jnp.int32, (CHUNK, CHUNK), 1)
    return (s >= t, s > t) if reverse else (s <= t, s < t)


def _ada_kernel(c_ref, w_ref, b_ref, o_ref):
    o_ref[...] = _dot(_silu(c_ref[...]), w_ref[...]) + b_ref[...]


def _ada(cc, w_ada, b_ada):
    rows, d = cc.shape
    n = w_ada.shape[1]
    tn = 1536
    return pl.pallas_call(
        _ada_kernel,
        grid=(n // tn,),
        in_specs=[pl.BlockSpec((rows, d), lambda j: (0, 0)),
                  pl.BlockSpec((d, tn), lambda j: (0, j)),
                  pl.BlockSpec((1, tn), lambda j: (0, j))],
        out_specs=pl.BlockSpec((rows, tn), lambda j: (0, j)),
        out_shape=jax.ShapeDtypeStruct((rows, n), F32),
        compiler_params=_cparams("arbitrary"),
        name="ada",
    )(cc, w_ada, b_ada.reshape(1, n))


GRID_PITCH = GRID_W + 8


def _to_grid_view(src_ref, dst_ref):
    ng = src_ref.shape[0]
    r = src_ref.shape[1] // GRID_PITCH
    for c in range(GRID_W):
        for g in range(ng):
            lo = (c * ng + g) * LANES
            dst_ref[0, :, lo:lo + LANES] = src_ref[g, pl.ds(c, r, stride=GRID_PITCH), :]


def _from_grid_view(src_ref, dst_ref):
    ng = dst_ref.shape[0]
    r = dst_ref.shape[1] // GRID_PITCH
    for c in range(GRID_W):
        for g in range(ng):
            lo = (c * ng + g) * LANES
            dst_ref[g, pl.ds(c, r, stride=GRID_PITCH), :] = src_ref[0, :, lo:lo + LANES]


def _pitched_rows(ref, g):
    r = ref.shape[1] // GRID_PITCH
    return jnp.concatenate([ref[g, i * GRID_PITCH:i * GRID_PITCH + GRID_W, :] for i in range(r)], axis=0)


def _proj_kernel(grid_view, x_ref, nw_ref, sh_ref, sc_ref, wml_ref, wgq_ref, wgz_ref, wg_ref, gb_ref, *refs):
    if grid_view:
        ml_ref, gqv_ref, gz_ref, g_ref, gv_ref, gq_scr, g_scr = refs
    else:
        ml_ref, gq_ref, gz_ref, g_ref = refs
    x = x_ref[0]
    xn = x * lax.rsqrt(jnp.mean(x * x, axis=-1, keepdims=True) + EPS) * nw_ref[...]
    h = (xn * (1.0 + sc_ref[0]) + sh_ref[0]).astype(BF16)
    ml_ref[0] = jnp.dot(h, wml_ref[...], preferred_element_type=F32)
    gz_ref[0] = jnp.dot(h, wgz_ref[...], preferred_element_type=F32)
    gates = jnp.dot(h, wg_ref[...], preferred_element_type=F32) + gb_ref[...]
    g_ref[0] = gates
    gq = jnp.dot(h, wgq_ref[...], preferred_element_type=F32)
    if grid_view:
        for r in range(x.shape[0] // GRID_W):
            rows = slice(r * GRID_W, (r + 1) * GRID_W)
            prow = slice(r * GRID_PITCH, r * GRID_PITCH + GRID_W)
            g_scr[0, prow, :] = gates[rows]
            for g in range(gq_scr.shape[0]):
                gq_scr[g, prow, :] = gq[rows, g * LANES:(g + 1) * LANES]
        _to_grid_view(gq_scr, gqv_ref)
        _to_grid_view(g_scr, gv_ref)
    else:
        gq_ref[0] = gq


def _proj(x, norm_w, shift, scale, wml, wgq, wgz, wg, gbias, tm, grid_view):
    b, s, d = x.shape
    nml, ngq, ngz = wml.shape[1], wgq.shape[1], wgz.shape[1]
    full = lambda shp: pl.BlockSpec(shp, lambda bi, i: (0,) * len(shp))
    tok = lambda n: pl.BlockSpec((1, tm, n), lambda bi, i: (bi, i, 0))
    mod = pl.BlockSpec((1, 1, d), lambda bi, i: (bi, 0, 0))
    if grid_view:
        rt = tm // GRID_W
        view = lambda n: pl.BlockSpec((1, rt, GRID_W * n), lambda bi, i: (bi, i, 0))
        vshape = lambda n: jax.ShapeDtypeStruct((b, s // GRID_W, GRID_W * n), F32)
        out_specs = [tok(nml), view(ngq), tok(ngz), tok(GATE_LANES), view(GATE_LANES)]
        out_shape = [jax.ShapeDtypeStruct((b, s, nml), F32), vshape(ngq), jax.ShapeDtypeStruct((b, s, ngz), F32),
                     jax.ShapeDtypeStruct((b, s, GATE_LANES), F32), vshape(GATE_LANES)]
        scratch = [pltpu.VMEM((ngq // LANES, rt * GRID_PITCH, LANES), F32), pltpu.VMEM((1, rt * GRID_PITCH, LANES), F32)]
    else:
        out_specs = [tok(nml), tok(ngq), tok(ngz), tok(GATE_LANES)]
        out_shape = [jax.ShapeDtypeStruct((b, s, n), F32) for n in (nml, ngq, ngz, GATE_LANES)]
        scratch = []
    return pl.pallas_call(
        functools.partial(_proj_kernel, grid_view),
        grid=(b, s // tm),
        in_specs=[tok(d), full((1, d)), mod, mod, full((d, nml)), full((d, ngq)), full((d, ngz)),
                  full((d, GATE_LANES)), full((1, GATE_LANES))],
        out_specs=out_specs,
        out_shape=out_shape,
        scratch_shapes=scratch,
        compiler_params=_cparams("parallel", "arbitrary"),
        name="proj",
    )(x, norm_w, shift, scale, wml, wgq, wgz, wg, gbias)


def _mlstm_kernel(mlf_ref, mlb_ref, gf_ref, gb_ref, c0_ref, n0_ref, m0_ref,
                  hf_ref, hb_ref, cn_ref, nn_ref, mn_ref, c_scr, n_scr, m_scr):
    i = pl.program_id(1)

    @pl.when(i == 0)
    def _():
        c_scr[...] = c0_ref[0]
        n_scr[...] = n0_ref[0]
        m_scr[...] = m0_ref[0]

    past = [_past_mask(d == 1)[0] for d in range(2)]
    g = [r[0] for r in (gf_ref, gb_ref)]
    ls = [jax.nn.log_sigmoid(x) for x in g]
    bcol = [_dot_hi(past[d].astype(F32), ls[d]) for d in range(2)]
    tot = [jnp.sum(x, axis=0, keepdims=True) for x in ls]
    g_t = [_transpose_hi(x) for x in g]
    b_t = [_transpose_hi(x) for x in bcol]

    chains = [(d, hd) for d in range(2) for hd in range(ML_HEADS)]
    nc = range(len(chains))
    ml_refs, h_refs = (mlf_ref, mlb_ref), (hf_ref, hb_ref)
    k0, v0 = ML_HEADS * ML_QK, 2 * ML_HEADS * ML_QK
    q, k, v, i_col, b_col, b_end, log_d = [], [], [], [], [], [], []
    for d, hd in chains:
        ci = ML_GATE0 + d * 8 + hd
        cf = ci + ML_HEADS
        q.append(ml_refs[d][0, :, hd * ML_QK:(hd + 1) * ML_QK])
        k.append(ml_refs[d][0, :, k0 + hd * ML_QK:k0 + (hd + 1) * ML_QK] * (ML_QK ** -0.5))
        v.append(ml_refs[d][0, :, v0 + hd * ML_V:v0 + (hd + 1) * ML_V])
        i_col.append(g[d][:, ci:ci + 1])
        b_col.append(bcol[d][:, cf:cf + 1])
        b_end.append(tot[d][:, cf:cf + 1])
        log_d.append(jnp.where(past[d], b_col[-1] - b_t[d][cf:cf + 1, :] + g_t[d][ci:ci + 1, :], -jnp.inf))
    c_st = [c_scr[c] for c in nc]
    n_st = [n_scr[c] for c in nc]
    m_st = [m_scr[c] for c in nc]
    log_prev = [b_col[c] + m_st[c] for c in nc]
    m_t = [jnp.maximum(log_prev[c], jnp.max(log_d[c], axis=-1, keepdims=True)) for c in nc]
    qk = [_dot_nt(q[c], k[c]) for c in nc]
    qc = [_dot(q[c], c_st[c]) for c in nc]
    s = [qk[c] * jnp.exp(log_d[c] - m_t[c]) for c in nc]
    w_prev = [jnp.exp(log_prev[c] - m_t[c]) for c in nc]
    sv = [_dot(s[c], v[c]) for c in nc]
    log_s = [b_end[c] - b_col[c] + i_col[c] for c in nc]
    m_new = [jnp.maximum(b_end[c] + m_st[c], jnp.max(log_s[c], axis=0, keepdims=True)) for c in nc]
    kw = [k[c] * jnp.exp(log_s[c] - m_new[c]) for c in nc]
    w_c = [jnp.exp(b_end[c] + m_st[c] - m_new[c]) for c in nc]
    kv = [_dot_tn(kw[c], v[c]) for c in nc]
    for c, (d, hd) in enumerate(chains):
        num = sv[c] + w_prev[c] * qc[c]
        den = jnp.sum(s[c], axis=-1, keepdims=True) + w_prev[c] * jnp.sum(q[c] * n_st[c], axis=-1, keepdims=True)
        h_refs[d][0, :, hd * ML_V:(hd + 1) * ML_V] = num / jnp.maximum(jnp.abs(den), jnp.exp(-m_t[c]))
        c_scr[c] = w_c[c] * c_st[c] + kv[c]
        n_scr[c] = w_c[c] * n_st[c] + jnp.sum(kw[c], axis=0, keepdims=True)
        m_scr[c] = m_new[c]

    @pl.when(i == pl.num_programs(1) - 1)
    def _():
        cn_ref[0] = c_scr[...]
        nn_ref[0] = n_scr[...]
        mn_ref[0] = m_scr[...]


def _mlstm(ml, gates, c0, n0, m0):
    b, s, nml = ml.shape
    nc = s // CHUNK
    fwd = lambda n: pl.BlockSpec((1, CHUNK, n), lambda bi, i: (bi, i, 0))
    bwd = lambda n: pl.BlockSpec((1, CHUNK, n), lambda bi, i: (bi, nc - 1 - i, 0))
    st = lambda shp: pl.BlockSpec((1,) + shp, lambda bi, i: (bi,) + (0,) * len(shp))
    hdim = ML_HEADS * ML_V
    return pl.pallas_call(
        _mlstm_kernel,
        grid=(b, nc),
        in_specs=[fwd(nml), bwd(nml), fwd(GATE_LANES), bwd(GATE_LANES),
                  st((N_CHAINS, ML_QK, ML_V)), st((N_CHAINS, 1, ML_QK)), st((N_CHAINS, 1, 1))],
        out_specs=[fwd(hdim), bwd(hdim),
                   st((N_CHAINS, ML_QK, ML_V)), st((N_CHAINS, 1, ML_QK)), st((N_CHAINS, 1, 1))],
        out_shape=[jax.ShapeDtypeStruct((b, s, hdim), F32), jax.ShapeDtypeStruct((b, s, hdim), F32),
                   jax.ShapeDtypeStruct(c0.shape, F32), jax.ShapeDtypeStruct(n0.shape, F32),
                   jax.ShapeDtypeStruct(m0.shape, F32)],
        scratch_shapes=[pltpu.VMEM((N_CHAINS, ML_QK, ML_V), F32), pltpu.VMEM((N_CHAINS, 1, ML_QK), F32),
                        pltpu.VMEM((N_CHAINS, 1, 1), F32)],
        compiler_params=_cparams("parallel", "arbitrary"),
        name="mlstm",
    )(ml, ml, gates, gates, c0, n0, m0)


def _gdconv_kernel(has_halo, *refs):
    if has_halo:
        x_ref, prev_ref, next_ref, w_ref, o_ref, xp_ref = refs
    else:
        x_ref, w_ref, o_ref, xp_ref = refs
    rows = x_ref.shape[1]
    nch = x_ref.shape[2]
    pad = 8
    zero = jnp.zeros((pad, nch), F32)
    if has_halo:
        c = pl.program_id(1)
        xp_ref[0:pad, :] = jnp.where(c > 0, prev_ref[0], zero)
        xp_ref[pad + rows:, :] = jnp.where(c < pl.num_programs(1) - 1, next_ref[0], zero)
    else:
        xp_ref[0:pad, :] = zero
        xp_ref[pad + rows:, :] = zero
    xp_ref[pad:pad + rows, :] = x_ref[0]
    half = CONV_W // 2
    for lc in range(nch // 128):
        sl = slice(lc * 128, (lc + 1) * 128)
        acc = None
        for j in range(CONV_W):
            term = xp_ref[pad - half + j:pad - half + j + rows, sl] * w_ref[j:j + 1, sl]
            acc = term if acc is None else acc + term
        y = _silu(acc)
        if lc < 2 * GD_HEADS:
            y = y * lax.rsqrt(jnp.sum(y * y, axis=-1, keepdims=True) + EPS)
        if lc < GD_HEADS:
            y = y * (GD_QK ** -0.5)
        o_ref[0, :, sl] = y


def _gdconv_ctx(qkv, conv_w):
    b, s, nch = qkv.shape
    return pl.pallas_call(
        functools.partial(_gdconv_kernel, False),
        grid=(b,),
        in_specs=[pl.BlockSpec((1, s, nch), lambda bi: (bi, 0, 0)), pl.BlockSpec((CONV_W, nch), lambda bi: (0, 0))],
        out_specs=pl.BlockSpec((1, s, nch), lambda bi: (bi, 0, 0)),
        out_shape=jax.ShapeDtypeStruct((b, s, nch), F32),
        scratch_shapes=[pltpu.VMEM((s + 16, nch), F32)],
        compiler_params=_cparams("parallel"),
        name="gdconv_ctx",
    )(qkv, conv_w)


def _gdconv_lat(view, conv_w):
    b, rows, wn = view.shape
    nch = wn // GRID_W
    rb = rows // 8
    return pl.pallas_call(
        functools.partial(_gdconv_kernel, True),
        grid=(b, GRID_W),
        in_specs=[pl.BlockSpec((1, rows, nch), lambda bi, c: (bi, 0, c)),
                  pl.BlockSpec((1, 8, nch), lambda bi, c: (bi, rb - 1, jnp.maximum(c - 1, 0))),
                  pl.BlockSpec((1, 8, nch), lambda bi, c: (bi, 0, jnp.minimum(c + 1, GRID_W - 1))),
                  pl.BlockSpec((CONV_W, nch), lambda bi, c: (0, 0))],
        out_specs=pl.BlockSpec((1, rows, nch), lambda bi, c: (bi, 0, c)),
        out_shape=jax.ShapeDtypeStruct(view.shape, F32),
        scratch_shapes=[pltpu.VMEM((rows + 16, nch), F32)],
        compiler_params=_cparams("parallel", "arbitrary"),
        name="gdconv_lat",
    )(view, view, view, conv_w)


def _unit_lower_inverse(n_strict, eye):
    x = n_strict
    p = eye - n_strict
    for _ in range(5):
        x = _dot_hi(x, x)
        p = p + _dot_hi(p, x)
    return p


def _gdn_kernel(qf_ref, qb_ref, gf_ref, gb_ref, na_ref, s0_ref, of_ref, ob_ref, sn_ref, s_scr):
    i = pl.program_id(1)

    @pl.when(i == 0)
    def _():
        s_scr[...] = s0_ref[0]

    eye = (lax.broadcasted_iota(jnp.int32, (CHUNK, CHUNK), 0)
           == lax.broadcasted_iota(jnp.int32, (CHUNK, CHUNK), 1)).astype(F32)
    nqk = GD_HEADS * GD_QK
    masks = [_past_mask(d == 1) for d in range(2)]
    gates = [r[0] for r in (gf_ref, gb_ref)]
    glog = [na_ref[...] * jax.nn.softplus(g) for g in gates]
    beta_all = [jax.nn.sigmoid(g) for g in gates]
    gcum = [_dot_hi(masks[d][0].astype(F32), glog[d]) for d in range(2)]
    gtot = [jnp.sum(g, axis=0, keepdims=True) for g in glog]
    gcum_t = [_transpose_hi(g) for g in gcum]

    chains = [(d, hd) for d in range(2) for hd in range(GD_HEADS)]
    x_refs, o_refs = (qf_ref, qb_ref), (of_ref, ob_ref)
    q, k, v, g_col, beta, g_end, decay = [], [], [], [], [], [], []
    for d, hd in chains:
        ca = GD_GATE0 + d * 8 + hd
        q.append(x_refs[d][0, :, hd * GD_QK:(hd + 1) * GD_QK])
        k.append(x_refs[d][0, :, nqk + hd * GD_QK:nqk + (hd + 1) * GD_QK])
        v.append(x_refs[d][0, :, 2 * nqk + hd * GD_V:2 * nqk + (hd + 1) * GD_V])
        g_col.append(gcum[d][:, ca:ca + 1])
        beta.append(beta_all[d][:, ca + GD_HEADS:ca + GD_HEADS + 1])
        g_end.append(gtot[d][:, ca:ca + 1])
        decay.append(jnp.exp(jnp.where(masks[d][0], g_col[-1] - gcum_t[d][ca:ca + 1, :], -jnp.inf)))
    nc = range(len(chains))
    kk = [_dot_nt(k[c], k[c]) for c in nc]
    xs = [jnp.where(masks[chains[c][0]][1], beta[c] * kk[c] * decay[c], 0.0) for c in nc]
    ps = [eye - x for x in xs]
    for _ in range(5):
        xs = [_dot_hi(x, x) for x in xs]
        ps = [p + _dot_hi(p, x) for p, x in zip(ps, xs)]
    uw = [_dot_hi(ps[c], jnp.concatenate([v[c] * beta[c], k[c] * (beta[c] * jnp.exp(g_col[c]))], axis=-1)) for c in nc]
    qk = [_dot_nt(q[c], k[c]) * decay[c] for c in nc]
    s_st = [s_scr[c] for c in nc]
    v_new = [uw[c][:, :GD_V] - _dot(uw[c][:, GD_V:], s_st[c]) for c in nc]
    o_loc = [_dot(q[c] * jnp.exp(g_col[c]), s_st[c]) for c in nc]
    o_new = [o_loc[c] + _dot(qk[c], v_new[c]) for c in nc]
    s_new = [s_st[c] * jnp.exp(g_end[c]) + _dot_tn(k[c] * jnp.exp(g_end[c] - g_col[c]), v_new[c]) for c in nc]
    for c, (d, hd) in enumerate(chains):
        o_refs[d][0, :, hd * GD_V:(hd + 1) * GD_V] = o_new[c]
        s_scr[c] = s_new[c]

    @pl.when(i == pl.num_programs(1) - 1)
    def _():
        sn_ref[0] = s_scr[...]


def _gdn(qkv_view, gates_view, neg_a, s0, nc, idx_fn, out_view_shape):
    b = qkv_view.shape[0]
    nqkv = 2 * GD_HEADS * GD_QK + GD_HEADS * GD_V
    hdim = GD_HEADS * GD_V
    fwd = lambda n: pl.BlockSpec((1, CHUNK, n), lambda bi, i: idx_fn(bi, i))
    bwd = lambda n: pl.BlockSpec((1, CHUNK, n), lambda bi, i: idx_fn(bi, nc - 1 - i))
    st = pl.BlockSpec((1, N_CHAINS, GD_QK, GD_V), lambda bi, i: (bi, 0, 0, 0))
    return pl.pallas_call(
        _gdn_kernel,
        grid=(b, nc),
        in_specs=[fwd(nqkv), bwd(nqkv), fwd(GATE_LANES), bwd(GATE_LANES),
                  pl.BlockSpec((1, GATE_LANES), lambda bi, i: (0, 0)), st],
        out_specs=[fwd(hdim), bwd(hdim), st],
        out_shape=[jax.ShapeDtypeStruct(out_view_shape, F32), jax.ShapeDtypeStruct(out_view_shape, F32),
                   jax.ShapeDtypeStruct(s0.shape, F32)],
        scratch_shapes=[pltpu.VMEM((N_CHAINS, GD_QK, GD_V), F32)],
        compiler_params=_cparams("parallel", "arbitrary"),
        name="gdn",
    )(qkv_view, qkv_view, gates_view, gates_view, neg_a, s0)


def _head_rms(t, nheads, width):
    outs = []
    for hd in range(nheads):
        th = t[:, hd * width:(hd + 1) * width]
        outs.append(th * lax.rsqrt(jnp.mean(th * th, axis=-1, keepdims=True) + EPS))
    return jnp.concatenate(outs, axis=-1)


def _rms(t, w):
    return t * lax.rsqrt(jnp.mean(t * t, axis=-1, keepdims=True) + EPS) * w


def _post_kernel(x_ref, hf_ref, hb_ref, og_ref, of_ref, ob_ref, z_ref, mlw_ref, gdw_ref, wout_ref,
                 npost_ref, g2_ref, npre_ref, sh_ref, sc_ref, rwt_ref, wsg_ref, wsu_ref, wsd_ref,
                 x1_ref, hffn_ref, lt_ref, ys_ref, of_scr, ob_scr):
    ml_y = _head_rms(hf_ref[0] + hb_ref[0], ML_HEADS, ML_V) * mlw_ref[...] * jax.nn.sigmoid(og_ref[0])
    _from_grid_view(of_ref, of_scr)
    _from_grid_view(ob_ref, ob_scr)
    o_sum = jnp.concatenate([_pitched_rows(of_scr, g) + _pitched_rows(ob_scr, g) for g in range(of_scr.shape[0])],
                            axis=-1)
    gd_y = _head_rms(o_sum, GD_HEADS, GD_V) * gdw_ref[...] * _silu(z_ref[0])
    y = _dot(jnp.concatenate([ml_y, gd_y], axis=-1), wout_ref[...])
    x1 = x_ref[0] + g2_ref[0] * _rms(y, npost_ref[...])
    x1_ref[0] = x1
    hffn = _rms(x1, npre_ref[...]) * (1.0 + sc_ref[0]) + sh_ref[0]
    nct = hffn.shape[1] // LANES
    for c in range(nct):
        hffn_ref[0, pl.ds(c, hffn.shape[0], stride=nct), :] = hffn[:, c * LANES:(c + 1) * LANES]
    hb = hffn.astype(BF16)
    lt_ref[...] = lax.dot_general(rwt_ref[...], hb, (((1,), (1,)), ((), ())), preferred_element_type=F32)
    hs = _silu(jnp.dot(hb, wsg_ref[...], preferred_element_type=F32)) * jnp.dot(hb, wsu_ref[...],
                                                                                preferred_element_type=F32)
    ys_ref[0] = _dot(hs, wsd_ref[...])


def _post(x, hf, hb, ml, of, ob, gz, mlw, gdw, wout, npost, g2, npre, sh, sc, rwt, wsg, wsu, wsd, tm):
    b, s, d = x.shape
    nt = s // tm
    hw = ML_HEADS * ML_V
    og_blk = (2 * ML_HEADS * ML_QK + ML_HEADS * ML_V) // hw
    tok = lambda n: pl.BlockSpec((1, tm, n), lambda bi, i: (bi, i, 0))
    full = lambda shp: pl.BlockSpec(shp, lambda bi, i: (0,) * len(shp))
    mod = pl.BlockSpec((1, 1, d), lambda bi, i: (bi, 0, 0))
    ne = rwt.shape[0]
    ds = wsg.shape[1]
    gview = pl.BlockSpec((1, tm // GRID_W, GRID_W * hw), lambda bi, i: (bi, i, 0))
    return pl.pallas_call(
        _post_kernel,
        grid=(b, nt),
        in_specs=[tok(d), tok(hw), tok(hw), pl.BlockSpec((1, tm, hw), lambda bi, i: (bi, i, og_blk)),
                  gview, gview, tok(hw), full((1, hw)), full((1, hw)), full((d, d)),
                  full((1, d)), mod, full((1, d)), mod, mod, full((ne, d)), full((d, ds)), full((d, ds)),
                  full((ds, d))],
        out_specs=[tok(d), pl.BlockSpec((1, tm * (d // LANES), LANES), lambda bi, i: (bi, i, 0)),
                   pl.BlockSpec((ne, tm), lambda bi, i: (0, bi * nt + i)), tok(d)],
        out_shape=[jax.ShapeDtypeStruct((b, s, d), F32), jax.ShapeDtypeStruct((b, s * (d // LANES), LANES), F32),
                   jax.ShapeDtypeStruct((ne, b * s), F32), jax.ShapeDtypeStruct((b, s, d), F32)],
        scratch_shapes=[pltpu.VMEM((hw // LANES, tm // GRID_W * GRID_PITCH, LANES), F32)] * 2,
        compiler_params=_cparams("parallel", "arbitrary"),
        name="post",
    )(x, hf, hb, ml, of, ob, gz, mlw, gdw, wout, npost, g2, npre, sh, sc, rwt, wsg, wsu, wsd)


def _route_kernel(lt_ref, bias_ref, idx_ref, gate_ref):
    ne, tn = lt_ref.shape
    gsz = ne // N_GROUPS
    scores = jax.nn.sigmoid(lt_ref[...])
    sel = scores + bias_ref[...]
    neg = -jnp.inf
    sel3 = sel.reshape(N_GROUPS, gsz, tn)
    io3 = lax.broadcasted_iota(jnp.int32, sel3.shape, 1)
    top1 = jnp.max(sel3, axis=1, keepdims=True)
    first = jnp.min(jnp.where(sel3 == top1, io3, gsz), axis=1, keepdims=True)
    top2 = jnp.max(jnp.where(io3 == first, neg, sel3), axis=1, keepdims=True)
    grp = (top1 + top2).reshape(N_GROUPS, tn)
    iog = lax.broadcasted_iota(jnp.int32, grp.shape, 0)
    keep = jnp.zeros(grp.shape, jnp.bool_)
    for _ in range(TOPK_GROUPS):
        m = jnp.max(grp, axis=0, keepdims=True)
        pick = iog == jnp.min(jnp.where(grp == m, iog, N_GROUPS), axis=0, keepdims=True)
        keep = keep | pick
        grp = jnp.where(pick, neg, grp)
    cand = jnp.where(keep.reshape(N_GROUPS, 1, tn), sel3, neg).reshape(ne, tn)
    ioe = lax.broadcasted_iota(jnp.int32, cand.shape, 0)
    idxs, gates = [], []
    for _ in range(TOP_K):
        m = jnp.max(cand, axis=0, keepdims=True)
        e = jnp.min(jnp.where(cand == m, ioe, ne), axis=0, keepdims=True)
        pick = ioe == e
        idxs.append(e)
        gates.append(jnp.sum(jnp.where(pick, scores, 0.0), axis=0, keepdims=True))
        cand = jnp.where(pick, neg, cand)
    gate = jnp.concatenate(gates, axis=0)
    idx_ref[...] = jnp.concatenate(idxs, axis=0)
    gate_ref[...] = gate / jnp.sum(gate, axis=0, keepdims=True) * ROUTED_SCALE


def _route(logits_t, bias_col, tn):
    ne, t = logits_t.shape
    return pl.pallas_call(
        _route_kernel,
        grid=(t // tn,),
        in_specs=[pl.BlockSpec((ne, tn), lambda i: (0, i)), pl.BlockSpec((ne, 1), lambda i: (0, 0))],
        out_specs=[pl.BlockSpec((TOP_K, tn), lambda i: (0, i)), pl.BlockSpec((TOP_K, tn), lambda i: (0, i))],
        out_shape=[jax.ShapeDtypeStruct((TOP_K, t), jnp.int32), jax.ShapeDtypeStruct((TOP_K, t), F32)],
        compiler_params=_cparams("parallel"),
        name="route",
    )(logits_t, bias_col)


def _experts_kernel(be_ref, nu_ref,
                    tok_ref, tokn_ref, slot_ref, w_ref, wg_ref, wu_ref, wd_ref, h_hbm,
                    slots_hbm, xg, ob, wgb, wub, wdb, gsem, ssem):
    i = pl.program_id(0)
    n_used = nu_ref[0]
    cur = i % 2
    nxt = 1 - cur
    nct = xg.shape[1] // EXPERT_BLOCK
    rows = nct * EXPERT_BLOCK

    def gather_copy(tref, j, buf):
        src = h_hbm.at[pl.ds(pl.multiple_of(tref[0, 0, j] * nct, nct), nct)]
        return pltpu.make_async_copy(src, xg.at[buf, pl.ds(j * nct, nct)], gsem.at[buf])

    def scatter_copy(j, buf):
        dst = slots_hbm.at[pl.ds(pl.multiple_of(slot_ref[0, 0, j] * nct, nct), nct)]
        return pltpu.make_async_copy(ob.at[buf, pl.ds(j * nct, nct)], dst, ssem.at[buf])

    def gather_wait(buf):
        pltpu.make_async_copy(h_hbm.at[pl.ds(0, rows)], xg.at[buf], gsem.at[buf]).wait()

    def scatter_wait(buf):
        pltpu.make_async_copy(ob.at[buf], slots_hbm.at[pl.ds(0, rows)], ssem.at[buf]).wait()

    @pl.when(i < n_used)
    def _():
        @pl.when(i == 0)
        def _():
            ob[...] = jnp.zeros(ob.shape, F32)
            spare0 = slots_hbm.shape[0] - ob.shape[0] * rows
            for buf in range(ob.shape[0]):
                init = pltpu.make_async_copy(ob.at[buf], slots_hbm.at[pl.ds(spare0 + buf * rows, rows)], ssem.at[buf])
                init.start()
                init.wait()
            for j in range(EXPERT_BLOCK):
                gather_copy(tok_ref, j, cur).start(priority=j % 2)

        @pl.when(i + 1 < n_used)
        def _():
            for j in range(EXPERT_BLOCK):
                gather_copy(tokn_ref, j, nxt).start(priority=j % 2)

        gather_wait(cur)

        @pl.when(i >= 2)
        def _():
            scatter_wait(cur)

        @pl.when((i == 0) | (be_ref[i] != be_ref[jnp.maximum(i - 1, 0)]))
        def _():
            wgb[...] = wg_ref[0].astype(BF16)
            wub[...] = wu_ref[0].astype(BF16)
            wdb[...] = wd_ref[0].astype(BF16)

        xb = jnp.concatenate([xg[cur, pl.ds(c, EXPERT_BLOCK, stride=nct), :] for c in range(nct)],
                             axis=-1).astype(BF16)
        hmid = _silu(jnp.dot(xb, wgb[...], preferred_element_type=F32)) * jnp.dot(xb, wub[...],
                                                                                  preferred_element_type=F32)
        out = _dot(hmid, wdb[...])
        eye = (lax.broadcasted_iota(jnp.int32, (EXPERT_BLOCK, EXPERT_BLOCK), 0)
               == lax.broadcasted_iota(jnp.int32, (EXPERT_BLOCK, EXPERT_BLOCK), 1))
        w_col = jnp.sum(jnp.where(eye, w_ref[0], 0.0), axis=1, keepdims=True)
        out = out * w_col
        for c in range(nct):
            ob[cur, pl.ds(c, EXPERT_BLOCK, stride=nct), :] = out[:, c * LANES:(c + 1) * LANES]

        for j in range(EXPERT_BLOCK):
            scatter_copy(j, cur).start(priority=j % 2)

        @pl.when(i == n_used - 1)
        def _():
            scatter_wait(cur)

            @pl.when(i >= 1)
            def _():
                scatter_wait(nxt)


def _experts(hffn, block_e, n_used, row_tok, row_slot, row_w, wg, wu, wd, n_slots):
    d = wg.shape[1]
    nct = d // LANES
    nb = block_e.shape[0]
    de = wg.shape[2]
    last = nb - 1
    smem_blk = lambda f: pl.BlockSpec((1, 1, EXPERT_BLOCK), f, memory_space=pltpu.SMEM)
    grid_spec = pltpu.PrefetchScalarGridSpec(
        num_scalar_prefetch=2,
        grid=(nb,),
        in_specs=[smem_blk(lambda i, be, nu: (i, 0, 0)),
                  smem_blk(lambda i, be, nu: (jnp.minimum(i + 1, last), 0, 0)),
                  smem_blk(lambda i, be, nu: (i, 0, 0)),
                  pl.BlockSpec((1, 1, EXPERT_BLOCK), lambda i, be, nu: (i, 0, 0)),
                  pl.BlockSpec((1, d, de), lambda i, be, nu: (be[i], 0, 0)),
                  pl.BlockSpec((1, d, de), lambda i, be, nu: (be[i], 0, 0)),
                  pl.BlockSpec((1, de, d), lambda i, be, nu: (be[i], 0, 0)),
                  pl.BlockSpec(memory_space=pl.ANY)],
        out_specs=pl.BlockSpec(memory_space=pl.ANY),
        scratch_shapes=[pltpu.VMEM((2, EXPERT_BLOCK * nct, LANES), F32), pltpu.VMEM((2, EXPERT_BLOCK * nct, LANES), F32),
                        pltpu.VMEM((d, de), BF16), pltpu.VMEM((d, de), BF16), pltpu.VMEM((de, d), BF16),
                        pltpu.SemaphoreType.DMA((2,)), pltpu.SemaphoreType.DMA((2,))],
    )
    return pl.pallas_call(
        _experts_kernel,
        grid_spec=grid_spec,
        out_shape=jax.ShapeDtypeStruct(((n_slots + 2 * EXPERT_BLOCK) * nct, LANES), F32),
        compiler_params=_cparams("arbitrary"),
        name="experts",
    )(block_e, n_used, row_tok, row_tok, row_slot, row_w, wg, wu, wd, hffn)


def _combine_kernel(x1_ref, ys_ref, *refs):
    slot_refs, (npost_ref, g5_ref, o_ref) = refs[:TOP_K], refs[TOP_K:]
    tm = x1_ref.shape[1]
    nct = slot_refs[0].shape[0] // tm
    routed = []
    for c in range(nct):
        acc = slot_refs[0][pl.ds(c, tm, stride=nct), :]
        for k in range(1, TOP_K):
            acc = acc + slot_refs[k][pl.ds(c, tm, stride=nct), :]
        routed.append(acc)
    y = ys_ref[0] + jnp.concatenate(routed, axis=-1)
    o_ref[0] = x1_ref[0] + g5_ref[0] * _rms(y, npost_ref[...])


def _combine(x1, ys, slots, npost, g5, tm):
    b, s, d = x1.shape
    nt = s // tm
    nct = d // LANES
    tok = pl.BlockSpec((1, tm, d), lambda bi, i: (bi, i, 0))
    slot = lambda k: pl.BlockSpec((tm * nct, LANES), lambda bi, i: (k * b * nt + bi * nt + i, 0))
    return pl.pallas_call(
        _combine_kernel,
        grid=(b, nt),
        in_specs=[tok, tok] + [slot(k) for k in range(TOP_K)]
                 + [pl.BlockSpec((1, d), lambda bi, i: (0, 0)), pl.BlockSpec((1, 1, d), lambda bi, i: (bi, 0, 0))],
        out_specs=tok,
        out_shape=jax.ShapeDtypeStruct((b, s, d), F32),
        compiler_params=_cparams("parallel", "arbitrary"),
        name="combine",
    )(x1, ys, *([slots] * TOP_K), npost, g5)


def _dispatch_plan(idx_t, gate_t):
    k, t = idx_t.shape
    n_asg = k * t
    nb = n_asg // EXPERT_BLOCK + N_EXPERTS
    flat_e = idx_t.reshape(-1)
    order = jnp.argsort(flat_e).astype(jnp.int32)
    counts = jnp.zeros((N_EXPERTS,), jnp.int32).at[flat_e].add(1)
    padded = (counts + EXPERT_BLOCK - 1) // EXPERT_BLOCK * EXPERT_BLOCK
    start = jnp.cumsum(counts) - counts
    pend = jnp.cumsum(padded)
    pstart = pend - padded
    blk0 = jnp.arange(nb, dtype=jnp.int32) * EXPERT_BLOCK
    block_e = jnp.minimum(jnp.searchsorted(pend, blk0, side='right'), N_EXPERTS - 1).astype(jnp.int32)
    n_used = (pend[-1] // EXPERT_BLOCK).astype(jnp.int32).reshape(1)
    pos = blk0[:, None] - pstart[block_e][:, None] + jnp.arange(EXPERT_BLOCK, dtype=jnp.int32)[None, :]
    valid = pos < counts[block_e][:, None]
    src = jnp.clip(start[block_e][:, None] + pos, 0, n_asg - 1)
    asg = order[src]
    row_tok = jnp.where(valid, asg % t, 0).astype(jnp.int32)
    lane = jnp.arange(EXPERT_BLOCK, dtype=jnp.int32)[None, :]
    spare = n_asg + (jnp.arange(nb, dtype=jnp.int32)[:, None] % 2) * EXPERT_BLOCK + lane
    row_slot = jnp.where(valid, asg, spare).astype(jnp.int32)
    row_w = jnp.where(valid, gate_t.reshape(-1)[asg], 0.0).astype(F32)
    shp = (nb, 1, EXPERT_BLOCK)
    return block_e, n_used, row_tok.reshape(shp), row_slot.reshape(shp), row_w.reshape(shp)


def _pack_in_weights(w_in, ml_i_bias, ml_f_bias, gd_dt_bias):
    d = w_in.shape[0]
    nml = 2 * ML_HEADS * ML_QK + 2 * ML_HEADS * ML_V
    ml_cols = nml + 4 * ML_HEADS
    ngq = GD_HEADS * (2 * GD_QK + GD_V)
    ngz = GD_HEADS * GD_V
    wml = w_in[:, :nml].astype(BF16)
    wgq = w_in[:, ml_cols:ml_cols + ngq].astype(BF16)
    wgz = w_in[:, ml_cols + ngq:ml_cols + ngq + ngz].astype(BF16)
    wg = jnp.zeros((d, GATE_LANES), F32)
    wg = wg.at[:, ML_GATE0:ML_GATE0 + 16].set(w_in[:, nml:ml_cols])
    wg = wg.at[:, GD_GATE0:GD_GATE0 + 16].set(w_in[:, ml_cols + ngq + ngz:])
    gb = jnp.zeros((GATE_LANES,), F32)
    gb = gb.at[ML_GATE0:ML_GATE0 + 16].set(jnp.stack([ml_i_bias, ml_f_bias], axis=1).reshape(-1))
    gb = gb.at[GD_GATE0:GD_GATE0 + 16].set(jnp.stack([gd_dt_bias, jnp.zeros_like(gd_dt_bias)], axis=1).reshape(-1))
    return wml, wgq, wgz, wg.astype(BF16), gb.reshape(1, GATE_LANES)


def _mixer(x, ctx, mod, mod_ctx, norm_pre_mix, w_in, ml_i_bias, ml_f_bias, gd_conv_w, gd_a_log, gd_dt_bias):
    b, s, d = x.shape
    sc = ctx.shape[1]
    wml, wgq, wgz, wg, gb = _pack_in_weights(w_in, ml_i_bias, ml_f_bias, gd_dt_bias)
    nw = norm_pre_mix.reshape(1, d)
    ctx_mod = lambda j: jnp.broadcast_to(mod_ctx[j].reshape(1, 1, d), (b, 1, d))
    ml_c, gq_c, _, g_c = _proj(ctx, nw, ctx_mod(0), ctx_mod(1), wml, wgq, wgz, wg, gb, tm=sc, grid_view=False)
    ml_l, gqv_l, gz_l, g_l, gv_l = _proj(x, nw, mod[0], mod[1], wml, wgq, wgz, wg, gb, tm=512, grid_view=True)

    c0 = jnp.zeros((b, N_CHAINS, ML_QK, ML_V), F32)
    n0 = jnp.zeros((b, N_CHAINS, 1, ML_QK), F32)
    m0 = jnp.zeros((b, N_CHAINS, 1, 1), F32)
    _, _, c1, n1, m1 = _mlstm(ml_c, g_c, c0, n0, m0)
    hf, hb, _, _, _ = _mlstm(ml_l, g_l, c1, n1, m1)

    neg_a = jnp.zeros((GATE_LANES,), F32)
    neg_a = neg_a.at[GD_GATE0:GD_GATE0 + 16].set(
        jnp.stack([-jnp.exp(gd_a_log), jnp.zeros_like(gd_a_log)], axis=1).reshape(-1)).reshape(1, GATE_LANES)
    qn_c = _gdconv_ctx(gq_c, gd_conv_w)
    qnv_l = _gdconv_lat(gqv_l, gd_conv_w)
    s0 = jnp.zeros((b, N_CHAINS, GD_QK, GD_V), F32)
    hdim = GD_HEADS * GD_V
    _, _, s1 = _gdn(qn_c, g_c, neg_a, s0, sc // CHUNK, lambda bi, n: (bi, n, 0), (b, sc, hdim))
    rows = s // GRID_W
    cpc = rows // CHUNK
    col_idx = lambda bi, n: (bi, n % cpc, n // cpc)
    ofv, obv, _ = _gdn(qnv_l, gv_l, neg_a, s1, s // CHUNK, col_idx, (b, rows, GRID_W * hdim))
    return hf, hb, ml_l, ofv, obv, gz_l


def kernel(x, c, ctx, c_ctx, w_ada, b_ada, norm_pre_mix, norm_post_mix, norm_pre_ffn, norm_post_ffn, w_in,
           ml_i_bias, ml_f_bias, ml_norm_w, gd_conv_w, gd_a_log, gd_dt_bias, gd_norm_w, w_out, router_w,
           router_bias, w_gate, w_up, w_down, ws_gate, ws_up, ws_down):
    b, s, d = x.shape
    depth = w_ada.shape[0]
    assert depth == 1, "the context stream update of deeper stacks is not implemented"
    ly = 0
    cc = jnp.zeros((16, d), F32).at[:b].set(c).at[b].set(c_ctx)
    mod_all = _ada(cc, w_ada[ly], b_ada[ly])
    mod = [mod_all[:b, j * d:(j + 1) * d].reshape(b, 1, d) for j in range(6)]
    mod_ctx = [mod_all[b, j * d:(j + 1) * d] for j in range(6)]

    hf, hb, ml_l, of, ob, gz_l = _mixer(x, ctx, mod, mod_ctx, norm_pre_mix[ly], w_in[ly], ml_i_bias[ly],
                                        ml_f_bias[ly], gd_conv_w[ly], gd_a_log[ly], gd_dt_bias[ly])

    row = lambda v: v.reshape(1, -1)
    x1, hffn, logits_t, ys = _post(
        x, hf, hb, ml_l, of, ob, gz_l, row(ml_norm_w[ly]), row(jnp.tile(gd_norm_w[ly], GD_HEADS)),
        w_out[ly].astype(BF16), row(norm_post_mix[ly]), mod[2], row(norm_pre_ffn[ly]), mod[3], mod[4],
        router_w[ly].T.astype(BF16), ws_gate[ly].astype(BF16), ws_up[ly].astype(BF16), ws_down[ly].astype(BF16),
        tm=512)

    idx_t, gate_t = _route(logits_t, router_bias[ly].reshape(-1, 1), tn=512)
    block_e, n_used, row_tok, row_slot, row_w = _dispatch_plan(idx_t, gate_t)
    t = b * s
    slots = _experts(hffn.reshape(t * (d // LANES), LANES), block_e, n_used, row_tok, row_slot, row_w,
                     w_gate[ly], w_up[ly], w_down[ly], TOP_K * t)
    return _combine(x1, ys, slots, row(norm_post_ffn[ly]), mod[5], tm=256)
```

```python
import functools

import jax
import jax.numpy as jnp
from jax import lax
from jax.experimental import pallas as pl
from jax.experimental.pallas import tpu as pltpu

EPS = 1e-6
CHUNK = 64
GRID_W = 64
ML_HEADS, ML_QK, ML_V = 4, 64, 128
GD_HEADS, GD_QK, GD_V = 4, 128, 128
CONV_W = 5
N_EXPERTS, TOP_K, N_GROUPS, TOPK_GROUPS = 256, 8, 8, 4
ROUTED_SCALE = 2.5
EXPERT_BLOCK = 128
N_CHAINS = 8
LANES = 128
GATE_LANES = LANES
ML_GATE0, GD_GATE0 = 0, 16

F32 = jnp.float32
BF16 = jnp.bfloat16
HI = lax.Precision.HIGHEST
VMEM_LIMIT = 56 * 1024 * 1024


def _cparams(*sem):
    return pltpu.CompilerParams(dimension_semantics=sem, vmem_limit_bytes=VMEM_LIMIT)


def _dot(a, b):
    return jnp.dot(a.astype(BF16), b.astype(BF16), preferred_element_type=F32)


def _dot_nt(a, b):
    return lax.dot_general(a.astype(BF16), b.astype(BF16), (((1,), (1,)), ((), ())), preferred_element_type=F32)


def _dot_tn(a, b):
    return lax.dot_general(a.astype(BF16), b.astype(BF16), (((0,), (0,)), ((), ())), preferred_element_type=F32)


def _dot_hi(a, b):
    return jnp.dot(a, b, precision=HI, preferred_element_type=F32)


def _dot_nt_hi(a, b):
    return lax.dot_general(a, b, (((1,), (1,)), ((), ())), precision=HI, preferred_element_type=F32)


def _transpose_hi(x):
    n = x.shape[1]
    eye = (lax.broadcasted_iota(jnp.int32, (n, n), 0) == lax.broadcasted_iota(jnp.int32, (n, n), 1)).astype(F32)
    return _dot_nt_hi(eye, x)


def _silu(x):
    return x * jax.nn.sigmoid(x)


def _past_mask(reverse):
    t = lax.broadcasted_iota(jnp.int32, (CHUNK, CHUNK), 0)
    s = lax.broadcasted_iota(jnp.int32, (CHUNK, CHUNK), 1)
    return (s >= t, s > t) if reverse else (s <= t, s < t)


def _ada_kernel(c_ref, w_ref, b_ref, o_ref):
    o_ref[...] = _dot(_silu(c_ref[...]), w_ref[...]) + b_ref[...]


def _ada(cc, w_ada, b_ada):
    rows, d = cc.shape
    n = w_ada.shape[1]
    tn = 1536
    return pl.pallas_call(
        _ada_kernel,
        grid=(n // tn,),
        in_specs=[pl.BlockSpec((rows, d), lambda j: (0, 0)),
                  pl.BlockSpec((d, tn), lambda j: (0, j)),
                  pl.BlockSpec((1, tn), lambda j: (0, j))],
        out_specs=pl.BlockSpec((rows, tn), lambda j: (0, j)),
        out_shape=jax.ShapeDtypeStruct((rows, n), F32),
        compiler_params=_cparams("arbitrary"),
        name="ada",
    )(cc, w_ada, b_ada.reshape(1, n))


GRID_PITCH = GRID_W + 8


def _to_grid_view(src_ref, dst_ref):
    ng = src_ref.shape[0]
    r = src_ref.shape[1] // GRID_PITCH
    for c in range(GRID_W):
        for g in range(ng):
            lo = (c * ng + g) * LANES
            dst_ref[0, :, lo:lo + LANES] = src_ref[g, pl.ds(c, r, stride=GRID_PITCH), :]


def _from_grid_view(src_ref, dst_ref):
    ng = dst_ref.shape[0]
    r = dst_ref.shape[1] // GRID_PITCH
    for c in range(GRID_W):
        for g in range(ng):
            lo = (c * ng + g) * LANES
            dst_ref[g, pl.ds(c, r, stride=GRID_PITCH), :] = src_ref[0, :, lo:lo + LANES]


def _pitched_rows(ref, g):
    r = ref.shape[1] // GRID_PITCH
    return jnp.concatenate([ref[g, i * GRID_PITCH:i * GRID_PITCH + GRID_W, :] for i in range(r)], axis=0)


def _proj_kernel(grid_view, x_ref, nw_ref, sh_ref, sc_ref, wml_ref, wgq_ref, wgz_ref, wg_ref, gb_ref, *refs):
    if grid_view:
        ml_ref, gqv_ref, gz_ref, g_ref, gv_ref, gq_scr, g_scr = refs
    else:
        ml_ref, gq_ref, gz_ref, g_ref = refs
    x = x_ref[0]
    xn = x * lax.rsqrt(jnp.mean(x * x, axis=-1, keepdims=True) + EPS) * nw_ref[...]
    h = (xn * (1.0 + sc_ref[0]) + sh_ref[0]).astype(BF16)
    ml_ref[0] = jnp.dot(h, wml_ref[...], preferred_element_type=F32)
    gz_ref[0] = jnp.dot(h, wgz_ref[...], preferred_element_type=F32)
    gates = jnp.dot(h, wg_ref[...], preferred_element_type=F32) + gb_ref[...]
    g_ref[0] = gates
    gq = jnp.dot(h, wgq_ref[...], preferred_element_type=F32)
    if grid_view:
        for r in range(x.shape[0] // GRID_W):
            rows = slice(r * GRID_W, (r + 1) * GRID_W)
            prow = slice(r * GRID_PITCH, r * GRID_PITCH + GRID_W)
            g_scr[0, prow, :] = gates[rows]
            for g in range(gq_scr.shape[0]):
                gq_scr[g, prow, :] = gq[rows, g * LANES:(g + 1) * LANES]
        _to_grid_view(gq_scr, gqv_ref)
        _to_grid_view(g_scr, gv_ref)
    else:
        gq_ref[0] = gq


def _proj(x, norm_w, shift, scale, wml, wgq, wgz, wg, gbias, tm, grid_view):
    b, s, d = x.shape
    nml, ngq, ngz = wml.shape[1], wgq.shape[1], wgz.shape[1]
    full = lambda shp: pl.BlockSpec(shp, lambda bi, i: (0,) * len(shp))
    tok = lambda n: pl.BlockSpec((1, tm, n), lambda bi, i: (bi, i, 0))
    mod = pl.BlockSpec((1, 1, d), lambda bi, i: (bi, 0, 0))
    if grid_view:
        rt = tm // GRID_W
        view = lambda n: pl.BlockSpec((1, rt, GRID_W * n), lambda bi, i: (bi, i, 0))
        vshape = lambda n: jax.ShapeDtypeStruct((b, s // GRID_W, GRID_W * n), F32)
        out_specs = [tok(nml), view(ngq), tok(ngz), tok(GATE_LANES), view(GATE_LANES)]
        out_shape = [jax.ShapeDtypeStruct((b, s, nml), F32), vshape(ngq), jax.ShapeDtypeStruct((b, s, ngz), F32),
                     jax.ShapeDtypeStruct((b, s, GATE_LANES), F32), vshape(GATE_LANES)]
        scratch = [pltpu.VMEM((ngq // LANES, rt * GRID_PITCH, LANES), F32), pltpu.VMEM((1, rt * GRID_PITCH, LANES), F32)]
    else:
        out_specs = [tok(nml), tok(ngq), tok(ngz), tok(GATE_LANES)]
        out_shape = [jax.ShapeDtypeStruct((b, s, n), F32) for n in (nml, ngq, ngz, GATE_LANES)]
        scratch = []
    return pl.pallas_call(
        functools.partial(_proj_kernel, grid_view),
        grid=(b, s // tm),
        in_specs=[tok(d), full((1, d)), mod, mod, full((d, nml)), full((d, ngq)), full((d, ngz)),
                  full((d, GATE_LANES)), full((1, GATE_LANES))],
        out_specs=out_specs,
        out_shape=out_shape,
        scratch_shapes=scratch,
        compiler_params=_cparams("parallel", "arbitrary"),
        name="proj",
    )(x, norm_w, shift, scale, wml, wgq, wgz, wg, gbias)


def _mlstm_kernel(mlf_ref, mlb_ref, gf_ref, gb_ref, c0_ref, n0_ref, m0_ref,
                  hf_ref, hb_ref, cn_ref, nn_ref, mn_ref, c_scr, n_scr, m_scr):
    i = pl.program_id(1)

    @pl.when(i == 0)
    def _():
        c_scr[...] = c0_ref[0]
        n_scr[...] = n0_ref[0]
        m_scr[...] = m0_ref[0]

    past = [_past_mask(d == 1)[0] for d in range(2)]
    g = [r[0] for r in (gf_ref, gb_ref)]
    ls = [jax.nn.log_sigmoid(x) for x in g]
    bcol = [_dot_hi(past[d].astype(F32), ls[d]) for d in range(2)]
    tot = [jnp.sum(x, axis=0, keepdims=True) for x in ls]
    g_t = [_transpose_hi(x) for x in g]
    b_t = [_transpose_hi(x) for x in bcol]

    chains = [(d, hd) for d in range(2) for hd in range(ML_HEADS)]
    nc = range(len(chains))
    ml_refs, h_refs = (mlf_ref, mlb_ref), (hf_ref, hb_ref)
    k0, v0 = ML_HEADS * ML_QK, 2 * ML_HEADS * ML_QK
    q, k, v, i_col, b_col, b_end, log_d = [], [], [], [], [], [], []
    for d, hd in chains:
        ci = ML_GATE0 + d * 8 + hd
        cf = ci + ML_HEADS
        q.append(ml_refs[d][0, :, hd * ML_QK:(hd + 1) * ML_QK])
        k.append(ml_refs[d][0, :, k0 + hd * ML_QK:k0 + (hd + 1) * ML_QK] * (ML_QK ** -0.5))
        v.append(ml_refs[d][0, :, v0 + hd * ML_V:v0 + (hd + 1) * ML_V])
        i_col.append(g[d][:, ci:ci + 1])
        b_col.append(bcol[d][:, cf:cf + 1])
        b_end.append(tot[d][:, cf:cf + 1])
        log_d.append(jnp.where(past[d], b_col[-1] - b_t[d][cf:cf + 1, :] + g_t[d][ci:ci + 1, :], -jnp.inf))
    c_st = [c_scr[c] for c in nc]
    n_st = [n_scr[c] for c in nc]
    m_st = [m_scr[c] for c in nc]
    log_prev = [b_col[c] + m_st[c] for c in nc]
    m_t = [jnp.maximum(log_prev[c], jnp.max(log_d[c], axis=-1, keepdims=True)) for c in nc]
    qk = [_dot_nt(q[c], k[c]) for c in nc]
    qc = [_dot(q[c], c_st[c]) for c in nc]
    s = [qk[c] * jnp.exp(log_d[c] - m_t[c]) for c in nc]
    w_prev = [jnp.exp(log_prev[c] - m_t[c]) for c in nc]
    sv = [_dot(s[c], v[c]) for c in nc]
    log_s = [b_end[c] - b_col[c] + i_col[c] for c in nc]
    m_new = [jnp.maximum(b_end[c] + m_st[c], jnp.max(log_s[c], axis=0, keepdims=True)) for c in nc]
    kw = [k[c] * jnp.exp(log_s[c] - m_new[c]) for c in nc]
    w_c = [jnp.exp(b_end[c] + m_st[c] - m_new[c]) for c in nc]
    kv = [_dot_tn(kw[c], v[c]) for c in nc]
    for c, (d, hd) in enumerate(chains):
        num = sv[c] + w_prev[c] * qc[c]
        den = jnp.sum(s[c], axis=-1, keepdims=True) + w_prev[c] * jnp.sum(q[c] * n_st[c], axis=-1, keepdims=True)
        h_refs[d][0, :, hd * ML_V:(hd + 1) * ML_V] = num / jnp.maximum(jnp.abs(den), jnp.exp(-m_t[c]))
        c_scr[c] = w_c[c] * c_st[c] + kv[c]
        n_scr[c] = w_c[c] * n_st[c] + jnp.sum(kw[c], axis=0, keepdims=True)
        m_scr[c] = m_new[c]

    @pl.when(i == pl.num_programs(1) - 1)
    def _():
        cn_ref[0] = c_scr[...]
        nn_ref[0] = n_scr[...]
        mn_ref[0] = m_scr[...]


def _mlstm(ml, gates, c0, n0, m0):
    b, s, nml = ml.shape
    nc = s // CHUNK
    fwd = lambda n: pl.BlockSpec((1, CHUNK, n), lambda bi, i: (bi, i, 0))
    bwd = lambda n: pl.BlockSpec((1, CHUNK, n), lambda bi, i: (bi, nc - 1 - i, 0))
    st = lambda shp: pl.BlockSpec((1,) + shp, lambda bi, i: (bi,) + (0,) * len(shp))
    hdim = ML_HEADS * ML_V
    return pl.pallas_call(
        _mlstm_kernel,
        grid=(b, nc),
        in_specs=[fwd(nml), bwd(nml), fwd(GATE_LANES), bwd(GATE_LANES),
                  st((N_CHAINS, ML_QK, ML_V)), st((N_CHAINS, 1, ML_QK)), st((N_CHAINS, 1, 1))],
        out_specs=[fwd(hdim), bwd(hdim),
                   st((N_CHAINS, ML_QK, ML_V)), st((N_CHAINS, 1, ML_QK)), st((N_CHAINS, 1, 1))],
        out_shape=[jax.ShapeDtypeStruct((b, s, hdim), F32), jax.ShapeDtypeStruct((b, s, hdim), F32),
                   jax.ShapeDtypeStruct(c0.shape, F32), jax.ShapeDtypeStruct(n0.shape, F32),
                   jax.ShapeDtypeStruct(m0.shape, F32)],
        scratch_shapes=[pltpu.VMEM((N_CHAINS, ML_QK, ML_V), F32), pltpu.VMEM((N_CHAINS, 1, ML_QK), F32),
                        pltpu.VMEM((N_CHAINS, 1, 1), F32)],
        compiler_params=_cparams("parallel", "arbitrary"),
        name="mlstm",
    )(ml, ml, gates, gates, c0, n0, m0)


def _gdconv_kernel(has_halo, *refs):
    if has_halo:
        x_ref, prev_ref, next_ref, w_ref, o_ref, xp_ref = refs
    else:
        x_ref, w_ref, o_ref, xp_ref = refs
    rows = x_ref.shape[1]
    nch = x_ref.shape[2]
    pad = 8
    zero = jnp.zeros((pad, nch), F32)
    if has_halo:
        c = pl.program_id(1)
        xp_ref[0:pad, :] = jnp.where(c > 0, prev_ref[0], zero)
        xp_ref[pad + rows:, :] = jnp.where(c < pl.num_programs(1) - 1, next_ref[0], zero)
    else:
        xp_ref[0:pad, :] = zero
        xp_ref[pad + rows:, :] = zero
    xp_ref[pad:pad + rows, :] = x_ref[0]
    half = CONV_W // 2
    for lc in range(nch // 128):
        sl = slice(lc * 128, (lc + 1) * 128)
        acc = None
        for j in range(CONV_W):
            term = xp_ref[pad - half + j:pad - half + j + rows, sl] * w_ref[j:j + 1, sl]
            acc = term if acc is None else acc + term
        y = _silu(acc)
        if lc < 2 * GD_HEADS:
            y = y * lax.rsqrt(jnp.sum(y * y, axis=-1, keepdims=True) + EPS)
        if lc < GD_HEADS:
            y = y * (GD_QK ** -0.5)
        o_ref[0, :, sl] = y


def _gdconv_ctx(qkv, conv_w):
    b, s, nch = qkv.shape
    return pl.pallas_call(
        functools.partial(_gdconv_kernel, False),
        grid=(b,),
        in_specs=[pl.BlockSpec((1, s, nch), lambda bi: (bi, 0, 0)), pl.BlockSpec((CONV_W, nch), lambda bi: (0, 0))],
        out_specs=pl.BlockSpec((1, s, nch), lambda bi: (bi, 0, 0)),
        out_shape=jax.ShapeDtypeStruct((b, s, nch), F32),
        scratch_shapes=[pltpu.VMEM((s + 16, nch), F32)],
        compiler_params=_cparams("parallel"),
        name="gdconv_ctx",
    )(qkv, conv_w)


def _gdconv_lat(view, conv_w):
    b, rows, wn = view.shape
    nch = wn // GRID_W
    rb = rows // 8
    return pl.pallas_call(
        functools.partial(_gdconv_kernel, True),
        grid=(b, GRID_W),
        in_specs=[pl.BlockSpec((1, rows, nch), lambda bi, c: (bi, 0, c)),
                  pl.BlockSpec((1, 8, nch), lambda bi, c: (bi, rb - 1, jnp.maximum(c - 1, 0))),
                  pl.BlockSpec((1, 8, nch), lambda bi, c: (bi, 0, jnp.minimum(c + 1, GRID_W - 1))),
                  pl.BlockSpec((CONV_W, nch), lambda bi, c: (0, 0))],
        out_specs=pl.BlockSpec((1, rows, nch), lambda bi, c: (bi, 0, c)),
        out_shape=jax.ShapeDtypeStruct(view.shape, F32),
        scratch_shapes=[pltpu.VMEM((rows + 16, nch), F32)],
        compiler_params=_cparams("parallel", "arbitrary"),
        name="gdconv_lat",
    )(view, view, view, conv_w)


SOLVE_BLOCK = 16


def _hi_lo(x):
    hi = x.astype(BF16).astype(F32)
    return hi, x - hi


def _dot_split(a, b):
    a_hi, a_lo = _hi_lo(a)
    b_hi, b_lo = _hi_lo(b)
    lhs = jnp.concatenate([a_hi, a_hi, a_lo], axis=1).astype(BF16)
    rhs = jnp.concatenate([b_hi, b_lo, b_hi], axis=0).astype(BF16)
    return jnp.dot(lhs, rhs, preferred_element_type=F32)


def _unit_triangular_inverses(ns):
    c = ns[0].shape[0]
    row = lax.broadcasted_iota(jnp.int32, (c, c), 0)
    col = lax.broadcasted_iota(jnp.int32, (c, c), 1)
    eye = (row == col).astype(F32)
    in_diag_block = (row // SOLVE_BLOCK) == (col // SOLVE_BLOCK)
    mm = lambda a_list, b_list: [_dot_split(a, b) for a, b in zip(a_list, b_list)]

    n_d = [jnp.where(in_diag_block, n, 0.0) for n in ns]
    x = n_d
    d_inv = [eye - n for n in n_d]
    for _ in range(SOLVE_BLOCK.bit_length() - 2):
        x = mm(x, x)
        d_inv = [d + dx for d, dx in zip(d_inv, mm(d_inv, x))]
    m = mm(d_inv, [n - nd for n, nd in zip(ns, n_d)])
    assert c // SOLVE_BLOCK == 4
    i_minus_m = [eye - mi for mi in m]
    q = [a + b for a, b in zip(i_minus_m, mm(i_minus_m, mm(m, m)))]
    return mm(q, d_inv)


def _gdn_kernel(qf_ref, qb_ref, gf_ref, gb_ref, na_ref, s0_ref, of_ref, ob_ref, sn_ref, s_scr):
    i = pl.program_id(1)

    @pl.when(i == 0)
    def _():
        s_scr[...] = s0_ref[0]

    eye = (lax.broadcasted_iota(jnp.int32, (CHUNK, CHUNK), 0)
           == lax.broadcasted_iota(jnp.int32, (CHUNK, CHUNK), 1)).astype(F32)
    nqk = GD_HEADS * GD_QK
    masks = [_past_mask(d == 1) for d in range(2)]
    gates = [r[0] for r in (gf_ref, gb_ref)]
    glog = [na_ref[...] * jax.nn.softplus(g) for g in gates]
    beta_all = [jax.nn.sigmoid(g) for g in gates]
    gcum = [_dot_hi(masks[d][0].astype(F32), glog[d]) for d in range(2)]
    gtot = [jnp.sum(g, axis=0, keepdims=True) for g in glog]
    gcum_t = [_transpose_hi(g) for g in gcum]

    chains = [(d, hd) for d in range(2) for hd in range(GD_HEADS)]
    x_refs, o_refs = (qf_ref, qb_ref), (of_ref, ob_ref)
    q, k, v, g_col, beta, g_end, decay = [], [], [], [], [], [], []
    for d, hd in chains:
        ca = GD_GATE0 + d * 8 + hd
        q.append(x_refs[d][0, :, hd * GD_QK:(hd + 1) * GD_QK])
        k.append(x_refs[d][0, :, nqk + hd * GD_QK:nqk + (hd + 1) * GD_QK])
        v.append(x_refs[d][0, :, 2 * nqk + hd * GD_V:2 * nqk + (hd + 1) * GD_V])
        g_col.append(gcum[d][:, ca:ca + 1])
        beta.append(beta_all[d][:, ca + GD_HEADS:ca + GD_HEADS + 1])
        g_end.append(gtot[d][:, ca:ca + 1])
        decay.append(jnp.exp(jnp.where(masks[d][0], g_col[-1] - gcum_t[d][ca:ca + 1, :], -jnp.inf)))
    nc = range(len(chains))
    kk = [_dot_nt(k[c], k[c]) for c in nc]
    xs = [jnp.where(masks[chains[c][0]][1], beta[c] * kk[c] * decay[c], 0.0) for c in nc]
    ps = _unit_triangular_inverses(xs)
    uw = [_dot_split(ps[c], jnp.concatenate([v[c] * beta[c], k[c] * (beta[c] * jnp.exp(g_col[c]))], axis=-1))
          for c in nc]
    qk = [_dot_nt(q[c], k[c]) * decay[c] for c in nc]
    s_st = [s_scr[c] for c in nc]
    v_new = [uw[c][:, :GD_V] - _dot(uw[c][:, GD_V:], s_st[c]) for c in nc]
    o_loc = [_dot(q[c] * jnp.exp(g_col[c]), s_st[c]) for c in nc]
    o_new = [o_loc[c] + _dot(qk[c], v_new[c]) for c in nc]
    s_new = [s_st[c] * jnp.exp(g_end[c]) + _dot_tn(k[c] * jnp.exp(g_end[c] - g_col[c]), v_new[c]) for c in nc]
    for c, (d, hd) in enumerate(chains):
        o_refs[d][0, :, hd * GD_V:(hd + 1) * GD_V] = o_new[c]
        s_scr[c] = s_new[c]

    @pl.when(i == pl.num_programs(1) - 1)
    def _():
        sn_ref[0] = s_scr[...]


def _gdn(qkv_view, gates_view, neg_a, s0, nc, idx_fn, out_view_shape):
    b = qkv_view.shape[0]
    nqkv = 2 * GD_HEADS * GD_QK + GD_HEADS * GD_V
    hdim = GD_HEADS * GD_V
    fwd = lambda n: pl.BlockSpec((1, CHUNK, n), lambda bi, i: idx_fn(bi, i))
    bwd = lambda n: pl.BlockSpec((1, CHUNK, n), lambda bi, i: idx_fn(bi, nc - 1 - i))
    st = pl.BlockSpec((1, N_CHAINS, GD_QK, GD_V), lambda bi, i: (bi, 0, 0, 0))
    return pl.pallas_call(
        _gdn_kernel,
        grid=(b, nc),
        in_specs=[fwd(nqkv), bwd(nqkv), fwd(GATE_LANES), bwd(GATE_LANES),
                  pl.BlockSpec((1, GATE_LANES), lambda bi, i: (0, 0)), st],
        out_specs=[fwd(hdim), bwd(hdim), st],
        out_shape=[jax.ShapeDtypeStruct(out_view_shape, F32), jax.ShapeDtypeStruct(out_view_shape, F32),
                   jax.ShapeDtypeStruct(s0.shape, F32)],
        scratch_shapes=[pltpu.VMEM((N_CHAINS, GD_QK, GD_V), F32)],
        compiler_params=_cparams("parallel", "arbitrary"),
        name="gdn",
    )(qkv_view, qkv_view, gates_view, gates_view, neg_a, s0)


def _head_rms(t, nheads, width):
    outs = []
    for hd in range(nheads):
        th = t[:, hd * width:(hd + 1) * width]
        outs.append(th * lax.rsqrt(jnp.mean(th * th, axis=-1, keepdims=True) + EPS))
    return jnp.concatenate(outs, axis=-1)


def _rms(t, w):
    return t * lax.rsqrt(jnp.mean(t * t, axis=-1, keepdims=True) + EPS) * w


def _post_kernel(x_ref, hf_ref, hb_ref, og_ref, of_ref, ob_ref, z_ref, mlw_ref, gdw_ref, wout_ref,
                 npost_ref, g2_ref, npre_ref, sh_ref, sc_ref, rwt_ref, wsg_ref, wsu_ref, wsd_ref,
                 x1_ref, hffn_ref, lt_ref, ys_ref, of_scr, ob_scr):
    ml_y = _head_rms(hf_ref[0] + hb_ref[0], ML_HEADS, ML_V) * mlw_ref[...] * jax.nn.sigmoid(og_ref[0])
    _from_grid_view(of_ref, of_scr)
    _from_grid_view(ob_ref, ob_scr)
    o_sum = jnp.concatenate([_pitched_rows(of_scr, g) + _pitched_rows(ob_scr, g) for g in range(of_scr.shape[0])],
                            axis=-1)
    gd_y = _head_rms(o_sum, GD_HEADS, GD_V) * gdw_ref[...] * _silu(z_ref[0])
    y = _dot(jnp.concatenate([ml_y, gd_y], axis=-1), wout_ref[...])
    x1 = x_ref[0] + g2_ref[0] * _rms(y, npost_ref[...])
    x1_ref[0] = x1
    hffn = _rms(x1, npre_ref[...]) * (1.0 + sc_ref[0]) + sh_ref[0]
    nct = hffn.shape[1] // LANES
    for c in range(nct):
        hffn_ref[0, pl.ds(c, hffn.shape[0], stride=nct), :] = hffn[:, c * LANES:(c + 1) * LANES]
    hb = hffn.astype(BF16)
    lt_ref[...] = lax.dot_general(rwt_ref[...], hb, (((1,), (1,)), ((), ())), preferred_element_type=F32)
    hs = _silu(jnp.dot(hb, wsg_ref[...], preferred_element_type=F32)) * jnp.dot(hb, wsu_ref[...],
                                                                                preferred_element_type=F32)
    ys_ref[0] = _dot(hs, wsd_ref[...])


def _post(x, hf, hb, ml, of, ob, gz, mlw, gdw, wout, npost, g2, npre, sh, sc, rwt, wsg, wsu, wsd, tm):
    b, s, d = x.shape
    nt = s // tm
    hw = ML_HEADS * ML_V
    og_blk = (2 * ML_HEADS * ML_QK + ML_HEADS * ML_V) // hw
    tok = lambda n: pl.BlockSpec((1, tm, n), lambda bi, i: (bi, i, 0))
    full = lambda shp: pl.BlockSpec(shp, lambda bi, i: (0,) * len(shp))
    mod = pl.BlockSpec((1, 1, d), lambda bi, i: (bi, 0, 0))
    ne = rwt.shape[0]
    ds = wsg.shape[1]
    gview = pl.BlockSpec((1, tm // GRID_W, GRID_W * hw), lambda bi, i: (bi, i, 0))
    return pl.pallas_call(
        _post_kernel,
        grid=(b, nt),
        in_specs=[tok(d), tok(hw), tok(hw), pl.BlockSpec((1, tm, hw), lambda bi, i: (bi, i, og_blk)),
                  gview, gview, tok(hw), full((1, hw)), full((1, hw)), full((d, d)),
                  full((1, d)), mod, full((1, d)), mod, mod, full((ne, d)), full((d, ds)), full((d, ds)),
                  full((ds, d))],
        out_specs=[tok(d), pl.BlockSpec((1, tm * (d // LANES), LANES), lambda bi, i: (bi, i, 0)),
                   pl.BlockSpec((ne, tm), lambda bi, i: (0, bi * nt + i)), tok(d)],
        out_shape=[jax.ShapeDtypeStruct((b, s, d), F32), jax.ShapeDtypeStruct((b, s * (d // LANES), LANES), F32),
                   jax.ShapeDtypeStruct((ne, b * s), F32), jax.ShapeDtypeStruct((b, s, d), F32)],
        scratch_shapes=[pltpu.VMEM((hw // LANES, tm // GRID_W * GRID_PITCH, LANES), F32)] * 2,
        compiler_params=_cparams("parallel", "arbitrary"),
        name="post",
    )(x, hf, hb, ml, of, ob, gz, mlw, gdw, wout, npost, g2, npre, sh, sc, rwt, wsg, wsu, wsd)


def _route_kernel(lt_ref, bias_ref, idx_ref, gate_ref):
    ne, tn = lt_ref.shape
    gsz = ne // N_GROUPS
    scores = jax.nn.sigmoid(lt_ref[...])
    sel = scores + bias_ref[...]
    neg = -jnp.inf
    sel3 = sel.reshape(N_GROUPS, gsz, tn)
    io3 = lax.broadcasted_iota(jnp.int32, sel3.shape, 1)
    top1 = jnp.max(sel3, axis=1, keepdims=True)
    first = jnp.min(jnp.where(sel3 == top1, io3, gsz), axis=1, keepdims=True)
    top2 = jnp.max(jnp.where(io3 == first, neg, sel3), axis=1, keepdims=True)
    grp = (top1 + top2).reshape(N_GROUPS, tn)
    iog = lax.broadcasted_iota(jnp.int32, grp.shape, 0)
    keep = jnp.zeros(grp.shape, jnp.bool_)
    for _ in range(TOPK_GROUPS):
        m = jnp.max(grp, axis=0, keepdims=True)
        pick = iog == jnp.min(jnp.where(grp == m, iog, N_GROUPS), axis=0, keepdims=True)
        keep = keep | pick
        grp = jnp.where(pick, neg, grp)
    cand = jnp.where(keep.reshape(N_GROUPS, 1, tn), sel3, neg).reshape(ne, tn)
    ioe = lax.broadcasted_iota(jnp.int32, cand.shape, 0)
    idxs, gates = [], []
    for _ in range(TOP_K):
        m = jnp.max(cand, axis=0, keepdims=True)
        e = jnp.min(jnp.where(cand == m, ioe, ne), axis=0, keepdims=True)
        pick = ioe == e
        idxs.append(e)
        gates.append(jnp.sum(jnp.where(pick, scores, 0.0), axis=0, keepdims=True))
        cand = jnp.where(pick, neg, cand)
    gate = jnp.concatenate(gates, axis=0)
    idx_ref[...] = jnp.concatenate(idxs, axis=0)
    gate_ref[...] = gate / jnp.sum(gate, axis=0, keepdims=True) * ROUTED_SCALE


def _route(logits_t, bias_col, tn):
    ne, t = logits_t.shape
    return pl.pallas_call(
        _route_kernel,
        grid=(t // tn,),
        in_specs=[pl.BlockSpec((ne, tn), lambda i: (0, i)), pl.BlockSpec((ne, 1), lambda i: (0, 0))],
        out_specs=[pl.BlockSpec((TOP_K, tn), lambda i: (0, i)), pl.BlockSpec((TOP_K, tn), lambda i: (0, i))],
        out_shape=[jax.ShapeDtypeStruct((TOP_K, t), jnp.int32), jax.ShapeDtypeStruct((TOP_K, t), F32)],
        compiler_params=_cparams("parallel"),
        name="route",
    )(logits_t, bias_col)


def _experts_kernel(be_ref, nu_ref,
                    tok_ref, tokn_ref, slot_ref, w_ref, wg_ref, wu_ref, wd_ref, h_hbm,
                    slots_hbm, xg, ob, wgb, wub, wdb, gsem, ssem):
    i = pl.program_id(0)
    n_used = nu_ref[0]
    cur = i % 2
    nxt = 1 - cur
    nct = xg.shape[1] // EXPERT_BLOCK
    rows = nct * EXPERT_BLOCK

    def gather_copy(tref, j, buf):
        src = h_hbm.at[pl.ds(pl.multiple_of(tref[0, 0, j] * nct, nct), nct)]
        return pltpu.make_async_copy(src, xg.at[buf, pl.ds(j * nct, nct)], gsem.at[buf])

    def scatter_copy(j, buf):
        dst = slots_hbm.at[pl.ds(pl.multiple_of(slot_ref[0, 0, j] * nct, nct), nct)]
        return pltpu.make_async_copy(ob.at[buf, pl.ds(j * nct, nct)], dst, ssem.at[buf])

    def gather_wait(buf):
        pltpu.make_async_copy(h_hbm.at[pl.ds(0, rows)], xg.at[buf], gsem.at[buf]).wait()

    def scatter_wait(buf):
        pltpu.make_async_copy(ob.at[buf], slots_hbm.at[pl.ds(0, rows)], ssem.at[buf]).wait()

    @pl.when(i < n_used)
    def _():
        @pl.when(i == 0)
        def _():
            ob[...] = jnp.zeros(ob.shape, F32)
            spare0 = slots_hbm.shape[0] - ob.shape[0] * rows
            for buf in range(ob.shape[0]):
                init = pltpu.make_async_copy(ob.at[buf], slots_hbm.at[pl.ds(spare0 + buf * rows, rows)], ssem.at[buf])
                init.start()
                init.wait()
            for j in range(EXPERT_BLOCK):
                gather_copy(tok_ref, j, cur).start(priority=j % 2)

        @pl.when(i + 1 < n_used)
        def _():
            for j in range(EXPERT_BLOCK):
                gather_copy(tokn_ref, j, nxt).start(priority=j % 2)

        gather_wait(cur)

        @pl.when(i >= 2)
        def _():
            scatter_wait(cur)

        @pl.when((i == 0) | (be_ref[i] != be_ref[jnp.maximum(i - 1, 0)]))
        def _():
            wgb[...] = wg_ref[0].astype(BF16)
            wub[...] = wu_ref[0].astype(BF16)
            wdb[...] = wd_ref[0].astype(BF16)

        xb = jnp.concatenate([xg[cur, pl.ds(c, EXPERT_BLOCK, stride=nct), :] for c in range(nct)],
                             axis=-1).astype(BF16)
        hmid = _silu(jnp.dot(xb, wgb[...], preferred_element_type=F32)) * jnp.dot(xb, wub[...],
                                                                                  preferred_element_type=F32)
        out = _dot(hmid, wdb[...])
        eye = (lax.broadcasted_iota(jnp.int32, (EXPERT_BLOCK, EXPERT_BLOCK), 0)
               == lax.broadcasted_iota(jnp.int32, (EXPERT_BLOCK, EXPERT_BLOCK), 1))
        w_col = jnp.sum(jnp.where(eye, w_ref[0], 0.0), axis=1, keepdims=True)
        out = out * w_col
        for c in range(nct):
            ob[cur, pl.ds(c, EXPERT_BLOCK, stride=nct), :] = out[:, c * LANES:(c + 1) * LANES]

        for j in range(EXPERT_BLOCK):
            scatter_copy(j, cur).start(priority=j % 2)

        @pl.when(i == n_used - 1)
        def _():
            scatter_wait(cur)

            @pl.when(i >= 1)
            def _():
                scatter_wait(nxt)


def _experts(hffn, block_e, n_used, row_tok, row_slot, row_w, wg, wu, wd, n_slots):
    d = wg.shape[1]
    nct = d // LANES
    nb = block_e.shape[0]
    de = wg.shape[2]
    last = nb - 1
    smem_blk = lambda f: pl.BlockSpec((1, 1, EXPERT_BLOCK), f, memory_space=pltpu.SMEM)
    grid_spec = pltpu.PrefetchScalarGridSpec(
        num_scalar_prefetch=2,
        grid=(nb,),
        in_specs=[smem_blk(lambda i, be, nu: (i, 0, 0)),
                  smem_blk(lambda i, be, nu: (jnp.minimum(i + 1, last), 0, 0)),
                  smem_blk(lambda i, be, nu: (i, 0, 0)),
                  pl.BlockSpec((1, 1, EXPERT_BLOCK), lambda i, be, nu: (i, 0, 0)),
                  pl.BlockSpec((1, d, de), lambda i, be, nu: (be[i], 0, 0)),
                  pl.BlockSpec((1, d, de), lambda i, be, nu: (be[i], 0, 0)),
                  pl.BlockSpec((1, de, d), lambda i, be, nu: (be[i], 0, 0)),
                  pl.BlockSpec(memory_space=pl.ANY)],
        out_specs=pl.BlockSpec(memory_space=pl.ANY),
        scratch_shapes=[pltpu.VMEM((2, EXPERT_BLOCK * nct, LANES), F32), pltpu.VMEM((2, EXPERT_BLOCK * nct, LANES), F32),
                        pltpu.VMEM((d, de), BF16), pltpu.VMEM((d, de), BF16), pltpu.VMEM((de, d), BF16),
                        pltpu.SemaphoreType.DMA((2,)), pltpu.SemaphoreType.DMA((2,))],
    )
    return pl.pallas_call(
        _experts_kernel,
        grid_spec=grid_spec,
        out_shape=jax.ShapeDtypeStruct(((n_slots + 2 * EXPERT_BLOCK) * nct, LANES), F32),
        compiler_params=_cparams("arbitrary"),
        name="experts",
    )(block_e, n_used, row_tok, row_tok, row_slot, row_w, wg, wu, wd, hffn)


def _combine_kernel(x1_ref, ys_ref, *refs):
    slot_refs, (npost_ref, g5_ref, o_ref) = refs[:TOP_K], refs[TOP_K:]
    tm = x1_ref.shape[1]
    nct = slot_refs[0].shape[0] // tm
    routed = []
    for c in range(nct):
        acc = slot_refs[0][pl.ds(c, tm, stride=nct), :]
        for k in range(1, TOP_K):
            acc = acc + slot_refs[k][pl.ds(c, tm, stride=nct), :]
        routed.append(acc)
    y = ys_ref[0] + jnp.concatenate(routed, axis=-1)
    o_ref[0] = x1_ref[0] + g5_ref[0] * _rms(y, npost_ref[...])


def _combine(x1, ys, slots, npost, g5, tm):
    b, s, d = x1.shape
    nt = s // tm
    nct = d // LANES
    tok = pl.BlockSpec((1, tm, d), lambda bi, i: (bi, i, 0))
    slot = lambda k: pl.BlockSpec((tm * nct, LANES), lambda bi, i: (k * b * nt + bi * nt + i, 0))
    return pl.pallas_call(
        _combine_kernel,
        grid=(b, nt),
        in_specs=[tok, tok] + [slot(k) for k in range(TOP_K)]
                 + [pl.BlockSpec((1, d), lambda bi, i: (0, 0)), pl.BlockSpec((1, 1, d), lambda bi, i: (bi, 0, 0))],
        out_specs=tok,
        out_shape=jax.ShapeDtypeStruct((b, s, d), F32),
        compiler_params=_cparams("parallel", "arbitrary"),
        name="combine",
    )(x1, ys, *([slots] * TOP_K), npost, g5)


def _dispatch_plan(idx_t, gate_t):
    k, t = idx_t.shape
    n_asg = k * t
    nb = n_asg // EXPERT_BLOCK + N_EXPERTS
    flat_e = idx_t.reshape(-1)
    order = jnp.argsort(flat_e).astype(jnp.int32)
    counts = jnp.zeros((N_EXPERTS,), jnp.int32).at[flat_e].add(1)
    padded = (counts + EXPERT_BLOCK - 1) // EXPERT_BLOCK * EXPERT_BLOCK
    start = jnp.cumsum(counts) - counts
    pend = jnp.cumsum(padded)
    pstart = pend - padded
    blk0 = jnp.arange(nb, dtype=jnp.int32) * EXPERT_BLOCK
    block_e = jnp.minimum(jnp.searchsorted(pend, blk0, side='right'), N_EXPERTS - 1).astype(jnp.int32)
    n_used = (pend[-1] // EXPERT_BLOCK).astype(jnp.int32).reshape(1)
    pos = blk0[:, None] - pstart[block_e][:, None] + jnp.arange(EXPERT_BLOCK, dtype=jnp.int32)[None, :]
    valid = pos < counts[block_e][:, None]
    src = jnp.clip(start[block_e][:, None] + pos, 0, n_asg - 1)
    asg = order[src]
    row_tok = jnp.where(valid, asg % t, 0).astype(jnp.int32)
    lane = jnp.arange(EXPERT_BLOCK, dtype=jnp.int32)[None, :]
    spare = n_asg + (jnp.arange(nb, dtype=jnp.int32)[:, None] % 2) * EXPERT_BLOCK + lane
    row_slot = jnp.where(valid, asg, spare).astype(jnp.int32)
    row_w = jnp.where(valid, gate_t.reshape(-1)[asg], 0.0).astype(F32)
    shp = (nb, 1, EXPERT_BLOCK)
    return block_e, n_used, row_tok.reshape(shp), row_slot.reshape(shp), row_w.reshape(shp)


def _pack_in_weights(w_in, ml_i_bias, ml_f_bias, gd_dt_bias):
    d = w_in.shape[0]
    nml = 2 * ML_HEADS * ML_QK + 2 * ML_HEADS * ML_V
    ml_cols = nml + 4 * ML_HEADS
    ngq = GD_HEADS * (2 * GD_QK + GD_V)
    ngz = GD_HEADS * GD_V
    wml = w_in[:, :nml].astype(BF16)
    wgq = w_in[:, ml_cols:ml_cols + ngq].astype(BF16)
    wgz = w_in[:, ml_cols + ngq:ml_cols + ngq + ngz].astype(BF16)
    wg = jnp.zeros((d, GATE_LANES), F32)
    wg = wg.at[:, ML_GATE0:ML_GATE0 + 16].set(w_in[:, nml:ml_cols])
    wg = wg.at[:, GD_GATE0:GD_GATE0 + 16].set(w_in[:, ml_cols + ngq + ngz:])
    gb = jnp.zeros((GATE_LANES,), F32)
    gb = gb.at[ML_GATE0:ML_GATE0 + 16].set(jnp.stack([ml_i_bias, ml_f_bias], axis=1).reshape(-1))
    gb = gb.at[GD_GATE0:GD_GATE0 + 16].set(jnp.stack([gd_dt_bias, jnp.zeros_like(gd_dt_bias)], axis=1).reshape(-1))
    return wml, wgq, wgz, wg.astype(BF16), gb.reshape(1, GATE_LANES)


def _mixer(x, ctx, mod, mod_ctx, norm_pre_mix, w_in, ml_i_bias, ml_f_bias, gd_conv_w, gd_a_log, gd_dt_bias):
    b, s, d = x.shape
    sc = ctx.shape[1]
    wml, wgq, wgz, wg, gb = _pack_in_weights(w_in, ml_i_bias, ml_f_bias, gd_dt_bias)
    nw = norm_pre_mix.reshape(1, d)
    ctx_mod = lambda j: jnp.broadcast_to(mod_ctx[j].reshape(1, 1, d), (b, 1, d))
    ml_c, gq_c, _, g_c = _proj(ctx, nw, ctx_mod(0), ctx_mod(1), wml, wgq, wgz, wg, gb, tm=sc, grid_view=False)
    ml_l, gqv_l, gz_l, g_l, gv_l = _proj(x, nw, mod[0], mod[1], wml, wgq, wgz, wg, gb, tm=512, grid_view=True)

    c0 = jnp.zeros((b, N_CHAINS, ML_QK, ML_V), F32)
    n0 = jnp.zeros((b, N_CHAINS, 1, ML_QK), F32)
    m0 = jnp.zeros((b, N_CHAINS, 1, 1), F32)
    _, _, c1, n1, m1 = _mlstm(ml_c, g_c, c0, n0, m0)
    hf, hb, _, _, _ = _mlstm(ml_l, g_l, c1, n1, m1)

    neg_a = jnp.zeros((GATE_LANES,), F32)
    neg_a = neg_a.at[GD_GATE0:GD_GATE0 + 16].set(
        jnp.stack([-jnp.exp(gd_a_log), jnp.zeros_like(gd_a_log)], axis=1).reshape(-1)).reshape(1, GATE_LANES)
    qn_c = _gdconv_ctx(gq_c, gd_conv_w)
    qnv_l = _gdconv_lat(gqv_l, gd_conv_w)
    s0 = jnp.zeros((b, N_CHAINS, GD_QK, GD_V), F32)
    hdim = GD_HEADS * GD_V
    _, _, s1 = _gdn(qn_c, g_c, neg_a, s0, sc // CHUNK, lambda bi, n: (bi, n, 0), (b, sc, hdim))
    rows = s // GRID_W
    cpc = rows // CHUNK
    col_idx = lambda bi, n: (bi, n % cpc, n // cpc)
    ofv, obv, _ = _gdn(qnv_l, gv_l, neg_a, s1, s // CHUNK, col_idx, (b, rows, GRID_W * hdim))
    return hf, hb, ml_l, ofv, obv, gz_l


def kernel(x, c, ctx, c_ctx, w_ada, b_ada, norm_pre_mix, norm_post_mix, norm_pre_ffn, norm_post_ffn, w_in,
           ml_i_bias, ml_f_bias, ml_norm_w, gd_conv_w, gd_a_log, gd_dt_bias, gd_norm_w, w_out, router_w,
           router_bias, w_gate, w_up, w_down, ws_gate, ws_up, ws_down):
    b, s, d = x.shape
    depth = w_ada.shape[0]
    assert depth == 1, "the context stream update of deeper stacks is not implemented"
    ly = 0
    cc = jnp.zeros((16, d), F32).at[:b].set(c).at[b].set(c_ctx)
    mod_all = _ada(cc, w_ada[ly], b_ada[ly])
    mod = [mod_all[:b, j * d:(j + 1) * d].reshape(b, 1, d) for j in range(6)]
    mod_ctx = [mod_all[b, j * d:(j + 1) * d] for j in range(6)]

    hf, hb, ml_l, of, ob, gz_l = _mixer(x, ctx, mod, mod_ctx, norm_pre_mix[ly], w_in[ly], ml_i_bias[ly],
                                        ml_f_bias[ly], gd_conv_w[ly], gd_a_log[ly], gd_dt_bias[ly])

    row = lambda v: v.reshape(1, -1)
    x1, hffn, logits_t, ys = _post(
        x, hf, hb, ml_l, of, ob, gz_l, row(ml_norm_w[ly]), row(jnp.tile(gd_norm_w[ly], GD_HEADS)),
        w_out[ly].astype(BF16), row(norm_post_mix[ly]), mod[2], row(norm_pre_ffn[ly]), mod[3], mod[4],
        router_w[ly].T.astype(BF16), ws_gate[ly].astype(BF16), ws_up[ly].astype(BF16), ws_down[ly].astype(BF16),
        tm=512)

    idx_t, gate_t = _route(logits_t, router_bias[ly].reshape(-1, 1), tn=512)
    block_e, n_used, row_tok, row_slot, row_w = _dispatch_plan(idx_t, gate_t)
    t = b * s
    slots = _experts(hffn.reshape(t * (d // LANES), LANES), block_e, n_used, row_tok, row_slot, row_w,
                     w_gate[ly], w_up[ly], w_down[ly], TOP_K * t)
    return _combine(x1, ys, slots, row(norm_post_ffn[ly]), mod[5], tm=256)
```

```python
import functools

import jax
import jax.numpy as jnp
from jax import lax
from jax.experimental import pallas as pl
from jax.experimental.pallas import tpu as pltpu

EPS = 1e-6
CHUNK = 64
GRID_W = 64
ML_HEADS, ML_QK, ML_V = 4, 64, 128
GD_HEADS, GD_QK, GD_V = 4, 128, 128
CONV_W = 5
N_EXPERTS, TOP_K, N_GROUPS, TOPK_GROUPS = 256, 8, 8, 4
ROUTED_SCALE = 2.5
EXPERT_BLOCK = 128
N_CHAINS = 8
LANES = 128
GATE_LANES = LANES
ML_GATE0, GD_GATE0 = 0, 16

F32 = jnp.float32
BF16 = jnp.bfloat16
HI = lax.Precision.HIGHEST
VMEM_LIMIT = 56 * 1024 * 1024


def _cparams(*sem):
    return pltpu.CompilerParams(dimension_semantics=sem, vmem_limit_bytes=VMEM_LIMIT)


def _dot(a, b):
    return jnp.dot(a.astype(BF16), b.astype(BF16), preferred_element_type=F32)


def _dot_nt(a, b):
    return lax.dot_general(a.astype(BF16), b.astype(BF16), (((1,), (1,)), ((), ())), preferred_element_type=F32)


def _dot_tn(a, b):
    return lax.dot_general(a.astype(BF16), b.astype(BF16), (((0,), (0,)), ((), ())), preferred_element_type=F32)


def _dot_hi(a, b):
    return jnp.dot(a, b, precision=HI, preferred_element_type=F32)


def _dot_nt_hi(a, b):
    return lax.dot_general(a, b, (((1,), (1,)), ((), ())), precision=HI, preferred_element_type=F32)


def _transpose_hi(x):
    n = x.shape[1]
    eye = (lax.broadcasted_iota(jnp.int32, (n, n), 0) == lax.broadcasted_iota(jnp.int32, (n, n), 1)).astype(F32)
    return _dot_nt_hi(eye, x)


def _silu(x):
    return x * jax.nn.sigmoid(x)


def _past_mask(reverse):
    t = lax.broadcasted_iota(jnp.int32, (CHUNK, CHUNK), 0)
    s = lax.broadcasted_iota(jnp.int32, (CHUNK, CHUNK), 1)
    return (s >= t, s > t) if reverse else (s <= t, s < t)


def _ada_kernel(c_ref, w_ref, b_ref, o_ref):
    o_ref[...] = _dot(_silu(c_ref[...]), w_ref[...]) + b_ref[...]


def _ada(cc, w_ada, b_ada):
    rows, d = cc.shape
    n = w_ada.shape[1]
    tn = 1536
    return pl.pallas_call(
        _ada_kernel,
        grid=(n // tn,),
        in_specs=[pl.BlockSpec((rows, d), lambda j: (0, 0)),
                  pl.BlockSpec((d, tn), lambda j: (0, j)),
                  pl.BlockSpec((1, tn), lambda j: (0, j))],
        out_specs=pl.BlockSpec((rows, tn), lambda j: (0, j)),
        out_shape=jax.ShapeDtypeStruct((rows, n), F32),
        compiler_params=_cparams("arbitrary"),
        name="ada",
    )(cc, w_ada, b_ada.reshape(1, n))


GRID_PITCH = GRID_W + 8


def _to_grid_view(src_ref, dst_ref):
    ng = src_ref.shape[0]
    r = src_ref.shape[1] // GRID_PITCH
    for c in range(GRID_W):
        for g in range(ng):
            lo = (c * ng + g) * LANES
            dst_ref[0, :, lo:lo + LANES] = src_ref[g, pl.ds(c, r, stride=GRID_PITCH), :]


def _from_grid_view(src_ref, dst_ref):
    ng = dst_ref.shape[0]
    r = dst_ref.shape[1] // GRID_PITCH
    for c in range(GRID_W):
        for g in range(ng):
            lo = (c * ng + g) * LANES
            dst_ref[g, pl.ds(c, r, stride=GRID_PITCH), :] = src_ref[0, :, lo:lo + LANES]


def _pitched_rows(ref, g):
    r = ref.shape[1] // GRID_PITCH
    return jnp.concatenate([ref[g, i * GRID_PITCH:i * GRID_PITCH + GRID_W, :] for i in range(r)], axis=0)


def _proj_kernel(grid_view, x_ref, nw_ref, sh_ref, sc_ref, wml_ref, wgq_ref, wgz_ref, wg_ref, gb_ref, *refs):
    if grid_view:
        ml_ref, gqv_ref, gz_ref, g_ref, gv_ref, gq_scr, g_scr = refs
    else:
        ml_ref, gq_ref, gz_ref, g_ref = refs
    x = x_ref[0]
    xn = x * lax.rsqrt(jnp.mean(x * x, axis=-1, keepdims=True) + EPS) * nw_ref[...]
    h = (xn * (1.0 + sc_ref[0]) + sh_ref[0]).astype(BF16)
    ml_ref[0] = jnp.dot(h, wml_ref[...], preferred_element_type=F32)
    gz_ref[0] = jnp.dot(h, wgz_ref[...], preferred_element_type=F32)
    gates = jnp.dot(h, wg_ref[...], preferred_element_type=F32) + gb_ref[...]
    g_ref[0] = gates
    gq = jnp.dot(h, wgq_ref[...], preferred_element_type=F32)
    if grid_view:
        for r in range(x.shape[0] // GRID_W):
            rows = slice(r * GRID_W, (r + 1) * GRID_W)
            prow = slice(r * GRID_PITCH, r * GRID_PITCH + GRID_W)
            g_scr[0, prow, :] = gates[rows]
            for g in range(gq_scr.shape[0]):
                gq_scr[g, prow, :] = gq[rows, g * LANES:(g + 1) * LANES]
        _to_grid_view(gq_scr, gqv_ref)
        _to_grid_view(g_scr, gv_ref)
    else:
        gq_ref[0] = gq


def _proj(x, norm_w, shift, scale, wml, wgq, wgz, wg, gbias, tm, grid_view):
    b, s, d = x.shape
    nml, ngq, ngz = wml.shape[1], wgq.shape[1], wgz.shape[1]
    full = lambda shp: pl.BlockSpec(shp, lambda bi, i: (0,) * len(shp))
    tok = lambda n: pl.BlockSpec((1, tm, n), lambda bi, i: (bi, i, 0))
    mod = pl.BlockSpec((1, 1, d), lambda bi, i: (bi, 0, 0))
    if grid_view:
        rt = tm // GRID_W
        view = lambda n: pl.BlockSpec((1, rt, GRID_W * n), lambda bi, i: (bi, i, 0))
        vshape = lambda n: jax.ShapeDtypeStruct((b, s // GRID_W, GRID_W * n), F32)
        out_specs = [tok(nml), view(ngq), tok(ngz), tok(GATE_LANES), view(GATE_LANES)]
        out_shape = [jax.ShapeDtypeStruct((b, s, nml), F32), vshape(ngq), jax.ShapeDtypeStruct((b, s, ngz), F32),
                     jax.ShapeDtypeStruct((b, s, GATE_LANES), F32), vshape(GATE_LANES)]
        scratch = [pltpu.VMEM((ngq // LANES, rt * GRID_PITCH, LANES), F32), pltpu.VMEM((1, rt * GRID_PITCH, LANES), F32)]
    else:
        out_specs = [tok(nml), tok(ngq), tok(ngz), tok(GATE_LANES)]
        out_shape = [jax.ShapeDtypeStruct((b, s, n), F32) for n in (nml, ngq, ngz, GATE_LANES)]
        scratch = []
    return pl.pallas_call(
        functools.partial(_proj_kernel, grid_view),
        grid=(b, s // tm),
        in_specs=[tok(d), full((1, d)), mod, mod, full((d, nml)), full((d, ngq)), full((d, ngz)),
                  full((d, GATE_LANES)), full((1, GATE_LANES))],
        out_specs=out_specs,
        out_shape=out_shape,
        scratch_shapes=scratch,
        compiler_params=_cparams("parallel", "arbitrary"),
        name="proj",
    )(x, norm_w, shift, scale, wml, wgq, wgz, wg, gbias)


def _mlstm_kernel(mlf_ref, mlb_ref, gf_ref, gb_ref, c0_ref, n0_ref, m0_ref,
                  hf_ref, hb_ref, cn_ref, nn_ref, mn_ref, c_scr, n_scr, m_scr):
    i = pl.program_id(1)

    @pl.when(i == 0)
    def _():
        c_scr[...] = c0_ref[0]
        n_scr[...] = n0_ref[0]
        m_scr[...] = m0_ref[0]

    past = [_past_mask(d == 1)[0] for d in range(2)]
    g = [r[0] for r in (gf_ref, gb_ref)]
    ls = [jax.nn.log_sigmoid(x) for x in g]
    bcol = [_dot_hi(past[d].astype(F32), ls[d]) for d in range(2)]
    tot = [jnp.sum(x, axis=0, keepdims=True) for x in ls]
    g_t = [_transpose_hi(x) for x in g]
    b_t = [_transpose_hi(x) for x in bcol]

    chains = [(d, hd) for d in range(2) for hd in range(ML_HEADS)]
    nc = range(len(chains))
    ml_refs, h_refs = (mlf_ref, mlb_ref), (hf_ref, hb_ref)
    k0, v0 = ML_HEADS * ML_QK, 2 * ML_HEADS * ML_QK
    q, k, v, i_col, b_col, b_end, log_d = [], [], [], [], [], [], []
    for d, hd in chains:
        ci = ML_GATE0 + d * 8 + hd
        cf = ci + ML_HEADS
        q.append(ml_refs[d][0, :, hd * ML_QK:(hd + 1) * ML_QK])
        k.append(ml_refs[d][0, :, k0 + hd * ML_QK:k0 + (hd + 1) * ML_QK] * (ML_QK ** -0.5))
        v.append(ml_refs[d][0, :, v0 + hd * ML_V:v0 + (hd + 1) * ML_V])
        i_col.append(g[d][:, ci:ci + 1])
        b_col.append(bcol[d][:, cf:cf + 1])
        b_end.append(tot[d][:, cf:cf + 1])
        log_d.append(jnp.where(past[d], b_col[-1] - b_t[d][cf:cf + 1, :] + g_t[d][ci:ci + 1, :], -jnp.inf))
    c_st = [c_scr[c] for c in nc]
    n_st = [n_scr[c] for c in nc]
    m_st = [m_scr[c] for c in nc]
    log_prev = [b_col[c] + m_st[c] for c in nc]
    m_t = [jnp.maximum(log_prev[c], jnp.max(log_d[c], axis=-1, keepdims=True)) for c in nc]
    qk = [_dot_nt(q[c], k[c]) for c in nc]
    qc = [_dot(q[c], c_st[c]) for c in nc]
    s = [qk[c] * jnp.exp(log_d[c] - m_t[c]) for c in nc]
    w_prev = [jnp.exp(log_prev[c] - m_t[c]) for c in nc]
    sv = [_dot(s[c], v[c]) for c in nc]
    log_s = [b_end[c] - b_col[c] + i_col[c] for c in nc]
    m_new = [jnp.maximum(b_end[c] + m_st[c], jnp.max(log_s[c], axis=0, keepdims=True)) for c in nc]
    kw = [k[c] * jnp.exp(log_s[c] - m_new[c]) for c in nc]
    w_c = [jnp.exp(b_end[c] + m_st[c] - m_new[c]) for c in nc]
    kv = [_dot_tn(kw[c], v[c]) for c in nc]
    for c, (d, hd) in enumerate(chains):
        num = sv[c] + w_prev[c] * qc[c]
        den = jnp.sum(s[c], axis=-1, keepdims=True) + w_prev[c] * jnp.sum(q[c] * n_st[c], axis=-1, keepdims=True)
        h_refs[d][0, :, hd * ML_V:(hd + 1) * ML_V] = num / jnp.maximum(jnp.abs(den), jnp.exp(-m_t[c]))
        c_scr[c] = w_c[c] * c_st[c] + kv[c]
        n_scr[c] = w_c[c] * n_st[c] + jnp.sum(kw[c], axis=0, keepdims=True)
        m_scr[c] = m_new[c]

    @pl.when(i == pl.num_programs(1) - 1)
    def _():
        cn_ref[0] = c_scr[...]
        nn_ref[0] = n_scr[...]
        mn_ref[0] = m_scr[...]


def _mlstm(ml, gates, c0, n0, m0):
    b, s, nml = ml.shape
    nc = s // CHUNK
    fwd = lambda n: pl.BlockSpec((1, CHUNK, n), lambda bi, i: (bi, i, 0))
    bwd = lambda n: pl.BlockSpec((1, CHUNK, n), lambda bi, i: (bi, nc - 1 - i, 0))
    st = lambda shp: pl.BlockSpec((1,) + shp, lambda bi, i: (bi,) + (0,) * len(shp))
    hdim = ML_HEADS * ML_V
    return pl.pallas_call(
        _mlstm_kernel,
        grid=(b, nc),
        in_specs=[fwd(nml), bwd(nml), fwd(GATE_LANES), bwd(GATE_LANES),
                  st((N_CHAINS, ML_QK, ML_V)), st((N_CHAINS, 1, ML_QK)), st((N_CHAINS, 1, 1))],
        out_specs=[fwd(hdim), bwd(hdim),
                   st((N_CHAINS, ML_QK, ML_V)), st((N_CHAINS, 1, ML_QK)), st((N_CHAINS, 1, 1))],
        out_shape=[jax.ShapeDtypeStruct((b, s, hdim), F32), jax.ShapeDtypeStruct((b, s, hdim), F32),
                   jax.ShapeDtypeStruct(c0.shape, F32), jax.ShapeDtypeStruct(n0.shape, F32),
                   jax.ShapeDtypeStruct(m0.shape, F32)],
        scratch_shapes=[pltpu.VMEM((N_CHAINS, ML_QK, ML_V), F32), pltpu.VMEM((N_CHAINS, 1, ML_QK), F32),
                        pltpu.VMEM((N_CHAINS, 1, 1), F32)],
        compiler_params=_cparams("parallel", "arbitrary"),
        name="mlstm",
    )(ml, ml, gates, gates, c0, n0, m0)


def _gdconv_kernel(has_halo, *refs):
    if has_halo:
        x_ref, prev_ref, next_ref, w_ref, o_ref, xp_ref = refs
    else:
        x_ref, w_ref, o_ref, xp_ref = refs
    rows = x_ref.shape[1]
    nch = x_ref.shape[2]
    pad = 8
    zero = jnp.zeros((pad, nch), F32)
    if has_halo:
        c = pl.program_id(1)
        xp_ref[0:pad, :] = jnp.where(c > 0, prev_ref[0], zero)
        xp_ref[pad + rows:, :] = jnp.where(c < pl.num_programs(1) - 1, next_ref[0], zero)
    else:
        xp_ref[0:pad, :] = zero
        xp_ref[pad + rows:, :] = zero
    xp_ref[pad:pad + rows, :] = x_ref[0]
    half = CONV_W // 2
    for lc in range(nch // 128):
        sl = slice(lc * 128, (lc + 1) * 128)
        acc = None
        for j in range(CONV_W):
            term = xp_ref[pad - half + j:pad - half + j + rows, sl] * w_ref[j:j + 1, sl]
            acc = term if acc is None else acc + term
        y = _silu(acc)
        if lc < 2 * GD_HEADS:
            y = y * lax.rsqrt(jnp.sum(y * y, axis=-1, keepdims=True) + EPS)
        if lc < GD_HEADS:
            y = y * (GD_QK ** -0.5)
        o_ref[0, :, sl] = y


def _gdconv_ctx(qkv, conv_w):
    b, s, nch = qkv.shape
    return pl.pallas_call(
        functools.partial(_gdconv_kernel, False),
        grid=(b,),
        in_specs=[pl.BlockSpec((1, s, nch), lambda bi: (bi, 0, 0)), pl.BlockSpec((CONV_W, nch), lambda bi: (0, 0))],
        out_specs=pl.BlockSpec((1, s, nch), lambda bi: (bi, 0, 0)),
        out_shape=jax.ShapeDtypeStruct((b, s, nch), F32),
        scratch_shapes=[pltpu.VMEM((s + 16, nch), F32)],
        compiler_params=_cparams("parallel"),
        name="gdconv_ctx",
    )(qkv, conv_w)


def _gdconv_lat(view, conv_w):
    b, rows, wn = view.shape
    nch = wn // GRID_W
    rb = rows // 8
    return pl.pallas_call(
        functools.partial(_gdconv_kernel, True),
        grid=(b, GRID_W),
        in_specs=[pl.BlockSpec((1, rows, nch), lambda bi, c: (bi, 0, c)),
                  pl.BlockSpec((1, 8, nch), lambda bi, c: (bi, rb - 1, jnp.maximum(c - 1, 0))),
                  pl.BlockSpec((1, 8, nch), lambda bi, c: (bi, 0, jnp.minimum(c + 1, GRID_W - 1))),
                  pl.BlockSpec((CONV_W, nch), lambda bi, c: (0, 0))],
        out_specs=pl.BlockSpec((1, rows, nch), lambda bi, c: (bi, 0, c)),
        out_shape=jax.ShapeDtypeStruct(view.shape, F32),
        scratch_shapes=[pltpu.VMEM((rows + 16, nch), F32)],
        compiler_params=_cparams("parallel", "arbitrary"),
        name="gdconv_lat",
    )(view, view, view, conv_w)


SOLVE_BLOCK = 16


def _hi_lo(x):
    hi = x.astype(BF16).astype(F32)
    return hi, x - hi


def _dot_split(a, b):
    a_hi, a_lo = _hi_lo(a)
    b_hi, b_lo = _hi_lo(b)
    lhs = jnp.concatenate([a_hi, a_hi, a_lo], axis=1).astype(BF16)
    rhs = jnp.concatenate([b_hi, b_lo, b_hi], axis=0).astype(BF16)
    return jnp.dot(lhs, rhs, preferred_element_type=F32)


def _unit_triangular_inverses(ns):
    c = ns[0].shape[0]
    row = lax.broadcasted_iota(jnp.int32, (c, c), 0)
    col = lax.broadcasted_iota(jnp.int32, (c, c), 1)
    eye = (row == col).astype(F32)
    in_diag_block = (row // SOLVE_BLOCK) == (col // SOLVE_BLOCK)
    mm = lambda a_list, b_list: [_dot_split(a, b) for a, b in zip(a_list, b_list)]

    n_d = [jnp.where(in_diag_block, n, 0.0) for n in ns]
    x = n_d
    d_inv = [eye - n for n in n_d]
    for _ in range(SOLVE_BLOCK.bit_length() - 2):
        x = mm(x, x)
        d_inv = [d + dx for d, dx in zip(d_inv, mm(d_inv, x))]
    m = mm(d_inv, [n - nd for n, nd in zip(ns, n_d)])
    assert c // SOLVE_BLOCK == 4
    i_minus_m = [eye - mi for mi in m]
    q = [a + b for a, b in zip(i_minus_m, mm(i_minus_m, mm(m, m)))]
    return mm(q, d_inv)


def _gdn_kernel(qf_ref, qb_ref, gf_ref, gb_ref, na_ref, s0_ref, of_ref, ob_ref, sn_ref, s_scr):
    i = pl.program_id(1)

    @pl.when(i == 0)
    def _():
        s_scr[...] = s0_ref[0]

    eye = (lax.broadcasted_iota(jnp.int32, (CHUNK, CHUNK), 0)
           == lax.broadcasted_iota(jnp.int32, (CHUNK, CHUNK), 1)).astype(F32)
    nqk = GD_HEADS * GD_QK
    masks = [_past_mask(d == 1) for d in range(2)]
    gates = [r[0] for r in (gf_ref, gb_ref)]
    glog = [na_ref[...] * jax.nn.softplus(g) for g in gates]
    beta_all = [jax.nn.sigmoid(g) for g in gates]
    gcum = [_dot_hi(masks[d][0].astype(F32), glog[d]) for d in range(2)]
    gtot = [jnp.sum(g, axis=0, keepdims=True) for g in glog]
    gcum_t = [_transpose_hi(g) for g in gcum]

    chains = [(d, hd) for d in range(2) for hd in range(GD_HEADS)]
    x_refs, o_refs = (qf_ref, qb_ref), (of_ref, ob_ref)
    q, k, v, g_col, beta, g_end, decay = [], [], [], [], [], [], []
    for d, hd in chains:
        ca = GD_GATE0 + d * 8 + hd
        q.append(x_refs[d][0, :, hd * GD_QK:(hd + 1) * GD_QK])
        k.append(x_refs[d][0, :, nqk + hd * GD_QK:nqk + (hd + 1) * GD_QK])
        v.append(x_refs[d][0, :, 2 * nqk + hd * GD_V:2 * nqk + (hd + 1) * GD_V])
        g_col.append(gcum[d][:, ca:ca + 1])
        beta.append(beta_all[d][:, ca + GD_HEADS:ca + GD_HEADS + 1])
        g_end.append(gtot[d][:, ca:ca + 1])
        decay.append(jnp.exp(jnp.where(masks[d][0], g_col[-1] - gcum_t[d][ca:ca + 1, :], -jnp.inf)))
    nc = range(len(chains))
    kk = [_dot_nt(k[c], k[c]) for c in nc]
    xs = [jnp.where(masks[chains[c][0]][1], beta[c] * kk[c] * decay[c], 0.0) for c in nc]
    ps = _unit_triangular_inverses(xs)
    uw = [_dot_split(ps[c], jnp.concatenate([v[c] * beta[c], k[c] * (beta[c] * jnp.exp(g_col[c]))], axis=-1))
          for c in nc]
    qk = [_dot_nt(q[c], k[c]) * decay[c] for c in nc]
    s_st = [s_scr[c] for c in nc]
    v_new = [uw[c][:, :GD_V] - _dot(uw[c][:, GD_V:], s_st[c]) for c in nc]
    o_loc = [_dot(q[c] * jnp.exp(g_col[c]), s_st[c]) for c in nc]
    o_new = [o_loc[c] + _dot(qk[c], v_new[c]) for c in nc]
    s_new = [s_st[c] * jnp.exp(g_end[c]) + _dot_tn(k[c] * jnp.exp(g_end[c] - g_col[c]), v_new[c]) for c in nc]
    for c, (d, hd) in enumerate(chains):
        o_refs[d][0, :, hd * GD_V:(hd + 1) * GD_V] = o_new[c]
        s_scr[c] = s_new[c]

    @pl.when(i == pl.num_programs(1) - 1)
    def _():
        sn_ref[0] = s_scr[...]


def _gdn(qkv_view, gates_view, neg_a, s0, nc, idx_fn, out_view_shape):
    b = qkv_view.shape[0]
    nqkv = 2 * GD_HEADS * GD_QK + GD_HEADS * GD_V
    hdim = GD_HEADS * GD_V
    fwd = lambda n: pl.BlockSpec((1, CHUNK, n), lambda bi, i: idx_fn(bi, i))
    bwd = lambda n: pl.BlockSpec((1, CHUNK, n), lambda bi, i: idx_fn(bi, nc - 1 - i))
    st = pl.BlockSpec((1, N_CHAINS, GD_QK, GD_V), lambda bi, i: (bi, 0, 0, 0))
    return pl.pallas_call(
        _gdn_kernel,
        grid=(b, nc),
        in_specs=[fwd(nqkv), bwd(nqkv), fwd(GATE_LANES), bwd(GATE_LANES),
                  pl.BlockSpec((1, GATE_LANES), lambda bi, i: (0, 0)), st],
        out_specs=[fwd(hdim), bwd(hdim), st],
        out_shape=[jax.ShapeDtypeStruct(out_view_shape, F32), jax.ShapeDtypeStruct(out_view_shape, F32),
                   jax.ShapeDtypeStruct(s0.shape, F32)],
        scratch_shapes=[pltpu.VMEM((N_CHAINS, GD_QK, GD_V), F32)],
        compiler_params=_cparams("parallel", "arbitrary"),
        name="gdn",
    )(qkv_view, qkv_view, gates_view, gates_view, neg_a, s0)


def _head_rms(t, nheads, width):
    outs = []
    for hd in range(nheads):
        th = t[:, hd * width:(hd + 1) * width]
        outs.append(th * lax.rsqrt(jnp.mean(th * th, axis=-1, keepdims=True) + EPS))
    return jnp.concatenate(outs, axis=-1)


def _rms(t, w):
    return t * lax.rsqrt(jnp.mean(t * t, axis=-1, keepdims=True) + EPS) * w


def _post_kernel(x_ref, hf_ref, hb_ref, og_ref, of_ref, ob_ref, z_ref, mlw_ref, gdw_ref, wout_ref,
                 npost_ref, g2_ref, npre_ref, sh_ref, sc_ref, rwt_ref, wsg_ref, wsu_ref, wsd_ref,
                 x1_ref, hffn_ref, lt_ref, ys_ref, of_scr, ob_scr):
    ml_y = _head_rms(hf_ref[0] + hb_ref[0], ML_HEADS, ML_V) * mlw_ref[...] * jax.nn.sigmoid(og_ref[0])
    _from_grid_view(of_ref, of_scr)
    _from_grid_view(ob_ref, ob_scr)
    o_sum = jnp.concatenate([_pitched_rows(of_scr, g) + _pitched_rows(ob_scr, g) for g in range(of_scr.shape[0])],
                            axis=-1)
    gd_y = _head_rms(o_sum, GD_HEADS, GD_V) * gdw_ref[...] * _silu(z_ref[0])
    y = _dot(jnp.concatenate([ml_y, gd_y], axis=-1), wout_ref[...])
    x1 = x_ref[0] + g2_ref[0] * _rms(y, npost_ref[...])
    x1_ref[0] = x1
    hffn = _rms(x1, npre_ref[...]) * (1.0 + sc_ref[0]) + sh_ref[0]
    nct = hffn.shape[1] // LANES
    for c in range(nct):
        hffn_ref[0, pl.ds(c, hffn.shape[0], stride=nct), :] = hffn[:, c * LANES:(c + 1) * LANES]
    hb = hffn.astype(BF16)
    lt_ref[...] = lax.dot_general(rwt_ref[...], hb, (((1,), (1,)), ((), ())), preferred_element_type=F32)
    hs = _silu(jnp.dot(hb, wsg_ref[...], preferred_element_type=F32)) * jnp.dot(hb, wsu_ref[...],
                                                                                preferred_element_type=F32)
    ys_ref[0] = _dot(hs, wsd_ref[...])


def _post(x, hf, hb, ml, of, ob, gz, mlw, gdw, wout, npost, g2, npre, sh, sc, rwt, wsg, wsu, wsd, tm):
    b, s, d = x.shape
    nt = s // tm
    hw = ML_HEADS * ML_V
    og_blk = (2 * ML_HEADS * ML_QK + ML_HEADS * ML_V) // hw
    tok = lambda n: pl.BlockSpec((1, tm, n), lambda bi, i: (bi, i, 0))
    full = lambda shp: pl.BlockSpec(shp, lambda bi, i: (0,) * len(shp))
    mod = pl.BlockSpec((1, 1, d), lambda bi, i: (bi, 0, 0))
    ne = rwt.shape[0]
    ds = wsg.shape[1]
    gview = pl.BlockSpec((1, tm // GRID_W, GRID_W * hw), lambda bi, i: (bi, i, 0))
    return pl.pallas_call(
        _post_kernel,
        grid=(b, nt),
        in_specs=[tok(d), tok(hw), tok(hw), pl.BlockSpec((1, tm, hw), lambda bi, i: (bi, i, og_blk)),
                  gview, gview, tok(hw), full((1, hw)), full((1, hw)), full((d, d)),
                  full((1, d)), mod, full((1, d)), mod, mod, full((ne, d)), full((d, ds)), full((d, ds)),
                  full((ds, d))],
        out_specs=[tok(d), pl.BlockSpec((1, tm * (d // LANES), LANES), lambda bi, i: (bi, i, 0)),
                   pl.BlockSpec((ne, tm), lambda bi, i: (0, bi * nt + i)), tok(d)],
        out_shape=[jax.ShapeDtypeStruct((b, s, d), F32), jax.ShapeDtypeStruct((b, s * (d // LANES), LANES), F32),
                   jax.ShapeDtypeStruct((ne, b * s), F32), jax.ShapeDtypeStruct((b, s, d), F32)],
        scratch_shapes=[pltpu.VMEM((hw // LANES, tm // GRID_W * GRID_PITCH, LANES), F32)] * 2,
        compiler_params=_cparams("parallel", "arbitrary"),
        name="post",
    )(x, hf, hb, ml, of, ob, gz, mlw, gdw, wout, npost, g2, npre, sh, sc, rwt, wsg, wsu, wsd)


def _route_kernel(lt_ref, bias_ref, idx_ref, gate_ref):
    ne, tn = lt_ref.shape
    gsz = ne // N_GROUPS
    scores = jax.nn.sigmoid(lt_ref[...])
    sel = scores + bias_ref[...]
    neg = -jnp.inf
    sel3 = sel.reshape(N_GROUPS, gsz, tn)
    io3 = lax.broadcasted_iota(jnp.int32, sel3.shape, 1)
    top1 = jnp.max(sel3, axis=1, keepdims=True)
    first = jnp.min(jnp.where(sel3 == top1, io3, gsz), axis=1, keepdims=True)
    top2 = jnp.max(jnp.where(io3 == first, neg, sel3), axis=1, keepdims=True)
    grp = (top1 + top2).reshape(N_GROUPS, tn)
    iog = lax.broadcasted_iota(jnp.int32, grp.shape, 0)
    keep = jnp.zeros(grp.shape, jnp.bool_)
    for _ in range(TOPK_GROUPS):
        m = jnp.max(grp, axis=0, keepdims=True)
        pick = iog == jnp.min(jnp.where(grp == m, iog, N_GROUPS), axis=0, keepdims=True)
        keep = keep | pick
        grp = jnp.where(pick, neg, grp)
    cand = jnp.where(keep.reshape(N_GROUPS, 1, tn), sel3, neg).reshape(ne, tn)
    ioe = lax.broadcasted_iota(jnp.int32, cand.shape, 0)
    idxs, gates = [], []
    for _ in range(TOP_K):
        m = jnp.max(cand, axis=0, keepdims=True)
        e = jnp.min(jnp.where(cand == m, ioe, ne), axis=0, keepdims=True)
        pick = ioe == e
        idxs.append(e)
        gates.append(jnp.sum(jnp.where(pick, scores, 0.0), axis=0, keepdims=True))
        cand = jnp.where(pick, neg, cand)
    gate = jnp.concatenate(gates, axis=0)
    idx_ref[...] = jnp.concatenate(idxs, axis=0)
    gate_ref[...] = gate / jnp.sum(gate, axis=0, keepdims=True) * ROUTED_SCALE


def _route(logits_t, bias_col, tn):
    ne, t = logits_t.shape
    return pl.pallas_call(
        _route_kernel,
        grid=(t // tn,),
        in_specs=[pl.BlockSpec((ne, tn), lambda i: (0, i)), pl.BlockSpec((ne, 1), lambda i: (0, 0))],
        out_specs=[pl.BlockSpec((TOP_K, tn), lambda i: (0, i)), pl.BlockSpec((TOP_K, tn), lambda i: (0, i))],
        out_shape=[jax.ShapeDtypeStruct((TOP_K, t), jnp.int32), jax.ShapeDtypeStruct((TOP_K, t), F32)],
        compiler_params=_cparams("parallel"),
        name="route",
    )(logits_t, bias_col)


def _experts_kernel(be_ref, np_ref,
                    tok_ref, tokn_ref, slot_ref, w_ref, wg0_ref, wu0_ref, wd0_ref, wg1_ref, wu1_ref, wd1_ref, h_hbm,
                    slots_hbm, xg, ob, wgc, wuc, wdc, gsem, ssem):
    s = pl.program_id(0)
    n_pairs = np_ref[0]
    nct = xg.shape[1] // EXPERT_BLOCK
    rows = nct * EXPERT_BLOCK
    w_refs = ((wg0_ref, wu0_ref, wd0_ref), (wg1_ref, wu1_ref, wd1_ref))

    def gather_copy(tref, p, j):
        src = h_hbm.at[pl.ds(pl.multiple_of(tref[0, p, j] * nct, nct), nct)]
        return pltpu.make_async_copy(src, xg.at[p, pl.ds(j * nct, nct)], gsem.at[p])

    def scatter_copy(p, j):
        dst = slots_hbm.at[pl.ds(pl.multiple_of(slot_ref[0, p, j] * nct, nct), nct)]
        return pltpu.make_async_copy(ob.at[p, pl.ds(j * nct, nct)], dst, ssem.at[p])

    def gather_wait(p):
        pltpu.make_async_copy(h_hbm.at[pl.ds(0, rows)], xg.at[p], gsem.at[p]).wait()

    def scatter_wait(p):
        pltpu.make_async_copy(ob.at[p], slots_hbm.at[pl.ds(0, rows)], ssem.at[p]).wait()

    @pl.when(s < n_pairs)
    def _():
        @pl.when(s == 0)
        def _():
            ob[...] = jnp.zeros(ob.shape, F32)
            spare0 = slots_hbm.shape[0] - ob.shape[0] * rows
            for p in range(2):
                init = pltpu.make_async_copy(ob.at[p], slots_hbm.at[pl.ds(spare0 + p * rows, rows)], ssem.at[p])
                init.start()
                init.wait()
            for p in range(2):
                for j in range(EXPERT_BLOCK):
                    gather_copy(tok_ref, p, j).start(priority=j % 2)

        for p in range(2):
            blk = 2 * s + p
            wg_ref, wu_ref, wd_ref = w_refs[p]
            gather_wait(p)

            @pl.when(s >= 1)
            def _():
                scatter_wait(p)

            @pl.when((blk == 0) | (be_ref[blk] != be_ref[jnp.maximum(blk - 1, 0)]))
            def _():
                wgc[...] = wg_ref[0].astype(BF16)
                wuc[...] = wu_ref[0].astype(BF16)
                wdc[...] = wd_ref[0].astype(BF16)

            xb = jnp.concatenate([xg[p, pl.ds(c, EXPERT_BLOCK, stride=nct), :] for c in range(nct)],
                                 axis=-1).astype(BF16)
            hmid = _silu(jnp.dot(xb, wgc[...], preferred_element_type=F32)) * jnp.dot(xb, wuc[...],
                                                                                      preferred_element_type=F32)
            out = _dot(hmid, wdc[...])
            eye = (lax.broadcasted_iota(jnp.int32, (EXPERT_BLOCK, EXPERT_BLOCK), 0)
                   == lax.broadcasted_iota(jnp.int32, (EXPERT_BLOCK, EXPERT_BLOCK), 1))
            w_col = jnp.sum(jnp.where(eye, w_ref[0, p:p + 1, :], 0.0), axis=1, keepdims=True)
            out = out * w_col
            for c in range(nct):
                ob[p, pl.ds(c, EXPERT_BLOCK, stride=nct), :] = out[:, c * LANES:(c + 1) * LANES]

            for j in range(EXPERT_BLOCK):
                gather_copy(tokn_ref, p, j).start(priority=j % 2)
                scatter_copy(p, j).start(priority=(j + 1) % 2)

        @pl.when(s == n_pairs - 1)
        def _():
            for p in range(2):
                gather_wait(p)
                scatter_wait(p)


def _experts(hffn, block_e, n_pairs, row_tok, row_slot, row_w, wg, wu, wd, n_slots):
    d = wg.shape[1]
    nct = d // LANES
    npairs = row_tok.shape[0]
    de = wg.shape[2]
    last = npairs - 1
    smem_blk = lambda f: pl.BlockSpec((1, 2, EXPERT_BLOCK), f, memory_space=pltpu.SMEM)
    wspec = lambda shp, p: pl.BlockSpec((1,) + shp, lambda s, be, npu: (be[2 * s + p], 0, 0))
    grid_spec = pltpu.PrefetchScalarGridSpec(
        num_scalar_prefetch=2,
        grid=(npairs,),
        in_specs=[smem_blk(lambda s, be, npu: (s, 0, 0)),
                  smem_blk(lambda s, be, npu: (jnp.minimum(s + 1, last), 0, 0)),
                  smem_blk(lambda s, be, npu: (s, 0, 0)),
                  pl.BlockSpec((1, 2, EXPERT_BLOCK), lambda s, be, npu: (s, 0, 0)),
                  wspec((d, de), 0), wspec((d, de), 0), wspec((de, d), 0),
                  wspec((d, de), 1), wspec((d, de), 1), wspec((de, d), 1),
                  pl.BlockSpec(memory_space=pl.ANY)],
        out_specs=pl.BlockSpec(memory_space=pl.ANY),
        scratch_shapes=[pltpu.VMEM((2, EXPERT_BLOCK * nct, LANES), F32), pltpu.VMEM((2, EXPERT_BLOCK * nct, LANES), F32),
                        pltpu.VMEM((d, de), BF16), pltpu.VMEM((d, de), BF16), pltpu.VMEM((de, d), BF16),
                        pltpu.SemaphoreType.DMA((2,)), pltpu.SemaphoreType.DMA((2,))],
    )
    return pl.pallas_call(
        _experts_kernel,
        grid_spec=grid_spec,
        out_shape=jax.ShapeDtypeStruct(((n_slots + 2 * EXPERT_BLOCK) * nct, LANES), F32),
        compiler_params=_cparams("arbitrary"),
        name="experts",
    )(block_e, n_pairs, row_tok, row_tok, row_slot, row_w, wg, wu, wd, wg, wu, wd, hffn)


def _combine_kernel(x1_ref, ys_ref, *refs):
    slot_refs, (npost_ref, g5_ref, o_ref) = refs[:TOP_K], refs[TOP_K:]
    tm = x1_ref.shape[1]
    nct = slot_refs[0].shape[0] // tm
    routed = []
    for c in range(nct):
        acc = slot_refs[0][pl.ds(c, tm, stride=nct), :]
        for k in range(1, TOP_K):
            acc = acc + slot_refs[k][pl.ds(c, tm, stride=nct), :]
        routed.append(acc)
    y = ys_ref[0] + jnp.concatenate(routed, axis=-1)
    o_ref[0] = x1_ref[0] + g5_ref[0] * _rms(y, npost_ref[...])


def _combine(x1, ys, slots, npost, g5, tm):
    b, s, d = x1.shape
    nt = s // tm
    nct = d // LANES
    tok = pl.BlockSpec((1, tm, d), lambda bi, i: (bi, i, 0))
    slot = lambda k: pl.BlockSpec((tm * nct, LANES), lambda bi, i: (k * b * nt + bi * nt + i, 0))
    return pl.pallas_call(
        _combine_kernel,
        grid=(b, nt),
        in_specs=[tok, tok] + [slot(k) for k in range(TOP_K)]
                 + [pl.BlockSpec((1, d), lambda bi, i: (0, 0)), pl.BlockSpec((1, 1, d), lambda bi, i: (bi, 0, 0))],
        out_specs=tok,
        out_shape=jax.ShapeDtypeStruct((b, s, d), F32),
        compiler_params=_cparams("parallel", "arbitrary"),
        name="combine",
    )(x1, ys, *([slots] * TOP_K), npost, g5)


def _dispatch_plan(idx_t, gate_t):
    k, t = idx_t.shape
    n_asg = k * t
    nb = n_asg // EXPERT_BLOCK + N_EXPERTS
    flat_e = idx_t.reshape(-1)
    order = jnp.argsort(flat_e).astype(jnp.int32)
    counts = jnp.zeros((N_EXPERTS,), jnp.int32).at[flat_e].add(1)
    padded = (counts + EXPERT_BLOCK - 1) // EXPERT_BLOCK * EXPERT_BLOCK
    start = jnp.cumsum(counts) - counts
    pend = jnp.cumsum(padded)
    pstart = pend - padded
    blk0 = jnp.arange(nb, dtype=jnp.int32) * EXPERT_BLOCK
    block_e = jnp.minimum(jnp.searchsorted(pend, blk0, side='right'), N_EXPERTS - 1).astype(jnp.int32)
    assert nb % 2 == 0
    n_pairs = ((pend[-1] // EXPERT_BLOCK + 1) // 2).astype(jnp.int32).reshape(1)
    pos = blk0[:, None] - pstart[block_e][:, None] + jnp.arange(EXPERT_BLOCK, dtype=jnp.int32)[None, :]
    valid = pos < counts[block_e][:, None]
    src = jnp.clip(start[block_e][:, None] + pos, 0, n_asg - 1)
    asg = order[src]
    row_tok = jnp.where(valid, asg % t, 0).astype(jnp.int32)
    lane = jnp.arange(EXPERT_BLOCK, dtype=jnp.int32)[None, :]
    spare = n_asg + (jnp.arange(nb, dtype=jnp.int32)[:, None] % 2) * EXPERT_BLOCK + lane
    row_slot = jnp.where(valid, asg, spare).astype(jnp.int32)
    row_w = jnp.where(valid, gate_t.reshape(-1)[asg], 0.0).astype(F32)
    shp = (nb // 2, 2, EXPERT_BLOCK)
    return block_e, n_pairs, row_tok.reshape(shp), row_slot.reshape(shp), row_w.reshape(shp)


def _pack_in_weights(w_in, ml_i_bias, ml_f_bias, gd_dt_bias):
    d = w_in.shape[0]
    nml = 2 * ML_HEADS * ML_QK + 2 * ML_HEADS * ML_V
    ml_cols = nml + 4 * ML_HEADS
    ngq = GD_HEADS * (2 * GD_QK + GD_V)
    ngz = GD_HEADS * GD_V
    wml = w_in[:, :nml].astype(BF16)
    wgq = w_in[:, ml_cols:ml_cols + ngq].astype(BF16)
    wgz = w_in[:, ml_cols + ngq:ml_cols + ngq + ngz].astype(BF16)
    wg = jnp.zeros((d, GATE_LANES), F32)
    wg = wg.at[:, ML_GATE0:ML_GATE0 + 16].set(w_in[:, nml:ml_cols])
    wg = wg.at[:, GD_GATE0:GD_GATE0 + 16].set(w_in[:, ml_cols + ngq + ngz:])
    gb = jnp.zeros((GATE_LANES,), F32)
    gb = gb.at[ML_GATE0:ML_GATE0 + 16].set(jnp.stack([ml_i_bias, ml_f_bias], axis=1).reshape(-1))
    gb = gb.at[GD_GATE0:GD_GATE0 + 16].set(jnp.stack([gd_dt_bias, jnp.zeros_like(gd_dt_bias)], axis=1).reshape(-1))
    return wml, wgq, wgz, wg.astype(BF16), gb.reshape(1, GATE_LANES)


def _mixer(x, ctx, mod, mod_ctx, norm_pre_mix, w_in, ml_i_bias, ml_f_bias, gd_conv_w, gd_a_log, gd_dt_bias):
    b, s, d = x.shape
    sc = ctx.shape[1]
    wml, wgq, wgz, wg, gb = _pack_in_weights(w_in, ml_i_bias, ml_f_bias, gd_dt_bias)
    nw = norm_pre_mix.reshape(1, d)
    ctx_mod = lambda j: jnp.broadcast_to(mod_ctx[j].reshape(1, 1, d), (b, 1, d))
    ml_c, gq_c, _, g_c = _proj(ctx, nw, ctx_mod(0), ctx_mod(1), wml, wgq, wgz, wg, gb, tm=sc, grid_view=False)
    ml_l, gqv_l, gz_l, g_l, gv_l = _proj(x, nw, mod[0], mod[1], wml, wgq, wgz, wg, gb, tm=512, grid_view=True)

    c0 = jnp.zeros((b, N_CHAINS, ML_QK, ML_V), F32)
    n0 = jnp.zeros((b, N_CHAINS, 1, ML_QK), F32)
    m0 = jnp.zeros((b, N_CHAINS, 1, 1), F32)
    _, _, c1, n1, m1 = _mlstm(ml_c, g_c, c0, n0, m0)
    hf, hb, _, _, _ = _mlstm(ml_l, g_l, c1, n1, m1)

    neg_a = jnp.zeros((GATE_LANES,), F32)
    neg_a = neg_a.at[GD_GATE0:GD_GATE0 + 16].set(
        jnp.stack([-jnp.exp(gd_a_log), jnp.zeros_like(gd_a_log)], axis=1).reshape(-1)).reshape(1, GATE_LANES)
    qn_c = _gdconv_ctx(gq_c, gd_conv_w)
    qnv_l = _gdconv_lat(gqv_l, gd_conv_w)
    s0 = jnp.zeros((b, N_CHAINS, GD_QK, GD_V), F32)
    hdim = GD_HEADS * GD_V
    _, _, s1 = _gdn(qn_c, g_c, neg_a, s0, sc // CHUNK, lambda bi, n: (bi, n, 0), (b, sc, hdim))
    rows = s // GRID_W
    cpc = rows // CHUNK
    col_idx = lambda bi, n: (bi, n % cpc, n // cpc)
    ofv, obv, _ = _gdn(qnv_l, gv_l, neg_a, s1, s // CHUNK, col_idx, (b, rows, GRID_W * hdim))
    return hf, hb, ml_l, ofv, obv, gz_l


def kernel(x, c, ctx, c_ctx, w_ada, b_ada, norm_pre_mix, norm_post_mix, norm_pre_ffn, norm_post_ffn, w_in,
           ml_i_bias, ml_f_bias, ml_norm_w, gd_conv_w, gd_a_log, gd_dt_bias, gd_norm_w, w_out, router_w,
           router_bias, w_gate, w_up, w_down, ws_gate, ws_up, ws_down):
    b, s, d = x.shape
    depth = w_ada.shape[0]
    assert depth == 1, "the context stream update of deeper stacks is not implemented"
    ly = 0
    cc = jnp.zeros((16, d), F32).at[:b].set(c).at[b].set(c_ctx)
    mod_all = _ada(cc, w_ada[ly], b_ada[ly])
    mod = [mod_all[:b, j * d:(j + 1) * d].reshape(b, 1, d) for j in range(6)]
    mod_ctx = [mod_all[b, j * d:(j + 1) * d] for j in range(6)]

    hf, hb, ml_l, of, ob, gz_l = _mixer(x, ctx, mod, mod_ctx, norm_pre_mix[ly], w_in[ly], ml_i_bias[ly],
                                        ml_f_bias[ly], gd_conv_w[ly], gd_a_log[ly], gd_dt_bias[ly])

    row = lambda v: v.reshape(1, -1)
    x1, hffn, logits_t, ys = _post(
        x, hf, hb, ml_l, of, ob, gz_l, row(ml_norm_w[ly]), row(jnp.tile(gd_norm_w[ly], GD_HEADS)),
        w_out[ly].astype(BF16), row(norm_post_mix[ly]), mod[2], row(norm_pre_ffn[ly]), mod[3], mod[4],
        router_w[ly].T.astype(BF16), ws_gate[ly].astype(BF16), ws_up[ly].astype(BF16), ws_down[ly].astype(BF16),
        tm=512)

    idx_t, gate_t = _route(logits_t, router_bias[ly].reshape(-1, 1), tn=512)
    block_e, n_pairs, row_tok, row_slot, row_w = _dispatch_plan(idx_t, gate_t)
    t = b * s
    slots = _experts(hffn.reshape(t * (d // LANES), LANES), block_e, n_pairs, row_tok, row_slot, row_w,
                     w_gate[ly], w_up[ly], w_down[ly], TOP_K * t)
    return _combine(x1, ys, slots, row(norm_post_ffn[ly]), mod[5], tm=256)
```

```python
import functools

import jax
import jax.numpy as jnp
from jax import lax
from jax.experimental import pallas as pl
from jax.experimental.pallas import tpu as pltpu

EPS = 1e-6
CHUNK = 64
GRID_W = 64
ML_HEADS, ML_QK, ML_V = 4, 64, 128
GD_HEADS, GD_QK, GD_V = 4, 128, 128
CONV_W = 5
N_EXPERTS, TOP_K, N_GROUPS, TOPK_GROUPS = 256, 8, 8, 4
ROUTED_SCALE = 2.5
EXPERT_BLOCK = 128
N_CHAINS = 8
LANES = 128
GATE_LANES = LANES
ML_GATE0, GD_GATE0 = 0, 16

F32 = jnp.float32
BF16 = jnp.bfloat16
HI = lax.Precision.HIGHEST
VMEM_LIMIT = 56 * 1024 * 1024


def _cparams(*sem):
    return pltpu.CompilerParams(dimension_semantics=sem, vmem_limit_bytes=VMEM_LIMIT)


def _dot(a, b):
    return jnp.dot(a.astype(BF16), b.astype(BF16), preferred_element_type=F32)


def _dot_nt(a, b):
    return lax.dot_general(a.astype(BF16), b.astype(BF16), (((1,), (1,)), ((), ())), preferred_element_type=F32)


def _dot_tn(a, b):
    return lax.dot_general(a.astype(BF16), b.astype(BF16), (((0,), (0,)), ((), ())), preferred_element_type=F32)


def _dot_hi(a, b):
    return jnp.dot(a, b, precision=HI, preferred_element_type=F32)


def _dot_nt_hi(a, b):
    return lax.dot_general(a, b, (((1,), (1,)), ((), ())), precision=HI, preferred_element_type=F32)


def _transpose_hi(x):
    n = x.shape[1]
    eye = (lax.broadcasted_iota(jnp.int32, (n, n), 0) == lax.broadcasted_iota(jnp.int32, (n, n), 1)).astype(F32)
    return _dot_nt_hi(eye, x)


def _silu(x):
    return x * jax.nn.sigmoid(x)


def _past_mask(reverse):
    t = lax.broadcasted_iota(jnp.int32, (CHUNK, CHUNK), 0)
    s = lax.broadcasted_iota(jnp.int32, (CHUNK, CHUNK), 1)
    return (s >= t, s > t) if reverse else (s <= t, s < t)


def _ada_kernel(c_ref, w_ref, b_ref, o_ref):
    o_ref[...] = _dot(_silu(c_ref[...]), w_ref[...]) + b_ref[...]


def _ada(cc, w_ada, b_ada):
    rows, d = cc.shape
    n = w_ada.shape[1]
    tn = 1536
    return pl.pallas_call(
        _ada_kernel,
        grid=(n // tn,),
        in_specs=[pl.BlockSpec((rows, d), lambda j: (0, 0)),
                  pl.BlockSpec((d, tn), lambda j: (0, j)),
                  pl.BlockSpec((1, tn), lambda j: (0, j))],
        out_specs=pl.BlockSpec((rows, tn), lambda j: (0, j)),
        out_shape=jax.ShapeDtypeStruct((rows, n), F32),
        compiler_params=_cparams("arbitrary"),
        name="ada",
    )(cc, w_ada, b_ada.reshape(1, n))


GRID_PITCH = GRID_W + 8


def _to_grid_view(src_ref, dst_ref):
    ng = src_ref.shape[0]
    r = src_ref.shape[1] // GRID_PITCH
    for c in range(GRID_W):
        for g in range(ng):
            lo = (c * ng + g) * LANES
            dst_ref[0, :, lo:lo + LANES] = src_ref[g, pl.ds(c, r, stride=GRID_PITCH), :]


def _from_grid_view(src_ref, dst_ref):
    ng = dst_ref.shape[0]
    r = dst_ref.shape[1] // GRID_PITCH
    for c in range(GRID_W):
        for g in range(ng):
            lo = (c * ng + g) * LANES
            dst_ref[g, pl.ds(c, r, stride=GRID_PITCH), :] = src_ref[0, :, lo:lo + LANES]


def _pitched_rows(ref, g):
    r = ref.shape[1] // GRID_PITCH
    return jnp.concatenate([ref[g, i * GRID_PITCH:i * GRID_PITCH + GRID_W, :] for i in range(r)], axis=0)


def _proj_kernel(grid_view, x_ref, nw_ref, sh_ref, sc_ref, wml_ref, wgq_ref, wgz_ref, wg_ref, gb_ref, *refs):
    if grid_view:
        ml_ref, gqv_ref, gz_ref, g_ref, gv_ref, gq_scr, g_scr = refs
    else:
        ml_ref, gq_ref, gz_ref, g_ref = refs
    x = x_ref[0]
    xn = x * lax.rsqrt(jnp.mean(x * x, axis=-1, keepdims=True) + EPS) * nw_ref[...]
    h = (xn * (1.0 + sc_ref[0]) + sh_ref[0]).astype(BF16)
    ml_ref[0] = jnp.dot(h, wml_ref[...], preferred_element_type=F32)
    gz_ref[0] = jnp.dot(h, wgz_ref[...], preferred_element_type=F32)
    gates = jnp.dot(h, wg_ref[...], preferred_element_type=F32) + gb_ref[...]
    g_ref[0] = gates
    gq = jnp.dot(h, wgq_ref[...], preferred_element_type=F32)
    if grid_view:
        for r in range(x.shape[0] // GRID_W):
            rows = slice(r * GRID_W, (r + 1) * GRID_W)
            prow = slice(r * GRID_PITCH, r * GRID_PITCH + GRID_W)
            g_scr[0, prow, :] = gates[rows]
            for g in range(gq_scr.shape[0]):
                gq_scr[g, prow, :] = gq[rows, g * LANES:(g + 1) * LANES]
        _to_grid_view(gq_scr, gqv_ref)
        _to_grid_view(g_scr, gv_ref)
    else:
        gq_ref[0] = gq


def _proj(x, norm_w, shift, scale, wml, wgq, wgz, wg, gbias, tm, grid_view):
    b, s, d = x.shape
    nml, ngq, ngz = wml.shape[1], wgq.shape[1], wgz.shape[1]
    full = lambda shp: pl.BlockSpec(shp, lambda bi, i: (0,) * len(shp))
    tok = lambda n: pl.BlockSpec((1, tm, n), lambda bi, i: (bi, i, 0))
    mod = pl.BlockSpec((1, 1, d), lambda bi, i: (bi, 0, 0))
    if grid_view:
        rt = tm // GRID_W
        view = lambda n: pl.BlockSpec((1, rt, GRID_W * n), lambda bi, i: (bi, i, 0))
        vshape = lambda n: jax.ShapeDtypeStruct((b, s // GRID_W, GRID_W * n), F32)
        out_specs = [tok(nml), view(ngq), tok(ngz), tok(GATE_LANES), view(GATE_LANES)]
        out_shape = [jax.ShapeDtypeStruct((b, s, nml), F32), vshape(ngq), jax.ShapeDtypeStruct((b, s, ngz), F32),
                     jax.ShapeDtypeStruct((b, s, GATE_LANES), F32), vshape(GATE_LANES)]
        scratch = [pltpu.VMEM((ngq // LANES, rt * GRID_PITCH, LANES), F32), pltpu.VMEM((1, rt * GRID_PITCH, LANES), F32)]
    else:
        out_specs = [tok(nml), tok(ngq), tok(ngz), tok(GATE_LANES)]
        out_shape = [jax.ShapeDtypeStruct((b, s, n), F32) for n in (nml, ngq, ngz, GATE_LANES)]
        scratch = []
    return pl.pallas_call(
        functools.partial(_proj_kernel, grid_view),
        grid=(b, s // tm),
        in_specs=[tok(d), full((1, d)), mod, mod, full((d, nml)), full((d, ngq)), full((d, ngz)),
                  full((d, GATE_LANES)), full((1, GATE_LANES))],
        out_specs=out_specs,
        out_shape=out_shape,
        scratch_shapes=scratch,
        compiler_params=_cparams("parallel", "arbitrary"),
        name="proj",
    )(x, norm_w, shift, scale, wml, wgq, wgz, wg, gbias)


def _mlstm_kernel(mlf_ref, mlb_ref, gf_ref, gb_ref, c0_ref, m0_ref, hf_ref, hb_ref, cn_ref, mn_ref, c_scr, m_scr):
    i = pl.program_id(1)

    @pl.when(i == 0)
    def _():
        c_scr[...] = c0_ref[0]
        m_scr[...] = m0_ref[0]

    past = [_past_mask(d == 1)[0] for d in range(2)]
    g = [r[0] for r in (gf_ref, gb_ref)]
    ls = [jax.nn.log_sigmoid(x) for x in g]
    bcol = [_dot_hi(past[d].astype(F32), ls[d]) for d in range(2)]
    tot = [jnp.sum(x, axis=0, keepdims=True) for x in ls]
    g_t = [_transpose_hi(x) for x in g]
    b_t = [_transpose_hi(x) for x in bcol]

    chains = [(d, hd) for d in range(2) for hd in range(ML_HEADS)]
    nc = range(len(chains))
    ml_refs, h_refs = (mlf_ref, mlb_ref), (hf_ref, hb_ref)
    k0, v0 = ML_HEADS * ML_QK, 2 * ML_HEADS * ML_QK
    ones_col = (lax.broadcasted_iota(jnp.int32, (CHUNK, ML_V), 1) == 0).astype(F32)
    q, k, v, i_col, b_col, b_end, log_d = [], [], [], [], [], [], []
    for d, hd in chains:
        ci = ML_GATE0 + d * 8 + hd
        cf = ci + ML_HEADS
        q.append(ml_refs[d][0, :, hd * ML_QK:(hd + 1) * ML_QK])
        k.append(ml_refs[d][0, :, k0 + hd * ML_QK:k0 + (hd + 1) * ML_QK] * (ML_QK ** -0.5))
        v.append(jnp.concatenate([ml_refs[d][0, :, v0 + hd * ML_V:v0 + (hd + 1) * ML_V], ones_col], axis=-1))
        i_col.append(g[d][:, ci:ci + 1])
        b_col.append(bcol[d][:, cf:cf + 1])
        b_end.append(tot[d][:, cf:cf + 1])
        log_d.append(jnp.where(past[d], b_col[-1] - b_t[d][cf:cf + 1, :] + g_t[d][ci:ci + 1, :], -jnp.inf))
    c_st = [c_scr[c] for c in nc]
    m_st = [m_scr[c] for c in nc]
    log_prev = [b_col[c] + m_st[c] for c in nc]
    m_t = [jnp.maximum(log_prev[c], jnp.max(log_d[c], axis=-1, keepdims=True)) for c in nc]
    qk = [_dot_nt(q[c], k[c]) for c in nc]
    qc = [_dot(q[c], c_st[c]) for c in nc]
    s = [qk[c] * jnp.exp(log_d[c] - m_t[c]) for c in nc]
    w_prev = [jnp.exp(log_prev[c] - m_t[c]) for c in nc]
    sv = [_dot(s[c], v[c]) for c in nc]
    log_s = [b_end[c] - b_col[c] + i_col[c] for c in nc]
    m_new = [jnp.maximum(b_end[c] + m_st[c], jnp.max(log_s[c], axis=0, keepdims=True)) for c in nc]
    kw = [k[c] * jnp.exp(log_s[c] - m_new[c]) for c in nc]
    w_c = [jnp.exp(b_end[c] + m_st[c] - m_new[c]) for c in nc]
    kv = [_dot_tn(kw[c], v[c]) for c in nc]
    numden = [sv[c] + w_prev[c] * qc[c] for c in nc]
    scale = [1.0 / jnp.maximum(jnp.abs(numden[c][:, ML_V:ML_V + 1]), jnp.exp(-m_t[c])) for c in nc]
    for c, (d, hd) in enumerate(chains):
        h_refs[d][0, :, hd * ML_V:(hd + 1) * ML_V] = numden[c][:, :ML_V] * scale[c]
        c_scr[c] = w_c[c] * c_st[c] + kv[c]
        m_scr[c] = m_new[c]

    @pl.when(i == pl.num_programs(1) - 1)
    def _():
        cn_ref[0] = c_scr[...]
        mn_ref[0] = m_scr[...]


def _mlstm(ml, gates, c0, m0):
    b, s, nml = ml.shape
    nc = s // CHUNK
    fwd = lambda n: pl.BlockSpec((1, CHUNK, n), lambda bi, i: (bi, i, 0))
    bwd = lambda n: pl.BlockSpec((1, CHUNK, n), lambda bi, i: (bi, nc - 1 - i, 0))
    st = lambda shp: pl.BlockSpec((1,) + shp, lambda bi, i: (bi,) + (0,) * len(shp))
    hdim = ML_HEADS * ML_V
    cshape = (N_CHAINS, ML_QK, 2 * ML_V)
    return pl.pallas_call(
        _mlstm_kernel,
        grid=(b, nc),
        in_specs=[fwd(nml), bwd(nml), fwd(GATE_LANES), bwd(GATE_LANES), st(cshape), st((N_CHAINS, 1, 1))],
        out_specs=[fwd(hdim), bwd(hdim), st(cshape), st((N_CHAINS, 1, 1))],
        out_shape=[jax.ShapeDtypeStruct((b, s, hdim), F32), jax.ShapeDtypeStruct((b, s, hdim), F32),
                   jax.ShapeDtypeStruct(c0.shape, F32), jax.ShapeDtypeStruct(m0.shape, F32)],
        scratch_shapes=[pltpu.VMEM(cshape, F32), pltpu.VMEM((N_CHAINS, 1, 1), F32)],
        compiler_params=_cparams("parallel", "arbitrary"),
        name="mlstm",
    )(ml, ml, gates, gates, c0, m0)


def _gdconv_kernel(has_halo, *refs):
    if has_halo:
        x_ref, prev_ref, next_ref, w_ref, o_ref, xp_ref = refs
    else:
        x_ref, w_ref, o_ref, xp_ref = refs
    rows = x_ref.shape[1]
    nch = x_ref.shape[2]
    pad = 8
    zero = jnp.zeros((pad, nch), F32)
    if has_halo:
        c = pl.program_id(1)
        xp_ref[0:pad, :] = jnp.where(c > 0, prev_ref[0], zero)
        xp_ref[pad + rows:, :] = jnp.where(c < pl.num_programs(1) - 1, next_ref[0], zero)
    else:
        xp_ref[0:pad, :] = zero
        xp_ref[pad + rows:, :] = zero
    xp_ref[pad:pad + rows, :] = x_ref[0]
    half = CONV_W // 2
    for lc in range(nch // 128):
        sl = slice(lc * 128, (lc + 1) * 128)
        acc = None
        for j in range(CONV_W):
            term = xp_ref[pad - half + j:pad - half + j + rows, sl] * w_ref[j:j + 1, sl]
            acc = term if acc is None else acc + term
        y = _silu(acc)
        if lc < 2 * GD_HEADS:
            y = y * lax.rsqrt(jnp.sum(y * y, axis=-1, keepdims=True) + EPS)
        if lc < GD_HEADS:
            y = y * (GD_QK ** -0.5)
        o_ref[0, :, sl] = y


def _gdconv_ctx(qkv, conv_w):
    b, s, nch = qkv.shape
    return pl.pallas_call(
        functools.partial(_gdconv_kernel, False),
        grid=(b,),
        in_specs=[pl.BlockSpec((1, s, nch), lambda bi: (bi, 0, 0)), pl.BlockSpec((CONV_W, nch), lambda bi: (0, 0))],
        out_specs=pl.BlockSpec((1, s, nch), lambda bi: (bi, 0, 0)),
        out_shape=jax.ShapeDtypeStruct((b, s, nch), F32),
        scratch_shapes=[pltpu.VMEM((s + 16, nch), F32)],
        compiler_params=_cparams("parallel"),
        name="gdconv_ctx",
    )(qkv, conv_w)


def _gdconv_lat(view, conv_w):
    b, rows, wn = view.shape
    nch = wn // GRID_W
    rb = rows // 8
    return pl.pallas_call(
        functools.partial(_gdconv_kernel, True),
        grid=(b, GRID_W),
        in_specs=[pl.BlockSpec((1, rows, nch), lambda bi, c: (bi, 0, c)),
                  pl.BlockSpec((1, 8, nch), lambda bi, c: (bi, rb - 1, jnp.maximum(c - 1, 0))),
                  pl.BlockSpec((1, 8, nch), lambda bi, c: (bi, 0, jnp.minimum(c + 1, GRID_W - 1))),
                  pl.BlockSpec((CONV_W, nch), lambda bi, c: (0, 0))],
        out_specs=pl.BlockSpec((1, rows, nch), lambda bi, c: (bi, 0, c)),
        out_shape=jax.ShapeDtypeStruct(view.shape, F32),
        scratch_shapes=[pltpu.VMEM((rows + 16, nch), F32)],
        compiler_params=_cparams("parallel", "arbitrary"),
        name="gdconv_lat",
    )(view, view, view, conv_w)


SOLVE_BLOCK = 16


def _hi_lo(x):
    hi = x.astype(BF16).astype(F32)
    return hi, x - hi


def _dot_split(a, b):
    a_hi, a_lo = _hi_lo(a)
    b_hi, b_lo = _hi_lo(b)
    lhs = jnp.concatenate([a_hi, a_hi, a_lo], axis=1).astype(BF16)
    rhs = jnp.concatenate([b_hi, b_lo, b_hi], axis=0).astype(BF16)
    return jnp.dot(lhs, rhs, preferred_element_type=F32)


def _unit_triangular_inverses(ns):
    c = ns[0].shape[0]
    row = lax.broadcasted_iota(jnp.int32, (c, c), 0)
    col = lax.broadcasted_iota(jnp.int32, (c, c), 1)
    eye = (row == col).astype(F32)
    in_diag_block = (row // SOLVE_BLOCK) == (col // SOLVE_BLOCK)
    mm = lambda a_list, b_list: [_dot_split(a, b) for a, b in zip(a_list, b_list)]

    n_d = [jnp.where(in_diag_block, n, 0.0) for n in ns]
    x = n_d
    d_inv = [eye - n for n in n_d]
    for _ in range(SOLVE_BLOCK.bit_length() - 2):
        x = mm(x, x)
        d_inv = [d + dx for d, dx in zip(d_inv, mm(d_inv, x))]
    m = mm(d_inv, [n - nd for n, nd in zip(ns, n_d)])
    assert c // SOLVE_BLOCK == 4
    i_minus_m = [eye - mi for mi in m]
    q = [a + b for a, b in zip(i_minus_m, mm(i_minus_m, mm(m, m)))]
    return mm(q, d_inv)


def _gdn_kernel(qf_ref, qb_ref, gf_ref, gb_ref, na_ref, s0_ref, of_ref, ob_ref, sn_ref, s_scr):
    i = pl.program_id(1)

    @pl.when(i == 0)
    def _():
        s_scr[...] = s0_ref[0]

    eye = (lax.broadcasted_iota(jnp.int32, (CHUNK, CHUNK), 0)
           == lax.broadcasted_iota(jnp.int32, (CHUNK, CHUNK), 1)).astype(F32)
    nqk = GD_HEADS * GD_QK
    masks = [_past_mask(d == 1) for d in range(2)]
    gates = [r[0] for r in (gf_ref, gb_ref)]
    glog = [na_ref[...] * jax.nn.softplus(g) for g in gates]
    beta_all = [jax.nn.sigmoid(g) for g in gates]
    gcum = [_dot_hi(masks[d][0].astype(F32), glog[d]) for d in range(2)]
    gtot = [jnp.sum(g, axis=0, keepdims=True) for g in glog]
    gcum_t = [_transpose_hi(g) for g in gcum]

    chains = [(d, hd) for d in range(2) for hd in range(GD_HEADS)]
    x_refs, o_refs = (qf_ref, qb_ref), (of_ref, ob_ref)
    q, k, v, g_col, beta, g_end, decay = [], [], [], [], [], [], []
    for d, hd in chains:
        ca = GD_GATE0 + d * 8 + hd
        q.append(x_refs[d][0, :, hd * GD_QK:(hd + 1) * GD_QK])
        k.append(x_refs[d][0, :, nqk + hd * GD_QK:nqk + (hd + 1) * GD_QK])
        v.append(x_refs[d][0, :, 2 * nqk + hd * GD_V:2 * nqk + (hd + 1) * GD_V])
        g_col.append(gcum[d][:, ca:ca + 1])
        beta.append(beta_all[d][:, ca + GD_HEADS:ca + GD_HEADS + 1])
        g_end.append(gtot[d][:, ca:ca + 1])
        decay.append(jnp.exp(jnp.where(masks[d][0], g_col[-1] - gcum_t[d][ca:ca + 1, :], -jnp.inf)))
    nc = range(len(chains))
    kk = [_dot_nt(k[c], k[c]) for c in nc]
    xs = [jnp.where(masks[chains[c][0]][1], beta[c] * kk[c] * decay[c], 0.0) for c in nc]
    ps = _unit_triangular_inverses(xs)
    uw = [_dot_split(ps[c], jnp.concatenate([v[c] * beta[c], k[c] * (beta[c] * jnp.exp(g_col[c]))], axis=-1))
          for c in nc]
    qk = [_dot_nt(q[c], k[c]) * decay[c] for c in nc]
    s_st = [s_scr[c] for c in nc]
    v_new = [uw[c][:, :GD_V] - _dot(uw[c][:, GD_V:], s_st[c]) for c in nc]
    o_loc = [_dot(q[c] * jnp.exp(g_col[c]), s_st[c]) for c in nc]
    o_new = [o_loc[c] + _dot(qk[c], v_new[c]) for c in nc]
    s_new = [s_st[c] * jnp.exp(g_end[c]) + _dot_tn(k[c] * jnp.exp(g_end[c] - g_col[c]), v_new[c]) for c in nc]
    for c, (d, hd) in enumerate(chains):
        o_refs[d][0, :, hd * GD_V:(hd + 1) * GD_V] = o_new[c]
        s_scr[c] = s_new[c]

    @pl.when(i == pl.num_programs(1) - 1)
    def _():
        sn_ref[0] = s_scr[...]


def _gdn(qkv_view, gates_view, neg_a, s0, nc, idx_fn, out_view_shape):
    b = qkv_view.shape[0]
    nqkv = 2 * GD_HEADS * GD_QK + GD_HEADS * GD_V
    hdim = GD_HEADS * GD_V
    fwd = lambda n: pl.BlockSpec((1, CHUNK, n), lambda bi, i: idx_fn(bi, i))
    bwd = lambda n: pl.BlockSpec((1, CHUNK, n), lambda bi, i: idx_fn(bi, nc - 1 - i))
    st = pl.BlockSpec((1, N_CHAINS, GD_QK, GD_V), lambda bi, i: (bi, 0, 0, 0))
    return pl.pallas_call(
        _gdn_kernel,
        grid=(b, nc),
        in_specs=[fwd(nqkv), bwd(nqkv), fwd(GATE_LANES), bwd(GATE_LANES),
                  pl.BlockSpec((1, GATE_LANES), lambda bi, i: (0, 0)), st],
        out_specs=[fwd(hdim), bwd(hdim), st],
        out_shape=[jax.ShapeDtypeStruct(out_view_shape, F32), jax.ShapeDtypeStruct(out_view_shape, F32),
                   jax.ShapeDtypeStruct(s0.shape, F32)],
        scratch_shapes=[pltpu.VMEM((N_CHAINS, GD_QK, GD_V), F32)],
        compiler_params=_cparams("parallel", "arbitrary"),
        name="gdn",
    )(qkv_view, qkv_view, gates_view, gates_view, neg_a, s0)


def _head_rms(t, nheads, width):
    outs = []
    for hd in range(nheads):
        th = t[:, hd * width:(hd + 1) * width]
        outs.append(th * lax.rsqrt(jnp.mean(th * th, axis=-1, keepdims=True) + EPS))
    return jnp.concatenate(outs, axis=-1)


def _rms(t, w):
    return t * lax.rsqrt(jnp.mean(t * t, axis=-1, keepdims=True) + EPS) * w


def _post_kernel(x_ref, hf_ref, hb_ref, og_ref, of_ref, ob_ref, z_ref, mlw_ref, gdw_ref, wout_ref,
                 npost_ref, g2_ref, npre_ref, sh_ref, sc_ref, rwt_ref, wsg_ref, wsu_ref, wsd_ref,
                 x1_ref, hffn_ref, lt_ref, ys_ref, of_scr, ob_scr):
    ml_y = _head_rms(hf_ref[0] + hb_ref[0], ML_HEADS, ML_V) * mlw_ref[...] * jax.nn.sigmoid(og_ref[0])
    _from_grid_view(of_ref, of_scr)
    _from_grid_view(ob_ref, ob_scr)
    o_sum = jnp.concatenate([_pitched_rows(of_scr, g) + _pitched_rows(ob_scr, g) for g in range(of_scr.shape[0])],
                            axis=-1)
    gd_y = _head_rms(o_sum, GD_HEADS, GD_V) * gdw_ref[...] * _silu(z_ref[0])
    y = _dot(jnp.concatenate([ml_y, gd_y], axis=-1), wout_ref[...])
    x1 = x_ref[0] + g2_ref[0] * _rms(y, npost_ref[...])
    x1_ref[0] = x1
    hffn = _rms(x1, npre_ref[...]) * (1.0 + sc_ref[0]) + sh_ref[0]
    nct = hffn.shape[1] // LANES
    for c in range(nct):
        hffn_ref[0, pl.ds(c, hffn.shape[0], stride=nct), :] = hffn[:, c * LANES:(c + 1) * LANES]
    hb = hffn.astype(BF16)
    lt_ref[...] = lax.dot_general(rwt_ref[...], hb, (((1,), (1,)), ((), ())), preferred_element_type=F32)
    hs = _silu(jnp.dot(hb, wsg_ref[...], preferred_element_type=F32)) * jnp.dot(hb, wsu_ref[...],
                                                                                preferred_element_type=F32)
    ys_ref[0] = _dot(hs, wsd_ref[...])


def _post(x, hf, hb, ml, of, ob, gz, mlw, gdw, wout, npost, g2, npre, sh, sc, rwt, wsg, wsu, wsd, tm):
    b, s, d = x.shape
    nt = s // tm
    hw = ML_HEADS * ML_V
    og_blk = (2 * ML_HEADS * ML_QK + ML_HEADS * ML_V) // hw
    tok = lambda n: pl.BlockSpec((1, tm, n), lambda bi, i: (bi, i, 0))
    full = lambda shp: pl.BlockSpec(shp, lambda bi, i: (0,) * len(shp))
    mod = pl.BlockSpec((1, 1, d), lambda bi, i: (bi, 0, 0))
    ne = rwt.shape[0]
    ds = wsg.shape[1]
    gview = pl.BlockSpec((1, tm // GRID_W, GRID_W * hw), lambda bi, i: (bi, i, 0))
    return pl.pallas_call(
        _post_kernel,
        grid=(b, nt),
        in_specs=[tok(d), tok(hw), tok(hw), pl.BlockSpec((1, tm, hw), lambda bi, i: (bi, i, og_blk)),
                  gview, gview, tok(hw), full((1, hw)), full((1, hw)), full((d, d)),
                  full((1, d)), mod, full((1, d)), mod, mod, full((ne, d)), full((d, ds)), full((d, ds)),
                  full((ds, d))],
        out_specs=[tok(d), pl.BlockSpec((1, tm * (d // LANES), LANES), lambda bi, i: (bi, i, 0)),
                   pl.BlockSpec((ne, tm), lambda bi, i: (0, bi * nt + i)), tok(d)],
        out_shape=[jax.ShapeDtypeStruct((b, s, d), F32), jax.ShapeDtypeStruct((b, s * (d // LANES), LANES), F32),
                   jax.ShapeDtypeStruct((ne, b * s), F32), jax.ShapeDtypeStruct((b, s, d), F32)],
        scratch_shapes=[pltpu.VMEM((hw // LANES, tm // GRID_W * GRID_PITCH, LANES), F32)] * 2,
        compiler_params=_cparams("parallel", "arbitrary"),
        name="post",
    )(x, hf, hb, ml, of, ob, gz, mlw, gdw, wout, npost, g2, npre, sh, sc, rwt, wsg, wsu, wsd)


def _route_kernel(lt_ref, bias_ref, idx_ref, gate_ref):
    ne, tn = lt_ref.shape
    gsz = ne // N_GROUPS
    scores = jax.nn.sigmoid(lt_ref[...])
    sel = scores + bias_ref[...]
    neg = -jnp.inf
    sel3 = sel.reshape(N_GROUPS, gsz, tn)
    io3 = lax.broadcasted_iota(jnp.int32, sel3.shape, 1)
    top1 = jnp.max(sel3, axis=1, keepdims=True)
    first = jnp.min(jnp.where(sel3 == top1, io3, gsz), axis=1, keepdims=True)
    top2 = jnp.max(jnp.where(io3 == first, neg, sel3), axis=1, keepdims=True)
    grp = (top1 + top2).reshape(N_GROUPS, tn)
    iog = lax.broadcasted_iota(jnp.int32, grp.shape, 0)
    keep = jnp.zeros(grp.shape, jnp.bool_)
    for _ in range(TOPK_GROUPS):
        m = jnp.max(grp, axis=0, keepdims=True)
        pick = iog == jnp.min(jnp.where(grp == m, iog, N_GROUPS), axis=0, keepdims=True)
        keep = keep | pick
        grp = jnp.where(pick, neg, grp)
    cand = jnp.where(keep.reshape(N_GROUPS, 1, tn), sel3, neg).reshape(ne, tn)
    ioe = lax.broadcasted_iota(jnp.int32, cand.shape, 0)
    idxs, gates = [], []
    for _ in range(TOP_K):
        m = jnp.max(cand, axis=0, keepdims=True)
        e = jnp.min(jnp.where(cand == m, ioe, ne), axis=0, keepdims=True)
        pick = ioe == e
        idxs.append(e)
        gates.append(jnp.sum(jnp.where(pick, scores, 0.0), axis=0, keepdims=True))
        cand = jnp.where(pick, neg, cand)
    gate = jnp.concatenate(gates, axis=0)
    idx_ref[...] = jnp.concatenate(idxs, axis=0)
    gate_ref[...] = gate / jnp.sum(gate, axis=0, keepdims=True) * ROUTED_SCALE


def _route(logits_t, bias_col, tn):
    ne, t = logits_t.shape
    return pl.pallas_call(
        _route_kernel,
        grid=(t // tn,),
        in_specs=[pl.BlockSpec((ne, tn), lambda i: (0, i)), pl.BlockSpec((ne, 1), lambda i: (0, 0))],
        out_specs=[pl.BlockSpec((TOP_K, tn), lambda i: (0, i)), pl.BlockSpec((TOP_K, tn), lambda i: (0, i))],
        out_shape=[jax.ShapeDtypeStruct((TOP_K, t), jnp.int32), jax.ShapeDtypeStruct((TOP_K, t), F32)],
        compiler_params=_cparams("parallel"),
        name="route",
    )(logits_t, bias_col)


def _experts_kernel(be_ref, np_ref,
                    tok_ref, tokn_ref, slot_ref, w_ref, wg0_ref, wu0_ref, wd0_ref, wg1_ref, wu1_ref, wd1_ref, h_hbm,
                    slots_hbm, xg, ob, wgc, wuc, wdc, gsem, ssem):
    s = pl.program_id(0)
    n_pairs = np_ref[0]
    nct = xg.shape[1] // EXPERT_BLOCK
    rows = nct * EXPERT_BLOCK
    w_refs = ((wg0_ref, wu0_ref, wd0_ref), (wg1_ref, wu1_ref, wd1_ref))

    def gather_copy(tref, p, j):
        src = h_hbm.at[pl.ds(pl.multiple_of(tref[0, p, j] * nct, nct), nct)]
        return pltpu.make_async_copy(src, xg.at[p, pl.ds(j * nct, nct)], gsem.at[p])

    def scatter_copy(p, j):
        dst = slots_hbm.at[pl.ds(pl.multiple_of(slot_ref[0, p, j] * nct, nct), nct)]
        return pltpu.make_async_copy(ob.at[p, pl.ds(j * nct, nct)], dst, ssem.at[p])

    def gather_wait(p):
        pltpu.make_async_copy(h_hbm.at[pl.ds(0, rows)], xg.at[p], gsem.at[p]).wait()

    def scatter_wait(p):
        pltpu.make_async_copy(ob.at[p], slots_hbm.at[pl.ds(0, rows)], ssem.at[p]).wait()

    @pl.when(s < n_pairs)
    def _():
        @pl.when(s == 0)
        def _():
            ob[...] = jnp.zeros(ob.shape, F32)
            spare0 = slots_hbm.shape[0] - ob.shape[0] * rows
            for p in range(2):
                init = pltpu.make_async_copy(ob.at[p], slots_hbm.at[pl.ds(spare0 + p * rows, rows)], ssem.at[p])
                init.start()
                init.wait()
            for p in range(2):
                for j in range(EXPERT_BLOCK):
                    gather_copy(tok_ref, p, j).start(priority=j % 2)

        for p in range(2):
            blk = 2 * s + p
            wg_ref, wu_ref, wd_ref = w_refs[p]
            gather_wait(p)

            @pl.when(s >= 1)
            def _():
                scatter_wait(p)

            @pl.when((blk == 0) | (be_ref[blk] != be_ref[jnp.maximum(blk - 1, 0)]))
            def _():
                wgc[...] = wg_ref[0].astype(BF16)
                wuc[...] = wu_ref[0].astype(BF16)
                wdc[...] = wd_ref[0].astype(BF16)

            xb = jnp.concatenate([xg[p, pl.ds(c, EXPERT_BLOCK, stride=nct), :] for c in range(nct)],
                                 axis=-1).astype(BF16)
            hmid = _silu(jnp.dot(xb, wgc[...], preferred_element_type=F32)) * jnp.dot(xb, wuc[...],
                                                                                      preferred_element_type=F32)
            out = _dot(hmid, wdc[...])
            eye = (lax.broadcasted_iota(jnp.int32, (EXPERT_BLOCK, EXPERT_BLOCK), 0)
                   == lax.broadcasted_iota(jnp.int32, (EXPERT_BLOCK, EXPERT_BLOCK), 1))
            w_col = jnp.sum(jnp.where(eye, w_ref[0, p:p + 1, :], 0.0), axis=1, keepdims=True)
            out = out * w_col
            for c in range(nct):
                ob[p, pl.ds(c, EXPERT_BLOCK, stride=nct), :] = out[:, c * LANES:(c + 1) * LANES]

            for j in range(EXPERT_BLOCK):
                gather_copy(tokn_ref, p, j).start(priority=j % 2)
                scatter_copy(p, j).start(priority=(j + 1) % 2)

        @pl.when(s == n_pairs - 1)
        def _():
            for p in range(2):
                gather_wait(p)
                scatter_wait(p)


def _experts(hffn, block_e, n_pairs, row_tok, row_slot, row_w, wg, wu, wd, n_slots):
    d = wg.shape[1]
    nct = d // LANES
    npairs = row_tok.shape[0]
    de = wg.shape[2]
    last = npairs - 1
    smem_blk = lambda f: pl.BlockSpec((1, 2, EXPERT_BLOCK), f, memory_space=pltpu.SMEM)
    wspec = lambda shp, p: pl.BlockSpec((1,) + shp, lambda s, be, npu: (be[2 * s + p], 0, 0))
    grid_spec = pltpu.PrefetchScalarGridSpec(
        num_scalar_prefetch=2,
        grid=(npairs,),
        in_specs=[smem_blk(lambda s, be, npu: (s, 0, 0)),
                  smem_blk(lambda s, be, npu: (jnp.minimum(s + 1, last), 0, 0)),
                  smem_blk(lambda s, be, npu: (s, 0, 0)),
                  pl.BlockSpec((1, 2, EXPERT_BLOCK), lambda s, be, npu: (s, 0, 0)),
                  wspec((d, de), 0), wspec((d, de), 0), wspec((de, d), 0),
                  wspec((d, de), 1), wspec((d, de), 1), wspec((de, d), 1),
                  pl.BlockSpec(memory_space=pl.ANY)],
        out_specs=pl.BlockSpec(memory_space=pl.ANY),
        scratch_shapes=[pltpu.VMEM((2, EXPERT_BLOCK * nct, LANES), F32), pltpu.VMEM((2, EXPERT_BLOCK * nct, LANES), F32),
                        pltpu.VMEM((d, de), BF16), pltpu.VMEM((d, de), BF16), pltpu.VMEM((de, d), BF16),
                        pltpu.SemaphoreType.DMA((2,)), pltpu.SemaphoreType.DMA((2,))],
    )
    return pl.pallas_call(
        _experts_kernel,
        grid_spec=grid_spec,
        out_shape=jax.ShapeDtypeStruct(((n_slots + 2 * EXPERT_BLOCK) * nct, LANES), F32),
        compiler_params=_cparams("arbitrary"),
        name="experts",
    )(block_e, n_pairs, row_tok, row_tok, row_slot, row_w, wg, wu, wd, wg, wu, wd, hffn)


def _combine_kernel(x1_ref, ys_ref, *refs):
    slot_refs, (npost_ref, g5_ref, o_ref) = refs[:TOP_K], refs[TOP_K:]
    tm = x1_ref.shape[1]
    nct = slot_refs[0].shape[0] // tm
    routed = []
    for c in range(nct):
        acc = slot_refs[0][pl.ds(c, tm, stride=nct), :]
        for k in range(1, TOP_K):
            acc = acc + slot_refs[k][pl.ds(c, tm, stride=nct), :]
        routed.append(acc)
    y = ys_ref[0] + jnp.concatenate(routed, axis=-1)
    o_ref[0] = x1_ref[0] + g5_ref[0] * _rms(y, npost_ref[...])


def _combine(x1, ys, slots, npost, g5, tm):
    b, s, d = x1.shape
    nt = s // tm
    nct = d // LANES
    tok = pl.BlockSpec((1, tm, d), lambda bi, i: (bi, i, 0))
    slot = lambda k: pl.BlockSpec((tm * nct, LANES), lambda bi, i: (k * b * nt + bi * nt + i, 0))
    return pl.pallas_call(
        _combine_kernel,
        grid=(b, nt),
        in_specs=[tok, tok] + [slot(k) for k in range(TOP_K)]
                 + [pl.BlockSpec((1, d), lambda bi, i: (0, 0)), pl.BlockSpec((1, 1, d), lambda bi, i: (bi, 0, 0))],
        out_specs=tok,
        out_shape=jax.ShapeDtypeStruct((b, s, d), F32),
        compiler_params=_cparams("parallel", "arbitrary"),
        name="combine",
    )(x1, ys, *([slots] * TOP_K), npost, g5)


def _dispatch_plan(idx_t, gate_t):
    k, t = idx_t.shape
    n_asg = k * t
    nb = n_asg // EXPERT_BLOCK + N_EXPERTS
    flat_e = idx_t.reshape(-1)
    order = jnp.argsort(flat_e).astype(jnp.int32)
    counts = jnp.zeros((N_EXPERTS,), jnp.int32).at[flat_e].add(1)
    padded = (counts + EXPERT_BLOCK - 1) // EXPERT_BLOCK * EXPERT_BLOCK
    start = jnp.cumsum(counts) - counts
    pend = jnp.cumsum(padded)
    pstart = pend - padded
    blk0 = jnp.arange(nb, dtype=jnp.int32) * EXPERT_BLOCK
    block_e = jnp.minimum(jnp.searchsorted(pend, blk0, side='right'), N_EXPERTS - 1).astype(jnp.int32)
    assert nb % 2 == 0
    n_pairs = ((pend[-1] // EXPERT_BLOCK + 1) // 2).astype(jnp.int32).reshape(1)
    pos = blk0[:, None] - pstart[block_e][:, None] + jnp.arange(EXPERT_BLOCK, dtype=jnp.int32)[None, :]
    valid = pos < counts[block_e][:, None]
    src = jnp.clip(start[block_e][:, None] + pos, 0, n_asg - 1)
    asg = order[src]
    row_tok = jnp.where(valid, asg % t, 0).astype(jnp.int32)
    lane = jnp.arange(EXPERT_BLOCK, dtype=jnp.int32)[None, :]
    spare = n_asg + (jnp.arange(nb, dtype=jnp.int32)[:, None] % 2) * EXPERT_BLOCK + lane
    row_slot = jnp.where(valid, asg, spare).astype(jnp.int32)
    row_w = jnp.where(valid, gate_t.reshape(-1)[asg], 0.0).astype(F32)
    shp = (nb // 2, 2, EXPERT_BLOCK)
    return block_e, n_pairs, row_tok.reshape(shp), row_slot.reshape(shp), row_w.reshape(shp)


def _pack_in_weights(w_in, ml_i_bias, ml_f_bias, gd_dt_bias):
    d = w_in.shape[0]
    nml = 2 * ML_HEADS * ML_QK + 2 * ML_HEADS * ML_V
    ml_cols = nml + 4 * ML_HEADS
    ngq = GD_HEADS * (2 * GD_QK + GD_V)
    ngz = GD_HEADS * GD_V
    wml = w_in[:, :nml].astype(BF16)
    wgq = w_in[:, ml_cols:ml_cols + ngq].astype(BF16)
    wgz = w_in[:, ml_cols + ngq:ml_cols + ngq + ngz].astype(BF16)
    wg = jnp.zeros((d, GATE_LANES), F32)
    wg = wg.at[:, ML_GATE0:ML_GATE0 + 16].set(w_in[:, nml:ml_cols])
    wg = wg.at[:, GD_GATE0:GD_GATE0 + 16].set(w_in[:, ml_cols + ngq + ngz:])
    gb = jnp.zeros((GATE_LANES,), F32)
    gb = gb.at[ML_GATE0:ML_GATE0 + 16].set(jnp.stack([ml_i_bias, ml_f_bias], axis=1).reshape(-1))
    gb = gb.at[GD_GATE0:GD_GATE0 + 16].set(jnp.stack([gd_dt_bias, jnp.zeros_like(gd_dt_bias)], axis=1).reshape(-1))
    return wml, wgq, wgz, wg.astype(BF16), gb.reshape(1, GATE_LANES)


def _mixer(x, ctx, mod, mod_ctx, norm_pre_mix, w_in, ml_i_bias, ml_f_bias, gd_conv_w, gd_a_log, gd_dt_bias):
    b, s, d = x.shape
    sc = ctx.shape[1]
    wml, wgq, wgz, wg, gb = _pack_in_weights(w_in, ml_i_bias, ml_f_bias, gd_dt_bias)
    nw = norm_pre_mix.reshape(1, d)
    ctx_mod = lambda j: jnp.broadcast_to(mod_ctx[j].reshape(1, 1, d), (b, 1, d))
    ml_c, gq_c, _, g_c = _proj(ctx, nw, ctx_mod(0), ctx_mod(1), wml, wgq, wgz, wg, gb, tm=sc, grid_view=False)
    ml_l, gqv_l, gz_l, g_l, gv_l = _proj(x, nw, mod[0], mod[1], wml, wgq, wgz, wg, gb, tm=512, grid_view=True)

    c0 = jnp.zeros((b, N_CHAINS, ML_QK, 2 * ML_V), F32)
    m0 = jnp.zeros((b, N_CHAINS, 1, 1), F32)
    _, _, c1, m1 = _mlstm(ml_c, g_c, c0, m0)
    hf, hb, _, _ = _mlstm(ml_l, g_l, c1, m1)

    neg_a = jnp.zeros((GATE_LANES,), F32)
    neg_a = neg_a.at[GD_GATE0:GD_GATE0 + 16].set(
        jnp.stack([-jnp.exp(gd_a_log), jnp.zeros_like(gd_a_log)], axis=1).reshape(-1)).reshape(1, GATE_LANES)
    qn_c = _gdconv_ctx(gq_c, gd_conv_w)
    qnv_l = _gdconv_lat(gqv_l, gd_conv_w)
    s0 = jnp.zeros((b, N_CHAINS, GD_QK, GD_V), F32)
    hdim = GD_HEADS * GD_V
    _, _, s1 = _gdn(qn_c, g_c, neg_a, s0, sc // CHUNK, lambda bi, n: (bi, n, 0), (b, sc, hdim))
    rows = s // GRID_W
    cpc = rows // CHUNK
    col_idx = lambda bi, n: (bi, n % cpc, n // cpc)
    ofv, obv, _ = _gdn(qnv_l, gv_l, neg_a, s1, s // CHUNK, col_idx, (b, rows, GRID_W * hdim))
    return hf, hb, ml_l, ofv, obv, gz_l


def kernel(x, c, ctx, c_ctx, w_ada, b_ada, norm_pre_mix, norm_post_mix, norm_pre_ffn, norm_post_ffn, w_in,
           ml_i_bias, ml_f_bias, ml_norm_w, gd_conv_w, gd_a_log, gd_dt_bias, gd_norm_w, w_out, router_w,
           router_bias, w_gate, w_up, w_down, ws_gate, ws_up, ws_down):
    b, s, d = x.shape
    depth = w_ada.shape[0]
    assert depth == 1, "the context stream update of deeper stacks is not implemented"
    ly = 0
    cc = jnp.zeros((16, d), F32).at[:b].set(c).at[b].set(c_ctx)
    mod_all = _ada(cc, w_ada[ly], b_ada[ly])
    mod = [mod_all[:b, j * d:(j + 1) * d].reshape(b, 1, d) for j in range(6)]
    mod_ctx = [mod_all[b, j * d:(j + 1) * d] for j in range(6)]

    hf, hb, ml_l, of, ob, gz_l = _mixer(x, ctx, mod, mod_ctx, norm_pre_mix[ly], w_in[ly], ml_i_bias[ly],
                                        ml_f_bias[ly], gd_conv_w[ly], gd_a_log[ly], gd_dt_bias[ly])

    row = lambda v: v.reshape(1, -1)
    x1, hffn, logits_t, ys = _post(
        x, hf, hb, ml_l, of, ob, gz_l, row(ml_norm_w[ly]), row(jnp.tile(gd_norm_w[ly], GD_HEADS)),
        w_out[ly].astype(BF16), row(norm_post_mix[ly]), mod[2], row(norm_pre_ffn[ly]), mod[3], mod[4],
        router_w[ly].T.astype(BF16), ws_gate[ly].astype(BF16), ws_up[ly].astype(BF16), ws_down[ly].astype(BF16),
        tm=512)

    idx_t, gate_t = _route(logits_t, router_bias[ly].reshape(-1, 1), tn=512)
    block_e, n_pairs, row_tok, row_slot, row_w = _dispatch_plan(idx_t, gate_t)
    t = b * s
    slots = _experts(hffn.reshape(t * (d // LANES), LANES), block_e, n_pairs, row_tok, row_slot, row_w,
                     w_gate[ly], w_up[ly], w_down[ly], TOP_K * t)
    return _combine(x1, ys, slots, row(norm_post_ffn[ly]), mod[5], tm=256)
```

```python
import functools

import jax
import jax.numpy as jnp
from jax import lax
from jax.experimental import pallas as pl
from jax.experimental.pallas import tpu as pltpu

EPS = 1e-6
CHUNK = 64
GRID_W = 64
ML_HEADS, ML_QK, ML_V = 4, 64, 128
GD_HEADS, GD_QK, GD_V = 4, 128, 128
CONV_W = 5
N_EXPERTS, TOP_K, N_GROUPS, TOPK_GROUPS = 256, 8, 8, 4
ROUTED_SCALE = 2.5
EXPERT_BLOCK = 128
N_CHAINS = 8
LANES = 128
GATE_LANES = LANES
ML_GATE0, GD_GATE0 = 0, 16

F32 = jnp.float32
BF16 = jnp.bfloat16
HI = lax.Precision.HIGHEST
VMEM_LIMIT = 56 * 1024 * 1024


def _cparams(*sem):
    return pltpu.CompilerParams(dimension_semantics=sem, vmem_limit_bytes=VMEM_LIMIT)


def _dot(a, b):
    return jnp.dot(a.astype(BF16), b.astype(BF16), preferred_element_type=F32)


def _dot_nt(a, b):
    return lax.dot_general(a.astype(BF16), b.astype(BF16), (((1,), (1,)), ((), ())), preferred_element_type=F32)


def _dot_tn(a, b):
    return lax.dot_general(a.astype(BF16), b.astype(BF16), (((0,), (0,)), ((), ())), preferred_element_type=F32)


def _dot_hi(a, b):
    return jnp.dot(a, b, precision=HI, preferred_element_type=F32)


def _dot_nt_hi(a, b):
    return lax.dot_general(a, b, (((1,), (1,)), ((), ())), precision=HI, preferred_element_type=F32)


def _transpose_hi(x):
    n = x.shape[1]
    eye = (lax.broadcasted_iota(jnp.int32, (n, n), 0) == lax.broadcasted_iota(jnp.int32, (n, n), 1)).astype(F32)
    return _dot_nt_hi(eye, x)


def _silu(x):
    return x * jax.nn.sigmoid(x)


def _past_mask(reverse):
    t = lax.broadcasted_iota(jnp.int32, (CHUNK, CHUNK), 0)
    s = lax.broadcasted_iota(jnp.int32, (CHUNK, CHUNK), 1)
    return (s >= t, s > t) if reverse else (s <= t, s < t)


def _ada_kernel(c_ref, w_ref, b_ref, o_ref):
    o_ref[...] = _dot(_silu(c_ref[...]), w_ref[...]) + b_ref[...]


def _ada(cc, w_ada, b_ada):
    rows, d = cc.shape
    n = w_ada.shape[1]
    tn = 1536
    return pl.pallas_call(
        _ada_kernel,
        grid=(n // tn,),
        in_specs=[pl.BlockSpec((rows, d), lambda j: (0, 0)),
                  pl.BlockSpec((d, tn), lambda j: (0, j)),
                  pl.BlockSpec((1, tn), lambda j: (0, j))],
        out_specs=pl.BlockSpec((rows, tn), lambda j: (0, j)),
        out_shape=jax.ShapeDtypeStruct((rows, n), F32),
        compiler_params=_cparams("arbitrary"),
        name="ada",
    )(cc, w_ada, b_ada.reshape(1, n))


GRID_PITCH = GRID_W + 8


def _to_grid_view(src_ref, dst_ref):
    ng = src_ref.shape[0]
    r = src_ref.shape[1] // GRID_PITCH
    for c in range(GRID_W):
        for g in range(ng):
            lo = (c * ng + g) * LANES
            dst_ref[0, :, lo:lo + LANES] = src_ref[g, pl.ds(c, r, stride=GRID_PITCH), :]


def _from_grid_view(src_ref, dst_ref):
    ng = dst_ref.shape[0]
    r = dst_ref.shape[1] // GRID_PITCH
    for c in range(GRID_W):
        for g in range(ng):
            lo = (c * ng + g) * LANES
            dst_ref[g, pl.ds(c, r, stride=GRID_PITCH), :] = src_ref[0, :, lo:lo + LANES]


def _pitched_rows(ref, g):
    r = ref.shape[1] // GRID_PITCH
    return jnp.concatenate([ref[g, i * GRID_PITCH:i * GRID_PITCH + GRID_W, :] for i in range(r)], axis=0)


def _proj_kernel(grid_view, x_ref, nw_ref, sh_ref, sc_ref, wml_ref, wgq_ref, wgz_ref, wg_ref, gb_ref, *refs):
    if grid_view:
        ml_ref, gqv_ref, gz_ref, g_ref, gv_ref, gq_scr, g_scr = refs
    else:
        ml_ref, gq_ref, gz_ref, g_ref = refs
    x = x_ref[0]
    xn = x * lax.rsqrt(jnp.mean(x * x, axis=-1, keepdims=True) + EPS) * nw_ref[...]
    h = (xn * (1.0 + sc_ref[0]) + sh_ref[0]).astype(BF16)
    ml_ref[0] = jnp.dot(h, wml_ref[...], preferred_element_type=F32)
    gz_ref[0] = jnp.dot(h, wgz_ref[...], preferred_element_type=F32)
    gates = jnp.dot(h, wg_ref[...], preferred_element_type=F32) + gb_ref[...]
    g_ref[0] = gates
    gq = jnp.dot(h, wgq_ref[...], preferred_element_type=F32)
    if grid_view:
        for r in range(x.shape[0] // GRID_W):
            rows = slice(r * GRID_W, (r + 1) * GRID_W)
            prow = slice(r * GRID_PITCH, r * GRID_PITCH + GRID_W)
            g_scr[0, prow, :] = gates[rows]
            for g in range(gq_scr.shape[0]):
                gq_scr[g, prow, :] = gq[rows, g * LANES:(g + 1) * LANES]
        _to_grid_view(gq_scr, gqv_ref)
        _to_grid_view(g_scr, gv_ref)
    else:
        gq_ref[0] = gq


def _proj(x, norm_w, shift, scale, wml, wgq, wgz, wg, gbias, tm, grid_view):
    b, s, d = x.shape
    nml, ngq, ngz = wml.shape[1], wgq.shape[1], wgz.shape[1]
    full = lambda shp: pl.BlockSpec(shp, lambda bi, i: (0,) * len(shp))
    tok = lambda n: pl.BlockSpec((1, tm, n), lambda bi, i: (bi, i, 0))
    mod = pl.BlockSpec((1, 1, d), lambda bi, i: (bi, 0, 0))
    if grid_view:
        rt = tm // GRID_W
        view = lambda n: pl.BlockSpec((1, rt, GRID_W * n), lambda bi, i: (bi, i, 0))
        vshape = lambda n: jax.ShapeDtypeStruct((b, s // GRID_W, GRID_W * n), F32)
        out_specs = [tok(nml), view(ngq), tok(ngz), tok(GATE_LANES), view(GATE_LANES)]
        out_shape = [jax.ShapeDtypeStruct((b, s, nml), F32), vshape(ngq), jax.ShapeDtypeStruct((b, s, ngz), F32),
                     jax.ShapeDtypeStruct((b, s, GATE_LANES), F32), vshape(GATE_LANES)]
        scratch = [pltpu.VMEM((ngq // LANES, rt * GRID_PITCH, LANES), F32), pltpu.VMEM((1, rt * GRID_PITCH, LANES), F32)]
    else:
        out_specs = [tok(nml), tok(ngq), tok(ngz), tok(GATE_LANES)]
        out_shape = [jax.ShapeDtypeStruct((b, s, n), F32) for n in (nml, ngq, ngz, GATE_LANES)]
        scratch = []
    return pl.pallas_call(
        functools.partial(_proj_kernel, grid_view),
        grid=(b, s // tm),
        in_specs=[tok(d), full((1, d)), mod, mod, full((d, nml)), full((d, ngq)), full((d, ngz)),
                  full((d, GATE_LANES)), full((1, GATE_LANES))],
        out_specs=out_specs,
        out_shape=out_shape,
        scratch_shapes=scratch,
        compiler_params=_cparams("parallel", "arbitrary"),
        name="proj",
    )(x, norm_w, shift, scale, wml, wgq, wgz, wg, gbias)


def _mlstm_kernel(mlf_ref, mlb_ref, gf_ref, gb_ref, c0_ref, m0_ref, hf_ref, hb_ref, cn_ref, mn_ref, c_scr, m_scr):
    i = pl.program_id(1)

    @pl.when(i == 0)
    def _():
        c_scr[...] = c0_ref[0]
        m_scr[...] = m0_ref[0]

    past = [_past_mask(d == 1)[0] for d in range(2)]
    g = [r[0] for r in (gf_ref, gb_ref)]
    ls = [jax.nn.log_sigmoid(x) for x in g]
    bcol = [_dot_hi(past[d].astype(F32), ls[d]) for d in range(2)]
    tot = [jnp.sum(x, axis=0, keepdims=True) for x in ls]
    g_t = [_transpose_hi(x) for x in g]
    b_t = [_transpose_hi(x) for x in bcol]

    chains = [(d, hd) for d in range(2) for hd in range(ML_HEADS)]
    nc = range(len(chains))
    ml_refs, h_refs = (mlf_ref, mlb_ref), (hf_ref, hb_ref)
    k0, v0 = ML_HEADS * ML_QK, 2 * ML_HEADS * ML_QK
    ones_col = (lax.broadcasted_iota(jnp.int32, (CHUNK, ML_V), 1) == 0).astype(F32)
    q, k, v, i_col, b_col, b_end, log_d = [], [], [], [], [], [], []
    for d, hd in chains:
        ci = ML_GATE0 + d * 8 + hd
        cf = ci + ML_HEADS
        q.append(ml_refs[d][0, :, hd * ML_QK:(hd + 1) * ML_QK])
        k.append(ml_refs[d][0, :, k0 + hd * ML_QK:k0 + (hd + 1) * ML_QK] * (ML_QK ** -0.5))
        v.append(jnp.concatenate([ml_refs[d][0, :, v0 + hd * ML_V:v0 + (hd + 1) * ML_V], ones_col], axis=-1))
        i_col.append(g[d][:, ci:ci + 1])
        b_col.append(bcol[d][:, cf:cf + 1])
        b_end.append(tot[d][:, cf:cf + 1])
        log_d.append(jnp.where(past[d], b_col[-1] - b_t[d][cf:cf + 1, :] + g_t[d][ci:ci + 1, :], -jnp.inf))
    c_st = [c_scr[c] for c in nc]
    m_st = [m_scr[c] for c in nc]
    log_prev = [b_col[c] + m_st[c] for c in nc]
    m_t = [jnp.maximum(log_prev[c], jnp.max(log_d[c], axis=-1, keepdims=True)) for c in nc]
    qk = [_dot_nt(q[c], k[c]) for c in nc]
    qc = [_dot(q[c], c_st[c]) for c in nc]
    s = [qk[c] * jnp.exp(log_d[c] - m_t[c]) for c in nc]
    w_prev = [jnp.exp(log_prev[c] - m_t[c]) for c in nc]
    sv = [_dot(s[c], v[c]) for c in nc]
    log_s = [b_end[c] - b_col[c] + i_col[c] for c in nc]
    m_new = [jnp.maximum(b_end[c] + m_st[c], jnp.max(log_s[c], axis=0, keepdims=True)) for c in nc]
    kw = [k[c] * jnp.exp(log_s[c] - m_new[c]) for c in nc]
    w_c = [jnp.exp(b_end[c] + m_st[c] - m_new[c]) for c in nc]
    kv = [_dot_tn(kw[c], v[c]) for c in nc]
    numden = [sv[c] + w_prev[c] * qc[c] for c in nc]
    scale = [1.0 / jnp.maximum(jnp.abs(numden[c][:, ML_V:ML_V + 1]), jnp.exp(-m_t[c])) for c in nc]
    for c, (d, hd) in enumerate(chains):
        h_refs[d][0, :, hd * ML_V:(hd + 1) * ML_V] = numden[c][:, :ML_V] * scale[c]
        c_scr[c] = w_c[c] * c_st[c] + kv[c]
        m_scr[c] = m_new[c]

    @pl.when(i == pl.num_programs(1) - 1)
    def _():
        cn_ref[0] = c_scr[...]
        mn_ref[0] = m_scr[...]


def _mlstm(ml, gates, c0, m0):
    b, s, nml = ml.shape
    nc = s // CHUNK
    fwd = lambda n: pl.BlockSpec((1, CHUNK, n), lambda bi, i: (bi, i, 0))
    bwd = lambda n: pl.BlockSpec((1, CHUNK, n), lambda bi, i: (bi, nc - 1 - i, 0))
    st = lambda shp: pl.BlockSpec((1,) + shp, lambda bi, i: (bi,) + (0,) * len(shp))
    hdim = ML_HEADS * ML_V
    cshape = (N_CHAINS, ML_QK, 2 * ML_V)
    return pl.pallas_call(
        _mlstm_kernel,
        grid=(b, nc),
        in_specs=[fwd(nml), bwd(nml), fwd(GATE_LANES), bwd(GATE_LANES), st(cshape), st((N_CHAINS, 1, 1))],
        out_specs=[fwd(hdim), bwd(hdim), st(cshape), st((N_CHAINS, 1, 1))],
        out_shape=[jax.ShapeDtypeStruct((b, s, hdim), F32), jax.ShapeDtypeStruct((b, s, hdim), F32),
                   jax.ShapeDtypeStruct(c0.shape, F32), jax.ShapeDtypeStruct(m0.shape, F32)],
        scratch_shapes=[pltpu.VMEM(cshape, F32), pltpu.VMEM((N_CHAINS, 1, 1), F32)],
        compiler_params=_cparams("parallel", "arbitrary"),
        name="mlstm",
    )(ml, ml, gates, gates, c0, m0)


def _gdconv_kernel(has_halo, *refs):
    if has_halo:
        x_ref, prev_ref, next_ref, w_ref, o_ref, xp_ref = refs
    else:
        x_ref, w_ref, o_ref, xp_ref = refs
    rows = x_ref.shape[1]
    nch = x_ref.shape[2]
    pad = 8
    zero = jnp.zeros((pad, nch), F32)
    if has_halo:
        c = pl.program_id(1)
        xp_ref[0:pad, :] = jnp.where(c > 0, prev_ref[0], zero)
        xp_ref[pad + rows:, :] = jnp.where(c < pl.num_programs(1) - 1, next_ref[0], zero)
    else:
        xp_ref[0:pad, :] = zero
        xp_ref[pad + rows:, :] = zero
    xp_ref[pad:pad + rows, :] = x_ref[0]
    half = CONV_W // 2
    for lc in range(nch // 128):
        sl = slice(lc * 128, (lc + 1) * 128)
        acc = None
        for j in range(CONV_W):
            term = xp_ref[pad - half + j:pad - half + j + rows, sl] * w_ref[j:j + 1, sl]
            acc = term if acc is None else acc + term
        y = _silu(acc)
        if lc < 2 * GD_HEADS:
            y = y * lax.rsqrt(jnp.sum(y * y, axis=-1, keepdims=True) + EPS)
        if lc < GD_HEADS:
            y = y * (GD_QK ** -0.5)
        o_ref[0, :, sl] = y


def _gdconv_ctx(qkv, conv_w):
    b, s, nch = qkv.shape
    return pl.pallas_call(
        functools.partial(_gdconv_kernel, False),
        grid=(b,),
        in_specs=[pl.BlockSpec((1, s, nch), lambda bi: (bi, 0, 0)), pl.BlockSpec((CONV_W, nch), lambda bi: (0, 0))],
        out_specs=pl.BlockSpec((1, s, nch), lambda bi: (bi, 0, 0)),
        out_shape=jax.ShapeDtypeStruct((b, s, nch), F32),
        scratch_shapes=[pltpu.VMEM((s + 16, nch), F32)],
        compiler_params=_cparams("parallel"),
        name="gdconv_ctx",
    )(qkv, conv_w)


def _gdconv_lat(view, conv_w):
    b, rows, wn = view.shape
    nch = wn // GRID_W
    rb = rows // 8
    return pl.pallas_call(
        functools.partial(_gdconv_kernel, True),
        grid=(b, GRID_W),
        in_specs=[pl.BlockSpec((1, rows, nch), lambda bi, c: (bi, 0, c)),
                  pl.BlockSpec((1, 8, nch), lambda bi, c: (bi, rb - 1, jnp.maximum(c - 1, 0))),
                  pl.BlockSpec((1, 8, nch), lambda bi, c: (bi, 0, jnp.minimum(c + 1, GRID_W - 1))),
                  pl.BlockSpec((CONV_W, nch), lambda bi, c: (0, 0))],
        out_specs=pl.BlockSpec((1, rows, nch), lambda bi, c: (bi, 0, c)),
        out_shape=jax.ShapeDtypeStruct(view.shape, F32),
        scratch_shapes=[pltpu.VMEM((rows + 16, nch), F32)],
        compiler_params=_cparams("parallel", "arbitrary"),
        name="gdconv_lat",
    )(view, view, view, conv_w)


SOLVE_BLOCK = 16


def _hi_lo(x):
    hi = x.astype(BF16).astype(F32)
    return hi, x - hi


def _dot_split(a, b):
    a_hi, a_lo = _hi_lo(a)
    b_hi, b_lo = _hi_lo(b)
    lhs = jnp.concatenate([a_hi, a_hi, a_lo], axis=1).astype(BF16)
    rhs = jnp.concatenate([b_hi, b_lo, b_hi], axis=0).astype(BF16)
    return jnp.dot(lhs, rhs, preferred_element_type=F32)


def _unit_triangular_inverses(ns):
    c = ns[0].shape[0]
    row = lax.broadcasted_iota(jnp.int32, (c, c), 0)
    col = lax.broadcasted_iota(jnp.int32, (c, c), 1)
    eye = (row == col).astype(F32)
    in_diag_block = (row // SOLVE_BLOCK) == (col // SOLVE_BLOCK)
    mm = lambda a_list, b_list: [_dot_split(a, b) for a, b in zip(a_list, b_list)]

    n_d = [jnp.where(in_diag_block, n, 0.0) for n in ns]
    x = n_d
    d_inv = [eye - n for n in n_d]
    for _ in range(SOLVE_BLOCK.bit_length() - 2):
        x = mm(x, x)
        d_inv = [d + dx for d, dx in zip(d_inv, mm(d_inv, x))]
    m = mm(d_inv, [n - nd for n, nd in zip(ns, n_d)])
    assert c // SOLVE_BLOCK == 4
    i_minus_m = [eye - mi for mi in m]
    q = [a + b for a, b in zip(i_minus_m, mm(i_minus_m, mm(m, m)))]
    return mm(q, d_inv)


def _gdn_kernel(qf_ref, qb_ref, gf_ref, gb_ref, na_ref, s0_ref, of_ref, ob_ref, sn_ref, s_scr):
    i = pl.program_id(1)

    @pl.when(i == 0)
    def _():
        s_scr[...] = s0_ref[0]

    eye = (lax.broadcasted_iota(jnp.int32, (CHUNK, CHUNK), 0)
           == lax.broadcasted_iota(jnp.int32, (CHUNK, CHUNK), 1)).astype(F32)
    nqk = GD_HEADS * GD_QK
    masks = [_past_mask(d == 1) for d in range(2)]
    gates = [r[0] for r in (gf_ref, gb_ref)]
    glog = [na_ref[...] * jax.nn.softplus(g) for g in gates]
    beta_all = [jax.nn.sigmoid(g) for g in gates]
    gcum = [_dot_hi(masks[d][0].astype(F32), glog[d]) for d in range(2)]
    gtot = [jnp.sum(g, axis=0, keepdims=True) for g in glog]
    gcum_t = [_transpose_hi(g) for g in gcum]

    chains = [(d, hd) for d in range(2) for hd in range(GD_HEADS)]
    x_refs, o_refs = (qf_ref, qb_ref), (of_ref, ob_ref)
    q, k, v, g_col, beta, g_end, decay = [], [], [], [], [], [], []
    for d, hd in chains:
        ca = GD_GATE0 + d * 8 + hd
        q.append(x_refs[d][0, :, hd * GD_QK:(hd + 1) * GD_QK])
        k.append(x_refs[d][0, :, nqk + hd * GD_QK:nqk + (hd + 1) * GD_QK])
        v.append(x_refs[d][0, :, 2 * nqk + hd * GD_V:2 * nqk + (hd + 1) * GD_V])
        g_col.append(gcum[d][:, ca:ca + 1])
        beta.append(beta_all[d][:, ca + GD_HEADS:ca + GD_HEADS + 1])
        g_end.append(gtot[d][:, ca:ca + 1])
        decay.append(jnp.exp(jnp.where(masks[d][0], g_col[-1] - gcum_t[d][ca:ca + 1, :], -jnp.inf)))
    nc = range(len(chains))
    kk = [_dot_nt(k[c], k[c]) for c in nc]
    xs = [jnp.where(masks[chains[c][0]][1], beta[c] * kk[c] * decay[c], 0.0) for c in nc]
    ps = _unit_triangular_inverses(xs)
    uw = [_dot_split(ps[c], jnp.concatenate([v[c] * beta[c], k[c] * (beta[c] * jnp.exp(g_col[c]))], axis=-1))
          for c in nc]
    qk = [_dot_nt(q[c], k[c]) * decay[c] for c in nc]
    s_st = [s_scr[c] for c in nc]
    v_new = [uw[c][:, :GD_V] - _dot(uw[c][:, GD_V:], s_st[c]) for c in nc]
    o_loc = [_dot(q[c] * jnp.exp(g_col[c]), s_st[c]) for c in nc]
    o_new = [o_loc[c] + _dot(qk[c], v_new[c]) for c in nc]
    s_new = [s_st[c] * jnp.exp(g_end[c]) + _dot_tn(k[c] * jnp.exp(g_end[c] - g_col[c]), v_new[c]) for c in nc]
    for c, (d, hd) in enumerate(chains):
        o_refs[d][0, :, hd * GD_V:(hd + 1) * GD_V] = o_new[c]
        s_scr[c] = s_new[c]

    @pl.when(i == pl.num_programs(1) - 1)
    def _():
        sn_ref[0] = s_scr[...]


def _gdn(qkv_view, gates_view, neg_a, s0, nc, idx_fn, out_view_shape):
    b = qkv_view.shape[0]
    nqkv = 2 * GD_HEADS * GD_QK + GD_HEADS * GD_V
    hdim = GD_HEADS * GD_V
    fwd = lambda n: pl.BlockSpec((1, CHUNK, n), lambda bi, i: idx_fn(bi, i))
    bwd = lambda n: pl.BlockSpec((1, CHUNK, n), lambda bi, i: idx_fn(bi, nc - 1 - i))
    st = pl.BlockSpec((1, N_CHAINS, GD_QK, GD_V), lambda bi, i: (bi, 0, 0, 0))
    return pl.pallas_call(
        _gdn_kernel,
        grid=(b, nc),
        in_specs=[fwd(nqkv), bwd(nqkv), fwd(GATE_LANES), bwd(GATE_LANES),
                  pl.BlockSpec((1, GATE_LANES), lambda bi, i: (0, 0)), st],
        out_specs=[fwd(hdim), bwd(hdim), st],
        out_shape=[jax.ShapeDtypeStruct(out_view_shape, F32), jax.ShapeDtypeStruct(out_view_shape, F32),
                   jax.ShapeDtypeStruct(s0.shape, F32)],
        scratch_shapes=[pltpu.VMEM((N_CHAINS, GD_QK, GD_V), F32)],
        compiler_params=_cparams("parallel", "arbitrary"),
        name="gdn",
    )(qkv_view, qkv_view, gates_view, gates_view, neg_a, s0)


def _head_rms(t, nheads, width):
    outs = []
    for hd in range(nheads):
        th = t[:, hd * width:(hd + 1) * width]
        outs.append(th * lax.rsqrt(jnp.mean(th * th, axis=-1, keepdims=True) + EPS))
    return jnp.concatenate(outs, axis=-1)


def _rms(t, w):
    return t * lax.rsqrt(jnp.mean(t * t, axis=-1, keepdims=True) + EPS) * w


def _post_kernel(x_ref, hf_ref, hb_ref, og_ref, of_ref, ob_ref, z_ref, mlw_ref, gdw_ref, wout_ref,
                 npost_ref, g2_ref, npre_ref, sh_ref, sc_ref, rwt_ref, wsg_ref, wsu_ref, wsd_ref,
                 x1_ref, hffn_ref, lt_ref, ys_ref, of_scr, ob_scr):
    ml_y = _head_rms(hf_ref[0] + hb_ref[0], ML_HEADS, ML_V) * mlw_ref[...] * jax.nn.sigmoid(og_ref[0])
    _from_grid_view(of_ref, of_scr)
    _from_grid_view(ob_ref, ob_scr)
    o_sum = jnp.concatenate([_pitched_rows(of_scr, g) + _pitched_rows(ob_scr, g) for g in range(of_scr.shape[0])],
                            axis=-1)
    gd_y = _head_rms(o_sum, GD_HEADS, GD_V) * gdw_ref[...] * _silu(z_ref[0])
    y = _dot(jnp.concatenate([ml_y, gd_y], axis=-1), wout_ref[...])
    x1 = x_ref[0] + g2_ref[0] * _rms(y, npost_ref[...])
    x1_ref[0] = x1
    hffn = _rms(x1, npre_ref[...]) * (1.0 + sc_ref[0]) + sh_ref[0]
    nct = hffn.shape[1] // LANES
    for c in range(nct):
        hffn_ref[0, pl.ds(c, hffn.shape[0], stride=nct), :] = hffn[:, c * LANES:(c + 1) * LANES]
    hb = hffn.astype(BF16)
    lt_ref[...] = lax.dot_general(rwt_ref[...], hb, (((1,), (1,)), ((), ())), preferred_element_type=F32)
    hs = _silu(jnp.dot(hb, wsg_ref[...], preferred_element_type=F32)) * jnp.dot(hb, wsu_ref[...],
                                                                                preferred_element_type=F32)
    ys_ref[0] = _dot(hs, wsd_ref[...])


def _post(x, hf, hb, ml, of, ob, gz, mlw, gdw, wout, npost, g2, npre, sh, sc, rwt, wsg, wsu, wsd, tm):
    b, s, d = x.shape
    nt = s // tm
    hw = ML_HEADS * ML_V
    og_blk = (2 * ML_HEADS * ML_QK + ML_HEADS * ML_V) // hw
    tok = lambda n: pl.BlockSpec((1, tm, n), lambda bi, i: (bi, i, 0))
    full = lambda shp: pl.BlockSpec(shp, lambda bi, i: (0,) * len(shp))
    mod = pl.BlockSpec((1, 1, d), lambda bi, i: (bi, 0, 0))
    ne = rwt.shape[0]
    ds = wsg.shape[1]
    gview = pl.BlockSpec((1, tm // GRID_W, GRID_W * hw), lambda bi, i: (bi, i, 0))
    return pl.pallas_call(
        _post_kernel,
        grid=(b, nt),
        in_specs=[tok(d), tok(hw), tok(hw), pl.BlockSpec((1, tm, hw), lambda bi, i: (bi, i, og_blk)),
                  gview, gview, tok(hw), full((1, hw)), full((1, hw)), full((d, d)),
                  full((1, d)), mod, full((1, d)), mod, mod, full((ne, d)), full((d, ds)), full((d, ds)),
                  full((ds, d))],
        out_specs=[tok(d), pl.BlockSpec((1, tm * (d // LANES), LANES), lambda bi, i: (bi, i, 0)),
                   pl.BlockSpec((ne, tm), lambda bi, i: (0, bi * nt + i)), tok(d)],
        out_shape=[jax.ShapeDtypeStruct((b, s, d), F32), jax.ShapeDtypeStruct((b, s * (d // LANES), LANES), F32),
                   jax.ShapeDtypeStruct((ne, b * s), F32), jax.ShapeDtypeStruct((b, s, d), F32)],
        scratch_shapes=[pltpu.VMEM((hw // LANES, tm // GRID_W * GRID_PITCH, LANES), F32)] * 2,
        compiler_params=_cparams("parallel", "arbitrary"),
        name="post",
    )(x, hf, hb, ml, of, ob, gz, mlw, gdw, wout, npost, g2, npre, sh, sc, rwt, wsg, wsu, wsd)


def _route_kernel(lt_ref, bias_ref, idx_ref, gate_ref):
    ne, tn = lt_ref.shape
    gsz = ne // N_GROUPS
    scores = jax.nn.sigmoid(lt_ref[...])
    sel = scores + bias_ref[...]
    neg = -jnp.inf
    sel3 = sel.reshape(N_GROUPS, gsz, tn)
    io3 = lax.broadcasted_iota(jnp.int32, sel3.shape, 1)
    top1 = jnp.max(sel3, axis=1, keepdims=True)
    first = jnp.min(jnp.where(sel3 == top1, io3, gsz), axis=1, keepdims=True)
    top2 = jnp.max(jnp.where(io3 == first, neg, sel3), axis=1, keepdims=True)
    grp = (top1 + top2).reshape(N_GROUPS, tn)
    iog = lax.broadcasted_iota(jnp.int32, grp.shape, 0)
    keep = jnp.zeros(grp.shape, jnp.bool_)
    for _ in range(TOPK_GROUPS):
        m = jnp.max(grp, axis=0, keepdims=True)
        pick = iog == jnp.min(jnp.where(grp == m, iog, N_GROUPS), axis=0, keepdims=True)
        keep = keep | pick
        grp = jnp.where(pick, neg, grp)
    cand = jnp.where(keep.reshape(N_GROUPS, 1, tn), sel3, neg).reshape(ne, tn)
    ioe = lax.broadcasted_iota(jnp.int32, cand.shape, 0)
    idxs, gates = [], []
    for _ in range(TOP_K):
        m = jnp.max(cand, axis=0, keepdims=True)
        e = jnp.min(jnp.where(cand == m, ioe, ne), axis=0, keepdims=True)
        pick = ioe == e
        idxs.append(e)
        gates.append(jnp.sum(jnp.where(pick, scores, 0.0), axis=0, keepdims=True))
        cand = jnp.where(pick, neg, cand)
    gate = jnp.concatenate(gates, axis=0)
    idx_ref[...] = jnp.concatenate(idxs, axis=0)
    gate_ref[...] = gate / jnp.sum(gate, axis=0, keepdims=True) * ROUTED_SCALE


def _route(logits_t, bias_col, tn):
    ne, t = logits_t.shape
    return pl.pallas_call(
        _route_kernel,
        grid=(t // tn,),
        in_specs=[pl.BlockSpec((ne, tn), lambda i: (0, i)), pl.BlockSpec((ne, 1), lambda i: (0, 0))],
        out_specs=[pl.BlockSpec((TOP_K, tn), lambda i: (0, i)), pl.BlockSpec((TOP_K, tn), lambda i: (0, i))],
        out_shape=[jax.ShapeDtypeStruct((TOP_K, t), jnp.int32), jax.ShapeDtypeStruct((TOP_K, t), F32)],
        compiler_params=_cparams("parallel"),
        name="route",
    )(logits_t, bias_col)


def _experts_kernel(be_ref, np_ref,
                    tok_ref, tokn_ref, w_ref, wg0_ref, wu0_ref, wd0_ref, wg1_ref, wu1_ref, wd1_ref, h_hbm,
                    o_ref, xg, wgc, wuc, wdc, gsem):
    s = pl.program_id(0)
    n_pairs = np_ref[0]
    nct = xg.shape[1] // EXPERT_BLOCK
    rows = nct * EXPERT_BLOCK
    w_refs = ((wg0_ref, wu0_ref, wd0_ref), (wg1_ref, wu1_ref, wd1_ref))

    def gather_copy(tref, p, j):
        src = h_hbm.at[pl.ds(pl.multiple_of(tref[0, p, j] * nct, nct), nct)]
        return pltpu.make_async_copy(src, xg.at[p, pl.ds(j * nct, nct)], gsem.at[p])

    def gather_wait(p):
        pltpu.make_async_copy(h_hbm.at[pl.ds(0, rows)], xg.at[p], gsem.at[p]).wait()

    @pl.when(s >= n_pairs)
    def _():
        o_ref[...] = jnp.zeros(o_ref.shape, F32)

    @pl.when(s < n_pairs)
    def _():
        @pl.when(s == 0)
        def _():
            for p in range(2):
                for j in range(EXPERT_BLOCK):
                    gather_copy(tok_ref, p, j).start(priority=j % 2)

        for p in range(2):
            blk = 2 * s + p
            wg_ref, wu_ref, wd_ref = w_refs[p]
            gather_wait(p)

            @pl.when((blk == 0) | (be_ref[blk] != be_ref[jnp.maximum(blk - 1, 0)]))
            def _():
                wgc[...] = wg_ref[0].astype(BF16)
                wuc[...] = wu_ref[0].astype(BF16)
                wdc[...] = wd_ref[0].astype(BF16)

            xb = jnp.concatenate([xg[p, pl.ds(c, EXPERT_BLOCK, stride=nct), :] for c in range(nct)],
                                 axis=-1).astype(BF16)
            for j in range(EXPERT_BLOCK):
                gather_copy(tokn_ref, p, j).start(priority=j % 2)
            hmid = _silu(jnp.dot(xb, wgc[...], preferred_element_type=F32)) * jnp.dot(xb, wuc[...],
                                                                                      preferred_element_type=F32)
            out = _dot(hmid, wdc[...])
            eye = (lax.broadcasted_iota(jnp.int32, (EXPERT_BLOCK, EXPERT_BLOCK), 0)
                   == lax.broadcasted_iota(jnp.int32, (EXPERT_BLOCK, EXPERT_BLOCK), 1))
            w_col = jnp.sum(jnp.where(eye, w_ref[0, p:p + 1, :], 0.0), axis=1, keepdims=True)
            out = out * w_col
            for c in range(nct):
                o_ref[pl.ds(p * rows + c, EXPERT_BLOCK, stride=nct), :] = out[:, c * LANES:(c + 1) * LANES]

        @pl.when(s == n_pairs - 1)
        def _():
            for p in range(2):
                gather_wait(p)


def _experts(hffn, block_e, n_pairs, row_tok, row_w, wg, wu, wd):
    d = wg.shape[1]
    nct = d // LANES
    npairs = row_tok.shape[0]
    de = wg.shape[2]
    last = npairs - 1
    smem_blk = lambda f: pl.BlockSpec((1, 2, EXPERT_BLOCK), f, memory_space=pltpu.SMEM)
    wspec = lambda shp, p: pl.BlockSpec((1,) + shp, lambda s, be, npu: (be[2 * s + p], 0, 0))
    pair_rows = 2 * EXPERT_BLOCK * nct
    grid_spec = pltpu.PrefetchScalarGridSpec(
        num_scalar_prefetch=2,
        grid=(npairs,),
        in_specs=[smem_blk(lambda s, be, npu: (s, 0, 0)),
                  smem_blk(lambda s, be, npu: (jnp.minimum(s + 1, last), 0, 0)),
                  pl.BlockSpec((1, 2, EXPERT_BLOCK), lambda s, be, npu: (s, 0, 0)),
                  wspec((d, de), 0), wspec((d, de), 0), wspec((de, d), 0),
                  wspec((d, de), 1), wspec((d, de), 1), wspec((de, d), 1),
                  pl.BlockSpec(memory_space=pl.ANY)],
        out_specs=pl.BlockSpec((pair_rows, LANES), lambda s, be, npu: (s, 0)),
        scratch_shapes=[pltpu.VMEM((2, EXPERT_BLOCK * nct, LANES), F32),
                        pltpu.VMEM((d, de), BF16), pltpu.VMEM((d, de), BF16), pltpu.VMEM((de, d), BF16),
                        pltpu.SemaphoreType.DMA((2,))],
    )
    return pl.pallas_call(
        _experts_kernel,
        grid_spec=grid_spec,
        out_shape=jax.ShapeDtypeStruct((npairs * pair_rows, LANES), F32),
        compiler_params=_cparams("arbitrary"),
        name="experts",
    )(block_e, n_pairs, row_tok, row_tok, row_w, wg, wu, wd, wg, wu, wd, hffn)


def _combine_kernel(pos_ref, posn_ref, x1_ref, ys_ref, npost_ref, g5_ref, rows_hbm, o_ref, buf, sem):
    i = pl.program_id(0)
    tm = x1_ref.shape[1]
    nct = x1_ref.shape[2] // LANES
    cur = i % 2
    nxt = 1 - cur

    def issue(pref, b):
        def body(t, carry):
            for k in range(TOP_K):
                src = rows_hbm.at[pl.ds(pl.multiple_of(pref[k, t] * nct, nct), nct)]
                dst = buf.at[b, pl.ds(pl.multiple_of((k * tm + t) * nct, nct), nct)]
                pltpu.make_async_copy(src, dst, sem.at[b]).start(priority=k % 2)
            return carry
        lax.fori_loop(0, tm, body, 0)

    @pl.when(i == 0)
    def _():
        issue(pos_ref, cur)

    @pl.when(i + 1 < pl.num_programs(0))
    def _():
        issue(posn_ref, nxt)

    pltpu.make_async_copy(rows_hbm.at[pl.ds(0, TOP_K * tm * nct)], buf.at[cur], sem.at[cur]).wait()
    routed = []
    for c in range(nct):
        acc = buf[cur, pl.ds(c, tm, stride=nct), :]
        for k in range(1, TOP_K):
            acc = acc + buf[cur, pl.ds(k * tm * nct + c, tm, stride=nct), :]
        routed.append(acc)
    y = ys_ref[0] + jnp.concatenate(routed, axis=-1)
    o_ref[0] = x1_ref[0] + g5_ref[0] * _rms(y, npost_ref[...])


def _combine(x1, ys, rows, pos, npost, g5, tm):
    b, s, d = x1.shape
    nt = s // tm
    nct = d // LANES
    last = b * nt - 1
    tok = pl.BlockSpec((1, tm, d), lambda i: (i // nt, i % nt, 0))
    pos_blk = lambda f: pl.BlockSpec((TOP_K, tm), f, memory_space=pltpu.SMEM)
    return pl.pallas_call(
        _combine_kernel,
        grid=(b * nt,),
        in_specs=[pos_blk(lambda i: (0, i)), pos_blk(lambda i: (0, jnp.minimum(i + 1, last))), tok, tok,
                  pl.BlockSpec((1, d), lambda i: (0, 0)), pl.BlockSpec((1, 1, d), lambda i: (i // nt, 0, 0)),
                  pl.BlockSpec(memory_space=pl.ANY)],
        out_specs=tok,
        out_shape=jax.ShapeDtypeStruct((b, s, d), F32),
        scratch_shapes=[pltpu.VMEM((2, TOP_K * tm * nct, LANES), F32), pltpu.SemaphoreType.DMA((2,))],
        compiler_params=_cparams("arbitrary"),
        name="combine",
    )(pos, pos, x1, ys, npost, g5, rows)


def _dispatch_plan(idx_t, gate_t):
    k, t = idx_t.shape
    n_asg = k * t
    nb = n_asg // EXPERT_BLOCK + N_EXPERTS
    flat_e = idx_t.reshape(-1)
    order = jnp.argsort(flat_e).astype(jnp.int32)
    counts = jnp.zeros((N_EXPERTS,), jnp.int32).at[flat_e].add(1)
    padded = (counts + EXPERT_BLOCK - 1) // EXPERT_BLOCK * EXPERT_BLOCK
    start = jnp.cumsum(counts) - counts
    pend = jnp.cumsum(padded)
    pstart = pend - padded
    blk0 = jnp.arange(nb, dtype=jnp.int32) * EXPERT_BLOCK
    block_e = jnp.minimum(jnp.sum((pend[None, :] <= blk0[:, None]).astype(jnp.int32), axis=1), N_EXPERTS - 1)
    of_block = block_e[:, None] == jnp.arange(N_EXPERTS, dtype=jnp.int32)[None, :]
    per_block = lambda v: jnp.sum(jnp.where(of_block, v[None, :], 0), axis=1)
    assert nb % 2 == 0
    n_pairs = ((pend[-1] // EXPERT_BLOCK + 1) // 2).astype(jnp.int32).reshape(1)
    pos = blk0[:, None] - per_block(pstart)[:, None] + jnp.arange(EXPERT_BLOCK, dtype=jnp.int32)[None, :]
    valid = pos < per_block(counts)[:, None]
    src = jnp.clip(per_block(start)[:, None] + pos, 0, n_asg - 1)
    asg = order[src]
    row_tok = jnp.where(valid, asg % t, 0).astype(jnp.int32)
    row_w = jnp.where(valid, gate_t.reshape(-1)[asg], 0.0).astype(F32)
    row_id = jnp.arange(nb * EXPERT_BLOCK, dtype=jnp.int32).reshape(nb, EXPERT_BLOCK)
    pos = jnp.zeros((n_asg,), jnp.int32).at[jnp.where(valid, asg, n_asg).reshape(-1)].set(
        row_id.reshape(-1), mode='drop').reshape(k, t)
    shp = (nb // 2, 2, EXPERT_BLOCK)
    return block_e, n_pairs, row_tok.reshape(shp), row_w.reshape(shp), pos


def _pack_in_weights(w_in, ml_i_bias, ml_f_bias, gd_dt_bias):
    d = w_in.shape[0]
    nml = 2 * ML_HEADS * ML_QK + 2 * ML_HEADS * ML_V
    ml_cols = nml + 4 * ML_HEADS
    ngq = GD_HEADS * (2 * GD_QK + GD_V)
    ngz = GD_HEADS * GD_V
    wml = w_in[:, :nml].astype(BF16)
    wgq = w_in[:, ml_cols:ml_cols + ngq].astype(BF16)
    wgz = w_in[:, ml_cols + ngq:ml_cols + ngq + ngz].astype(BF16)
    wg = jnp.zeros((d, GATE_LANES), F32)
    wg = wg.at[:, ML_GATE0:ML_GATE0 + 16].set(w_in[:, nml:ml_cols])
    wg = wg.at[:, GD_GATE0:GD_GATE0 + 16].set(w_in[:, ml_cols + ngq + ngz:])
    gb = jnp.zeros((GATE_LANES,), F32)
    gb = gb.at[ML_GATE0:ML_GATE0 + 16].set(jnp.stack([ml_i_bias, ml_f_bias], axis=1).reshape(-1))
    gb = gb.at[GD_GATE0:GD_GATE0 + 16].set(jnp.stack([gd_dt_bias, jnp.zeros_like(gd_dt_bias)], axis=1).reshape(-1))
    return wml, wgq, wgz, wg.astype(BF16), gb.reshape(1, GATE_LANES)


def _mixer(x, ctx, mod, mod_ctx, norm_pre_mix, w_in, ml_i_bias, ml_f_bias, gd_conv_w, gd_a_log, gd_dt_bias):
    b, s, d = x.shape
    sc = ctx.shape[1]
    wml, wgq, wgz, wg, gb = _pack_in_weights(w_in, ml_i_bias, ml_f_bias, gd_dt_bias)
    nw = norm_pre_mix.reshape(1, d)
    ctx_mod = lambda j: jnp.broadcast_to(mod_ctx[j].reshape(1, 1, d), (b, 1, d))
    ml_c, gq_c, _, g_c = _proj(ctx, nw, ctx_mod(0), ctx_mod(1), wml, wgq, wgz, wg, gb, tm=sc, grid_view=False)
    ml_l, gqv_l, gz_l, g_l, gv_l = _proj(x, nw, mod[0], mod[1], wml, wgq, wgz, wg, gb, tm=512, grid_view=True)

    c0 = jnp.zeros((b, N_CHAINS, ML_QK, 2 * ML_V), F32)
    m0 = jnp.zeros((b, N_CHAINS, 1, 1), F32)
    _, _, c1, m1 = _mlstm(ml_c, g_c, c0, m0)
    hf, hb, _, _ = _mlstm(ml_l, g_l, c1, m1)

    neg_a = jnp.zeros((GATE_LANES,), F32)
    neg_a = neg_a.at[GD_GATE0:GD_GATE0 + 16].set(
        jnp.stack([-jnp.exp(gd_a_log), jnp.zeros_like(gd_a_log)], axis=1).reshape(-1)).reshape(1, GATE_LANES)
    qn_c = _gdconv_ctx(gq_c, gd_conv_w)
    qnv_l = _gdconv_lat(gqv_l, gd_conv_w)
    s0 = jnp.zeros((b, N_CHAINS, GD_QK, GD_V), F32)
    hdim = GD_HEADS * GD_V
    _, _, s1 = _gdn(qn_c, g_c, neg_a, s0, sc // CHUNK, lambda bi, n: (bi, n, 0), (b, sc, hdim))
    rows = s // GRID_W
    cpc = rows // CHUNK
    col_idx = lambda bi, n: (bi, n % cpc, n // cpc)
    ofv, obv, _ = _gdn(qnv_l, gv_l, neg_a, s1, s // CHUNK, col_idx, (b, rows, GRID_W * hdim))
    return hf, hb, ml_l, ofv, obv, gz_l


def kernel(x, c, ctx, c_ctx, w_ada, b_ada, norm_pre_mix, norm_post_mix, norm_pre_ffn, norm_post_ffn, w_in,
           ml_i_bias, ml_f_bias, ml_norm_w, gd_conv_w, gd_a_log, gd_dt_bias, gd_norm_w, w_out, router_w,
           router_bias, w_gate, w_up, w_down, ws_gate, ws_up, ws_down):
    b, s, d = x.shape
    depth = w_ada.shape[0]
    assert depth == 1, "the context stream update of deeper stacks is not implemented"
    ly = 0
    cc = jnp.zeros((16, d), F32).at[:b].set(c).at[b].set(c_ctx)
    mod_all = _ada(cc, w_ada[ly], b_ada[ly])
    mod = [mod_all[:b, j * d:(j + 1) * d].reshape(b, 1, d) for j in range(6)]
    mod_ctx = [mod_all[b, j * d:(j + 1) * d] for j in range(6)]

    hf, hb, ml_l, of, ob, gz_l = _mixer(x, ctx, mod, mod_ctx, norm_pre_mix[ly], w_in[ly], ml_i_bias[ly],
                                        ml_f_bias[ly], gd_conv_w[ly], gd_a_log[ly], gd_dt_bias[ly])

    row = lambda v: v.reshape(1, -1)
    x1, hffn, logits_t, ys = _post(
        x, hf, hb, ml_l, of, ob, gz_l, row(ml_norm_w[ly]), row(jnp.tile(gd_norm_w[ly], GD_HEADS)),
        w_out[ly].astype(BF16), row(norm_post_mix[ly]), mod[2], row(norm_pre_ffn[ly]), mod[3], mod[4],
        router_w[ly].T.astype(BF16), ws_gate[ly].astype(BF16), ws_up[ly].astype(BF16), ws_down[ly].astype(BF16),
        tm=512)

    idx_t, gate_t = _route(logits_t, router_bias[ly].reshape(-1, 1), tn=512)
    block_e, n_pairs, row_tok, row_w, pos = _dispatch_plan(idx_t, gate_t)
    t = b * s
    rows = _experts(hffn.reshape(t * (d // LANES), LANES), block_e, n_pairs, row_tok, row_w,
                    w_gate[ly], w_up[ly], w_down[ly])
    return _combine(x1, ys, rows, pos, row(norm_post_ffn[ly]), mod[5], tm=256)
```

```python
import functools

import jax
import jax.numpy as jnp
from jax import lax
from jax.experimental import pallas as pl
from jax.experimental.pallas import tpu as pltpu

EPS = 1e-6
CHUNK = 64
GRID_W = 64
ML_HEADS, ML_QK, ML_V = 4, 64, 128
GD_HEADS, GD_QK, GD_V = 4, 128, 128
CONV_W = 5
N_EXPERTS, TOP_K, N_GROUPS, TOPK_GROUPS = 256, 8, 8, 4
ROUTED_SCALE = 2.5
EXPERT_BLOCK = 128
N_CHAINS = 8
ROW_COPY_PRIORITY = 1
LANES = 128
GATE_LANES = LANES
ML_GATE0, GD_GATE0 = 0, 16

F32 = jnp.float32
BF16 = jnp.bfloat16
HI = lax.Precision.HIGHEST
VMEM_LIMIT = 56 * 1024 * 1024


def _cparams(*sem):
    return pltpu.CompilerParams(dimension_semantics=sem, vmem_limit_bytes=VMEM_LIMIT)


def _dot(a, b):
    return jnp.dot(a.astype(BF16), b.astype(BF16), preferred_element_type=F32)


def _dot_nt(a, b):
    return lax.dot_general(a.astype(BF16), b.astype(BF16), (((1,), (1,)), ((), ())), preferred_element_type=F32)


def _dot_tn(a, b):
    return lax.dot_general(a.astype(BF16), b.astype(BF16), (((0,), (0,)), ((), ())), preferred_element_type=F32)


def _dot_hi(a, b):
    return jnp.dot(a, b, precision=HI, preferred_element_type=F32)


def _dot_nt_hi(a, b):
    return lax.dot_general(a, b, (((1,), (1,)), ((), ())), precision=HI, preferred_element_type=F32)


def _transpose_hi(x):
    n = x.shape[1]
    eye = (lax.broadcasted_iota(jnp.int32, (n, n), 0) == lax.broadcasted_iota(jnp.int32, (n, n), 1)).astype(F32)
    return _dot_nt_hi(eye, x)


def _silu(x):
    return x * jax.nn.sigmoid(x)


def _past_mask(reverse):
    t = lax.broadcasted_iota(jnp.int32, (CHUNK, CHUNK), 0)
    s = lax.broadcasted_iota(jnp.int32, (CHUNK, CHUNK), 1)
    return (s >= t, s > t) if reverse else (s <= t, s < t)


def _ada_kernel(c_ref, w_ref, b_ref, o_ref):
    o_ref[...] = _dot(_silu(c_ref[...]), w_ref[...]) + b_ref[...]


def _ada(cc, w_ada, b_ada):
    rows, d = cc.shape
    n = w_ada.shape[1]
    tn = 1536
    return pl.pallas_call(
        _ada_kernel,
        grid=(n // tn,),
        in_specs=[pl.BlockSpec((rows, d), lambda j: (0, 0)),
                  pl.BlockSpec((d, tn), lambda j: (0, j)),
                  pl.BlockSpec((1, tn), lambda j: (0, j))],
        out_specs=pl.BlockSpec((rows, tn), lambda j: (0, j)),
        out_shape=jax.ShapeDtypeStruct((rows, n), F32),
        compiler_params=_cparams("arbitrary"),
        name="ada",
    )(cc, w_ada, b_ada.reshape(1, n))


GRID_PITCH = GRID_W + 8


def _to_grid_view(src_ref, dst_ref):
    ng = src_ref.shape[0]
    r = src_ref.shape[1] // GRID_PITCH
    for c in range(GRID_W):
        for g in range(ng):
            lo = (c * ng + g) * LANES
            dst_ref[0, :, lo:lo + LANES] = src_ref[g, pl.ds(c, r, stride=GRID_PITCH), :]


def _from_grid_view(src_ref, dst_ref):
    ng = dst_ref.shape[0]
    r = dst_ref.shape[1] // GRID_PITCH
    for c in range(GRID_W):
        for g in range(ng):
            lo = (c * ng + g) * LANES
            dst_ref[g, pl.ds(c, r, stride=GRID_PITCH), :] = src_ref[0, :, lo:lo + LANES]


def _pitched_rows(ref, g):
    r = ref.shape[1] // GRID_PITCH
    return jnp.concatenate([ref[g, i * GRID_PITCH:i * GRID_PITCH + GRID_W, :] for i in range(r)], axis=0)


def _proj_kernel(grid_view, x_ref, nw_ref, sh_ref, sc_ref, wml_ref, wgq_ref, wgz_ref, wg_ref, gb_ref, *refs):
    if grid_view:
        ml_ref, gqv_ref, gz_ref, g_ref, gv_ref, gq_scr, g_scr = refs
    else:
        ml_ref, gq_ref, gz_ref, g_ref = refs
    x = x_ref[0]
    xn = x * lax.rsqrt(jnp.mean(x * x, axis=-1, keepdims=True) + EPS) * nw_ref[...]
    h = (xn * (1.0 + sc_ref[0]) + sh_ref[0]).astype(BF16)
    ml_ref[0] = jnp.dot(h, wml_ref[...], preferred_element_type=F32)
    gz_ref[0] = jnp.dot(h, wgz_ref[...], preferred_element_type=F32)
    gates = jnp.dot(h, wg_ref[...], preferred_element_type=F32) + gb_ref[...]
    g_ref[0] = gates
    gq = jnp.dot(h, wgq_ref[...], preferred_element_type=F32)
    if grid_view:
        for r in range(x.shape[0] // GRID_W):
            rows = slice(r * GRID_W, (r + 1) * GRID_W)
            prow = slice(r * GRID_PITCH, r * GRID_PITCH + GRID_W)
            g_scr[0, prow, :] = gates[rows]
            for g in range(gq_scr.shape[0]):
                gq_scr[g, prow, :] = gq[rows, g * LANES:(g + 1) * LANES]
        _to_grid_view(gq_scr, gqv_ref)
        _to_grid_view(g_scr, gv_ref)
    else:
        gq_ref[0] = gq


def _proj(x, norm_w, shift, scale, wml, wgq, wgz, wg, gbias, tm, grid_view):
    b, s, d = x.shape
    nml, ngq, ngz = wml.shape[1], wgq.shape[1], wgz.shape[1]
    full = lambda shp: pl.BlockSpec(shp, lambda bi, i: (0,) * len(shp))
    tok = lambda n: pl.BlockSpec((1, tm, n), lambda bi, i: (bi, i, 0))
    mod = pl.BlockSpec((1, 1, d), lambda bi, i: (bi, 0, 0))
    if grid_view:
        rt = tm // GRID_W
        view = lambda n: pl.BlockSpec((1, rt, GRID_W * n), lambda bi, i: (bi, i, 0))
        vshape = lambda n: jax.ShapeDtypeStruct((b, s // GRID_W, GRID_W * n), F32)
        out_specs = [tok(nml), view(ngq), tok(ngz), tok(GATE_LANES), view(GATE_LANES)]
        out_shape = [jax.ShapeDtypeStruct((b, s, nml), F32), vshape(ngq), jax.ShapeDtypeStruct((b, s, ngz), F32),
                     jax.ShapeDtypeStruct((b, s, GATE_LANES), F32), vshape(GATE_LANES)]
        scratch = [pltpu.VMEM((ngq // LANES, rt * GRID_PITCH, LANES), F32), pltpu.VMEM((1, rt * GRID_PITCH, LANES), F32)]
    else:
        out_specs = [tok(nml), tok(ngq), tok(ngz), tok(GATE_LANES)]
        out_shape = [jax.ShapeDtypeStruct((b, s, n), F32) for n in (nml, ngq, ngz, GATE_LANES)]
        scratch = []
    return pl.pallas_call(
        functools.partial(_proj_kernel, grid_view),
        grid=(b, s // tm),
        in_specs=[tok(d), full((1, d)), mod, mod, full((d, nml)), full((d, ngq)), full((d, ngz)),
                  full((d, GATE_LANES)), full((1, GATE_LANES))],
        out_specs=out_specs,
        out_shape=out_shape,
        scratch_shapes=scratch,
        compiler_params=_cparams("parallel", "arbitrary"),
        name="proj",
    )(x, norm_w, shift, scale, wml, wgq, wgz, wg, gbias)


def _mlstm_kernel(mlf_ref, mlb_ref, gf_ref, gb_ref, c0_ref, m0_ref, hf_ref, hb_ref, cn_ref, mn_ref, c_scr, m_scr):
    i = pl.program_id(1)

    @pl.when(i == 0)
    def _():
        c_scr[...] = c0_ref[0]
        m_scr[...] = m0_ref[0]

    past = [_past_mask(d == 1)[0] for d in range(2)]
    g = [r[0] for r in (gf_ref, gb_ref)]
    ls = [jax.nn.log_sigmoid(x) for x in g]
    bcol = [_dot_hi(past[d].astype(F32), ls[d]) for d in range(2)]
    tot = [jnp.sum(x, axis=0, keepdims=True) for x in ls]
    g_t = [_transpose_hi(x) for x in g]
    b_t = [_transpose_hi(x) for x in bcol]

    chains = [(d, hd) for d in range(2) for hd in range(ML_HEADS)]
    nc = range(len(chains))
    ml_refs, h_refs = (mlf_ref, mlb_ref), (hf_ref, hb_ref)
    k0, v0 = ML_HEADS * ML_QK, 2 * ML_HEADS * ML_QK
    ones_col = (lax.broadcasted_iota(jnp.int32, (CHUNK, ML_V), 1) == 0).astype(F32)
    q, k, v, i_col, b_col, b_end, log_d = [], [], [], [], [], [], []
    for d, hd in chains:
        ci = ML_GATE0 + d * 8 + hd
        cf = ci + ML_HEADS
        q.append(ml_refs[d][0, :, hd * ML_QK:(hd + 1) * ML_QK])
        k.append(ml_refs[d][0, :, k0 + hd * ML_QK:k0 + (hd + 1) * ML_QK] * (ML_QK ** -0.5))
        v.append(jnp.concatenate([ml_refs[d][0, :, v0 + hd * ML_V:v0 + (hd + 1) * ML_V], ones_col], axis=-1))
        i_col.append(g[d][:, ci:ci + 1])
        b_col.append(bcol[d][:, cf:cf + 1])
        b_end.append(tot[d][:, cf:cf + 1])
        log_d.append(jnp.where(past[d], b_col[-1] - b_t[d][cf:cf + 1, :] + g_t[d][ci:ci + 1, :], -jnp.inf))
    c_st = [c_scr[c] for c in nc]
    m_st = [m_scr[c] for c in nc]
    log_prev = [b_col[c] + m_st[c] for c in nc]
    m_t = [jnp.maximum(log_prev[c], jnp.max(log_d[c], axis=-1, keepdims=True)) for c in nc]
    qk = [_dot_nt(q[c], k[c]) for c in nc]
    qc = [_dot(q[c], c_st[c]) for c in nc]
    s = [qk[c] * jnp.exp(log_d[c] - m_t[c]) for c in nc]
    w_prev = [jnp.exp(log_prev[c] - m_t[c]) for c in nc]
    sv = [_dot(s[c], v[c]) for c in nc]
    log_s = [b_end[c] - b_col[c] + i_col[c] for c in nc]
    m_new = [jnp.maximum(b_end[c] + m_st[c], jnp.max(log_s[c], axis=0, keepdims=True)) for c in nc]
    kw = [k[c] * jnp.exp(log_s[c] - m_new[c]) for c in nc]
    w_c = [jnp.exp(b_end[c] + m_st[c] - m_new[c]) for c in nc]
    kv = [_dot_tn(kw[c], v[c]) for c in nc]
    numden = [sv[c] + w_prev[c] * qc[c] for c in nc]
    scale = [1.0 / jnp.maximum(jnp.abs(numden[c][:, ML_V:ML_V + 1]), jnp.exp(-m_t[c])) for c in nc]
    for c, (d, hd) in enumerate(chains):
        h_refs[d][0, :, hd * ML_V:(hd + 1) * ML_V] = numden[c][:, :ML_V] * scale[c]
        c_scr[c] = w_c[c] * c_st[c] + kv[c]
        m_scr[c] = m_new[c]

    @pl.when(i == pl.num_programs(1) - 1)
    def _():
        cn_ref[0] = c_scr[...]
        mn_ref[0] = m_scr[...]


def _mlstm(ml, gates, c0, m0):
    b, s, nml = ml.shape
    nc = s // CHUNK
    fwd = lambda n: pl.BlockSpec((1, CHUNK, n), lambda bi, i: (bi, i, 0))
    bwd = lambda n: pl.BlockSpec((1, CHUNK, n), lambda bi, i: (bi, nc - 1 - i, 0))
    st = lambda shp: pl.BlockSpec((1,) + shp, lambda bi, i: (bi,) + (0,) * len(shp))
    hdim = ML_HEADS * ML_V
    cshape = (N_CHAINS, ML_QK, 2 * ML_V)
    return pl.pallas_call(
        _mlstm_kernel,
        grid=(b, nc),
        in_specs=[fwd(nml), bwd(nml), fwd(GATE_LANES), bwd(GATE_LANES), st(cshape), st((N_CHAINS, 1, 1))],
        out_specs=[fwd(hdim), bwd(hdim), st(cshape), st((N_CHAINS, 1, 1))],
        out_shape=[jax.ShapeDtypeStruct((b, s, hdim), F32), jax.ShapeDtypeStruct((b, s, hdim), F32),
                   jax.ShapeDtypeStruct(c0.shape, F32), jax.ShapeDtypeStruct(m0.shape, F32)],
        scratch_shapes=[pltpu.VMEM(cshape, F32), pltpu.VMEM((N_CHAINS, 1, 1), F32)],
        compiler_params=_cparams("parallel", "arbitrary"),
        name="mlstm",
    )(ml, ml, gates, gates, c0, m0)


def _gdconv_kernel(has_halo, *refs):
    if has_halo:
        x_ref, prev_ref, next_ref, w_ref, o_ref, xp_ref = refs
    else:
        x_ref, w_ref, o_ref, xp_ref = refs
    rows = x_ref.shape[1]
    nch = x_ref.shape[2]
    pad = 8
    zero = jnp.zeros((pad, nch), F32)
    if has_halo:
        c = pl.program_id(1)
        xp_ref[0:pad, :] = jnp.where(c > 0, prev_ref[0], zero)
        xp_ref[pad + rows:, :] = jnp.where(c < pl.num_programs(1) - 1, next_ref[0], zero)
    else:
        xp_ref[0:pad, :] = zero
        xp_ref[pad + rows:, :] = zero
    xp_ref[pad:pad + rows, :] = x_ref[0]
    half = CONV_W // 2
    for lc in range(nch // 128):
        sl = slice(lc * 128, (lc + 1) * 128)
        acc = None
        for j in range(CONV_W):
            term = xp_ref[pad - half + j:pad - half + j + rows, sl] * w_ref[j:j + 1, sl]
            acc = term if acc is None else acc + term
        y = _silu(acc)
        if lc < 2 * GD_HEADS:
            y = y * lax.rsqrt(jnp.sum(y * y, axis=-1, keepdims=True) + EPS)
        if lc < GD_HEADS:
            y = y * (GD_QK ** -0.5)
        o_ref[0, :, sl] = y


def _gdconv_ctx(qkv, conv_w):
    b, s, nch = qkv.shape
    return pl.pallas_call(
        functools.partial(_gdconv_kernel, False),
        grid=(b,),
        in_specs=[pl.BlockSpec((1, s, nch), lambda bi: (bi, 0, 0)), pl.BlockSpec((CONV_W, nch), lambda bi: (0, 0))],
        out_specs=pl.BlockSpec((1, s, nch), lambda bi: (bi, 0, 0)),
        out_shape=jax.ShapeDtypeStruct((b, s, nch), F32),
        scratch_shapes=[pltpu.VMEM((s + 16, nch), F32)],
        compiler_params=_cparams("parallel"),
        name="gdconv_ctx",
    )(qkv, conv_w)


def _gdconv_lat(view, conv_w):
    b, rows, wn = view.shape
    nch = wn // GRID_W
    rb = rows // 8
    return pl.pallas_call(
        functools.partial(_gdconv_kernel, True),
        grid=(b, GRID_W),
        in_specs=[pl.BlockSpec((1, rows, nch), lambda bi, c: (bi, 0, c)),
                  pl.BlockSpec((1, 8, nch), lambda bi, c: (bi, rb - 1, jnp.maximum(c - 1, 0))),
                  pl.BlockSpec((1, 8, nch), lambda bi, c: (bi, 0, jnp.minimum(c + 1, GRID_W - 1))),
                  pl.BlockSpec((CONV_W, nch), lambda bi, c: (0, 0))],
        out_specs=pl.BlockSpec((1, rows, nch), lambda bi, c: (bi, 0, c)),
        out_shape=jax.ShapeDtypeStruct(view.shape, F32),
        scratch_shapes=[pltpu.VMEM((rows + 16, nch), F32)],
        compiler_params=_cparams("parallel", "arbitrary"),
        name="gdconv_lat",
    )(view, view, view, conv_w)


SOLVE_BLOCK = 16


def _hi_lo(x):
    hi = x.astype(BF16).astype(F32)
    return hi, x - hi


def _dot_split(a, b):
    a_hi, a_lo = _hi_lo(a)
    b_hi, b_lo = _hi_lo(b)
    lhs = jnp.concatenate([a_hi, a_hi, a_lo], axis=1).astype(BF16)
    rhs = jnp.concatenate([b_hi, b_lo, b_hi], axis=0).astype(BF16)
    return jnp.dot(lhs, rhs, preferred_element_type=F32)


def _unit_triangular_inverses(ns):
    c = ns[0].shape[0]
    row = lax.broadcasted_iota(jnp.int32, (c, c), 0)
    col = lax.broadcasted_iota(jnp.int32, (c, c), 1)
    eye = (row == col).astype(F32)
    in_diag_block = (row // SOLVE_BLOCK) == (col // SOLVE_BLOCK)
    mm = lambda a_list, b_list: [_dot_split(a, b) for a, b in zip(a_list, b_list)]

    n_d = [jnp.where(in_diag_block, n, 0.0) for n in ns]
    x = n_d
    d_inv = [eye - n for n in n_d]
    for _ in range(SOLVE_BLOCK.bit_length() - 2):
        x = mm(x, x)
        d_inv = [d + dx for d, dx in zip(d_inv, mm(d_inv, x))]
    m = mm(d_inv, [n - nd for n, nd in zip(ns, n_d)])
    assert c // SOLVE_BLOCK == 4
    i_minus_m = [eye - mi for mi in m]
    q = [a + b for a, b in zip(i_minus_m, mm(i_minus_m, mm(m, m)))]
    return mm(q, d_inv)


def _gdn_kernel(qf_ref, qb_ref, gf_ref, gb_ref, na_ref, s0_ref, of_ref, ob_ref, sn_ref, s_scr):
    i = pl.program_id(1)

    @pl.when(i == 0)
    def _():
        s_scr[...] = s0_ref[0]

    eye = (lax.broadcasted_iota(jnp.int32, (CHUNK, CHUNK), 0)
           == lax.broadcasted_iota(jnp.int32, (CHUNK, CHUNK), 1)).astype(F32)
    nqk = GD_HEADS * GD_QK
    masks = [_past_mask(d == 1) for d in range(2)]
    gates = [r[0] for r in (gf_ref, gb_ref)]
    glog = [na_ref[...] * jax.nn.softplus(g) for g in gates]
    beta_all = [jax.nn.sigmoid(g) for g in gates]
    gcum = [_dot_hi(masks[d][0].astype(F32), glog[d]) for d in range(2)]
    gtot = [jnp.sum(g, axis=0, keepdims=True) for g in glog]
    gcum_t = [_transpose_hi(g) for g in gcum]

    chains = [(d, hd) for d in range(2) for hd in range(GD_HEADS)]
    x_refs, o_refs = (qf_ref, qb_ref), (of_ref, ob_ref)
    q, k, v, g_col, beta, g_end, decay = [], [], [], [], [], [], []
    for d, hd in chains:
        ca = GD_GATE0 + d * 8 + hd
        q.append(x_refs[d][0, :, hd * GD_QK:(hd + 1) * GD_QK])
        k.append(x_refs[d][0, :, nqk + hd * GD_QK:nqk + (hd + 1) * GD_QK])
        v.append(x_refs[d][0, :, 2 * nqk + hd * GD_V:2 * nqk + (hd + 1) * GD_V])
        g_col.append(gcum[d][:, ca:ca + 1])
        beta.append(beta_all[d][:, ca + GD_HEADS:ca + GD_HEADS + 1])
        g_end.append(gtot[d][:, ca:ca + 1])
        decay.append(jnp.exp(jnp.where(masks[d][0], g_col[-1] - gcum_t[d][ca:ca + 1, :], -jnp.inf)))
    nc = range(len(chains))
    kk = [_dot_nt(k[c], k[c]) for c in nc]
    xs = [jnp.where(masks[chains[c][0]][1], beta[c] * kk[c] * decay[c], 0.0) for c in nc]
    ps = _unit_triangular_inverses(xs)
    uw = [_dot_split(ps[c], jnp.concatenate([v[c] * beta[c], k[c] * (beta[c] * jnp.exp(g_col[c]))], axis=-1))
          for c in nc]
    qk = [_dot_nt(q[c], k[c]) * decay[c] for c in nc]
    s_st = [s_scr[c] for c in nc]
    v_new = [uw[c][:, :GD_V] - _dot(uw[c][:, GD_V:], s_st[c]) for c in nc]
    o_loc = [_dot(q[c] * jnp.exp(g_col[c]), s_st[c]) for c in nc]
    o_new = [o_loc[c] + _dot(qk[c], v_new[c]) for c in nc]
    s_new = [s_st[c] * jnp.exp(g_end[c]) + _dot_tn(k[c] * jnp.exp(g_end[c] - g_col[c]), v_new[c]) for c in nc]
    for c, (d, hd) in enumerate(chains):
        o_refs[d][0, :, hd * GD_V:(hd + 1) * GD_V] = o_new[c]
        s_scr[c] = s_new[c]

    @pl.when(i == pl.num_programs(1) - 1)
    def _():
        sn_ref[0] = s_scr[...]


def _gdn(qkv_view, gates_view, neg_a, s0, nc, idx_fn, out_view_shape):
    b = qkv_view.shape[0]
    nqkv = 2 * GD_HEADS * GD_QK + GD_HEADS * GD_V
    hdim = GD_HEADS * GD_V
    fwd = lambda n: pl.BlockSpec((1, CHUNK, n), lambda bi, i: idx_fn(bi, i))
    bwd = lambda n: pl.BlockSpec((1, CHUNK, n), lambda bi, i: idx_fn(bi, nc - 1 - i))
    st = pl.BlockSpec((1, N_CHAINS, GD_QK, GD_V), lambda bi, i: (bi, 0, 0, 0))
    return pl.pallas_call(
        _gdn_kernel,
        grid=(b, nc),
        in_specs=[fwd(nqkv), bwd(nqkv), fwd(GATE_LANES), bwd(GATE_LANES),
                  pl.BlockSpec((1, GATE_LANES), lambda bi, i: (0, 0)), st],
        out_specs=[fwd(hdim), bwd(hdim), st],
        out_shape=[jax.ShapeDtypeStruct(out_view_shape, F32), jax.ShapeDtypeStruct(out_view_shape, F32),
                   jax.ShapeDtypeStruct(s0.shape, F32)],
        scratch_shapes=[pltpu.VMEM((N_CHAINS, GD_QK, GD_V), F32)],
        compiler_params=_cparams("parallel", "arbitrary"),
        name="gdn",
    )(qkv_view, qkv_view, gates_view, gates_view, neg_a, s0)


def _head_rms(t, nheads, width):
    outs = []
    for hd in range(nheads):
        th = t[:, hd * width:(hd + 1) * width]
        outs.append(th * lax.rsqrt(jnp.mean(th * th, axis=-1, keepdims=True) + EPS))
    return jnp.concatenate(outs, axis=-1)


def _rms(t, w):
    return t * lax.rsqrt(jnp.mean(t * t, axis=-1, keepdims=True) + EPS) * w


def _post_kernel(x_ref, hf_ref, hb_ref, og_ref, of_ref, ob_ref, z_ref, mlw_ref, gdw_ref, wout_ref,
                 npost_ref, g2_ref, npre_ref, sh_ref, sc_ref, rwt_ref, wsg_ref, wsu_ref, wsd_ref,
                 x1_ref, hffn_ref, lt_ref, ys_ref, of_scr, ob_scr):
    ml_y = _head_rms(hf_ref[0] + hb_ref[0], ML_HEADS, ML_V) * mlw_ref[...] * jax.nn.sigmoid(og_ref[0])
    _from_grid_view(of_ref, of_scr)
    _from_grid_view(ob_ref, ob_scr)
    o_sum = jnp.concatenate([_pitched_rows(of_scr, g) + _pitched_rows(ob_scr, g) for g in range(of_scr.shape[0])],
                            axis=-1)
    gd_y = _head_rms(o_sum, GD_HEADS, GD_V) * gdw_ref[...] * _silu(z_ref[0])
    y = _dot(jnp.concatenate([ml_y, gd_y], axis=-1), wout_ref[...])
    x1 = x_ref[0] + g2_ref[0] * _rms(y, npost_ref[...])
    x1_ref[0] = x1
    hffn = _rms(x1, npre_ref[...]) * (1.0 + sc_ref[0]) + sh_ref[0]
    nct = hffn.shape[1] // LANES
    for c in range(nct):
        hffn_ref[0, pl.ds(c, hffn.shape[0], stride=nct), :] = hffn[:, c * LANES:(c + 1) * LANES]
    hb = hffn.astype(BF16)
    lt_ref[...] = lax.dot_general(rwt_ref[...], hb, (((1,), (1,)), ((), ())), preferred_element_type=F32)
    hs = _silu(jnp.dot(hb, wsg_ref[...], preferred_element_type=F32)) * jnp.dot(hb, wsu_ref[...],
                                                                                preferred_element_type=F32)
    ys_ref[0] = _dot(hs, wsd_ref[...])


def _post(x, hf, hb, ml, of, ob, gz, mlw, gdw, wout, npost, g2, npre, sh, sc, rwt, wsg, wsu, wsd, tm):
    b, s, d = x.shape
    nt = s // tm
    hw = ML_HEADS * ML_V
    og_blk = (2 * ML_HEADS * ML_QK + ML_HEADS * ML_V) // hw
    tok = lambda n: pl.BlockSpec((1, tm, n), lambda bi, i: (bi, i, 0))
    full = lambda shp: pl.BlockSpec(shp, lambda bi, i: (0,) * len(shp))
    mod = pl.BlockSpec((1, 1, d), lambda bi, i: (bi, 0, 0))
    ne = rwt.shape[0]
    ds = wsg.shape[1]
    gview = pl.BlockSpec((1, tm // GRID_W, GRID_W * hw), lambda bi, i: (bi, i, 0))
    return pl.pallas_call(
        _post_kernel,
        grid=(b, nt),
        in_specs=[tok(d), tok(hw), tok(hw), pl.BlockSpec((1, tm, hw), lambda bi, i: (bi, i, og_blk)),
                  gview, gview, tok(hw), full((1, hw)), full((1, hw)), full((d, d)),
                  full((1, d)), mod, full((1, d)), mod, mod, full((ne, d)), full((d, ds)), full((d, ds)),
                  full((ds, d))],
        out_specs=[tok(d), pl.BlockSpec((1, tm * (d // LANES), LANES), lambda bi, i: (bi, i, 0)),
                   pl.BlockSpec((ne, tm), lambda bi, i: (0, bi * nt + i)), tok(d)],
        out_shape=[jax.ShapeDtypeStruct((b, s, d), F32), jax.ShapeDtypeStruct((b, s * (d // LANES), LANES), F32),
                   jax.ShapeDtypeStruct((ne, b * s), F32), jax.ShapeDtypeStruct((b, s, d), F32)],
        scratch_shapes=[pltpu.VMEM((hw // LANES, tm // GRID_W * GRID_PITCH, LANES), F32)] * 2,
        compiler_params=_cparams("parallel", "arbitrary"),
        name="post",
    )(x, hf, hb, ml, of, ob, gz, mlw, gdw, wout, npost, g2, npre, sh, sc, rwt, wsg, wsu, wsd)


def _route_kernel(lt_ref, bias_ref, idx_ref, gate_ref):
    ne, tn = lt_ref.shape
    gsz = ne // N_GROUPS
    scores = jax.nn.sigmoid(lt_ref[...])
    sel = scores + bias_ref[...]
    neg = -jnp.inf
    sel3 = sel.reshape(N_GROUPS, gsz, tn)
    io3 = lax.broadcasted_iota(jnp.int32, sel3.shape, 1)
    top1 = jnp.max(sel3, axis=1, keepdims=True)
    first = jnp.min(jnp.where(sel3 == top1, io3, gsz), axis=1, keepdims=True)
    top2 = jnp.max(jnp.where(io3 == first, neg, sel3), axis=1, keepdims=True)
    grp = (top1 + top2).reshape(N_GROUPS, tn)
    iog = lax.broadcasted_iota(jnp.int32, grp.shape, 0)
    keep = jnp.zeros(grp.shape, jnp.bool_)
    for _ in range(TOPK_GROUPS):
        m = jnp.max(grp, axis=0, keepdims=True)
        pick = iog == jnp.min(jnp.where(grp == m, iog, N_GROUPS), axis=0, keepdims=True)
        keep = keep | pick
        grp = jnp.where(pick, neg, grp)
    cand = jnp.where(keep.reshape(N_GROUPS, 1, tn), sel3, neg).reshape(ne, tn)
    ioe = lax.broadcasted_iota(jnp.int32, cand.shape, 0)
    idxs, gates = [], []
    for _ in range(TOP_K):
        m = jnp.max(cand, axis=0, keepdims=True)
        e = jnp.min(jnp.where(cand == m, ioe, ne), axis=0, keepdims=True)
        pick = ioe == e
        idxs.append(e)
        gates.append(jnp.sum(jnp.where(pick, scores, 0.0), axis=0, keepdims=True))
        cand = jnp.where(pick, neg, cand)
    gate = jnp.concatenate(gates, axis=0)
    idx_ref[...] = jnp.concatenate(idxs, axis=0)
    gate_ref[...] = gate / jnp.sum(gate, axis=0, keepdims=True) * ROUTED_SCALE


def _route(logits_t, bias_col, tn):
    ne, t = logits_t.shape
    return pl.pallas_call(
        _route_kernel,
        grid=(t // tn,),
        in_specs=[pl.BlockSpec((ne, tn), lambda i: (0, i)), pl.BlockSpec((ne, 1), lambda i: (0, 0))],
        out_specs=[pl.BlockSpec((TOP_K, tn), lambda i: (0, i)), pl.BlockSpec((TOP_K, tn), lambda i: (0, i))],
        out_shape=[jax.ShapeDtypeStruct((TOP_K, t), jnp.int32), jax.ShapeDtypeStruct((TOP_K, t), F32)],
        compiler_params=_cparams("parallel"),
        name="route",
    )(logits_t, bias_col)


def _experts_kernel(be_ref, np_ref,
                    tok_ref, tokn_ref, w_ref, wg0_ref, wu0_ref, wd0_ref, wg1_ref, wu1_ref, wd1_ref, h_hbm,
                    o_ref, xg, wgc, wuc, wdc, gsem):
    s = pl.program_id(0)
    n_pairs = np_ref[0]
    nct = xg.shape[1] // EXPERT_BLOCK
    rows = nct * EXPERT_BLOCK
    w_refs = ((wg0_ref, wu0_ref, wd0_ref), (wg1_ref, wu1_ref, wd1_ref))

    def gather_copy(tref, p, j):
        src = h_hbm.at[pl.ds(pl.multiple_of(tref[0, p, j] * nct, nct), nct)]
        return pltpu.make_async_copy(src, xg.at[p, pl.ds(j * nct, nct)], gsem.at[p])

    def gather_wait(p):
        pltpu.make_async_copy(h_hbm.at[pl.ds(0, rows)], xg.at[p], gsem.at[p]).wait()

    @pl.when(s >= n_pairs)
    def _():
        o_ref[...] = jnp.zeros(o_ref.shape, F32)

    @pl.when(s < n_pairs)
    def _():
        @pl.when(s == 0)
        def _():
            for p in range(2):
                for j in range(EXPERT_BLOCK):
                    gather_copy(tok_ref, p, j).start(priority=ROW_COPY_PRIORITY)

        for p in range(2):
            blk = 2 * s + p
            wg_ref, wu_ref, wd_ref = w_refs[p]
            gather_wait(p)

            @pl.when((blk == 0) | (be_ref[blk] != be_ref[jnp.maximum(blk - 1, 0)]))
            def _():
                wgc[...] = wg_ref[0].astype(BF16)
                wuc[...] = wu_ref[0].astype(BF16)
                wdc[...] = wd_ref[0].astype(BF16)

            xb = jnp.concatenate([xg[p, pl.ds(c, EXPERT_BLOCK, stride=nct), :] for c in range(nct)],
                                 axis=-1).astype(BF16)
            for j in range(EXPERT_BLOCK):
                gather_copy(tokn_ref, p, j).start(priority=ROW_COPY_PRIORITY)
            hmid = _silu(jnp.dot(xb, wgc[...], preferred_element_type=F32)) * jnp.dot(xb, wuc[...],
                                                                                      preferred_element_type=F32)
            out = _dot(hmid, wdc[...])
            eye = (lax.broadcasted_iota(jnp.int32, (EXPERT_BLOCK, EXPERT_BLOCK), 0)
                   == lax.broadcasted_iota(jnp.int32, (EXPERT_BLOCK, EXPERT_BLOCK), 1))
            w_col = jnp.sum(jnp.where(eye, w_ref[0, p:p + 1, :], 0.0), axis=1, keepdims=True)
            out = out * w_col
            for c in range(nct):
                o_ref[pl.ds(p * rows + c, EXPERT_BLOCK, stride=nct), :] = out[:, c * LANES:(c + 1) * LANES]

        @pl.when(s == n_pairs - 1)
        def _():
            for p in range(2):
                gather_wait(p)


def _experts(hffn, block_e, n_pairs, row_tok, row_w, wg, wu, wd):
    d = wg.shape[1]
    nct = d // LANES
    npairs = row_tok.shape[0]
    de = wg.shape[2]
    last = npairs - 1
    smem_blk = lambda f: pl.BlockSpec((1, 2, EXPERT_BLOCK), f, memory_space=pltpu.SMEM)
    wspec = lambda shp, p: pl.BlockSpec((1,) + shp, lambda s, be, npu: (be[2 * s + p], 0, 0))
    pair_rows = 2 * EXPERT_BLOCK * nct
    grid_spec = pltpu.PrefetchScalarGridSpec(
        num_scalar_prefetch=2,
        grid=(npairs,),
        in_specs=[smem_blk(lambda s, be, npu: (s, 0, 0)),
                  smem_blk(lambda s, be, npu: (jnp.minimum(s + 1, last), 0, 0)),
                  pl.BlockSpec((1, 2, EXPERT_BLOCK), lambda s, be, npu: (s, 0, 0)),
                  wspec((d, de), 0), wspec((d, de), 0), wspec((de, d), 0),
                  wspec((d, de), 1), wspec((d, de), 1), wspec((de, d), 1),
                  pl.BlockSpec(memory_space=pl.ANY)],
        out_specs=pl.BlockSpec((pair_rows, LANES), lambda s, be, npu: (s, 0)),
        scratch_shapes=[pltpu.VMEM((2, EXPERT_BLOCK * nct, LANES), F32),
                        pltpu.VMEM((d, de), BF16), pltpu.VMEM((d, de), BF16), pltpu.VMEM((de, d), BF16),
                        pltpu.SemaphoreType.DMA((2,))],
    )
    return pl.pallas_call(
        _experts_kernel,
        grid_spec=grid_spec,
        out_shape=jax.ShapeDtypeStruct((npairs * pair_rows, LANES), F32),
        compiler_params=_cparams("arbitrary"),
        name="experts",
    )(block_e, n_pairs, row_tok, row_tok, row_w, wg, wu, wd, wg, wu, wd, hffn)


def _combine_kernel(pos_ref, posn_ref, x1_ref, ys_ref, npost_ref, g5_ref, rows_hbm, o_ref, buf, sem):
    i = pl.program_id(0)
    tm = x1_ref.shape[1]
    nct = x1_ref.shape[2] // LANES
    cur = i % 2
    nxt = 1 - cur

    def copy(pref, b, k, t):
        src = rows_hbm.at[pl.ds(pl.multiple_of(pref[k, t] * nct, nct), nct)]
        dst = buf.at[b, pl.ds(pl.multiple_of((k * tm + t) * nct, nct), nct)]
        return pltpu.make_async_copy(src, dst, sem.at[b])

    def drain(b):
        pltpu.make_async_copy(rows_hbm.at[pl.ds(0, TOP_K * tm * nct)], buf.at[b], sem.at[b]).wait()

    def issue(pref, b):
        def body(t2, carry):
            for u in range(2):
                for k in range(TOP_K):
                    copy(pref, b, k, 2 * t2 + u).start(priority=k % 2)
            return carry
        lax.fori_loop(0, tm // 2, body, 0)

    @pl.when(i == 0)
    def _():
        issue(pos_ref, cur)

    @pl.when(i + 1 < pl.num_programs(0))
    def _():
        issue(posn_ref, nxt)

    drain(cur)
    routed = []
    for c in range(nct):
        acc = buf[cur, pl.ds(c, tm, stride=nct), :]
        for k in range(1, TOP_K):
            acc = acc + buf[cur, pl.ds(k * tm * nct + c, tm, stride=nct), :]
        routed.append(acc)
    y = ys_ref[0] + jnp.concatenate(routed, axis=-1)
    o_ref[0] = x1_ref[0] + g5_ref[0] * _rms(y, npost_ref[...])


def _combine(x1, ys, rows, pos, npost, g5, tm):
    b, s, d = x1.shape
    nt = s // tm
    nct = d // LANES
    last = b * nt - 1
    tok = pl.BlockSpec((1, tm, d), lambda i: (i // nt, i % nt, 0))
    pos_blk = lambda f: pl.BlockSpec((TOP_K, tm), f, memory_space=pltpu.SMEM)
    return pl.pallas_call(
        _combine_kernel,
        grid=(b * nt,),
        in_specs=[pos_blk(lambda i: (0, i)), pos_blk(lambda i: (0, jnp.minimum(i + 1, last))), tok, tok,
                  pl.BlockSpec((1, d), lambda i: (0, 0)), pl.BlockSpec((1, 1, d), lambda i: (i // nt, 0, 0)),
                  pl.BlockSpec(memory_space=pl.ANY)],
        out_specs=tok,
        out_shape=jax.ShapeDtypeStruct((b, s, d), F32),
        scratch_shapes=[pltpu.VMEM((2, TOP_K * tm * nct, LANES), F32), pltpu.SemaphoreType.DMA((2,))],
        compiler_params=_cparams("arbitrary"),
        name="combine",
    )(pos, pos, x1, ys, npost, g5, rows)


def _dispatch_plan(idx_t, gate_t):
    k, t = idx_t.shape
    n_asg = k * t
    nb = n_asg // EXPERT_BLOCK + N_EXPERTS
    flat_e = idx_t.reshape(-1)
    order = jnp.argsort(flat_e).astype(jnp.int32)
    counts = jnp.zeros((N_EXPERTS,), jnp.int32).at[flat_e].add(1)
    padded = (counts + EXPERT_BLOCK - 1) // EXPERT_BLOCK * EXPERT_BLOCK
    start = jnp.cumsum(counts) - counts
    pend = jnp.cumsum(padded)
    pstart = pend - padded
    blk0 = jnp.arange(nb, dtype=jnp.int32) * EXPERT_BLOCK
    block_e = jnp.minimum(jnp.sum((pend[None, :] <= blk0[:, None]).astype(jnp.int32), axis=1), N_EXPERTS - 1)
    of_block = block_e[:, None] == jnp.arange(N_EXPERTS, dtype=jnp.int32)[None, :]
    per_block = lambda v: jnp.sum(jnp.where(of_block, v[None, :], 0), axis=1)
    assert nb % 2 == 0
    n_pairs = ((pend[-1] // EXPERT_BLOCK + 1) // 2).astype(jnp.int32).reshape(1)
    pos = blk0[:, None] - per_block(pstart)[:, None] + jnp.arange(EXPERT_BLOCK, dtype=jnp.int32)[None, :]
    valid = pos < per_block(counts)[:, None]
    src = jnp.clip(per_block(start)[:, None] + pos, 0, n_asg - 1)
    asg = order[src]
    row_tok = jnp.where(valid, asg % t, 0).astype(jnp.int32)
    row_w = jnp.where(valid, gate_t.reshape(-1)[asg], 0.0).astype(F32)
    rank = jnp.argsort(order).astype(jnp.int32)
    pos = (rank + (pstart - start)[flat_e]).reshape(k, t)
    shp = (nb // 2, 2, EXPERT_BLOCK)
    return block_e, n_pairs, row_tok.reshape(shp), row_w.reshape(shp), pos


def _pack_in_weights(w_in, ml_i_bias, ml_f_bias, gd_dt_bias):
    d = w_in.shape[0]
    nml = 2 * ML_HEADS * ML_QK + 2 * ML_HEADS * ML_V
    ml_cols = nml + 4 * ML_HEADS
    ngq = GD_HEADS * (2 * GD_QK + GD_V)
    ngz = GD_HEADS * GD_V
    wml = w_in[:, :nml].astype(BF16)
    wgq = w_in[:, ml_cols:ml_cols + ngq].astype(BF16)
    wgz = w_in[:, ml_cols + ngq:ml_cols + ngq + ngz].astype(BF16)
    wg = jnp.zeros((d, GATE_LANES), F32)
    wg = wg.at[:, ML_GATE0:ML_GATE0 + 16].set(w_in[:, nml:ml_cols])
    wg = wg.at[:, GD_GATE0:GD_GATE0 + 16].set(w_in[:, ml_cols + ngq + ngz:])
    gb = jnp.zeros((GATE_LANES,), F32)
    gb = gb.at[ML_GATE0:ML_GATE0 + 16].set(jnp.stack([ml_i_bias, ml_f_bias], axis=1).reshape(-1))
    gb = gb.at[GD_GATE0:GD_GATE0 + 16].set(jnp.stack([gd_dt_bias, jnp.zeros_like(gd_dt_bias)], axis=1).reshape(-1))
    return wml, wgq, wgz, wg.astype(BF16), gb.reshape(1, GATE_LANES)


def _mixer(x, ctx, mod, mod_ctx, norm_pre_mix, w_in, ml_i_bias, ml_f_bias, gd_conv_w, gd_a_log, gd_dt_bias):
    b, s, d = x.shape
    sc = ctx.shape[1]
    wml, wgq, wgz, wg, gb = _pack_in_weights(w_in, ml_i_bias, ml_f_bias, gd_dt_bias)
    nw = norm_pre_mix.reshape(1, d)
    ctx_mod = lambda j: jnp.broadcast_to(mod_ctx[j].reshape(1, 1, d), (b, 1, d))
    ml_c, gq_c, _, g_c = _proj(ctx, nw, ctx_mod(0), ctx_mod(1), wml, wgq, wgz, wg, gb, tm=sc, grid_view=False)
    ml_l, gqv_l, gz_l, g_l, gv_l = _proj(x, nw, mod[0], mod[1], wml, wgq, wgz, wg, gb, tm=512, grid_view=True)

    c0 = jnp.zeros((b, N_CHAINS, ML_QK, 2 * ML_V), F32)
    m0 = jnp.zeros((b, N_CHAINS, 1, 1), F32)
    _, _, c1, m1 = _mlstm(ml_c, g_c, c0, m0)
    hf, hb, _, _ = _mlstm(ml_l, g_l, c1, m1)

    neg_a = jnp.zeros((GATE_LANES,), F32)
    neg_a = neg_a.at[GD_GATE0:GD_GATE0 + 16].set(
        jnp.stack([-jnp.exp(gd_a_log), jnp.zeros_like(gd_a_log)], axis=1).reshape(-1)).reshape(1, GATE_LANES)
    qn_c = _gdconv_ctx(gq_c, gd_conv_w)
    qnv_l = _gdconv_lat(gqv_l, gd_conv_w)
    s0 = jnp.zeros((b, N_CHAINS, GD_QK, GD_V), F32)
    hdim = GD_HEADS * GD_V
    _, _, s1 = _gdn(qn_c, g_c, neg_a, s0, sc // CHUNK, lambda bi, n: (bi, n, 0), (b, sc, hdim))
    rows = s // GRID_W
    cpc = rows // CHUNK
    col_idx = lambda bi, n: (bi, n % cpc, n // cpc)
    ofv, obv, _ = _gdn(qnv_l, gv_l, neg_a, s1, s // CHUNK, col_idx, (b, rows, GRID_W * hdim))
    return hf, hb, ml_l, ofv, obv, gz_l


def kernel(x, c, ctx, c_ctx, w_ada, b_ada, norm_pre_mix, norm_post_mix, norm_pre_ffn, norm_post_ffn, w_in,
           ml_i_bias, ml_f_bias, ml_norm_w, gd_conv_w, gd_a_log, gd_dt_bias, gd_norm_w, w_out, router_w,
           router_bias, w_gate, w_up, w_down, ws_gate, ws_up, ws_down):
    b, s, d = x.shape
    depth = w_ada.shape[0]
    assert depth == 1, "the context stream update of deeper stacks is not implemented"
    ly = 0
    cc = jnp.zeros((16, d), F32).at[:b].set(c).at[b].set(c_ctx)
    mod_all = _ada(cc, w_ada[ly], b_ada[ly])
    mod = [mod_all[:b, j * d:(j + 1) * d].reshape(b, 1, d) for j in range(6)]
    mod_ctx = [mod_all[b, j * d:(j + 1) * d] for j in range(6)]

    hf, hb, ml_l, of, ob, gz_l = _mixer(x, ctx, mod, mod_ctx, norm_pre_mix[ly], w_in[ly], ml_i_bias[ly],
                                        ml_f_bias[ly], gd_conv_w[ly], gd_a_log[ly], gd_dt_bias[ly])

    row = lambda v: v.reshape(1, -1)
    x1, hffn, logits_t, ys = _post(
        x, hf, hb, ml_l, of, ob, gz_l, row(ml_norm_w[ly]), row(jnp.tile(gd_norm_w[ly], GD_HEADS)),
        w_out[ly].astype(BF16), row(norm_post_mix[ly]), mod[2], row(norm_pre_ffn[ly]), mod[3], mod[4],
        router_w[ly].T.astype(BF16), ws_gate[ly].astype(BF16), ws_up[ly].astype(BF16), ws_down[ly].astype(BF16),
        tm=512)

    idx_t, gate_t = _route(logits_t, router_bias[ly].reshape(-1, 1), tn=512)
    block_e, n_pairs, row_tok, row_w, pos = _dispatch_plan(idx_t, gate_t)
    t = b * s
    rows = _experts(hffn.reshape(t * (d // LANES), LANES), block_e, n_pairs, row_tok, row_w,
                    w_gate[ly], w_up[ly], w_down[ly])
    return _combine(x1, ys, rows, pos, row(norm_post_ffn[ly]), mod[5], tm=256)
```

```python
import functools

import jax
import jax.numpy as jnp
from jax import lax
from jax.experimental import pallas as pl
from jax.experimental.pallas import tpu as pltpu

EPS = 1e-6
CHUNK = 64
GRID_W = 64
ML_HEADS, ML_QK, ML_V = 4, 64, 128
GD_HEADS, GD_QK, GD_V = 4, 128, 128
CONV_W = 5
N_EXPERTS, TOP_K, N_GROUPS, TOPK_GROUPS = 256, 8, 8, 4
ROUTED_SCALE = 2.5
EXPERT_BLOCK = 128
N_CHAINS = 8
LANES = 128
GATE_LANES = LANES
ML_GATE0, GD_GATE0 = 0, 16

F32 = jnp.float32
BF16 = jnp.bfloat16
HI = lax.Precision.HIGHEST
VMEM_LIMIT = 56 * 1024 * 1024


def _cparams(*sem):
    return pltpu.CompilerParams(dimension_semantics=sem, vmem_limit_bytes=VMEM_LIMIT)


def _dot(a, b):
    return jnp.dot(a.astype(BF16), b.astype(BF16), preferred_element_type=F32)


def _dot_nt(a, b):
    return lax.dot_general(a.astype(BF16), b.astype(BF16), (((1,), (1,)), ((), ())), preferred_element_type=F32)


def _dot_tn(a, b):
    return lax.dot_general(a.astype(BF16), b.astype(BF16), (((0,), (0,)), ((), ())), preferred_element_type=F32)


def _dot_hi(a, b):
    return jnp.dot(a, b, precision=HI, preferred_element_type=F32)


def _dot_nt_hi(a, b):
    return lax.dot_general(a, b, (((1,), (1,)), ((), ())), precision=HI, preferred_element_type=F32)


def _transpose_hi(x):
    n = x.shape[1]
    eye = (lax.broadcasted_iota(jnp.int32, (n, n), 0) == lax.broadcasted_iota(jnp.int32, (n, n), 1)).astype(F32)
    return _dot_nt_hi(eye, x)


def _silu(x):
    return x * jax.nn.sigmoid(x)


def _past_mask(reverse):
    t = lax.broadcasted_iota(jnp.int32, (CHUNK, CHUNK), 0)
    s = lax.broadcasted_iota(jnp.int32, (CHUNK, CHUNK), 1)
    return (s >= t, s > t) if reverse else (s <= t, s < t)


def _ada_kernel(c_ref, w_ref, b_ref, o_ref):
    o_ref[...] = _dot(_silu(c_ref[...]), w_ref[...]) + b_ref[...]


def _ada(cc, w_ada, b_ada):
    rows, d = cc.shape
    n = w_ada.shape[1]
    tn = 1536
    return pl.pallas_call(
        _ada_kernel,
        grid=(n // tn,),
        in_specs=[pl.BlockSpec((rows, d), lambda j: (0, 0)),
                  pl.BlockSpec((d, tn), lambda j: (0, j)),
                  pl.BlockSpec((1, tn), lambda j: (0, j))],
        out_specs=pl.BlockSpec((rows, tn), lambda j: (0, j)),
        out_shape=jax.ShapeDtypeStruct((rows, n), F32),
        compiler_params=_cparams("arbitrary"),
        name="ada",
    )(cc, w_ada, b_ada.reshape(1, n))


GRID_PITCH = GRID_W + 8


def _to_grid_view(src_ref, dst_ref):
    ng = src_ref.shape[0]
    r = src_ref.shape[1] // GRID_PITCH
    for c in range(GRID_W):
        for g in range(ng):
            lo = (c * ng + g) * LANES
            dst_ref[0, :, lo:lo + LANES] = src_ref[g, pl.ds(c, r, stride=GRID_PITCH), :]


def _from_grid_view(src_ref, dst_ref):
    ng = dst_ref.shape[0]
    r = dst_ref.shape[1] // GRID_PITCH
    for c in range(GRID_W):
        for g in range(ng):
            lo = (c * ng + g) * LANES
            dst_ref[g, pl.ds(c, r, stride=GRID_PITCH), :] = src_ref[0, :, lo:lo + LANES]


def _pitched_rows(ref, g):
    r = ref.shape[1] // GRID_PITCH
    return jnp.concatenate([ref[g, i * GRID_PITCH:i * GRID_PITCH + GRID_W, :] for i in range(r)], axis=0)


def _proj_kernel(grid_view, x_ref, nw_ref, sh_ref, sc_ref, wml_ref, wgq_ref, wgz_ref, wg_ref, gb_ref, *refs):
    if grid_view:
        ml_ref, gqv_ref, gz_ref, g_ref, gv_ref, gq_scr, g_scr = refs
    else:
        ml_ref, gq_ref, gz_ref, g_ref = refs
    x = x_ref[0]
    xn = x * lax.rsqrt(jnp.mean(x * x, axis=-1, keepdims=True) + EPS) * nw_ref[...]
    h = (xn * (1.0 + sc_ref[0]) + sh_ref[0]).astype(BF16)
    ml_ref[0] = jnp.dot(h, wml_ref[...], preferred_element_type=F32)
    gz_ref[0] = jnp.dot(h, wgz_ref[...], preferred_element_type=F32)
    gates = jnp.dot(h, wg_ref[...], preferred_element_type=F32) + gb_ref[...]
    g_ref[0] = gates
    gq = jnp.dot(h, wgq_ref[...], preferred_element_type=F32)
    if grid_view:
        for r in range(x.shape[0] // GRID_W):
            rows = slice(r * GRID_W, (r + 1) * GRID_W)
            prow = slice(r * GRID_PITCH, r * GRID_PITCH + GRID_W)
            g_scr[0, prow, :] = gates[rows]
            for g in range(gq_scr.shape[0]):
                gq_scr[g, prow, :] = gq[rows, g * LANES:(g + 1) * LANES]
        _to_grid_view(gq_scr, gqv_ref)
        _to_grid_view(g_scr, gv_ref)
    else:
        gq_ref[0] = gq


def _proj(x, norm_w, shift, scale, wml, wgq, wgz, wg, gbias, tm, grid_view):
    b, s, d = x.shape
    nml, ngq, ngz = wml.shape[1], wgq.shape[1], wgz.shape[1]
    full = lambda shp: pl.BlockSpec(shp, lambda bi, i: (0,) * len(shp))
    tok = lambda n: pl.BlockSpec((1, tm, n), lambda bi, i: (bi, i, 0))
    mod = pl.BlockSpec((1, 1, d), lambda bi, i: (bi, 0, 0))
    if grid_view:
        rt = tm // GRID_W
        view = lambda n: pl.BlockSpec((1, rt, GRID_W * n), lambda bi, i: (bi, i, 0))
        vshape = lambda n: jax.ShapeDtypeStruct((b, s // GRID_W, GRID_W * n), F32)
        out_specs = [tok(nml), view(ngq), tok(ngz), tok(GATE_LANES), view(GATE_LANES)]
        out_shape = [jax.ShapeDtypeStruct((b, s, nml), F32), vshape(ngq), jax.ShapeDtypeStruct((b, s, ngz), F32),
                     jax.ShapeDtypeStruct((b, s, GATE_LANES), F32), vshape(GATE_LANES)]
        scratch = [pltpu.VMEM((ngq // LANES, rt * GRID_PITCH, LANES), F32), pltpu.VMEM((1, rt * GRID_PITCH, LANES), F32)]
    else:
        out_specs = [tok(nml), tok(ngq), tok(ngz), tok(GATE_LANES)]
        out_shape = [jax.ShapeDtypeStruct((b, s, n), F32) for n in (nml, ngq, ngz, GATE_LANES)]
        scratch = []
    return pl.pallas_call(
        functools.partial(_proj_kernel, grid_view),
        grid=(b, s // tm),
        in_specs=[tok(d), full((1, d)), mod, mod, full((d, nml)), full((d, ngq)), full((d, ngz)),
                  full((d, GATE_LANES)), full((1, GATE_LANES))],
        out_specs=out_specs,
        out_shape=out_shape,
        scratch_shapes=scratch,
        compiler_params=_cparams("parallel", "arbitrary"),
        name="proj",
    )(x, norm_w, shift, scale, wml, wgq, wgz, wg, gbias)


def _mlstm_kernel(mlf_ref, mlb_ref, gf_ref, gb_ref, c0_ref, m0_ref, hf_ref, hb_ref, cn_ref, mn_ref, c_scr, m_scr):
    i = pl.program_id(1)

    @pl.when(i == 0)
    def _():
        c_scr[...] = c0_ref[0]
        m_scr[...] = m0_ref[0]

    past = [_past_mask(d == 1)[0] for d in range(2)]
    g = [r[0] for r in (gf_ref, gb_ref)]
    ls = [jax.nn.log_sigmoid(x) for x in g]
    bcol = [_dot_hi(past[d].astype(F32), ls[d]) for d in range(2)]
    tot = [jnp.sum(x, axis=0, keepdims=True) for x in ls]
    g_t = [_transpose_hi(x) for x in g]
    b_t = [_transpose_hi(x) for x in bcol]

    chains = [(d, hd) for d in range(2) for hd in range(ML_HEADS)]
    nc = range(len(chains))
    ml_refs, h_refs = (mlf_ref, mlb_ref), (hf_ref, hb_ref)
    k0, v0 = ML_HEADS * ML_QK, 2 * ML_HEADS * ML_QK
    ones_col = (lax.broadcasted_iota(jnp.int32, (CHUNK, ML_V), 1) == 0).astype(F32)
    q, k, v, i_col, b_col, b_end, log_d = [], [], [], [], [], [], []
    for d, hd in chains:
        ci = ML_GATE0 + d * 8 + hd
        cf = ci + ML_HEADS
        q.append(ml_refs[d][0, :, hd * ML_QK:(hd + 1) * ML_QK])
        k.append(ml_refs[d][0, :, k0 + hd * ML_QK:k0 + (hd + 1) * ML_QK] * (ML_QK ** -0.5))
        v.append(jnp.concatenate([ml_refs[d][0, :, v0 + hd * ML_V:v0 + (hd + 1) * ML_V], ones_col], axis=-1))
        i_col.append(g[d][:, ci:ci + 1])
        b_col.append(bcol[d][:, cf:cf + 1])
        b_end.append(tot[d][:, cf:cf + 1])
        log_d.append(jnp.where(past[d], b_col[-1] - b_t[d][cf:cf + 1, :] + g_t[d][ci:ci + 1, :], -jnp.inf))
    c_st = [c_scr[c] for c in nc]
    m_st = [m_scr[c] for c in nc]
    log_prev = [b_col[c] + m_st[c] for c in nc]
    m_t = [jnp.maximum(log_prev[c], jnp.max(log_d[c], axis=-1, keepdims=True)) for c in nc]
    qk = [_dot_nt(q[c], k[c]) for c in nc]
    qc = [_dot(q[c], c_st[c]) for c in nc]
    s = [qk[c] * jnp.exp(log_d[c] - m_t[c]) for c in nc]
    w_prev = [jnp.exp(log_prev[c] - m_t[c]) for c in nc]
    sv = [_dot(s[c], v[c]) for c in nc]
    log_s = [b_end[c] - b_col[c] + i_col[c] for c in nc]
    m_new = [jnp.maximum(b_end[c] + m_st[c], jnp.max(log_s[c], axis=0, keepdims=True)) for c in nc]
    kw = [k[c] * jnp.exp(log_s[c] - m_new[c]) for c in nc]
    w_c = [jnp.exp(b_end[c] + m_st[c] - m_new[c]) for c in nc]
    kv = [_dot_tn(kw[c], v[c]) for c in nc]
    numden = [sv[c] + w_prev[c] * qc[c] for c in nc]
    scale = [1.0 / jnp.maximum(jnp.abs(numden[c][:, ML_V:ML_V + 1]), jnp.exp(-m_t[c])) for c in nc]
    for c, (d, hd) in enumerate(chains):
        h_refs[d][0, :, hd * ML_V:(hd + 1) * ML_V] = numden[c][:, :ML_V] * scale[c]
        c_scr[c] = w_c[c] * c_st[c] + kv[c]
        m_scr[c] = m_new[c]

    @pl.when(i == pl.num_programs(1) - 1)
    def _():
        cn_ref[0] = c_scr[...]
        mn_ref[0] = m_scr[...]


def _mlstm(ml, gates, c0, m0):
    b, s, nml = ml.shape
    nc = s // CHUNK
    fwd = lambda n: pl.BlockSpec((1, CHUNK, n), lambda bi, i: (bi, i, 0))
    bwd = lambda n: pl.BlockSpec((1, CHUNK, n), lambda bi, i: (bi, nc - 1 - i, 0))
    st = lambda shp: pl.BlockSpec((1,) + shp, lambda bi, i: (bi,) + (0,) * len(shp))
    hdim = ML_HEADS * ML_V
    cshape = (N_CHAINS, ML_QK, 2 * ML_V)
    return pl.pallas_call(
        _mlstm_kernel,
        grid=(b, nc),
        in_specs=[fwd(nml), bwd(nml), fwd(GATE_LANES), bwd(GATE_LANES), st(cshape), st((N_CHAINS, 1, 1))],
        out_specs=[fwd(hdim), bwd(hdim), st(cshape), st((N_CHAINS, 1, 1))],
        out_shape=[jax.ShapeDtypeStruct((b, s, hdim), F32), jax.ShapeDtypeStruct((b, s, hdim), F32),
                   jax.ShapeDtypeStruct(c0.shape, F32), jax.ShapeDtypeStruct(m0.shape, F32)],
        scratch_shapes=[pltpu.VMEM(cshape, F32), pltpu.VMEM((N_CHAINS, 1, 1), F32)],
        compiler_params=_cparams("parallel", "arbitrary"),
        name="mlstm",
    )(ml, ml, gates, gates, c0, m0)


def _gdconv_kernel(has_halo, *refs):
    if has_halo:
        x_ref, prev_ref, next_ref, w_ref, o_ref, xp_ref = refs
    else:
        x_ref, w_ref, o_ref, xp_ref = refs
    rows = x_ref.shape[1]
    nch = x_ref.shape[2]
    pad = 8
    zero = jnp.zeros((pad, nch), F32)
    if has_halo:
        c = pl.program_id(1)
        xp_ref[0:pad, :] = jnp.where(c > 0, prev_ref[0], zero)
        xp_ref[pad + rows:, :] = jnp.where(c < pl.num_programs(1) - 1, next_ref[0], zero)
    else:
        xp_ref[0:pad, :] = zero
        xp_ref[pad + rows:, :] = zero
    xp_ref[pad:pad + rows, :] = x_ref[0]
    half = CONV_W // 2
    for lc in range(nch // 128):
        sl = slice(lc * 128, (lc + 1) * 128)
        acc = None
        for j in range(CONV_W):
            term = xp_ref[pad - half + j:pad - half + j + rows, sl] * w_ref[j:j + 1, sl]
            acc = term if acc is None else acc + term
        y = _silu(acc)
        if lc < 2 * GD_HEADS:
            y = y * lax.rsqrt(jnp.sum(y * y, axis=-1, keepdims=True) + EPS)
        if lc < GD_HEADS:
            y = y * (GD_QK ** -0.5)
        o_ref[0, :, sl] = y


def _gdconv_ctx(qkv, conv_w):
    b, s, nch = qkv.shape
    return pl.pallas_call(
        functools.partial(_gdconv_kernel, False),
        grid=(b,),
        in_specs=[pl.BlockSpec((1, s, nch), lambda bi: (bi, 0, 0)), pl.BlockSpec((CONV_W, nch), lambda bi: (0, 0))],
        out_specs=pl.BlockSpec((1, s, nch), lambda bi: (bi, 0, 0)),
        out_shape=jax.ShapeDtypeStruct((b, s, nch), F32),
        scratch_shapes=[pltpu.VMEM((s + 16, nch), F32)],
        compiler_params=_cparams("parallel"),
        name="gdconv_ctx",
    )(qkv, conv_w)


def _gdconv_lat(view, conv_w):
    b, rows, wn = view.shape
    nch = wn // GRID_W
    rb = rows // 8
    return pl.pallas_call(
        functools.partial(_gdconv_kernel, True),
        grid=(b, GRID_W),
        in_specs=[pl.BlockSpec((1, rows, nch), lambda bi, c: (bi, 0, c)),
                  pl.BlockSpec((1, 8, nch), lambda bi, c: (bi, rb - 1, jnp.maximum(c - 1, 0))),
                  pl.BlockSpec((1, 8, nch), lambda bi, c: (bi, 0, jnp.minimum(c + 1, GRID_W - 1))),
                  pl.BlockSpec((CONV_W, nch), lambda bi, c: (0, 0))],
        out_specs=pl.BlockSpec((1, rows, nch), lambda bi, c: (bi, 0, c)),
        out_shape=jax.ShapeDtypeStruct(view.shape, F32),
        scratch_shapes=[pltpu.VMEM((rows + 16, nch), F32)],
        compiler_params=_cparams("parallel", "arbitrary"),
        name="gdconv_lat",
    )(view, view, view, conv_w)


SOLVE_BLOCK = 16


def _hi_lo(x):
    hi = x.astype(BF16).astype(F32)
    return hi, x - hi


def _dot_split(a, b):
    a_hi, a_lo = _hi_lo(a)
    b_hi, b_lo = _hi_lo(b)
    lhs = jnp.concatenate([a_hi, a_hi, a_lo], axis=1).astype(BF16)
    rhs = jnp.concatenate([b_hi, b_lo, b_hi], axis=0).astype(BF16)
    return jnp.dot(lhs, rhs, preferred_element_type=F32)


def _unit_triangular_inverses(ns):
    c = ns[0].shape[0]
    row = lax.broadcasted_iota(jnp.int32, (c, c), 0)
    col = lax.broadcasted_iota(jnp.int32, (c, c), 1)
    eye = (row == col).astype(F32)
    in_diag_block = (row // SOLVE_BLOCK) == (col // SOLVE_BLOCK)
    mm = lambda a_list, b_list: [_dot_split(a, b) for a, b in zip(a_list, b_list)]

    n_d = [jnp.where(in_diag_block, n, 0.0) for n in ns]
    x = n_d
    d_inv = [eye - n for n in n_d]
    for _ in range(SOLVE_BLOCK.bit_length() - 2):
        x = mm(x, x)
        d_inv = [d + dx for d, dx in zip(d_inv, mm(d_inv, x))]
    m = mm(d_inv, [n - nd for n, nd in zip(ns, n_d)])
    assert c // SOLVE_BLOCK == 4
    i_minus_m = [eye - mi for mi in m]
    q = [a + b for a, b in zip(i_minus_m, mm(i_minus_m, mm(m, m)))]
    return mm(q, d_inv)


def _gdn_kernel(qf_ref, qb_ref, gf_ref, gb_ref, na_ref, s0_ref, of_ref, ob_ref, sn_ref, s_scr):
    i = pl.program_id(1)

    @pl.when(i == 0)
    def _():
        s_scr[...] = s0_ref[0]

    eye = (lax.broadcasted_iota(jnp.int32, (CHUNK, CHUNK), 0)
           == lax.broadcasted_iota(jnp.int32, (CHUNK, CHUNK), 1)).astype(F32)
    nqk = GD_HEADS * GD_QK
    masks = [_past_mask(d == 1) for d in range(2)]
    gates = [r[0] for r in (gf_ref, gb_ref)]
    glog = [na_ref[...] * jax.nn.softplus(g) for g in gates]
    beta_all = [jax.nn.sigmoid(g) for g in gates]
    gcum = [_dot_hi(masks[d][0].astype(F32), glog[d]) for d in range(2)]
    gtot = [jnp.sum(g, axis=0, keepdims=True) for g in glog]
    gcum_t = [_transpose_hi(g) for g in gcum]

    chains = [(d, hd) for d in range(2) for hd in range(GD_HEADS)]
    x_refs, o_refs = (qf_ref, qb_ref), (of_ref, ob_ref)
    q, k, v, g_col, beta, g_end, decay = [], [], [], [], [], [], []
    for d, hd in chains:
        ca = GD_GATE0 + d * 8 + hd
        q.append(x_refs[d][0, :, hd * GD_QK:(hd + 1) * GD_QK])
        k.append(x_refs[d][0, :, nqk + hd * GD_QK:nqk + (hd + 1) * GD_QK])
        v.append(x_refs[d][0, :, 2 * nqk + hd * GD_V:2 * nqk + (hd + 1) * GD_V])
        g_col.append(gcum[d][:, ca:ca + 1])
        beta.append(beta_all[d][:, ca + GD_HEADS:ca + GD_HEADS + 1])
        g_end.append(gtot[d][:, ca:ca + 1])
        decay.append(jnp.exp(jnp.where(masks[d][0], g_col[-1] - gcum_t[d][ca:ca + 1, :], -jnp.inf)))
    nc = range(len(chains))
    kk = [_dot_nt(k[c], k[c]) for c in nc]
    xs = [jnp.where(masks[chains[c][0]][1], beta[c] * kk[c] * decay[c], 0.0) for c in nc]
    ps = _unit_triangular_inverses(xs)
    uw = [_dot_split(ps[c], jnp.concatenate([v[c] * beta[c], k[c] * (beta[c] * jnp.exp(g_col[c]))], axis=-1))
          for c in nc]
    qk = [_dot_nt(q[c], k[c]) * decay[c] for c in nc]
    s_st = [s_scr[c] for c in nc]
    v_new = [uw[c][:, :GD_V] - _dot(uw[c][:, GD_V:], s_st[c]) for c in nc]
    o_loc = [_dot(q[c] * jnp.exp(g_col[c]), s_st[c]) for c in nc]
    o_new = [o_loc[c] + _dot(qk[c], v_new[c]) for c in nc]
    s_new = [s_st[c] * jnp.exp(g_end[c]) + _dot_tn(k[c] * jnp.exp(g_end[c] - g_col[c]), v_new[c]) for c in nc]
    for c, (d, hd) in enumerate(chains):
        o_refs[d][0, :, hd * GD_V:(hd + 1) * GD_V] = o_new[c]
        s_scr[c] = s_new[c]

    @pl.when(i == pl.num_programs(1) - 1)
    def _():
        sn_ref[0] = s_scr[...]


def _gdn(qkv_view, gates_view, neg_a, s0, nc, idx_fn, out_view_shape):
    b = qkv_view.shape[0]
    nqkv = 2 * GD_HEADS * GD_QK + GD_HEADS * GD_V
    hdim = GD_HEADS * GD_V
    fwd = lambda n: pl.BlockSpec((1, CHUNK, n), lambda bi, i: idx_fn(bi, i))
    bwd = lambda n: pl.BlockSpec((1, CHUNK, n), lambda bi, i: idx_fn(bi, nc - 1 - i))
    st = pl.BlockSpec((1, N_CHAINS, GD_QK, GD_V), lambda bi, i: (bi, 0, 0, 0))
    return pl.pallas_call(
        _gdn_kernel,
        grid=(b, nc),
        in_specs=[fwd(nqkv), bwd(nqkv), fwd(GATE_LANES), bwd(GATE_LANES),
                  pl.BlockSpec((1, GATE_LANES), lambda bi, i: (0, 0)), st],
        out_specs=[fwd(hdim), bwd(hdim), st],
        out_shape=[jax.ShapeDtypeStruct(out_view_shape, F32), jax.ShapeDtypeStruct(out_view_shape, F32),
                   jax.ShapeDtypeStruct(s0.shape, F32)],
        scratch_shapes=[pltpu.VMEM((N_CHAINS, GD_QK, GD_V), F32)],
        compiler_params=_cparams("parallel", "arbitrary"),
        name="gdn",
    )(qkv_view, qkv_view, gates_view, gates_view, neg_a, s0)


def _head_rms(t, nheads, width):
    outs = []
    for hd in range(nheads):
        th = t[:, hd * width:(hd + 1) * width]
        outs.append(th * lax.rsqrt(jnp.mean(th * th, axis=-1, keepdims=True) + EPS))
    return jnp.concatenate(outs, axis=-1)


def _rms(t, w):
    return t * lax.rsqrt(jnp.mean(t * t, axis=-1, keepdims=True) + EPS) * w


def _post_kernel(x_ref, hf_ref, hb_ref, og_ref, of_ref, ob_ref, z_ref, mlw_ref, gdw_ref, wout_ref,
                 npost_ref, g2_ref, npre_ref, sh_ref, sc_ref, rwt_ref, wsg_ref, wsu_ref, wsd_ref,
                 x1_ref, hffn_ref, lt_ref, ys_ref, of_scr, ob_scr):
    ml_y = _head_rms(hf_ref[0] + hb_ref[0], ML_HEADS, ML_V) * mlw_ref[...] * jax.nn.sigmoid(og_ref[0])
    _from_grid_view(of_ref, of_scr)
    _from_grid_view(ob_ref, ob_scr)
    o_sum = jnp.concatenate([_pitched_rows(of_scr, g) + _pitched_rows(ob_scr, g) for g in range(of_scr.shape[0])],
                            axis=-1)
    gd_y = _head_rms(o_sum, GD_HEADS, GD_V) * gdw_ref[...] * _silu(z_ref[0])
    y = _dot(jnp.concatenate([ml_y, gd_y], axis=-1), wout_ref[...])
    x1 = x_ref[0] + g2_ref[0] * _rms(y, npost_ref[...])
    x1_ref[0] = x1
    hffn = _rms(x1, npre_ref[...]) * (1.0 + sc_ref[0]) + sh_ref[0]
    nct = hffn.shape[1] // LANES
    for c in range(nct):
        hffn_ref[0, pl.ds(c, hffn.shape[0], stride=nct), :] = hffn[:, c * LANES:(c + 1) * LANES]
    hb = hffn.astype(BF16)
    lt_ref[...] = lax.dot_general(rwt_ref[...], hb, (((1,), (1,)), ((), ())), preferred_element_type=F32)
    hs = _silu(jnp.dot(hb, wsg_ref[...], preferred_element_type=F32)) * jnp.dot(hb, wsu_ref[...],
                                                                                preferred_element_type=F32)
    ys_ref[0] = _dot(hs, wsd_ref[...])


def _post(x, hf, hb, ml, of, ob, gz, mlw, gdw, wout, npost, g2, npre, sh, sc, rwt, wsg, wsu, wsd, tm):
    b, s, d = x.shape
    nt = s // tm
    hw = ML_HEADS * ML_V
    og_blk = (2 * ML_HEADS * ML_QK + ML_HEADS * ML_V) // hw
    tok = lambda n: pl.BlockSpec((1, tm, n), lambda bi, i: (bi, i, 0))
    full = lambda shp: pl.BlockSpec(shp, lambda bi, i: (0,) * len(shp))
    mod = pl.BlockSpec((1, 1, d), lambda bi, i: (bi, 0, 0))
    ne = rwt.shape[0]
    ds = wsg.shape[1]
    gview = pl.BlockSpec((1, tm // GRID_W, GRID_W * hw), lambda bi, i: (bi, i, 0))
    return pl.pallas_call(
        _post_kernel,
        grid=(b, nt),
        in_specs=[tok(d), tok(hw), tok(hw), pl.BlockSpec((1, tm, hw), lambda bi, i: (bi, i, og_blk)),
                  gview, gview, tok(hw), full((1, hw)), full((1, hw)), full((d, d)),
                  full((1, d)), mod, full((1, d)), mod, mod, full((ne, d)), full((d, ds)), full((d, ds)),
                  full((ds, d))],
        out_specs=[tok(d), pl.BlockSpec((1, tm * (d // LANES), LANES), lambda bi, i: (bi, i, 0)),
                   pl.BlockSpec((ne, tm), lambda bi, i: (0, bi * nt + i)), tok(d)],
        out_shape=[jax.ShapeDtypeStruct((b, s, d), F32), jax.ShapeDtypeStruct((b, s * (d // LANES), LANES), F32),
                   jax.ShapeDtypeStruct((ne, b * s), F32), jax.ShapeDtypeStruct((b, s, d), F32)],
        scratch_shapes=[pltpu.VMEM((hw // LANES, tm // GRID_W * GRID_PITCH, LANES), F32)] * 2,
        compiler_params=_cparams("parallel", "arbitrary"),
        name="post",
    )(x, hf, hb, ml, of, ob, gz, mlw, gdw, wout, npost, g2, npre, sh, sc, rwt, wsg, wsu, wsd)


def _route_kernel(lt_ref, bias_ref, idx_ref, gate_ref):
    ne, tn = lt_ref.shape
    gsz = ne // N_GROUPS
    scores = jax.nn.sigmoid(lt_ref[...])
    sel = scores + bias_ref[...]
    neg = -jnp.inf
    sel3 = sel.reshape(N_GROUPS, gsz, tn)
    io3 = lax.broadcasted_iota(jnp.int32, sel3.shape, 1)
    top1 = jnp.max(sel3, axis=1, keepdims=True)
    first = jnp.min(jnp.where(sel3 == top1, io3, gsz), axis=1, keepdims=True)
    top2 = jnp.max(jnp.where(io3 == first, neg, sel3), axis=1, keepdims=True)
    grp = (top1 + top2).reshape(N_GROUPS, tn)
    iog = lax.broadcasted_iota(jnp.int32, grp.shape, 0)
    keep = jnp.zeros(grp.shape, jnp.bool_)
    for _ in range(TOPK_GROUPS):
        m = jnp.max(grp, axis=0, keepdims=True)
        pick = iog == jnp.min(jnp.where(grp == m, iog, N_GROUPS), axis=0, keepdims=True)
        keep = keep | pick
        grp = jnp.where(pick, neg, grp)
    cand = jnp.where(keep.reshape(N_GROUPS, 1, tn), sel3, neg).reshape(ne, tn)
    ioe = lax.broadcasted_iota(jnp.int32, cand.shape, 0)
    idxs, gates = [], []
    for _ in range(TOP_K):
        m = jnp.max(cand, axis=0, keepdims=True)
        e = jnp.min(jnp.where(cand == m, ioe, ne), axis=0, keepdims=True)
        pick = ioe == e
        idxs.append(e)
        gates.append(jnp.sum(jnp.where(pick, scores, 0.0), axis=0, keepdims=True))
        cand = jnp.where(pick, neg, cand)
    gate = jnp.concatenate(gates, axis=0)
    idx_ref[...] = jnp.concatenate(idxs, axis=0)
    gate_ref[...] = gate / jnp.sum(gate, axis=0, keepdims=True) * ROUTED_SCALE


def _route(logits_t, bias_col, tn):
    ne, t = logits_t.shape
    return pl.pallas_call(
        _route_kernel,
        grid=(t // tn,),
        in_specs=[pl.BlockSpec((ne, tn), lambda i: (0, i)), pl.BlockSpec((ne, 1), lambda i: (0, 0))],
        out_specs=[pl.BlockSpec((TOP_K, tn), lambda i: (0, i)), pl.BlockSpec((TOP_K, tn), lambda i: (0, i))],
        out_shape=[jax.ShapeDtypeStruct((TOP_K, t), jnp.int32), jax.ShapeDtypeStruct((TOP_K, t), F32)],
        compiler_params=_cparams("parallel"),
        name="route",
    )(logits_t, bias_col)


def _experts_kernel(be_ref, np_ref,
                    tok_ref, tokn_ref, w_ref, wg0_ref, wu0_ref, wd0_ref, wg1_ref, wu1_ref, wd1_ref, h_hbm,
                    o_ref, xg, wgc, wuc, wdc, gsem):
    s = pl.program_id(0)
    n_pairs = np_ref[0]
    nct = xg.shape[1] // EXPERT_BLOCK
    rows = nct * EXPERT_BLOCK
    w_refs = ((wg0_ref, wu0_ref, wd0_ref), (wg1_ref, wu1_ref, wd1_ref))

    def gather_copy(tref, p, j):
        src = h_hbm.at[pl.ds(pl.multiple_of(tref[0, p, j] * nct, nct), nct)]
        return pltpu.make_async_copy(src, xg.at[p, pl.ds(j * nct, nct)], gsem.at[p])

    def gather_wait(p):
        pltpu.make_async_copy(h_hbm.at[pl.ds(0, rows)], xg.at[p], gsem.at[p]).wait()

    @pl.when(s >= n_pairs)
    def _():
        o_ref[...] = jnp.zeros(o_ref.shape, F32)

    @pl.when(s < n_pairs)
    def _():
        @pl.when(s == 0)
        def _():
            for p in range(2):
                for j in range(EXPERT_BLOCK):
                    gather_copy(tok_ref, p, j).start(priority=j % 2)

        for p in range(2):
            blk = 2 * s + p
            wg_ref, wu_ref, wd_ref = w_refs[p]
            gather_wait(p)

            @pl.when((blk == 0) | (be_ref[blk] != be_ref[jnp.maximum(blk - 1, 0)]))
            def _():
                wgc[...] = wg_ref[0].astype(BF16)
                wuc[...] = wu_ref[0].astype(BF16)
                wdc[...] = wd_ref[0].astype(BF16)

            xb = jnp.concatenate([xg[p, pl.ds(c, EXPERT_BLOCK, stride=nct), :] for c in range(nct)],
                                 axis=-1).astype(BF16)
            for j in range(EXPERT_BLOCK):
                gather_copy(tokn_ref, p, j).start(priority=j % 2)
            hmid = _silu(jnp.dot(xb, wgc[...], preferred_element_type=F32)) * jnp.dot(xb, wuc[...],
                                                                                      preferred_element_type=F32)
            out = _dot(hmid, wdc[...])
            eye = (lax.broadcasted_iota(jnp.int32, (EXPERT_BLOCK, EXPERT_BLOCK), 0)
                   == lax.broadcasted_iota(jnp.int32, (EXPERT_BLOCK, EXPERT_BLOCK), 1))
            w_col = jnp.sum(jnp.where(eye, w_ref[0, p:p + 1, :], 0.0), axis=1, keepdims=True)
            out = out * w_col
            for c in range(nct):
                o_ref[pl.ds(p * rows + c, EXPERT_BLOCK, stride=nct), :] = out[:, c * LANES:(c + 1) * LANES]

        @pl.when(s == n_pairs - 1)
        def _():
            for p in range(2):
                gather_wait(p)


def _experts(hffn, block_e, n_pairs, row_tok, row_w, wg, wu, wd):
    d = wg.shape[1]
    nct = d // LANES
    npairs = row_tok.shape[0]
    de = wg.shape[2]
    last = npairs - 1
    smem_blk = lambda f: pl.BlockSpec((1, 2, EXPERT_BLOCK), f, memory_space=pltpu.SMEM)
    wspec = lambda shp, p: pl.BlockSpec((1,) + shp, lambda s, be, npu: (be[2 * s + p], 0, 0))
    pair_rows = 2 * EXPERT_BLOCK * nct
    grid_spec = pltpu.PrefetchScalarGridSpec(
        num_scalar_prefetch=2,
        grid=(npairs,),
        in_specs=[smem_blk(lambda s, be, npu: (s, 0, 0)),
                  smem_blk(lambda s, be, npu: (jnp.minimum(s + 1, last), 0, 0)),
                  pl.BlockSpec((1, 2, EXPERT_BLOCK), lambda s, be, npu: (s, 0, 0)),
                  wspec((d, de), 0), wspec((d, de), 0), wspec((de, d), 0),
                  wspec((d, de), 1), wspec((d, de), 1), wspec((de, d), 1),
                  pl.BlockSpec(memory_space=pl.ANY)],
        out_specs=pl.BlockSpec((pair_rows, LANES), lambda s, be, npu: (s, 0)),
        scratch_shapes=[pltpu.VMEM((2, EXPERT_BLOCK * nct, LANES), F32),
                        pltpu.VMEM((d, de), BF16), pltpu.VMEM((d, de), BF16), pltpu.VMEM((de, d), BF16),
                        pltpu.SemaphoreType.DMA((2,))],
    )
    return pl.pallas_call(
        _experts_kernel,
        grid_spec=grid_spec,
        out_shape=jax.ShapeDtypeStruct((npairs * pair_rows, LANES), F32),
        compiler_params=_cparams("arbitrary"),
        name="experts",
    )(block_e, n_pairs, row_tok, row_tok, row_w, wg, wu, wd, wg, wu, wd, hffn)


def _combine_kernel(pos_ref, posn_ref, x1_ref, ys_ref, npost_ref, g5_ref, rows_hbm, o_ref, buf, sem):
    i = pl.program_id(0)
    tm = x1_ref.shape[1]
    nct = x1_ref.shape[2] // LANES
    cur = i % 2
    nxt = 1 - cur

    def copy(pref, b, k, t):
        src = rows_hbm.at[pl.ds(pl.multiple_of(pref[k, t] * nct, nct), nct)]
        dst = buf.at[b, pl.ds(pl.multiple_of((k * tm + t) * nct, nct), nct)]
        return pltpu.make_async_copy(src, dst, sem.at[b])

    def drain(b):
        pltpu.make_async_copy(rows_hbm.at[pl.ds(0, TOP_K * tm * nct)], buf.at[b], sem.at[b]).wait()

    def issue(pref, b):
        def body(t2, carry):
            for u in range(2):
                for k in range(TOP_K):
                    copy(pref, b, k, 2 * t2 + u).start(priority=k % 2)
            return carry
        lax.fori_loop(0, tm // 2, body, 0)

    @pl.when(i == 0)
    def _():
        issue(pos_ref, cur)

    @pl.when(i + 1 < pl.num_programs(0))
    def _():
        issue(posn_ref, nxt)

    drain(cur)
    routed = []
    for c in range(nct):
        acc = buf[cur, pl.ds(c, tm, stride=nct), :]
        for k in range(1, TOP_K):
            acc = acc + buf[cur, pl.ds(k * tm * nct + c, tm, stride=nct), :]
        routed.append(acc)
    y = ys_ref[0] + jnp.concatenate(routed, axis=-1)
    o_ref[0] = x1_ref[0] + g5_ref[0] * _rms(y, npost_ref[...])


def _combine(x1, ys, rows, pos, npost, g5, tm):
    b, s, d = x1.shape
    nt = s // tm
    nct = d // LANES
    last = b * nt - 1
    tok = pl.BlockSpec((1, tm, d), lambda i: (i // nt, i % nt, 0))
    pos_blk = lambda f: pl.BlockSpec((TOP_K, tm), f, memory_space=pltpu.SMEM)
    return pl.pallas_call(
        _combine_kernel,
        grid=(b * nt,),
        in_specs=[pos_blk(lambda i: (0, i)), pos_blk(lambda i: (0, jnp.minimum(i + 1, last))), tok, tok,
                  pl.BlockSpec((1, d), lambda i: (0, 0)), pl.BlockSpec((1, 1, d), lambda i: (i // nt, 0, 0)),
                  pl.BlockSpec(memory_space=pl.ANY)],
        out_specs=tok,
        out_shape=jax.ShapeDtypeStruct((b, s, d), F32),
        scratch_shapes=[pltpu.VMEM((2, TOP_K * tm * nct, LANES), F32), pltpu.SemaphoreType.DMA((2,))],
        compiler_params=_cparams("arbitrary"),
        name="combine",
    )(pos, pos, x1, ys, npost, g5, rows)


def _dispatch_plan(idx_t, gate_t):
    k, t = idx_t.shape
    n_asg = k * t
    nb = n_asg // EXPERT_BLOCK + N_EXPERTS
    flat_e = idx_t.reshape(-1)
    order = jnp.argsort(flat_e).astype(jnp.int32)
    counts = jnp.zeros((N_EXPERTS,), jnp.int32).at[flat_e].add(1)
    padded = (counts + EXPERT_BLOCK - 1) // EXPERT_BLOCK * EXPERT_BLOCK
    start = jnp.cumsum(counts) - counts
    pend = jnp.cumsum(padded)
    pstart = pend - padded
    blk0 = jnp.arange(nb, dtype=jnp.int32) * EXPERT_BLOCK
    block_e = jnp.minimum(jnp.sum((pend[None, :] <= blk0[:, None]).astype(jnp.int32), axis=1), N_EXPERTS - 1)
    of_block = block_e[:, None] == jnp.arange(N_EXPERTS, dtype=jnp.int32)[None, :]
    per_block = lambda v: jnp.sum(jnp.where(of_block, v[None, :], 0), axis=1)
    assert nb % 2 == 0
    n_pairs = ((pend[-1] // EXPERT_BLOCK + 1) // 2).astype(jnp.int32).reshape(1)
    pos = blk0[:, None] - per_block(pstart)[:, None] + jnp.arange(EXPERT_BLOCK, dtype=jnp.int32)[None, :]
    valid = pos < per_block(counts)[:, None]
    src = jnp.clip(per_block(start)[:, None] + pos, 0, n_asg - 1)
    asg = order[src]
    row_tok = jnp.where(valid, asg % t, 0).astype(jnp.int32)
    row_w = jnp.where(valid, gate_t.reshape(-1)[asg], 0.0).astype(F32)
    i_sorted = jnp.arange(n_asg, dtype=jnp.int32)
    pad_before = jnp.sum(jnp.where(i_sorted[:, None] >= (start + counts)[None, :], (padded - counts)[None, :], 0),
                         axis=1)
    _, pos = lax.sort((order, i_sorted + pad_before), num_keys=1)
    pos = pos.reshape(k, t)
    shp = (nb // 2, 2, EXPERT_BLOCK)
    return block_e, n_pairs, row_tok.reshape(shp), row_w.reshape(shp), pos


def _pack_in_weights(w_in, ml_i_bias, ml_f_bias, gd_dt_bias):
    d = w_in.shape[0]
    nml = 2 * ML_HEADS * ML_QK + 2 * ML_HEADS * ML_V
    ml_cols = nml + 4 * ML_HEADS
    ngq = GD_HEADS * (2 * GD_QK + GD_V)
    ngz = GD_HEADS * GD_V
    wml = w_in[:, :nml].astype(BF16)
    wgq = w_in[:, ml_cols:ml_cols + ngq].astype(BF16)
    wgz = w_in[:, ml_cols + ngq:ml_cols + ngq + ngz].astype(BF16)
    wg = jnp.zeros((d, GATE_LANES), F32)
    wg = wg.at[:, ML_GATE0:ML_GATE0 + 16].set(w_in[:, nml:ml_cols])
    wg = wg.at[:, GD_GATE0:GD_GATE0 + 16].set(w_in[:, ml_cols + ngq + ngz:])
    gb = jnp.zeros((GATE_LANES,), F32)
    gb = gb.at[ML_GATE0:ML_GATE0 + 16].set(jnp.stack([ml_i_bias, ml_f_bias], axis=1).reshape(-1))
    gb = gb.at[GD_GATE0:GD_GATE0 + 16].set(jnp.stack([gd_dt_bias, jnp.zeros_like(gd_dt_bias)], axis=1).reshape(-1))
    return wml, wgq, wgz, wg.astype(BF16), gb.reshape(1, GATE_LANES)


def _mixer(x, ctx, mod, mod_ctx, norm_pre_mix, w_in, ml_i_bias, ml_f_bias, gd_conv_w, gd_a_log, gd_dt_bias):
    b, s, d = x.shape
    sc = ctx.shape[1]
    wml, wgq, wgz, wg, gb = _pack_in_weights(w_in, ml_i_bias, ml_f_bias, gd_dt_bias)
    nw = norm_pre_mix.reshape(1, d)
    ctx_mod = lambda j: jnp.broadcast_to(mod_ctx[j].reshape(1, 1, d), (b, 1, d))
    ml_c, gq_c, _, g_c = _proj(ctx, nw, ctx_mod(0), ctx_mod(1), wml, wgq, wgz, wg, gb, tm=sc, grid_view=False)
    ml_l, gqv_l, gz_l, g_l, gv_l = _proj(x, nw, mod[0], mod[1], wml, wgq, wgz, wg, gb, tm=512, grid_view=True)

    c0 = jnp.zeros((b, N_CHAINS, ML_QK, 2 * ML_V), F32)
    m0 = jnp.zeros((b, N_CHAINS, 1, 1), F32)
    _, _, c1, m1 = _mlstm(ml_c, g_c, c0, m0)
    hf, hb, _, _ = _mlstm(ml_l, g_l, c1, m1)

    neg_a = jnp.zeros((GATE_LANES,), F32)
    neg_a = neg_a.at[GD_GATE0:GD_GATE0 + 16].set(
        jnp.stack([-jnp.exp(gd_a_log), jnp.zeros_like(gd_a_log)], axis=1).reshape(-1)).reshape(1, GATE_LANES)
    qn_c = _gdconv_ctx(gq_c, gd_conv_w)
    qnv_l = _gdconv_lat(gqv_l, gd_conv_w)
    s0 = jnp.zeros((b, N_CHAINS, GD_QK, GD_V), F32)
    hdim = GD_HEADS * GD_V
    _, _, s1 = _gdn(qn_c, g_c, neg_a, s0, sc // CHUNK, lambda bi, n: (bi, n, 0), (b, sc, hdim))
    rows = s // GRID_W
    cpc = rows // CHUNK
    col_idx = lambda bi, n: (bi, n % cpc, n // cpc)
    ofv, obv, _ = _gdn(qnv_l, gv_l, neg_a, s1, s // CHUNK, col_idx, (b, rows, GRID_W * hdim))
    return hf, hb, ml_l, ofv, obv, gz_l


def kernel(x, c, ctx, c_ctx, w_ada, b_ada, norm_pre_mix, norm_post_mix, norm_pre_ffn, norm_post_ffn, w_in,
           ml_i_bias, ml_f_bias, ml_norm_w, gd_conv_w, gd_a_log, gd_dt_bias, gd_norm_w, w_out, router_w,
           router_bias, w_gate, w_up, w_down, ws_gate, ws_up, ws_down):
    b, s, d = x.shape
    depth = w_ada.shape[0]
    assert depth == 1, "the context stream update of deeper stacks is not implemented"
    ly = 0
    cc = jnp.zeros((16, d), F32).at[:b].set(c).at[b].set(c_ctx)
    mod_all = _ada(cc, w_ada[ly], b_ada[ly])
    mod = [mod_all[:b, j * d:(j + 1) * d].reshape(b, 1, d) for j in range(6)]
    mod_ctx = [mod_all[b, j * d:(j + 1) * d] for j in range(6)]

    hf, hb, ml_l, of, ob, gz_l = _mixer(x, ctx, mod, mod_ctx, norm_pre_mix[ly], w_in[ly], ml_i_bias[ly],
                                        ml_f_bias[ly], gd_conv_w[ly], gd_a_log[ly], gd_dt_bias[ly])

    row = lambda v: v.reshape(1, -1)
    x1, hffn, logits_t, ys = _post(
        x, hf, hb, ml_l, of, ob, gz_l, row(ml_norm_w[ly]), row(jnp.tile(gd_norm_w[ly], GD_HEADS)),
        w_out[ly].astype(BF16), row(norm_post_mix[ly]), mod[2], row(norm_pre_ffn[ly]), mod[3], mod[4],
        router_w[ly].T.astype(BF16), ws_gate[ly].astype(BF16), ws_up[ly].astype(BF16), ws_down[ly].astype(BF16),
        tm=512)

    idx_t, gate_t = _route(logits_t, router_bias[ly].reshape(-1, 1), tn=512)
    block_e, n_pairs, row_tok, row_w, pos = _dispatch_plan(idx_t, gate_t)
    t = b * s
    rows = _experts(hffn.reshape(t * (d // LANES), LANES), block_e, n_pairs, row_tok, row_w,
                    w_gate[ly], w_up[ly], w_down[ly])
    return _combine(x1, ys, rows, pos, row(norm_post_ffn[ly]), mod[5], tm=256)
```

```python
import functools

import jax
import jax.numpy as jnp
from jax import lax
from jax.experimental import pallas as pl
from jax.experimental.pallas import tpu as pltpu

EPS = 1e-6
CHUNK = 64
GRID_W = 64
ML_HEADS, ML_QK, ML_V = 4, 64, 128
GD_HEADS, GD_QK, GD_V = 4, 128, 128
CONV_W = 5
N_EXPERTS, TOP_K, N_GROUPS, TOPK_GROUPS = 256, 8, 8, 4
ROUTED_SCALE = 2.5
EXPERT_BLOCK = 128
N_CHAINS = 8
LANES = 128
GATE_LANES = LANES
ML_GATE0, GD_GATE0 = 0, 16

F32 = jnp.float32
BF16 = jnp.bfloat16
HI = lax.Precision.HIGHEST
VMEM_LIMIT = 56 * 1024 * 1024


def _cparams(*sem):
    return pltpu.CompilerParams(dimension_semantics=sem, vmem_limit_bytes=VMEM_LIMIT)


def _dot(a, b):
    return jnp.dot(a.astype(BF16), b.astype(BF16), preferred_element_type=F32)


def _dot_nt(a, b):
    return lax.dot_general(a.astype(BF16), b.astype(BF16), (((1,), (1,)), ((), ())), preferred_element_type=F32)


def _dot_tn(a, b):
    return lax.dot_general(a.astype(BF16), b.astype(BF16), (((0,), (0,)), ((), ())), preferred_element_type=F32)


def _dot_hi(a, b):
    return jnp.dot(a, b, precision=HI, preferred_element_type=F32)


def _dot_nt_hi(a, b):
    return lax.dot_general(a, b, (((1,), (1,)), ((), ())), precision=HI, preferred_element_type=F32)


def _transpose_hi(x):
    n = x.shape[1]
    eye = (lax.broadcasted_iota(jnp.int32, (n, n), 0) == lax.broadcasted_iota(jnp.int32, (n, n), 1)).astype(F32)
    return _dot_nt_hi(eye, x)


def _silu(x):
    return x * jax.nn.sigmoid(x)


def _past_mask(reverse):
    t = lax.broadcasted_iota(jnp.int32, (CHUNK, CHUNK), 0)
    s = lax.broadcasted_iota(jnp.int32, (CHUNK, CHUNK), 1)
    return (s >= t, s > t) if reverse else (s <= t, s < t)


def _ada_kernel(c_ref, w_ref, b_ref, o_ref):
    o_ref[...] = _dot(_silu(c_ref[...]), w_ref[...]) + b_ref[...]


def _ada(cc, w_ada, b_ada):
    rows, d = cc.shape
    n = w_ada.shape[1]
    tn = 1536
    return pl.pallas_call(
        _ada_kernel,
        grid=(n // tn,),
        in_specs=[pl.BlockSpec((rows, d), lambda j: (0, 0)),
                  pl.BlockSpec((d, tn), lambda j: (0, j)),
                  pl.BlockSpec((1, tn), lambda j: (0, j))],
        out_specs=pl.BlockSpec((rows, tn), lambda j: (0, j)),
        out_shape=jax.ShapeDtypeStruct((rows, n), F32),
        compiler_params=_cparams("arbitrary"),
        name="ada",
    )(cc, w_ada, b_ada.reshape(1, n))


GRID_PITCH = GRID_W + 8


def _to_grid_view(src_ref, dst_ref):
    ng = src_ref.shape[0]
    r = src_ref.shape[1] // GRID_PITCH
    for c in range(GRID_W):
        for g in range(ng):
            lo = (c * ng + g) * LANES
            dst_ref[0, :, lo:lo + LANES] = src_ref[g, pl.ds(c, r, stride=GRID_PITCH), :]


def _from_grid_view(src_ref, dst_ref):
    ng = dst_ref.shape[0]
    r = dst_ref.shape[1] // GRID_PITCH
    for c in range(GRID_W):
        for g in range(ng):
            lo = (c * ng + g) * LANES
            dst_ref[g, pl.ds(c, r, stride=GRID_PITCH), :] = src_ref[0, :, lo:lo + LANES]


def _pitched_rows(ref, g):
    r = ref.shape[1] // GRID_PITCH
    return jnp.concatenate([ref[g, i * GRID_PITCH:i * GRID_PITCH + GRID_W, :] for i in range(r)], axis=0)


def _proj_kernel(grid_view, x_ref, nw_ref, sh_ref, sc_ref, wml_ref, wgq_ref, wgz_ref, wg_ref, gb_ref, *refs):
    if grid_view:
        ml_ref, gqv_ref, gz_ref, g_ref, gv_ref, gq_scr, g_scr = refs
    else:
        ml_ref, gq_ref, gz_ref, g_ref = refs
    x = x_ref[0]
    xn = x * lax.rsqrt(jnp.mean(x * x, axis=-1, keepdims=True) + EPS) * nw_ref[...]
    h = (xn * (1.0 + sc_ref[0]) + sh_ref[0]).astype(BF16)
    ml_ref[0] = jnp.dot(h, wml_ref[...], preferred_element_type=F32)
    gz_ref[0] = jnp.dot(h, wgz_ref[...], preferred_element_type=F32)
    gates = jnp.dot(h, wg_ref[...], preferred_element_type=F32) + gb_ref[...]
    g_ref[0] = gates
    gq = jnp.dot(h, wgq_ref[...], preferred_element_type=F32)
    if grid_view:
        for r in range(x.shape[0] // GRID_W):
            rows = slice(r * GRID_W, (r + 1) * GRID_W)
            prow = slice(r * GRID_PITCH, r * GRID_PITCH + GRID_W)
            g_scr[0, prow, :] = gates[rows]
            for g in range(gq_scr.shape[0]):
                gq_scr[g, prow, :] = gq[rows, g * LANES:(g + 1) * LANES]
        _to_grid_view(gq_scr, gqv_ref)
        _to_grid_view(g_scr, gv_ref)
    else:
        gq_ref[0] = gq


def _proj(x, norm_w, shift, scale, wml, wgq, wgz, wg, gbias, tm, grid_view):
    b, s, d = x.shape
    nml, ngq, ngz = wml.shape[1], wgq.shape[1], wgz.shape[1]
    full = lambda shp: pl.BlockSpec(shp, lambda bi, i: (0,) * len(shp))
    tok = lambda n: pl.BlockSpec((1, tm, n), lambda bi, i: (bi, i, 0))
    mod = pl.BlockSpec((1, 1, d), lambda bi, i: (bi, 0, 0))
    if grid_view:
        rt = tm // GRID_W
        view = lambda n: pl.BlockSpec((1, rt, GRID_W * n), lambda bi, i: (bi, i, 0))
        vshape = lambda n: jax.ShapeDtypeStruct((b, s // GRID_W, GRID_W * n), F32)
        out_specs = [tok(nml), view(ngq), tok(ngz), tok(GATE_LANES), view(GATE_LANES)]
        out_shape = [jax.ShapeDtypeStruct((b, s, nml), F32), vshape(ngq), jax.ShapeDtypeStruct((b, s, ngz), F32),
                     jax.ShapeDtypeStruct((b, s, GATE_LANES), F32), vshape(GATE_LANES)]
        scratch = [pltpu.VMEM((ngq // LANES, rt * GRID_PITCH, LANES), F32), pltpu.VMEM((1, rt * GRID_PITCH, LANES), F32)]
    else:
        out_specs = [tok(nml), tok(ngq), tok(ngz), tok(GATE_LANES)]
        out_shape = [jax.ShapeDtypeStruct((b, s, n), F32) for n in (nml, ngq, ngz, GATE_LANES)]
        scratch = []
    return pl.pallas_call(
        functools.partial(_proj_kernel, grid_view),
        grid=(b, s // tm),
        in_specs=[tok(d), full((1, d)), mod, mod, full((d, nml)), full((d, ngq)), full((d, ngz)),
                  full((d, GATE_LANES)), full((1, GATE_LANES))],
        out_specs=out_specs,
        out_shape=out_shape,
        scratch_shapes=scratch,
        compiler_params=_cparams("parallel", "arbitrary"),
        name="proj",
    )(x, norm_w, shift, scale, wml, wgq, wgz, wg, gbias)


def _mlstm_kernel(mlf_ref, mlb_ref, gf_ref, gb_ref, c0_ref, m0_ref, hf_ref, hb_ref, cn_ref, mn_ref, c_scr, m_scr):
    i = pl.program_id(1)

    @pl.when(i == 0)
    def _():
        c_scr[...] = c0_ref[0]
        m_scr[...] = m0_ref[0]

    past = [_past_mask(d == 1)[0] for d in range(2)]
    g = [r[0] for r in (gf_ref, gb_ref)]
    ls = [jax.nn.log_sigmoid(x) for x in g]
    bcol = [_dot_hi(past[d].astype(F32), ls[d]) for d in range(2)]
    tot = [jnp.sum(x, axis=0, keepdims=True) for x in ls]
    g_t = [_transpose_hi(x) for x in g]
    b_t = [_transpose_hi(x) for x in bcol]

    chains = [(d, hd) for d in range(2) for hd in range(ML_HEADS)]
    nc = range(len(chains))
    ml_refs, h_refs = (mlf_ref, mlb_ref), (hf_ref, hb_ref)
    k0, v0 = ML_HEADS * ML_QK, 2 * ML_HEADS * ML_QK
    ones_col = (lax.broadcasted_iota(jnp.int32, (CHUNK, ML_V), 1) == 0).astype(F32)
    q, k, v, i_col, b_col, b_end, log_d = [], [], [], [], [], [], []
    for d, hd in chains:
        ci = ML_GATE0 + d * 8 + hd
        cf = ci + ML_HEADS
        q.append(ml_refs[d][0, :, hd * ML_QK:(hd + 1) * ML_QK])
        k.append(ml_refs[d][0, :, k0 + hd * ML_QK:k0 + (hd + 1) * ML_QK] * (ML_QK ** -0.5))
        v.append(jnp.concatenate([ml_refs[d][0, :, v0 + hd * ML_V:v0 + (hd + 1) * ML_V], ones_col], axis=-1))
        i_col.append(g[d][:, ci:ci + 1])
        b_col.append(bcol[d][:, cf:cf + 1])
        b_end.append(tot[d][:, cf:cf + 1])
        log_d.append(jnp.where(past[d], b_col[-1] - b_t[d][cf:cf + 1, :] + g_t[d][ci:ci + 1, :], -jnp.inf))
    c_st = [c_scr[c] for c in nc]
    m_st = [m_scr[c] for c in nc]
    log_prev = [b_col[c] + m_st[c] for c in nc]
    m_t = [jnp.maximum(log_prev[c], jnp.max(log_d[c], axis=-1, keepdims=True)) for c in nc]
    qk = [_dot_nt(q[c], k[c]) for c in nc]
    qc = [_dot(q[c], c_st[c]) for c in nc]
    s = [qk[c] * jnp.exp(log_d[c] - m_t[c]) for c in nc]
    w_prev = [jnp.exp(log_prev[c] - m_t[c]) for c in nc]
    sv = [_dot(s[c], v[c]) for c in nc]
    log_s = [b_end[c] - b_col[c] + i_col[c] for c in nc]
    m_new = [jnp.maximum(b_end[c] + m_st[c], jnp.max(log_s[c], axis=0, keepdims=True)) for c in nc]
    kw = [k[c] * jnp.exp(log_s[c] - m_new[c]) for c in nc]
    w_c = [jnp.exp(b_end[c] + m_st[c] - m_new[c]) for c in nc]
    kv = [_dot_tn(kw[c], v[c]) for c in nc]
    numden = [sv[c] + w_prev[c] * qc[c] for c in nc]
    scale = [1.0 / jnp.maximum(jnp.abs(numden[c][:, ML_V:ML_V + 1]), jnp.exp(-m_t[c])) for c in nc]
    for c, (d, hd) in enumerate(chains):
        h_refs[d][0, :, hd * ML_V:(hd + 1) * ML_V] = numden[c][:, :ML_V] * scale[c]
        c_scr[c] = w_c[c] * c_st[c] + kv[c]
        m_scr[c] = m_new[c]

    @pl.when(i == pl.num_programs(1) - 1)
    def _():
        cn_ref[0] = c_scr[...]
        mn_ref[0] = m_scr[...]


def _mlstm(ml, gates, c0, m0):
    b, s, nml = ml.shape
    nc = s // CHUNK
    fwd = lambda n: pl.BlockSpec((1, CHUNK, n), lambda bi, i: (bi, i, 0))
    bwd = lambda n: pl.BlockSpec((1, CHUNK, n), lambda bi, i: (bi, nc - 1 - i, 0))
    st = lambda shp: pl.BlockSpec((1,) + shp, lambda bi, i: (bi,) + (0,) * len(shp))
    hdim = ML_HEADS * ML_V
    cshape = (N_CHAINS, ML_QK, 2 * ML_V)
    return pl.pallas_call(
        _mlstm_kernel,
        grid=(b, nc),
        in_specs=[fwd(nml), bwd(nml), fwd(GATE_LANES), bwd(GATE_LANES), st(cshape), st((N_CHAINS, 1, 1))],
        out_specs=[fwd(hdim), bwd(hdim), st(cshape), st((N_CHAINS, 1, 1))],
        out_shape=[jax.ShapeDtypeStruct((b, s, hdim), F32), jax.ShapeDtypeStruct((b, s, hdim), F32),
                   jax.ShapeDtypeStruct(c0.shape, F32), jax.ShapeDtypeStruct(m0.shape, F32)],
        scratch_shapes=[pltpu.VMEM(cshape, F32), pltpu.VMEM((N_CHAINS, 1, 1), F32)],
        compiler_params=_cparams("parallel", "arbitrary"),
        name="mlstm",
    )(ml, ml, gates, gates, c0, m0)


def _gdconv_kernel(has_halo, *refs):
    if has_halo:
        x_ref, prev_ref, next_ref, w_ref, o_ref, xp_ref = refs
    else:
        x_ref, w_ref, o_ref, xp_ref = refs
    rows = x_ref.shape[1]
    nch = x_ref.shape[2]
    pad = 8
    zero = jnp.zeros((pad, nch), F32)
    if has_halo:
        c = pl.program_id(1)
        xp_ref[0:pad, :] = jnp.where(c > 0, prev_ref[0], zero)
        xp_ref[pad + rows:, :] = jnp.where(c < pl.num_programs(1) - 1, next_ref[0], zero)
    else:
        xp_ref[0:pad, :] = zero
        xp_ref[pad + rows:, :] = zero
    xp_ref[pad:pad + rows, :] = x_ref[0]
    half = CONV_W // 2
    for lc in range(nch // 128):
        sl = slice(lc * 128, (lc + 1) * 128)
        acc = None
        for j in range(CONV_W):
            term = xp_ref[pad - half + j:pad - half + j + rows, sl] * w_ref[j:j + 1, sl]
            acc = term if acc is None else acc + term
        y = _silu(acc)
        if lc < 2 * GD_HEADS:
            y = y * lax.rsqrt(jnp.sum(y * y, axis=-1, keepdims=True) + EPS)
        if lc < GD_HEADS:
            y = y * (GD_QK ** -0.5)
        o_ref[0, :, sl] = y


def _gdconv_ctx(qkv, conv_w):
    b, s, nch = qkv.shape
    return pl.pallas_call(
        functools.partial(_gdconv_kernel, False),
        grid=(b,),
        in_specs=[pl.BlockSpec((1, s, nch), lambda bi: (bi, 0, 0)), pl.BlockSpec((CONV_W, nch), lambda bi: (0, 0))],
        out_specs=pl.BlockSpec((1, s, nch), lambda bi: (bi, 0, 0)),
        out_shape=jax.ShapeDtypeStruct((b, s, nch), F32),
        scratch_shapes=[pltpu.VMEM((s + 16, nch), F32)],
        compiler_params=_cparams("parallel"),
        name="gdconv_ctx",
    )(qkv, conv_w)


def _gdconv_lat(view, conv_w):
    b, rows, wn = view.shape
    nch = wn // GRID_W
    rb = rows // 8
    return pl.pallas_call(
        functools.partial(_gdconv_kernel, True),
        grid=(b, GRID_W),
        in_specs=[pl.BlockSpec((1, rows, nch), lambda bi, c: (bi, 0, c)),
                  pl.BlockSpec((1, 8, nch), lambda bi, c: (bi, rb - 1, jnp.maximum(c - 1, 0))),
                  pl.BlockSpec((1, 8, nch), lambda bi, c: (bi, 0, jnp.minimum(c + 1, GRID_W - 1))),
                  pl.BlockSpec((CONV_W, nch), lambda bi, c: (0, 0))],
        out_specs=pl.BlockSpec((1, rows, nch), lambda bi, c: (bi, 0, c)),
        out_shape=jax.ShapeDtypeStruct(view.shape, F32),
        scratch_shapes=[pltpu.VMEM((rows + 16, nch), F32)],
        compiler_params=_cparams("parallel", "arbitrary"),
        name="gdconv_lat",
    )(view, view, view, conv_w)


SOLVE_BLOCK = 16


def _hi_lo(x):
    hi = x.astype(BF16).astype(F32)
    return hi, x - hi


def _dot_split(a, b):
    a_hi, a_lo = _hi_lo(a)
    b_hi, b_lo = _hi_lo(b)
    lhs = jnp.concatenate([a_hi, a_hi, a_lo], axis=1).astype(BF16)
    rhs = jnp.concatenate([b_hi, b_lo, b_hi], axis=0).astype(BF16)
    return jnp.dot(lhs, rhs, preferred_element_type=F32)


def _unit_triangular_inverses(ns):
    c = ns[0].shape[0]
    row = lax.broadcasted_iota(jnp.int32, (c, c), 0)
    col = lax.broadcasted_iota(jnp.int32, (c, c), 1)
    eye = (row == col).astype(F32)
    in_diag_block = (row // SOLVE_BLOCK) == (col // SOLVE_BLOCK)
    mm = lambda a_list, b_list: [_dot_split(a, b) for a, b in zip(a_list, b_list)]

    n_d = [jnp.where(in_diag_block, n, 0.0) for n in ns]
    x = n_d
    d_inv = [eye - n for n in n_d]
    for _ in range(SOLVE_BLOCK.bit_length() - 2):
        x = mm(x, x)
        d_inv = [d + dx for d, dx in zip(d_inv, mm(d_inv, x))]
    m = mm(d_inv, [n - nd for n, nd in zip(ns, n_d)])
    assert c // SOLVE_BLOCK == 4
    i_minus_m = [eye - mi for mi in m]
    q = [a + b for a, b in zip(i_minus_m, mm(i_minus_m, mm(m, m)))]
    return mm(q, d_inv)


def _gdn_kernel(qf_ref, qb_ref, gf_ref, gb_ref, na_ref, s0_ref, of_ref, ob_ref, sn_ref, s_scr):
    i = pl.program_id(1)

    @pl.when(i == 0)
    def _():
        s_scr[...] = s0_ref[0]

    eye = (lax.broadcasted_iota(jnp.int32, (CHUNK, CHUNK), 0)
           == lax.broadcasted_iota(jnp.int32, (CHUNK, CHUNK), 1)).astype(F32)
    nqk = GD_HEADS * GD_QK
    masks = [_past_mask(d == 1) for d in range(2)]
    gates = [r[0] for r in (gf_ref, gb_ref)]
    glog = [na_ref[...] * jax.nn.softplus(g) for g in gates]
    beta_all = [jax.nn.sigmoid(g) for g in gates]
    gcum = [_dot_hi(masks[d][0].astype(F32), glog[d]) for d in range(2)]
    gtot = [jnp.sum(g, axis=0, keepdims=True) for g in glog]
    gcum_t = [_transpose_hi(g) for g in gcum]

    chains = [(d, hd) for d in range(2) for hd in range(GD_HEADS)]
    x_refs, o_refs = (qf_ref, qb_ref), (of_ref, ob_ref)
    q, k, v, g_col, beta, g_end, decay = [], [], [], [], [], [], []
    for d, hd in chains:
        ca = GD_GATE0 + d * 8 + hd
        q.append(x_refs[d][0, :, hd * GD_QK:(hd + 1) * GD_QK])
        k.append(x_refs[d][0, :, nqk + hd * GD_QK:nqk + (hd + 1) * GD_QK])
        v.append(x_refs[d][0, :, 2 * nqk + hd * GD_V:2 * nqk + (hd + 1) * GD_V])
        g_col.append(gcum[d][:, ca:ca + 1])
        beta.append(beta_all[d][:, ca + GD_HEADS:ca + GD_HEADS + 1])
        g_end.append(gtot[d][:, ca:ca + 1])
        decay.append(jnp.exp(jnp.where(masks[d][0], g_col[-1] - gcum_t[d][ca:ca + 1, :], -jnp.inf)))
    nc = range(len(chains))
    kk = [_dot_nt(k[c], k[c]) for c in nc]
    xs = [jnp.where(masks[chains[c][0]][1], beta[c] * kk[c] * decay[c], 0.0) for c in nc]
    ps = _unit_triangular_inverses(xs)
    uw = [_dot_split(ps[c], jnp.concatenate([v[c] * beta[c], k[c] * (beta[c] * jnp.exp(g_col[c]))], axis=-1))
          for c in nc]
    qk = [_dot_nt(q[c], k[c]) * decay[c] for c in nc]
    s_st = [s_scr[c] for c in nc]
    v_new = [uw[c][:, :GD_V] - _dot(uw[c][:, GD_V:], s_st[c]) for c in nc]
    o_loc = [_dot(q[c] * jnp.exp(g_col[c]), s_st[c]) for c in nc]
    o_new = [o_loc[c] + _dot(qk[c], v_new[c]) for c in nc]
    s_new = [s_st[c] * jnp.exp(g_end[c]) + _dot_tn(k[c] * jnp.exp(g_end[c] - g_col[c]), v_new[c]) for c in nc]
    for c, (d, hd) in enumerate(chains):
        o_refs[d][0, :, hd * GD_V:(hd + 1) * GD_V] = o_new[c]
        s_scr[c] = s_new[c]

    @pl.when(i == pl.num_programs(1) - 1)
    def _():
        sn_ref[0] = s_scr[...]


def _gdn(qkv_view, gates_view, neg_a, s0, nc, idx_fn, out_view_shape):
    b = qkv_view.shape[0]
    nqkv = 2 * GD_HEADS * GD_QK + GD_HEADS * GD_V
    hdim = GD_HEADS * GD_V
    fwd = lambda n: pl.BlockSpec((1, CHUNK, n), lambda bi, i: idx_fn(bi, i))
    bwd = lambda n: pl.BlockSpec((1, CHUNK, n), lambda bi, i: idx_fn(bi, nc - 1 - i))
    st = pl.BlockSpec((1, N_CHAINS, GD_QK, GD_V), lambda bi, i: (bi, 0, 0, 0))
    return pl.pallas_call(
        _gdn_kernel,
        grid=(b, nc),
        in_specs=[fwd(nqkv), bwd(nqkv), fwd(GATE_LANES), bwd(GATE_LANES),
                  pl.BlockSpec((1, GATE_LANES), lambda bi, i: (0, 0)), st],
        out_specs=[fwd(hdim), bwd(hdim), st],
        out_shape=[jax.ShapeDtypeStruct(out_view_shape, F32), jax.ShapeDtypeStruct(out_view_shape, F32),
                   jax.ShapeDtypeStruct(s0.shape, F32)],
        scratch_shapes=[pltpu.VMEM((N_CHAINS, GD_QK, GD_V), F32)],
        compiler_params=_cparams("parallel", "arbitrary"),
        name="gdn",
    )(qkv_view, qkv_view, gates_view, gates_view, neg_a, s0)


def _head_rms(t, nheads, width):
    outs = []
    for hd in range(nheads):
        th = t[:, hd * width:(hd + 1) * width]
        outs.append(th * lax.rsqrt(jnp.mean(th * th, axis=-1, keepdims=True) + EPS))
    return jnp.concatenate(outs, axis=-1)


def _rms(t, w):
    return t * lax.rsqrt(jnp.mean(t * t, axis=-1, keepdims=True) + EPS) * w


def _post_kernel(x_ref, hf_ref, hb_ref, og_ref, of_ref, ob_ref, z_ref, mlw_ref, gdw_ref, wout_ref,
                 npost_ref, g2_ref, npre_ref, sh_ref, sc_ref, rwt_ref, wsg_ref, wsu_ref, wsd_ref,
                 x1_ref, hffn_ref, lt_ref, ys_ref, of_scr, ob_scr):
    ml_y = _head_rms(hf_ref[0] + hb_ref[0], ML_HEADS, ML_V) * mlw_ref[...] * jax.nn.sigmoid(og_ref[0])
    _from_grid_view(of_ref, of_scr)
    _from_grid_view(ob_ref, ob_scr)
    o_sum = jnp.concatenate([_pitched_rows(of_scr, g) + _pitched_rows(ob_scr, g) for g in range(of_scr.shape[0])],
                            axis=-1)
    gd_y = _head_rms(o_sum, GD_HEADS, GD_V) * gdw_ref[...] * _silu(z_ref[0])
    y = _dot(jnp.concatenate([ml_y, gd_y], axis=-1), wout_ref[...])
    x1 = x_ref[0] + g2_ref[0] * _rms(y, npost_ref[...])
    x1_ref[0] = x1
    hffn = _rms(x1, npre_ref[...]) * (1.0 + sc_ref[0]) + sh_ref[0]
    nct = hffn.shape[1] // LANES
    for c in range(nct):
        hffn_ref[0, pl.ds(c, hffn.shape[0], stride=nct), :] = hffn[:, c * LANES:(c + 1) * LANES]
    hb = hffn.astype(BF16)
    lt_ref[...] = lax.dot_general(rwt_ref[...], hb, (((1,), (1,)), ((), ())), preferred_element_type=F32)
    hs = _silu(jnp.dot(hb, wsg_ref[...], preferred_element_type=F32)) * jnp.dot(hb, wsu_ref[...],
                                                                                preferred_element_type=F32)
    ys_ref[0] = _dot(hs, wsd_ref[...])


def _post(x, hf, hb, ml, of, ob, gz, mlw, gdw, wout, npost, g2, npre, sh, sc, rwt, wsg, wsu, wsd, tm):
    b, s, d = x.shape
    nt = s // tm
    hw = ML_HEADS * ML_V
    og_blk = (2 * ML_HEADS * ML_QK + ML_HEADS * ML_V) // hw
    tok = lambda n: pl.BlockSpec((1, tm, n), lambda bi, i: (bi, i, 0))
    full = lambda shp: pl.BlockSpec(shp, lambda bi, i: (0,) * len(shp))
    mod = pl.BlockSpec((1, 1, d), lambda bi, i: (bi, 0, 0))
    ne = rwt.shape[0]
    ds = wsg.shape[1]
    gview = pl.BlockSpec((1, tm // GRID_W, GRID_W * hw), lambda bi, i: (bi, i, 0))
    return pl.pallas_call(
        _post_kernel,
        grid=(b, nt),
        in_specs=[tok(d), tok(hw), tok(hw), pl.BlockSpec((1, tm, hw), lambda bi, i: (bi, i, og_blk)),
                  gview, gview, tok(hw), full((1, hw)), full((1, hw)), full((d, d)),
                  full((1, d)), mod, full((1, d)), mod, mod, full((ne, d)), full((d, ds)), full((d, ds)),
                  full((ds, d))],
        out_specs=[tok(d), pl.BlockSpec((1, tm * (d // LANES), LANES), lambda bi, i: (bi, i, 0)),
                   pl.BlockSpec((ne, tm), lambda bi, i: (0, bi * nt + i)), tok(d)],
        out_shape=[jax.ShapeDtypeStruct((b, s, d), F32), jax.ShapeDtypeStruct((b, s * (d // LANES), LANES), F32),
                   jax.ShapeDtypeStruct((ne, b * s), F32), jax.ShapeDtypeStruct((b, s, d), F32)],
        scratch_shapes=[pltpu.VMEM((hw // LANES, tm // GRID_W * GRID_PITCH, LANES), F32)] * 2,
        compiler_params=_cparams("parallel", "arbitrary"),
        name="post",
    )(x, hf, hb, ml, of, ob, gz, mlw, gdw, wout, npost, g2, npre, sh, sc, rwt, wsg, wsu, wsd)


def _route_kernel(lt_ref, bias_ref, idx_ref, gate_ref):
    ne, tn = lt_ref.shape
    gsz = ne // N_GROUPS
    scores = jax.nn.sigmoid(lt_ref[...])
    sel = scores + bias_ref[...]
    neg = -jnp.inf
    sel3 = sel.reshape(N_GROUPS, gsz, tn)
    io3 = lax.broadcasted_iota(jnp.int32, sel3.shape, 1)
    top1 = jnp.max(sel3, axis=1, keepdims=True)
    first = jnp.min(jnp.where(sel3 == top1, io3, gsz), axis=1, keepdims=True)
    top2 = jnp.max(jnp.where(io3 == first, neg, sel3), axis=1, keepdims=True)
    grp = (top1 + top2).reshape(N_GROUPS, tn)
    iog = lax.broadcasted_iota(jnp.int32, grp.shape, 0)
    keep = jnp.zeros(grp.shape, jnp.bool_)
    for _ in range(TOPK_GROUPS):
        m = jnp.max(grp, axis=0, keepdims=True)
        pick = iog == jnp.min(jnp.where(grp == m, iog, N_GROUPS), axis=0, keepdims=True)
        keep = keep | pick
        grp = jnp.where(pick, neg, grp)
    cand = jnp.where(keep.reshape(N_GROUPS, 1, tn), sel3, neg).reshape(ne, tn)
    ioe = lax.broadcasted_iota(jnp.int32, cand.shape, 0)
    idxs, gates = [], []
    for _ in range(TOP_K):
        m = jnp.max(cand, axis=0, keepdims=True)
        e = jnp.min(jnp.where(cand == m, ioe, ne), axis=0, keepdims=True)
        pick = ioe == e
        idxs.append(e)
        gates.append(jnp.sum(jnp.where(pick, scores, 0.0), axis=0, keepdims=True))
        cand = jnp.where(pick, neg, cand)
    gate = jnp.concatenate(gates, axis=0)
    idx_ref[...] = jnp.concatenate(idxs, axis=0)
    gate_ref[...] = gate / jnp.sum(gate, axis=0, keepdims=True) * ROUTED_SCALE


def _route(logits_t, bias_col, tn):
    ne, t = logits_t.shape
    return pl.pallas_call(
        _route_kernel,
        grid=(t // tn,),
        in_specs=[pl.BlockSpec((ne, tn), lambda i: (0, i)), pl.BlockSpec((ne, 1), lambda i: (0, 0))],
        out_specs=[pl.BlockSpec((TOP_K, tn), lambda i: (0, i)), pl.BlockSpec((TOP_K, tn), lambda i: (0, i))],
        out_shape=[jax.ShapeDtypeStruct((TOP_K, t), jnp.int32), jax.ShapeDtypeStruct((TOP_K, t), F32)],
        compiler_params=_cparams("parallel"),
        name="route",
    )(logits_t, bias_col)


def _experts_kernel(be_ref, np_ref,
                    tok_ref, tokn_ref, w_ref, wg0_ref, wu0_ref, wd0_ref, wg1_ref, wu1_ref, wd1_ref, h_hbm,
                    o_ref, xg, wgc, wuc, wdc, gsem):
    s = pl.program_id(0)
    n_pairs = np_ref[0]
    nct = xg.shape[1] // EXPERT_BLOCK
    rows = nct * EXPERT_BLOCK
    w_refs = ((wg0_ref, wu0_ref, wd0_ref), (wg1_ref, wu1_ref, wd1_ref))

    def gather_copy(tref, p, j):
        src = h_hbm.at[pl.ds(pl.multiple_of(tref[0, p, j] * nct, nct), nct)]
        return pltpu.make_async_copy(src, xg.at[p, pl.ds(j * nct, nct)], gsem.at[p])

    def gather_wait(p):
        pltpu.make_async_copy(h_hbm.at[pl.ds(0, rows)], xg.at[p], gsem.at[p]).wait()

    @pl.when(s >= n_pairs)
    def _():
        o_ref[...] = jnp.zeros(o_ref.shape, F32)

    @pl.when(s < n_pairs)
    def _():
        @pl.when(s == 0)
        def _():
            for p in range(2):
                for j in range(EXPERT_BLOCK):
                    gather_copy(tok_ref, p, j).start(priority=j % 2)

        for p in range(2):
            blk = 2 * s + p
            wg_ref, wu_ref, wd_ref = w_refs[p]
            gather_wait(p)

            @pl.when((blk == 0) | (be_ref[blk] != be_ref[jnp.maximum(blk - 1, 0)]))
            def _():
                wgc[...] = wg_ref[0].astype(BF16)
                wuc[...] = wu_ref[0].astype(BF16)
                wdc[...] = wd_ref[0].astype(BF16)

            xb = jnp.concatenate([xg[p, pl.ds(c, EXPERT_BLOCK, stride=nct), :] for c in range(nct)],
                                 axis=-1).astype(BF16)
            for j in range(EXPERT_BLOCK):
                gather_copy(tokn_ref, p, j).start(priority=j % 2)
            hmid = _silu(jnp.dot(xb, wgc[...], preferred_element_type=F32)) * jnp.dot(xb, wuc[...],
                                                                                      preferred_element_type=F32)
            out = _dot(hmid, wdc[...])
            eye = (lax.broadcasted_iota(jnp.int32, (EXPERT_BLOCK, EXPERT_BLOCK), 0)
                   == lax.broadcasted_iota(jnp.int32, (EXPERT_BLOCK, EXPERT_BLOCK), 1))
            w_col = jnp.sum(jnp.where(eye, w_ref[0, p:p + 1, :], 0.0), axis=1, keepdims=True)
            out = out * w_col
            for c in range(nct):
                o_ref[pl.ds(p * rows + c, EXPERT_BLOCK, stride=nct), :] = out[:, c * LANES:(c + 1) * LANES]

        @pl.when(s == n_pairs - 1)
        def _():
            for p in range(2):
                gather_wait(p)


def _experts(hffn, block_e, n_pairs, row_tok, row_w, wg, wu, wd):
    d = wg.shape[1]
    nct = d // LANES
    npairs = row_tok.shape[0]
    de = wg.shape[2]
    last = npairs - 1
    smem_blk = lambda f: pl.BlockSpec((1, 2, EXPERT_BLOCK), f, memory_space=pltpu.SMEM)
    wspec = lambda shp, p: pl.BlockSpec((1,) + shp, lambda s, be, npu: (be[2 * s + p], 0, 0))
    pair_rows = 2 * EXPERT_BLOCK * nct
    grid_spec = pltpu.PrefetchScalarGridSpec(
        num_scalar_prefetch=2,
        grid=(npairs,),
        in_specs=[smem_blk(lambda s, be, npu: (s, 0, 0)),
                  smem_blk(lambda s, be, npu: (jnp.minimum(s + 1, last), 0, 0)),
                  pl.BlockSpec((1, 2, EXPERT_BLOCK), lambda s, be, npu: (s, 0, 0)),
                  wspec((d, de), 0), wspec((d, de), 0), wspec((de, d), 0),
                  wspec((d, de), 1), wspec((d, de), 1), wspec((de, d), 1),
                  pl.BlockSpec(memory_space=pl.ANY)],
        out_specs=pl.BlockSpec((pair_rows, LANES), lambda s, be, npu: (s, 0)),
        scratch_shapes=[pltpu.VMEM((2, EXPERT_BLOCK * nct, LANES), F32),
                        pltpu.VMEM((d, de), BF16), pltpu.VMEM((d, de), BF16), pltpu.VMEM((de, d), BF16),
                        pltpu.SemaphoreType.DMA((2,))],
    )
    return pl.pallas_call(
        _experts_kernel,
        grid_spec=grid_spec,
        out_shape=jax.ShapeDtypeStruct((npairs * pair_rows, LANES), F32),
        compiler_params=_cparams("arbitrary"),
        name="experts",
    )(block_e, n_pairs, row_tok, row_tok, row_w, wg, wu, wd, wg, wu, wd, hffn)


def _combine_kernel(pos_ref, posn_ref, x1_ref, ys_ref, npost_ref, g5_ref, rows_hbm, o_ref, buf, sem):
    i = pl.program_id(0)
    tm = x1_ref.shape[1]
    nct = x1_ref.shape[2] // LANES
    cur = i % 2
    nxt = 1 - cur

    def copy(pref, b, k, t):
        src = rows_hbm.at[pl.ds(pl.multiple_of(pref[k, t] * nct, nct), nct)]
        dst = buf.at[b, pl.ds(pl.multiple_of((k * tm + t) * nct, nct), nct)]
        return pltpu.make_async_copy(src, dst, sem.at[b])

    def drain(b):
        pltpu.make_async_copy(rows_hbm.at[pl.ds(0, TOP_K * tm * nct)], buf.at[b], sem.at[b]).wait()

    def issue(pref, b):
        def body(t2, carry):
            for u in range(2):
                for k in range(TOP_K):
                    copy(pref, b, k, 2 * t2 + u).start(priority=k % 2)
            return carry
        lax.fori_loop(0, tm // 2, body, 0)

    @pl.when(i == 0)
    def _():
        issue(pos_ref, cur)

    @pl.when(i + 1 < pl.num_programs(0))
    def _():
        for t in range(tm):
            for k in range(TOP_K):
                copy(posn_ref, nxt, k, t).start(priority=k % 2)

    drain(cur)
    routed = []
    for c in range(nct):
        acc = buf[cur, pl.ds(c, tm, stride=nct), :]
        for k in range(1, TOP_K):
            acc = acc + buf[cur, pl.ds(k * tm * nct + c, tm, stride=nct), :]
        routed.append(acc)
    y = ys_ref[0] + jnp.concatenate(routed, axis=-1)
    o_ref[0] = x1_ref[0] + g5_ref[0] * _rms(y, npost_ref[...])


def _combine(x1, ys, rows, pos, npost, g5, tm):
    b, s, d = x1.shape
    nt = s // tm
    nct = d // LANES
    last = b * nt - 1
    tok = pl.BlockSpec((1, tm, d), lambda i: (i // nt, i % nt, 0))
    pos_blk = lambda f: pl.BlockSpec((TOP_K, tm), f, memory_space=pltpu.SMEM)
    return pl.pallas_call(
        _combine_kernel,
        grid=(b * nt,),
        in_specs=[pos_blk(lambda i: (0, i)), pos_blk(lambda i: (0, jnp.minimum(i + 1, last))), tok, tok,
                  pl.BlockSpec((1, d), lambda i: (0, 0)), pl.BlockSpec((1, 1, d), lambda i: (i // nt, 0, 0)),
                  pl.BlockSpec(memory_space=pl.ANY)],
        out_specs=tok,
        out_shape=jax.ShapeDtypeStruct((b, s, d), F32),
        scratch_shapes=[pltpu.VMEM((2, TOP_K * tm * nct, LANES), F32), pltpu.SemaphoreType.DMA((2,))],
        compiler_params=_cparams("arbitrary"),
        name="combine",
    )(pos, pos, x1, ys, npost, g5, rows)


def _dispatch_plan(idx_t, gate_t):
    k, t = idx_t.shape
    n_asg = k * t
    nb = n_asg // EXPERT_BLOCK + N_EXPERTS
    flat_e = idx_t.reshape(-1)
    id_bits = max(1, (n_asg - 1).bit_length())
    assert (N_EXPERTS - 1).bit_length() + id_bits <= 31
    packed = jnp.sort((flat_e << id_bits) | jnp.arange(n_asg, dtype=jnp.int32))
    order = packed & ((1 << id_bits) - 1)
    counts = jnp.zeros((N_EXPERTS,), jnp.int32).at[flat_e].add(1)
    padded = (counts + EXPERT_BLOCK - 1) // EXPERT_BLOCK * EXPERT_BLOCK
    start = jnp.cumsum(counts) - counts
    pend = jnp.cumsum(padded)
    pstart = pend - padded
    blk0 = jnp.arange(nb, dtype=jnp.int32) * EXPERT_BLOCK
    block_e = jnp.minimum(jnp.sum((pend[None, :] <= blk0[:, None]).astype(jnp.int32), axis=1), N_EXPERTS - 1)
    of_block = block_e[:, None] == jnp.arange(N_EXPERTS, dtype=jnp.int32)[None, :]
    per_block = lambda v: jnp.sum(jnp.where(of_block, v[None, :], 0), axis=1)
    assert nb % 2 == 0
    n_pairs = ((pend[-1] // EXPERT_BLOCK + 1) // 2).astype(jnp.int32).reshape(1)
    pos = blk0[:, None] - per_block(pstart)[:, None] + jnp.arange(EXPERT_BLOCK, dtype=jnp.int32)[None, :]
    valid = pos < per_block(counts)[:, None]
    src = jnp.clip(per_block(start)[:, None] + pos, 0, n_asg - 1)
    asg = order[src]
    row_tok = jnp.where(valid, asg % t, 0).astype(jnp.int32)
    row_w = jnp.where(valid, gate_t.reshape(-1)[asg], 0.0).astype(F32)
    i_sorted = jnp.arange(n_asg, dtype=jnp.int32)
    pad_before = jnp.sum(jnp.where(i_sorted[:, None] >= (start + counts)[None, :], (padded - counts)[None, :], 0),
                         axis=1)
    _, pos = lax.sort((order, i_sorted + pad_before), num_keys=1)
    pos = pos.reshape(k, t)
    shp = (nb // 2, 2, EXPERT_BLOCK)
    return block_e, n_pairs, row_tok.reshape(shp), row_w.reshape(shp), pos


def _pack_in_weights(w_in, ml_i_bias, ml_f_bias, gd_dt_bias):
    d = w_in.shape[0]
    nml = 2 * ML_HEADS * ML_QK + 2 * ML_HEADS * ML_V
    ml_cols = nml + 4 * ML_HEADS
    ngq = GD_HEADS * (2 * GD_QK + GD_V)
    ngz = GD_HEADS * GD_V
    wml = w_in[:, :nml].astype(BF16)
    wgq = w_in[:, ml_cols:ml_cols + ngq].astype(BF16)
    wgz = w_in[:, ml_cols + ngq:ml_cols + ngq + ngz].astype(BF16)
    wg = jnp.zeros((d, GATE_LANES), F32)
    wg = wg.at[:, ML_GATE0:ML_GATE0 + 16].set(w_in[:, nml:ml_cols])
    wg = wg.at[:, GD_GATE0:GD_GATE0 + 16].set(w_in[:, ml_cols + ngq + ngz:])
    gb = jnp.zeros((GATE_LANES,), F32)
    gb = gb.at[ML_GATE0:ML_GATE0 + 16].set(jnp.stack([ml_i_bias, ml_f_bias], axis=1).reshape(-1))
    gb = gb.at[GD_GATE0:GD_GATE0 + 16].set(jnp.stack([gd_dt_bias, jnp.zeros_like(gd_dt_bias)], axis=1).reshape(-1))
    return wml, wgq, wgz, wg.astype(BF16), gb.reshape(1, GATE_LANES)


def _mixer(x, ctx, mod, mod_ctx, norm_pre_mix, w_in, ml_i_bias, ml_f_bias, gd_conv_w, gd_a_log, gd_dt_bias):
    b, s, d = x.shape
    sc = ctx.shape[1]
    wml, wgq, wgz, wg, gb = _pack_in_weights(w_in, ml_i_bias, ml_f_bias, gd_dt_bias)
    nw = norm_pre_mix.reshape(1, d)
    ctx_mod = lambda j: jnp.broadcast_to(mod_ctx[j].reshape(1, 1, d), (b, 1, d))
    ml_c, gq_c, _, g_c = _proj(ctx, nw, ctx_mod(0), ctx_mod(1), wml, wgq, wgz, wg, gb, tm=sc, grid_view=False)
    ml_l, gqv_l, gz_l, g_l, gv_l = _proj(x, nw, mod[0], mod[1], wml, wgq, wgz, wg, gb, tm=512, grid_view=True)

    c0 = jnp.zeros((b, N_CHAINS, ML_QK, 2 * ML_V), F32)
    m0 = jnp.zeros((b, N_CHAINS, 1, 1), F32)
    _, _, c1, m1 = _mlstm(ml_c, g_c, c0, m0)
    hf, hb, _, _ = _mlstm(ml_l, g_l, c1, m1)

    neg_a = jnp.zeros((GATE_LANES,), F32)
    neg_a = neg_a.at[GD_GATE0:GD_GATE0 + 16].set(
        jnp.stack([-jnp.exp(gd_a_log), jnp.zeros_like(gd_a_log)], axis=1).reshape(-1)).reshape(1, GATE_LANES)
    qn_c = _gdconv_ctx(gq_c, gd_conv_w)
    qnv_l = _gdconv_lat(gqv_l, gd_conv_w)
    s0 = jnp.zeros((b, N_CHAINS, GD_QK, GD_V), F32)
    hdim = GD_HEADS * GD_V
    _, _, s1 = _gdn(qn_c, g_c, neg_a, s0, sc // CHUNK, lambda bi, n: (bi, n, 0), (b, sc, hdim))
    rows = s // GRID_W
    cpc = rows // CHUNK
    col_idx = lambda bi, n: (bi, n % cpc, n // cpc)
    ofv, obv, _ = _gdn(qnv_l, gv_l, neg_a, s1, s // CHUNK, col_idx, (b, rows, GRID_W * hdim))
    return hf, hb, ml_l, ofv, obv, gz_l


def kernel(x, c, ctx, c_ctx, w_ada, b_ada, norm_pre_mix, norm_post_mix, norm_pre_ffn, norm_post_ffn, w_in,
           ml_i_bias, ml_f_bias, ml_norm_w, gd_conv_w, gd_a_log, gd_dt_bias, gd_norm_w, w_out, router_w,
           router_bias, w_gate, w_up, w_down, ws_gate, ws_up, ws_down):
    b, s, d = x.shape
    depth = w_ada.shape[0]
    assert depth == 1, "the context stream update of deeper stacks is not implemented"
    ly = 0
    cc = jnp.zeros((16, d), F32).at[:b].set(c).at[b].set(c_ctx)
    mod_all = _ada(cc, w_ada[ly], b_ada[ly])
    mod = [mod_all[:b, j * d:(j + 1) * d].reshape(b, 1, d) for j in range(6)]
    mod_ctx = [mod_all[b, j * d:(j + 1) * d] for j in range(6)]

    hf, hb, ml_l, of, ob, gz_l = _mixer(x, ctx, mod, mod_ctx, norm_pre_mix[ly], w_in[ly], ml_i_bias[ly],
                                        ml_f_bias[ly], gd_conv_w[ly], gd_a_log[ly], gd_dt_bias[ly])

    row = lambda v: v.reshape(1, -1)
    x1, hffn, logits_t, ys = _post(
        x, hf, hb, ml_l, of, ob, gz_l, row(ml_norm_w[ly]), row(jnp.tile(gd_norm_w[ly], GD_HEADS)),
        w_out[ly].astype(BF16), row(norm_post_mix[ly]), mod[2], row(norm_pre_ffn[ly]), mod[3], mod[4],
        router_w[ly].T.astype(BF16), ws_gate[ly].astype(BF16), ws_up[ly].astype(BF16), ws_down[ly].astype(BF16),
        tm=512)

    idx_t, gate_t = _route(logits_t, router_bias[ly].reshape(-1, 1), tn=512)
    block_e, n_pairs, row_tok, row_w, pos = _dispatch_plan(idx_t, gate_t)
    t = b * s
    rows = _experts(hffn.reshape(t * (d // LANES), LANES), block_e, n_pairs, row_tok, row_w,
                    w_gate[ly], w_up[ly], w_down[ly])
    return _combine(x1, ys, rows, pos, row(norm_post_ffn[ly]), mod[5], tm=256)
```

```python
import functools

import jax
import jax.numpy as jnp
from jax import lax
from jax.experimental import pallas as pl
from jax.experimental.pallas import tpu as pltpu

EPS = 1e-6
CHUNK = 64
GRID_W = 64
ML_HEADS, ML_QK, ML_V = 4, 64, 128
GD_HEADS, GD_QK, GD_V = 4, 128, 128
CONV_W = 5
N_EXPERTS, TOP_K, N_GROUPS, TOPK_GROUPS = 256, 8, 8, 4
ROUTED_SCALE = 2.5
EXPERT_BLOCK = 128
N_CHAINS = 8
LANES = 128
GATE_LANES = LANES
ML_GATE0, GD_GATE0 = 0, 16

F32 = jnp.float32
BF16 = jnp.bfloat16
HI = lax.Precision.HIGHEST
VMEM_LIMIT = 56 * 1024 * 1024


def _cparams(*sem):
    return pltpu.CompilerParams(dimension_semantics=sem, vmem_limit_bytes=VMEM_LIMIT)


def _dot(a, b):
    return jnp.dot(a.astype(BF16), b.astype(BF16), preferred_element_type=F32)


def _dot_nt(a, b):
    return lax.dot_general(a.astype(BF16), b.astype(BF16), (((1,), (1,)), ((), ())), preferred_element_type=F32)


def _dot_tn(a, b):
    return lax.dot_general(a.astype(BF16), b.astype(BF16), (((0,), (0,)), ((), ())), preferred_element_type=F32)


def _dot_hi(a, b):
    return jnp.dot(a, b, precision=HI, preferred_element_type=F32)


def _dot_nt_hi(a, b):
    return lax.dot_general(a, b, (((1,), (1,)), ((), ())), precision=HI, preferred_element_type=F32)


def _transpose_hi(x):
    n = x.shape[1]
    eye = (lax.broadcasted_iota(jnp.int32, (n, n), 0) == lax.broadcasted_iota(jnp.int32, (n, n), 1)).astype(F32)
    return _dot_nt_hi(eye, x)


def _silu(x):
    return x * jax.nn.sigmoid(x)


def _past_mask(reverse):
    t = lax.broadcasted_iota(jnp.int32, (CHUNK, CHUNK), 0)
    s = lax.broadcasted_iota(jnp.int32, (CHUNK, CHUNK), 1)
    return (s >= t, s > t) if reverse else (s <= t, s < t)


def _ada_kernel(c_ref, w_ref, b_ref, o_ref):
    o_ref[...] = _dot(_silu(c_ref[...]), w_ref[...]) + b_ref[...]


def _ada(cc, w_ada, b_ada):
    rows, d = cc.shape
    n = w_ada.shape[1]
    tn = 1536
    return pl.pallas_call(
        _ada_kernel,
        grid=(n // tn,),
        in_specs=[pl.BlockSpec((rows, d), lambda j: (0, 0)),
                  pl.BlockSpec((d, tn), lambda j: (0, j)),
                  pl.BlockSpec((1, tn), lambda j: (0, j))],
        out_specs=pl.BlockSpec((rows, tn), lambda j: (0, j)),
        out_shape=jax.ShapeDtypeStruct((rows, n), F32),
        compiler_params=_cparams("arbitrary"),
        name="ada",
    )(cc, w_ada, b_ada.reshape(1, n))


GRID_PITCH = GRID_W + 8


def _to_grid_view(src_ref, dst_ref):
    ng = src_ref.shape[0]
    r = src_ref.shape[1] // GRID_PITCH
    for c in range(GRID_W):
        for g in range(ng):
            lo = (c * ng + g) * LANES
            dst_ref[0, :, lo:lo + LANES] = src_ref[g, pl.ds(c, r, stride=GRID_PITCH), :]


def _from_grid_view(src_ref, dst_ref):
    ng = dst_ref.shape[0]
    r = dst_ref.shape[1] // GRID_PITCH
    for c in range(GRID_W):
        for g in range(ng):
            lo = (c * ng + g) * LANES
            dst_ref[g, pl.ds(c, r, stride=GRID_PITCH), :] = src_ref[0, :, lo:lo + LANES]


def _pitched_rows(ref, g):
    r = ref.shape[1] // GRID_PITCH
    return jnp.concatenate([ref[g, i * GRID_PITCH:i * GRID_PITCH + GRID_W, :] for i in range(r)], axis=0)


def _proj_kernel(grid_view, x_ref, nw_ref, sh_ref, sc_ref, wml_ref, wgq_ref, wgz_ref, wg_ref, gb_ref, *refs):
    if grid_view:
        ml_ref, gqv_ref, gz_ref, g_ref, gv_ref, gq_scr, g_scr = refs
    else:
        ml_ref, gq_ref, gz_ref, g_ref = refs
    x = x_ref[0]
    xn = x * lax.rsqrt(jnp.mean(x * x, axis=-1, keepdims=True) + EPS) * nw_ref[...]
    h = (xn * (1.0 + sc_ref[0]) + sh_ref[0]).astype(BF16)
    ml_ref[0] = jnp.dot(h, wml_ref[...], preferred_element_type=F32)
    gz_ref[0] = jnp.dot(h, wgz_ref[...], preferred_element_type=F32)
    gates = jnp.dot(h, wg_ref[...], preferred_element_type=F32) + gb_ref[...]
    g_ref[0] = gates
    gq = jnp.dot(h, wgq_ref[...], preferred_element_type=F32)
    if grid_view:
        for r in range(x.shape[0] // GRID_W):
            rows = slice(r * GRID_W, (r + 1) * GRID_W)
            prow = slice(r * GRID_PITCH, r * GRID_PITCH + GRID_W)
            g_scr[0, prow, :] = gates[rows]
            for g in range(gq_scr.shape[0]):
                gq_scr[g, prow, :] = gq[rows, g * LANES:(g + 1) * LANES]
        _to_grid_view(gq_scr, gqv_ref)
        _to_grid_view(g_scr, gv_ref)
    else:
        gq_ref[0] = gq


def _proj(x, norm_w, shift, scale, wml, wgq, wgz, wg, gbias, tm, grid_view):
    b, s, d = x.shape
    nml, ngq, ngz = wml.shape[1], wgq.shape[1], wgz.shape[1]
    full = lambda shp: pl.BlockSpec(shp, lambda bi, i: (0,) * len(shp))
    tok = lambda n: pl.BlockSpec((1, tm, n), lambda bi, i: (bi, i, 0))
    mod = pl.BlockSpec((1, 1, d), lambda bi, i: (bi, 0, 0))
    if grid_view:
        rt = tm // GRID_W
        view = lambda n: pl.BlockSpec((1, rt, GRID_W * n), lambda bi, i: (bi, i, 0))
        vshape = lambda n: jax.ShapeDtypeStruct((b, s // GRID_W, GRID_W * n), F32)
        out_specs = [tok(nml), view(ngq), tok(ngz), tok(GATE_LANES), view(GATE_LANES)]
        out_shape = [jax.ShapeDtypeStruct((b, s, nml), F32), vshape(ngq), jax.ShapeDtypeStruct((b, s, ngz), F32),
                     jax.ShapeDtypeStruct((b, s, GATE_LANES), F32), vshape(GATE_LANES)]
        scratch = [pltpu.VMEM((ngq // LANES, rt * GRID_PITCH, LANES), F32), pltpu.VMEM((1, rt * GRID_PITCH, LANES), F32)]
    else:
        out_specs = [tok(nml), tok(ngq), tok(ngz), tok(GATE_LANES)]
        out_shape = [jax.ShapeDtypeStruct((b, s, n), F32) for n in (nml, ngq, ngz, GATE_LANES)]
        scratch = []
    return pl.pallas_call(
        functools.partial(_proj_kernel, grid_view),
        grid=(b, s // tm),
        in_specs=[tok(d), full((1, d)), mod, mod, full((d, nml)), full((d, ngq)), full((d, ngz)),
                  full((d, GATE_LANES)), full((1, GATE_LANES))],
        out_specs=out_specs,
        out_shape=out_shape,
        scratch_shapes=scratch,
        compiler_params=_cparams("parallel", "arbitrary"),
        name="proj",
    )(x, norm_w, shift, scale, wml, wgq, wgz, wg, gbias)


def _mlstm_kernel(mlf_ref, mlb_ref, gf_ref, gb_ref, c0_ref, m0_ref, hf_ref, hb_ref, cn_ref, mn_ref, c_scr, m_scr):
    i = pl.program_id(1)

    @pl.when(i == 0)
    def _():
        c_scr[...] = c0_ref[0]
        m_scr[...] = m0_ref[0]

    past = [_past_mask(d == 1)[0] for d in range(2)]
    g = [r[0] for r in (gf_ref, gb_ref)]
    ls = [jax.nn.log_sigmoid(x) for x in g]
    bcol = [_dot_hi(past[d].astype(F32), ls[d]) for d in range(2)]
    tot = [jnp.sum(x, axis=0, keepdims=True) for x in ls]
    g_t = [_transpose_hi(x) for x in g]
    b_t = [_transpose_hi(x) for x in bcol]

    chains = [(d, hd) for d in range(2) for hd in range(ML_HEADS)]
    nc = range(len(chains))
    ml_refs, h_refs = (mlf_ref, mlb_ref), (hf_ref, hb_ref)
    k0, v0 = ML_HEADS * ML_QK, 2 * ML_HEADS * ML_QK
    ones_col = (lax.broadcasted_iota(jnp.int32, (CHUNK, ML_V), 1) == 0).astype(F32)
    q, k, v, i_col, b_col, b_end, log_d = [], [], [], [], [], [], []
    for d, hd in chains:
        ci = ML_GATE0 + d * 8 + hd
        cf = ci + ML_HEADS
        q.append(ml_refs[d][0, :, hd * ML_QK:(hd + 1) * ML_QK])
        k.append(ml_refs[d][0, :, k0 + hd * ML_QK:k0 + (hd + 1) * ML_QK] * (ML_QK ** -0.5))
        v.append(jnp.concatenate([ml_refs[d][0, :, v0 + hd * ML_V:v0 + (hd + 1) * ML_V], ones_col], axis=-1))
        i_col.append(g[d][:, ci:ci + 1])
        b_col.append(bcol[d][:, cf:cf + 1])
        b_end.append(tot[d][:, cf:cf + 1])
        log_d.append(jnp.where(past[d], b_col[-1] - b_t[d][cf:cf + 1, :] + g_t[d][ci:ci + 1, :], -jnp.inf))
    c_st = [c_scr[c] for c in nc]
    m_st = [m_scr[c] for c in nc]
    log_prev = [b_col[c] + m_st[c] for c in nc]
    m_t = [jnp.maximum(log_prev[c], jnp.max(log_d[c], axis=-1, keepdims=True)) for c in nc]
    qk = [_dot_nt(q[c], k[c]) for c in nc]
    qc = [_dot(q[c], c_st[c]) for c in nc]
    s = [qk[c] * jnp.exp(log_d[c] - m_t[c]) for c in nc]
    w_prev = [jnp.exp(log_prev[c] - m_t[c]) for c in nc]
    sv = [_dot(s[c], v[c]) for c in nc]
    log_s = [b_end[c] - b_col[c] + i_col[c] for c in nc]
    m_new = [jnp.maximum(b_end[c] + m_st[c], jnp.max(log_s[c], axis=0, keepdims=True)) for c in nc]
    kw = [k[c] * jnp.exp(log_s[c] - m_new[c]) for c in nc]
    w_c = [jnp.exp(b_end[c] + m_st[c] - m_new[c]) for c in nc]
    kv = [_dot_tn(kw[c], v[c]) for c in nc]
    numden = [sv[c] + w_prev[c] * qc[c] for c in nc]
    scale = [1.0 / jnp.maximum(jnp.abs(numden[c][:, ML_V:ML_V + 1]), jnp.exp(-m_t[c])) for c in nc]
    for c, (d, hd) in enumerate(chains):
        h_refs[d][0, :, hd * ML_V:(hd + 1) * ML_V] = numden[c][:, :ML_V] * scale[c]
        c_scr[c] = w_c[c] * c_st[c] + kv[c]
        m_scr[c] = m_new[c]

    @pl.when(i == pl.num_programs(1) - 1)
    def _():
        cn_ref[0] = c_scr[...]
        mn_ref[0] = m_scr[...]


def _mlstm(ml, gates, c0, m0):
    b, s, nml = ml.shape
    nc = s // CHUNK
    fwd = lambda n: pl.BlockSpec((1, CHUNK, n), lambda bi, i: (bi, i, 0))
    bwd = lambda n: pl.BlockSpec((1, CHUNK, n), lambda bi, i: (bi, nc - 1 - i, 0))
    st = lambda shp: pl.BlockSpec((1,) + shp, lambda bi, i: (bi,) + (0,) * len(shp))
    hdim = ML_HEADS * ML_V
    cshape = (N_CHAINS, ML_QK, 2 * ML_V)
    return pl.pallas_call(
        _mlstm_kernel,
        grid=(b, nc),
        in_specs=[fwd(nml), bwd(nml), fwd(GATE_LANES), bwd(GATE_LANES), st(cshape), st((N_CHAINS, 1, 1))],
        out_specs=[fwd(hdim), bwd(hdim), st(cshape), st((N_CHAINS, 1, 1))],
        out_shape=[jax.ShapeDtypeStruct((b, s, hdim), F32), jax.ShapeDtypeStruct((b, s, hdim), F32),
                   jax.ShapeDtypeStruct(c0.shape, F32), jax.ShapeDtypeStruct(m0.shape, F32)],
        scratch_shapes=[pltpu.VMEM(cshape, F32), pltpu.VMEM((N_CHAINS, 1, 1), F32)],
        compiler_params=_cparams("parallel", "arbitrary"),
        name="mlstm",
    )(ml, ml, gates, gates, c0, m0)


def _gdconv_kernel(has_halo, *refs):
    if has_halo:
        x_ref, prev_ref, next_ref, w_ref, o_ref, xp_ref = refs
    else:
        x_ref, w_ref, o_ref, xp_ref = refs
    rows = x_ref.shape[1]
    nch = x_ref.shape[2]
    pad = 8
    zero = jnp.zeros((pad, nch), F32)
    if has_halo:
        c = pl.program_id(1)
        xp_ref[0:pad, :] = jnp.where(c > 0, prev_ref[0], zero)
        xp_ref[pad + rows:, :] = jnp.where(c < pl.num_programs(1) - 1, next_ref[0], zero)
    else:
        xp_ref[0:pad, :] = zero
        xp_ref[pad + rows:, :] = zero
    xp_ref[pad:pad + rows, :] = x_ref[0]
    half = CONV_W // 2
    for lc in range(nch // 128):
        sl = slice(lc * 128, (lc + 1) * 128)
        acc = None
        for j in range(CONV_W):
            term = xp_ref[pad - half + j:pad - half + j + rows, sl] * w_ref[j:j + 1, sl]
            acc = term if acc is None else acc + term
        y = _silu(acc)
        if lc < 2 * GD_HEADS:
            y = y * lax.rsqrt(jnp.sum(y * y, axis=-1, keepdims=True) + EPS)
        if lc < GD_HEADS:
            y = y * (GD_QK ** -0.5)
        o_ref[0, :, sl] = y


def _gdconv_ctx(qkv, conv_w):
    b, s, nch = qkv.shape
    return pl.pallas_call(
        functools.partial(_gdconv_kernel, False),
        grid=(b,),
        in_specs=[pl.BlockSpec((1, s, nch), lambda bi: (bi, 0, 0)), pl.BlockSpec((CONV_W, nch), lambda bi: (0, 0))],
        out_specs=pl.BlockSpec((1, s, nch), lambda bi: (bi, 0, 0)),
        out_shape=jax.ShapeDtypeStruct((b, s, nch), F32),
        scratch_shapes=[pltpu.VMEM((s + 16, nch), F32)],
        compiler_params=_cparams("parallel"),
        name="gdconv_ctx",
    )(qkv, conv_w)


def _gdconv_lat(view, conv_w):
    b, rows, wn = view.shape
    nch = wn // GRID_W
    rb = rows // 8
    return pl.pallas_call(
        functools.partial(_gdconv_kernel, True),
        grid=(b, GRID_W),
        in_specs=[pl.BlockSpec((1, rows, nch), lambda bi, c: (bi, 0, c)),
                  pl.BlockSpec((1, 8, nch), lambda bi, c: (bi, rb - 1, jnp.maximum(c - 1, 0))),
                  pl.BlockSpec((1, 8, nch), lambda bi, c: (bi, 0, jnp.minimum(c + 1, GRID_W - 1))),
                  pl.BlockSpec((CONV_W, nch), lambda bi, c: (0, 0))],
        out_specs=pl.BlockSpec((1, rows, nch), lambda bi, c: (bi, 0, c)),
        out_shape=jax.ShapeDtypeStruct(view.shape, F32),
        scratch_shapes=[pltpu.VMEM((rows + 16, nch), F32)],
        compiler_params=_cparams("parallel", "arbitrary"),
        name="gdconv_lat",
    )(view, view, view, conv_w)


SOLVE_BLOCK = 16


def _hi_lo(x):
    hi = x.astype(BF16).astype(F32)
    return hi, x - hi


def _dot_split(a, b):
    a_hi, a_lo = _hi_lo(a)
    b_hi, b_lo = _hi_lo(b)
    lhs = jnp.concatenate([a_hi, a_hi, a_lo], axis=1).astype(BF16)
    rhs = jnp.concatenate([b_hi, b_lo, b_hi], axis=0).astype(BF16)
    return jnp.dot(lhs, rhs, preferred_element_type=F32)


def _unit_triangular_inverses(ns):
    c = ns[0].shape[0]
    row = lax.broadcasted_iota(jnp.int32, (c, c), 0)
    col = lax.broadcasted_iota(jnp.int32, (c, c), 1)
    eye = (row == col).astype(F32)
    in_diag_block = (row // SOLVE_BLOCK) == (col // SOLVE_BLOCK)
    mm = lambda a_list, b_list: [_dot_split(a, b) for a, b in zip(a_list, b_list)]

    n_d = [jnp.where(in_diag_block, n, 0.0) for n in ns]
    x = n_d
    d_inv = [eye - n for n in n_d]
    for _ in range(SOLVE_BLOCK.bit_length() - 2):
        x = mm(x, x)
        d_inv = [d + dx for d, dx in zip(d_inv, mm(d_inv, x))]
    m = mm(d_inv, [n - nd for n, nd in zip(ns, n_d)])
    assert c // SOLVE_BLOCK == 4
    i_minus_m = [eye - mi for mi in m]
    q = [a + b for a, b in zip(i_minus_m, mm(i_minus_m, mm(m, m)))]
    return mm(q, d_inv)


def _gdn_kernel(qf_ref, qb_ref, gf_ref, gb_ref, na_ref, s0_ref, of_ref, ob_ref, sn_ref, s_scr):
    i = pl.program_id(1)

    @pl.when(i == 0)
    def _():
        s_scr[...] = s0_ref[0]

    eye = (lax.broadcasted_iota(jnp.int32, (CHUNK, CHUNK), 0)
           == lax.broadcasted_iota(jnp.int32, (CHUNK, CHUNK), 1)).astype(F32)
    nqk = GD_HEADS * GD_QK
    masks = [_past_mask(d == 1) for d in range(2)]
    gates = [r[0] for r in (gf_ref, gb_ref)]
    glog = [na_ref[...] * jax.nn.softplus(g) for g in gates]
    beta_all = [jax.nn.sigmoid(g) for g in gates]
    gcum = [_dot_hi(masks[d][0].astype(F32), glog[d]) for d in range(2)]
    gtot = [jnp.sum(g, axis=0, keepdims=True) for g in glog]
    gcum_t = [_transpose_hi(g) for g in gcum]

    chains = [(d, hd) for d in range(2) for hd in range(GD_HEADS)]
    x_refs, o_refs = (qf_ref, qb_ref), (of_ref, ob_ref)
    q, k, v, g_col, beta, g_end, decay = [], [], [], [], [], [], []
    for d, hd in chains:
        ca = GD_GATE0 + d * 8 + hd
        q.append(x_refs[d][0, :, hd * GD_QK:(hd + 1) * GD_QK])
        k.append(x_refs[d][0, :, nqk + hd * GD_QK:nqk + (hd + 1) * GD_QK])
        v.append(x_refs[d][0, :, 2 * nqk + hd * GD_V:2 * nqk + (hd + 1) * GD_V])
        g_col.append(gcum[d][:, ca:ca + 1])
        beta.append(beta_all[d][:, ca + GD_HEADS:ca + GD_HEADS + 1])
        g_end.append(gtot[d][:, ca:ca + 1])
        decay.append(jnp.exp(jnp.where(masks[d][0], g_col[-1] - gcum_t[d][ca:ca + 1, :], -jnp.inf)))
    nc = range(len(chains))
    kk = [_dot_nt(k[c], k[c]) for c in nc]
    xs = [jnp.where(masks[chains[c][0]][1], beta[c] * kk[c] * decay[c], 0.0) for c in nc]
    ps = _unit_triangular_inverses(xs)
    uw = [_dot_split(ps[c], jnp.concatenate([v[c] * beta[c], k[c] * (beta[c] * jnp.exp(g_col[c]))], axis=-1))
          for c in nc]
    qk = [_dot_nt(q[c], k[c]) * decay[c] for c in nc]
    s_st = [s_scr[c] for c in nc]
    v_new = [uw[c][:, :GD_V] - _dot(uw[c][:, GD_V:], s_st[c]) for c in nc]
    o_loc = [_dot(q[c] * jnp.exp(g_col[c]), s_st[c]) for c in nc]
    o_new = [o_loc[c] + _dot(qk[c], v_new[c]) for c in nc]
    s_new = [s_st[c] * jnp.exp(g_end[c]) + _dot_tn(k[c] * jnp.exp(g_end[c] - g_col[c]), v_new[c]) for c in nc]
    for c, (d, hd) in enumerate(chains):
        o_refs[d][0, :, hd * GD_V:(hd + 1) * GD_V] = o_new[c]
        s_scr[c] = s_new[c]

    @pl.when(i == pl.num_programs(1) - 1)
    def _():
        sn_ref[0] = s_scr[...]


def _gdn(qkv_view, gates_view, neg_a, s0, nc, idx_fn, out_view_shape):
    b = qkv_view.shape[0]
    nqkv = 2 * GD_HEADS * GD_QK + GD_HEADS * GD_V
    hdim = GD_HEADS * GD_V
    fwd = lambda n: pl.BlockSpec((1, CHUNK, n), lambda bi, i: idx_fn(bi, i))
    bwd = lambda n: pl.BlockSpec((1, CHUNK, n), lambda bi, i: idx_fn(bi, nc - 1 - i))
    st = pl.BlockSpec((1, N_CHAINS, GD_QK, GD_V), lambda bi, i: (bi, 0, 0, 0))
    return pl.pallas_call(
        _gdn_kernel,
        grid=(b, nc),
        in_specs=[fwd(nqkv), bwd(nqkv), fwd(GATE_LANES), bwd(GATE_LANES),
                  pl.BlockSpec((1, GATE_LANES), lambda bi, i: (0, 0)), st],
        out_specs=[fwd(hdim), bwd(hdim), st],
        out_shape=[jax.ShapeDtypeStruct(out_view_shape, F32), jax.ShapeDtypeStruct(out_view_shape, F32),
                   jax.ShapeDtypeStruct(s0.shape, F32)],
        scratch_shapes=[pltpu.VMEM((N_CHAINS, GD_QK, GD_V), F32)],
        compiler_params=_cparams("parallel", "arbitrary"),
        name="gdn",
    )(qkv_view, qkv_view, gates_view, gates_view, neg_a, s0)


def _head_rms(t, nheads, width):
    outs = []
    for hd in range(nheads):
        th = t[:, hd * width:(hd + 1) * width]
        outs.append(th * lax.rsqrt(jnp.mean(th * th, axis=-1, keepdims=True) + EPS))
    return jnp.concatenate(outs, axis=-1)


def _rms(t, w):
    return t * lax.rsqrt(jnp.mean(t * t, axis=-1, keepdims=True) + EPS) * w


def _post_kernel(x_ref, hf_ref, hb_ref, og_ref, of_ref, ob_ref, z_ref, mlw_ref, gdw_ref, wout_ref,
                 npost_ref, g2_ref, npre_ref, sh_ref, sc_ref, rwt_ref, wsg_ref, wsu_ref, wsd_ref,
                 x1_ref, hffn_ref, lt_ref, ys_ref, of_scr, ob_scr):
    ml_y = _head_rms(hf_ref[0] + hb_ref[0], ML_HEADS, ML_V) * mlw_ref[...] * jax.nn.sigmoid(og_ref[0])
    _from_grid_view(of_ref, of_scr)
    _from_grid_view(ob_ref, ob_scr)
    o_sum = jnp.concatenate([_pitched_rows(of_scr, g) + _pitched_rows(ob_scr, g) for g in range(of_scr.shape[0])],
                            axis=-1)
    gd_y = _head_rms(o_sum, GD_HEADS, GD_V) * gdw_ref[...] * _silu(z_ref[0])
    y = _dot(jnp.concatenate([ml_y, gd_y], axis=-1), wout_ref[...])
    x1 = x_ref[0] + g2_ref[0] * _rms(y, npost_ref[...])
    x1_ref[0] = x1
    hffn = _rms(x1, npre_ref[...]) * (1.0 + sc_ref[0]) + sh_ref[0]
    nct = hffn.shape[1] // LANES
    for c in range(nct):
        hffn_ref[0, pl.ds(c, hffn.shape[0], stride=nct), :] = hffn[:, c * LANES:(c + 1) * LANES]
    hb = hffn.astype(BF16)
    lt_ref[...] = lax.dot_general(rwt_ref[...], hb, (((1,), (1,)), ((), ())), preferred_element_type=F32)
    hs = _silu(jnp.dot(hb, wsg_ref[...], preferred_element_type=F32)) * jnp.dot(hb, wsu_ref[...],
                                                                                preferred_element_type=F32)
    ys_ref[0] = _dot(hs, wsd_ref[...])


def _post(x, hf, hb, ml, of, ob, gz, mlw, gdw, wout, npost, g2, npre, sh, sc, rwt, wsg, wsu, wsd, tm):
    b, s, d = x.shape
    nt = s // tm
    hw = ML_HEADS * ML_V
    og_blk = (2 * ML_HEADS * ML_QK + ML_HEADS * ML_V) // hw
    tok = lambda n: pl.BlockSpec((1, tm, n), lambda bi, i: (bi, i, 0))
    full = lambda shp: pl.BlockSpec(shp, lambda bi, i: (0,) * len(shp))
    mod = pl.BlockSpec((1, 1, d), lambda bi, i: (bi, 0, 0))
    ne = rwt.shape[0]
    ds = wsg.shape[1]
    gview = pl.BlockSpec((1, tm // GRID_W, GRID_W * hw), lambda bi, i: (bi, i, 0))
    return pl.pallas_call(
        _post_kernel,
        grid=(b, nt),
        in_specs=[tok(d), tok(hw), tok(hw), pl.BlockSpec((1, tm, hw), lambda bi, i: (bi, i, og_blk)),
                  gview, gview, tok(hw), full((1, hw)), full((1, hw)), full((d, d)),
                  full((1, d)), mod, full((1, d)), mod, mod, full((ne, d)), full((d, ds)), full((d, ds)),
                  full((ds, d))],
        out_specs=[tok(d), pl.BlockSpec((1, tm * (d // LANES), LANES), lambda bi, i: (bi, i, 0)),
                   pl.BlockSpec((ne, tm), lambda bi, i: (0, bi * nt + i)), tok(d)],
        out_shape=[jax.ShapeDtypeStruct((b, s, d), F32), jax.ShapeDtypeStruct((b, s * (d // LANES), LANES), F32),
                   jax.ShapeDtypeStruct((ne, b * s), F32), jax.ShapeDtypeStruct((b, s, d), F32)],
        scratch_shapes=[pltpu.VMEM((hw // LANES, tm // GRID_W * GRID_PITCH, LANES), F32)] * 2,
        compiler_params=_cparams("parallel", "arbitrary"),
        name="post",
    )(x, hf, hb, ml, of, ob, gz, mlw, gdw, wout, npost, g2, npre, sh, sc, rwt, wsg, wsu, wsd)


def _route_kernel(lt_ref, bias_ref, idx_ref, gate_ref):
    ne, tn = lt_ref.shape
    gsz = ne // N_GROUPS
    scores = jax.nn.sigmoid(lt_ref[...])
    sel = scores + bias_ref[...]
    neg = -jnp.inf
    sel3 = sel.reshape(N_GROUPS, gsz, tn)
    io3 = lax.broadcasted_iota(jnp.int32, sel3.shape, 1)
    top1 = jnp.max(sel3, axis=1, keepdims=True)
    first = jnp.min(jnp.where(sel3 == top1, io3, gsz), axis=1, keepdims=True)
    top2 = jnp.max(jnp.where(io3 == first, neg, sel3), axis=1, keepdims=True)
    grp = (top1 + top2).reshape(N_GROUPS, tn)
    iog = lax.broadcasted_iota(jnp.int32, grp.shape, 0)
    keep = jnp.zeros(grp.shape, jnp.bool_)
    for _ in range(TOPK_GROUPS):
        m = jnp.max(grp, axis=0, keepdims=True)
        pick = iog == jnp.min(jnp.where(grp == m, iog, N_GROUPS), axis=0, keepdims=True)
        keep = keep | pick
        grp = jnp.where(pick, neg, grp)
    cand = jnp.where(keep.reshape(N_GROUPS, 1, tn), sel3, neg).reshape(ne, tn)
    ioe = lax.broadcasted_iota(jnp.int32, cand.shape, 0)
    idxs, gates = [], []
    for _ in range(TOP_K):
        m = jnp.max(cand, axis=0, keepdims=True)
        e = jnp.min(jnp.where(cand == m, ioe, ne), axis=0, keepdims=True)
        pick = ioe == e
        idxs.append(e)
        gates.append(jnp.sum(jnp.where(pick, scores, 0.0), axis=0, keepdims=True))
        cand = jnp.where(pick, neg, cand)
    gate = jnp.concatenate(gates, axis=0)
    idx_ref[...] = jnp.concatenate(idxs, axis=0)
    gate_ref[...] = gate / jnp.sum(gate, axis=0, keepdims=True) * ROUTED_SCALE


def _route(logits_t, bias_col, tn):
    ne, t = logits_t.shape
    return pl.pallas_call(
        _route_kernel,
        grid=(t // tn,),
        in_specs=[pl.BlockSpec((ne, tn), lambda i: (0, i)), pl.BlockSpec((ne, 1), lambda i: (0, 0))],
        out_specs=[pl.BlockSpec((TOP_K, tn), lambda i: (0, i)), pl.BlockSpec((TOP_K, tn), lambda i: (0, i))],
        out_shape=[jax.ShapeDtypeStruct((TOP_K, t), jnp.int32), jax.ShapeDtypeStruct((TOP_K, t), F32)],
        compiler_params=_cparams("parallel"),
        name="route",
    )(logits_t, bias_col)


def _experts_kernel(be_ref, np_ref,
                    tok_ref, tokn_ref, w_ref, wg_ref, wu_ref, wd_ref, h_hbm,
                    o_ref, xg, wgc, wuc, wdc, gsem):
    s = pl.program_id(0)
    n_pairs = np_ref[0]
    nct = xg.shape[1] // EXPERT_BLOCK
    rows = nct * EXPERT_BLOCK

    def gather_copy(tref, p, j):
        src = h_hbm.at[pl.ds(pl.multiple_of(tref[0, p, j] * nct, nct), nct)]
        return pltpu.make_async_copy(src, xg.at[p, pl.ds(j * nct, nct)], gsem.at[p])

    def gather_wait(p):
        pltpu.make_async_copy(h_hbm.at[pl.ds(0, rows)], xg.at[p], gsem.at[p]).wait()

    @pl.when(s >= n_pairs)
    def _():
        o_ref[...] = jnp.zeros(o_ref.shape, F32)

    @pl.when(s < n_pairs)
    def _():
        @pl.when(s == 0)
        def _():
            for p in range(2):
                for j in range(EXPERT_BLOCK):
                    gather_copy(tok_ref, p, j).start(priority=j % 2)

        @pl.when((s == 0) | (be_ref[s] != be_ref[jnp.maximum(s - 1, 0)]))
        def _():
            wgc[...] = wg_ref[0].astype(BF16)
            wuc[...] = wu_ref[0].astype(BF16)
            wdc[...] = wd_ref[0].astype(BF16)

        for p in range(2):
            gather_wait(p)
            xb = jnp.concatenate([xg[p, pl.ds(c, EXPERT_BLOCK, stride=nct), :] for c in range(nct)],
                                 axis=-1).astype(BF16)
            for j in range(EXPERT_BLOCK):
                gather_copy(tokn_ref, p, j).start(priority=j % 2)
            hmid = _silu(jnp.dot(xb, wgc[...], preferred_element_type=F32)) * jnp.dot(xb, wuc[...],
                                                                                      preferred_element_type=F32)
            out = _dot(hmid, wdc[...])
            eye = (lax.broadcasted_iota(jnp.int32, (EXPERT_BLOCK, EXPERT_BLOCK), 0)
                   == lax.broadcasted_iota(jnp.int32, (EXPERT_BLOCK, EXPERT_BLOCK), 1))
            w_col = jnp.sum(jnp.where(eye, w_ref[0, p:p + 1, :], 0.0), axis=1, keepdims=True)
            out = out * w_col
            for c in range(nct):
                o_ref[pl.ds(p * rows + c, EXPERT_BLOCK, stride=nct), :] = out[:, c * LANES:(c + 1) * LANES]

        @pl.when(s == n_pairs - 1)
        def _():
            for p in range(2):
                gather_wait(p)


def _experts(hffn, block_e, n_pairs, row_tok, row_w, wg, wu, wd):
    d = wg.shape[1]
    nct = d // LANES
    npairs = row_tok.shape[0]
    de = wg.shape[2]
    last = npairs - 1
    smem_blk = lambda f: pl.BlockSpec((1, 2, EXPERT_BLOCK), f, memory_space=pltpu.SMEM)
    wspec = lambda shp: pl.BlockSpec((1,) + shp, lambda s, be, npu: (be[s], 0, 0))
    pair_rows = 2 * EXPERT_BLOCK * nct
    grid_spec = pltpu.PrefetchScalarGridSpec(
        num_scalar_prefetch=2,
        grid=(npairs,),
        in_specs=[smem_blk(lambda s, be, npu: (s, 0, 0)),
                  smem_blk(lambda s, be, npu: (jnp.minimum(s + 1, last), 0, 0)),
                  pl.BlockSpec((1, 2, EXPERT_BLOCK), lambda s, be, npu: (s, 0, 0)),
                  wspec((d, de)), wspec((d, de)), wspec((de, d)),
                  pl.BlockSpec(memory_space=pl.ANY)],
        out_specs=pl.BlockSpec((pair_rows, LANES), lambda s, be, npu: (s, 0)),
        scratch_shapes=[pltpu.VMEM((2, EXPERT_BLOCK * nct, LANES), F32),
                        pltpu.VMEM((d, de), BF16), pltpu.VMEM((d, de), BF16), pltpu.VMEM((de, d), BF16),
                        pltpu.SemaphoreType.DMA((2,))],
    )
    return pl.pallas_call(
        _experts_kernel,
        grid_spec=grid_spec,
        out_shape=jax.ShapeDtypeStruct((npairs * pair_rows, LANES), F32),
        compiler_params=_cparams("arbitrary"),
        name="experts",
    )(block_e, n_pairs, row_tok, row_tok, row_w, wg, wu, wd, hffn)


def _combine_kernel(pos_ref, posn_ref, x1_ref, ys_ref, npost_ref, g5_ref, rows_hbm, o_ref, buf, sem):
    i = pl.program_id(0)
    tm = x1_ref.shape[1]
    nct = x1_ref.shape[2] // LANES
    cur = i % 2
    nxt = 1 - cur

    def copy(pref, b, k, t):
        src = rows_hbm.at[pl.ds(pl.multiple_of(pref[k, t] * nct, nct), nct)]
        dst = buf.at[b, pl.ds(pl.multiple_of((k * tm + t) * nct, nct), nct)]
        return pltpu.make_async_copy(src, dst, sem.at[b])

    def drain(b):
        pltpu.make_async_copy(rows_hbm.at[pl.ds(0, TOP_K * tm * nct)], buf.at[b], sem.at[b]).wait()

    def issue(pref, b):
        def body(t2, carry):
            for u in range(2):
                for k in range(TOP_K):
                    copy(pref, b, k, 2 * t2 + u).start(priority=k % 2)
            return carry
        lax.fori_loop(0, tm // 2, body, 0)

    @pl.when(i == 0)
    def _():
        issue(pos_ref, cur)

    @pl.when(i + 1 < pl.num_programs(0))
    def _():
        for t in range(tm):
            for k in range(TOP_K):
                copy(posn_ref, nxt, k, t).start(priority=k % 2)

    drain(cur)
    routed = []
    for c in range(nct):
        acc = buf[cur, pl.ds(c, tm, stride=nct), :]
        for k in range(1, TOP_K):
            acc = acc + buf[cur, pl.ds(k * tm * nct + c, tm, stride=nct), :]
        routed.append(acc)
    y = ys_ref[0] + jnp.concatenate(routed, axis=-1)
    o_ref[0] = x1_ref[0] + g5_ref[0] * _rms(y, npost_ref[...])


def _combine(x1, ys, rows, pos, npost, g5, tm):
    b, s, d = x1.shape
    nt = s // tm
    nct = d // LANES
    last = b * nt - 1
    tok = pl.BlockSpec((1, tm, d), lambda i: (i // nt, i % nt, 0))
    pos_blk = lambda f: pl.BlockSpec((TOP_K, tm), f, memory_space=pltpu.SMEM)
    return pl.pallas_call(
        _combine_kernel,
        grid=(b * nt,),
        in_specs=[pos_blk(lambda i: (0, i)), pos_blk(lambda i: (0, jnp.minimum(i + 1, last))), tok, tok,
                  pl.BlockSpec((1, d), lambda i: (0, 0)), pl.BlockSpec((1, 1, d), lambda i: (i // nt, 0, 0)),
                  pl.BlockSpec(memory_space=pl.ANY)],
        out_specs=tok,
        out_shape=jax.ShapeDtypeStruct((b, s, d), F32),
        scratch_shapes=[pltpu.VMEM((2, TOP_K * tm * nct, LANES), F32), pltpu.SemaphoreType.DMA((2,))],
        compiler_params=_cparams("arbitrary"),
        name="combine",
    )(pos, pos, x1, ys, npost, g5, rows)


def _dispatch_plan(idx_t, gate_t):
    k, t = idx_t.shape
    n_asg = k * t
    pair = 2 * EXPERT_BLOCK
    nb = 2 * (n_asg // pair + N_EXPERTS)
    flat_e = idx_t.reshape(-1)
    id_bits = max(1, (n_asg - 1).bit_length())
    assert (N_EXPERTS - 1).bit_length() + id_bits <= 31
    packed = jnp.sort((flat_e << id_bits) | jnp.arange(n_asg, dtype=jnp.int32))
    order = packed & ((1 << id_bits) - 1)
    counts = jnp.zeros((N_EXPERTS,), jnp.int32).at[flat_e].add(1)
    padded = (counts + pair - 1) // pair * pair
    start = jnp.cumsum(counts) - counts
    pend = jnp.cumsum(padded)
    pstart = pend - padded
    blk0 = jnp.arange(nb, dtype=jnp.int32) * EXPERT_BLOCK
    block_e = jnp.minimum(jnp.sum((pend[None, :] <= blk0[:, None]).astype(jnp.int32), axis=1), N_EXPERTS - 1)
    of_block = block_e[:, None] == jnp.arange(N_EXPERTS, dtype=jnp.int32)[None, :]
    per_block = lambda v: jnp.sum(jnp.where(of_block, v[None, :], 0), axis=1)
    n_pairs = (pend[-1] // pair).astype(jnp.int32).reshape(1)
    pos = blk0[:, None] - per_block(pstart)[:, None] + jnp.arange(EXPERT_BLOCK, dtype=jnp.int32)[None, :]
    valid = pos < per_block(counts)[:, None]
    src = jnp.clip(per_block(start)[:, None] + pos, 0, n_asg - 1)
    asg = order[src]
    row_tok = jnp.where(valid, asg % t, 0).astype(jnp.int32)
    row_w = jnp.where(valid, gate_t.reshape(-1)[asg], 0.0).astype(F32)
    i_sorted = jnp.arange(n_asg, dtype=jnp.int32)
    pad_before = jnp.sum(jnp.where(i_sorted[:, None] >= (start + counts)[None, :], (padded - counts)[None, :], 0),
                         axis=1)
    _, pos = lax.sort((order, i_sorted + pad_before), num_keys=1)
    pos = pos.reshape(k, t)
    shp = (nb // 2, 2, EXPERT_BLOCK)
    return block_e[::2], n_pairs, row_tok.reshape(shp), row_w.reshape(shp), pos


def _pack_in_weights(w_in, ml_i_bias, ml_f_bias, gd_dt_bias):
    d = w_in.shape[0]
    nml = 2 * ML_HEADS * ML_QK + 2 * ML_HEADS * ML_V
    ml_cols = nml + 4 * ML_HEADS
    ngq = GD_HEADS * (2 * GD_QK + GD_V)
    ngz = GD_HEADS * GD_V
    wml = w_in[:, :nml].astype(BF16)
    wgq = w_in[:, ml_cols:ml_cols + ngq].astype(BF16)
    wgz = w_in[:, ml_cols + ngq:ml_cols + ngq + ngz].astype(BF16)
    wg = jnp.zeros((d, GATE_LANES), F32)
    wg = wg.at[:, ML_GATE0:ML_GATE0 + 16].set(w_in[:, nml:ml_cols])
    wg = wg.at[:, GD_GATE0:GD_GATE0 + 16].set(w_in[:, ml_cols + ngq + ngz:])
    gb = jnp.zeros((GATE_LANES,), F32)
    gb = gb.at[ML_GATE0:ML_GATE0 + 16].set(jnp.stack([ml_i_bias, ml_f_bias], axis=1).reshape(-1))
    gb = gb.at[GD_GATE0:GD_GATE0 + 16].set(jnp.stack([gd_dt_bias, jnp.zeros_like(gd_dt_bias)], axis=1).reshape(-1))
    return wml, wgq, wgz, wg.astype(BF16), gb.reshape(1, GATE_LANES)


def _mixer(x, ctx, mod, mod_ctx, norm_pre_mix, w_in, ml_i_bias, ml_f_bias, gd_conv_w, gd_a_log, gd_dt_bias):
    b, s, d = x.shape
    sc = ctx.shape[1]
    wml, wgq, wgz, wg, gb = _pack_in_weights(w_in, ml_i_bias, ml_f_bias, gd_dt_bias)
    nw = norm_pre_mix.reshape(1, d)
    ctx_mod = lambda j: jnp.broadcast_to(mod_ctx[j].reshape(1, 1, d), (b, 1, d))
    ml_c, gq_c, _, g_c = _proj(ctx, nw, ctx_mod(0), ctx_mod(1), wml, wgq, wgz, wg, gb, tm=sc, grid_view=False)
    ml_l, gqv_l, gz_l, g_l, gv_l = _proj(x, nw, mod[0], mod[1], wml, wgq, wgz, wg, gb, tm=512, grid_view=True)

    c0 = jnp.zeros((b, N_CHAINS, ML_QK, 2 * ML_V), F32)
    m0 = jnp.zeros((b, N_CHAINS, 1, 1), F32)
    _, _, c1, m1 = _mlstm(ml_c, g_c, c0, m0)
    hf, hb, _, _ = _mlstm(ml_l, g_l, c1, m1)

    neg_a = jnp.zeros((GATE_LANES,), F32)
    neg_a = neg_a.at[GD_GATE0:GD_GATE0 + 16].set(
        jnp.stack([-jnp.exp(gd_a_log), jnp.zeros_like(gd_a_log)], axis=1).reshape(-1)).reshape(1, GATE_LANES)
    qn_c = _gdconv_ctx(gq_c, gd_conv_w)
    qnv_l = _gdconv_lat(gqv_l, gd_conv_w)
    s0 = jnp.zeros((b, N_CHAINS, GD_QK, GD_V), F32)
    hdim = GD_HEADS * GD_V
    _, _, s1 = _gdn(qn_c, g_c, neg_a, s0, sc // CHUNK, lambda bi, n: (bi, n, 0), (b, sc, hdim))
    rows = s // GRID_W
    cpc = rows // CHUNK
    col_idx = lambda bi, n: (bi, n % cpc, n // cpc)
    ofv, obv, _ = _gdn(qnv_l, gv_l, neg_a, s1, s // CHUNK, col_idx, (b, rows, GRID_W * hdim))
    return hf, hb, ml_l, ofv, obv, gz_l


def kernel(x, c, ctx, c_ctx, w_ada, b_ada, norm_pre_mix, norm_post_mix, norm_pre_ffn, norm_post_ffn, w_in,
           ml_i_bias, ml_f_bias, ml_norm_w, gd_conv_w, gd_a_log, gd_dt_bias, gd_norm_w, w_out, router_w,
           router_bias, w_gate, w_up, w_down, ws_gate, ws_up, ws_down):
    b, s, d = x.shape
    depth = w_ada.shape[0]
    assert depth == 1, "the context stream update of deeper stacks is not implemented"
    ly = 0
    cc = jnp.zeros((16, d), F32).at[:b].set(c).at[b].set(c_ctx)
    mod_all = _ada(cc, w_ada[ly], b_ada[ly])
    mod = [mod_all[:b, j * d:(j + 1) * d].reshape(b, 1, d) for j in range(6)]
    mod_ctx = [mod_all[b, j * d:(j + 1) * d] for j in range(6)]

    hf, hb, ml_l, of, ob, gz_l = _mixer(x, ctx, mod, mod_ctx, norm_pre_mix[ly], w_in[ly], ml_i_bias[ly],
                                        ml_f_bias[ly], gd_conv_w[ly], gd_a_log[ly], gd_dt_bias[ly])

    row = lambda v: v.reshape(1, -1)
    x1, hffn, logits_t, ys = _post(
        x, hf, hb, ml_l, of, ob, gz_l, row(ml_norm_w[ly]), row(jnp.tile(gd_norm_w[ly], GD_HEADS)),
        w_out[ly].astype(BF16), row(norm_post_mix[ly]), mod[2], row(norm_pre_ffn[ly]), mod[3], mod[4],
        router_w[ly].T.astype(BF16), ws_gate[ly].astype(BF16), ws_up[ly].astype(BF16), ws_down[ly].astype(BF16),
        tm=512)

    idx_t, gate_t = _route(logits_t, router_bias[ly].reshape(-1, 1), tn=512)
    block_e, n_pairs, row_tok, row_w, pos = _dispatch_plan(idx_t, gate_t)
    t = b * s
    rows = _experts(hffn.reshape(t * (d // LANES), LANES), block_e, n_pairs, row_tok, row_w,
                    w_gate[ly], w_up[ly], w_down[ly])
    return _combine(x1, ys, rows, pos, row(norm_post_ffn[ly]), mod[5], tm=256)
```

```python
import functools

import jax
import jax.numpy as jnp
from jax import lax
from jax.experimental import pallas as pl
from jax.experimental.pallas import tpu as pltpu

EPS = 1e-6
CHUNK = 64
GRID_W = 64
ML_HEADS, ML_QK, ML_V = 4, 64, 128
GD_HEADS, GD_QK, GD_V = 4, 128, 128
CONV_W = 5
N_EXPERTS, TOP_K, N_GROUPS, TOPK_GROUPS = 256, 8, 8, 4
ROUTED_SCALE = 2.5
EXPERT_BLOCK = 128
N_CHAINS = 8
ML_SCAN_BATCH, GD_SCAN_BATCH = 1, 2
LANES = 128
GATE_LANES = LANES
ML_GATE0, GD_GATE0 = 0, 16

F32 = jnp.float32
BF16 = jnp.bfloat16
HI = lax.Precision.HIGHEST
VMEM_LIMIT = 56 * 1024 * 1024


def _cparams(*sem):
    return pltpu.CompilerParams(dimension_semantics=sem, vmem_limit_bytes=VMEM_LIMIT)


def _dot(a, b):
    return jnp.dot(a.astype(BF16), b.astype(BF16), preferred_element_type=F32)


def _dot_nt(a, b):
    return lax.dot_general(a.astype(BF16), b.astype(BF16), (((1,), (1,)), ((), ())), preferred_element_type=F32)


def _dot_tn(a, b):
    return lax.dot_general(a.astype(BF16), b.astype(BF16), (((0,), (0,)), ((), ())), preferred_element_type=F32)


def _dot_hi(a, b):
    return jnp.dot(a, b, precision=HI, preferred_element_type=F32)


def _dot_nt_hi(a, b):
    return lax.dot_general(a, b, (((1,), (1,)), ((), ())), precision=HI, preferred_element_type=F32)


def _transpose_hi(x):
    n = x.shape[1]
    eye = (lax.broadcasted_iota(jnp.int32, (n, n), 0) == lax.broadcasted_iota(jnp.int32, (n, n), 1)).astype(F32)
    return _dot_nt_hi(eye, x)


def _silu(x):
    return x * jax.nn.sigmoid(x)


def _past_mask(reverse):
    t = lax.broadcasted_iota(jnp.int32, (CHUNK, CHUNK), 0)
    s = lax.broadcasted_iota(jnp.int32, (CHUNK, CHUNK), 1)
    return (s >= t, s > t) if reverse else (s <= t, s < t)


def _ada_kernel(c_ref, w_ref, b_ref, o_ref):
    o_ref[...] = _dot(_silu(c_ref[...]), w_ref[...]) + b_ref[...]


def _ada(cc, w_ada, b_ada):
    rows, d = cc.shape
    n = w_ada.shape[1]
    tn = 1536
    return pl.pallas_call(
        _ada_kernel,
        grid=(n // tn,),
        in_specs=[pl.BlockSpec((rows, d), lambda j: (0, 0)),
                  pl.BlockSpec((d, tn), lambda j: (0, j)),
                  pl.BlockSpec((1, tn), lambda j: (0, j))],
        out_specs=pl.BlockSpec((rows, tn), lambda j: (0, j)),
        out_shape=jax.ShapeDtypeStruct((rows, n), F32),
        compiler_params=_cparams("arbitrary"),
        name="ada",
    )(cc, w_ada, b_ada.reshape(1, n))


GRID_PITCH = GRID_W + 8


def _to_grid_view(src_ref, dst_ref):
    ng = src_ref.shape[0]
    r = src_ref.shape[1] // GRID_PITCH
    for c in range(GRID_W):
        for g in range(ng):
            lo = (c * ng + g) * LANES
            dst_ref[0, :, lo:lo + LANES] = src_ref[g, pl.ds(c, r, stride=GRID_PITCH), :]


def _from_grid_view(src_ref, dst_ref):
    ng = dst_ref.shape[0]
    r = dst_ref.shape[1] // GRID_PITCH
    for c in range(GRID_W):
        for g in range(ng):
            lo = (c * ng + g) * LANES
            dst_ref[g, pl.ds(c, r, stride=GRID_PITCH), :] = src_ref[0, :, lo:lo + LANES]


def _pitched_rows(ref, g):
    r = ref.shape[1] // GRID_PITCH
    return jnp.concatenate([ref[g, i * GRID_PITCH:i * GRID_PITCH + GRID_W, :] for i in range(r)], axis=0)


def _proj_kernel(grid_view, x_ref, nw_ref, sh_ref, sc_ref, wml_ref, wgq_ref, wgz_ref, wg_ref, gb_ref, *refs):
    if grid_view:
        ml_ref, gqv_ref, gz_ref, g_ref, gv_ref, gq_scr, g_scr = refs
    else:
        ml_ref, gq_ref, gz_ref, g_ref = refs
    x = x_ref[0]
    xn = x * lax.rsqrt(jnp.mean(x * x, axis=-1, keepdims=True) + EPS) * nw_ref[...]
    h = (xn * (1.0 + sc_ref[0]) + sh_ref[0]).astype(BF16)
    ml_ref[0] = jnp.dot(h, wml_ref[...], preferred_element_type=F32)
    gz_ref[0] = jnp.dot(h, wgz_ref[...], preferred_element_type=F32)
    gates = jnp.dot(h, wg_ref[...], preferred_element_type=F32) + gb_ref[...]
    g_ref[0] = gates
    gq = jnp.dot(h, wgq_ref[...], preferred_element_type=F32)
    if grid_view:
        for r in range(x.shape[0] // GRID_W):
            rows = slice(r * GRID_W, (r + 1) * GRID_W)
            prow = slice(r * GRID_PITCH, r * GRID_PITCH + GRID_W)
            g_scr[0, prow, :] = gates[rows]
            for g in range(gq_scr.shape[0]):
                gq_scr[g, prow, :] = gq[rows, g * LANES:(g + 1) * LANES]
        _to_grid_view(gq_scr, gqv_ref)
        _to_grid_view(g_scr, gv_ref)
    else:
        gq_ref[0] = gq


def _proj(x, norm_w, shift, scale, wml, wgq, wgz, wg, gbias, tm, grid_view):
    b, s, d = x.shape
    nml, ngq, ngz = wml.shape[1], wgq.shape[1], wgz.shape[1]
    full = lambda shp: pl.BlockSpec(shp, lambda bi, i: (0,) * len(shp))
    tok = lambda n: pl.BlockSpec((1, tm, n), lambda bi, i: (bi, i, 0))
    mod = pl.BlockSpec((1, 1, d), lambda bi, i: (bi, 0, 0))
    if grid_view:
        rt = tm // GRID_W
        view = lambda n: pl.BlockSpec((1, rt, GRID_W * n), lambda bi, i: (bi, i, 0))
        vshape = lambda n: jax.ShapeDtypeStruct((b, s // GRID_W, GRID_W * n), F32)
        out_specs = [tok(nml), view(ngq), tok(ngz), tok(GATE_LANES), view(GATE_LANES)]
        out_shape = [jax.ShapeDtypeStruct((b, s, nml), F32), vshape(ngq), jax.ShapeDtypeStruct((b, s, ngz), F32),
                     jax.ShapeDtypeStruct((b, s, GATE_LANES), F32), vshape(GATE_LANES)]
        scratch = [pltpu.VMEM((ngq // LANES, rt * GRID_PITCH, LANES), F32), pltpu.VMEM((1, rt * GRID_PITCH, LANES), F32)]
    else:
        out_specs = [tok(nml), tok(ngq), tok(ngz), tok(GATE_LANES)]
        out_shape = [jax.ShapeDtypeStruct((b, s, n), F32) for n in (nml, ngq, ngz, GATE_LANES)]
        scratch = []
    return pl.pallas_call(
        functools.partial(_proj_kernel, grid_view),
        grid=(b, s // tm),
        in_specs=[tok(d), full((1, d)), mod, mod, full((d, nml)), full((d, ngq)), full((d, ngz)),
                  full((d, GATE_LANES)), full((1, GATE_LANES))],
        out_specs=out_specs,
        out_shape=out_shape,
        scratch_shapes=scratch,
        compiler_params=_cparams("parallel", "arbitrary"),
        name="proj",
    )(x, norm_w, shift, scale, wml, wgq, wgz, wg, gbias)


def _mlstm_kernel(mlf_ref, mlb_ref, gf_ref, gb_ref, c0_ref, m0_ref, hf_ref, hb_ref, cn_ref, mn_ref, c_scr, m_scr):
    i = pl.program_id(1)

    @pl.when(i == 0)
    def _():
        c_scr[...] = c0_ref[...]
        m_scr[...] = m0_ref[...]

    nbat = mlf_ref.shape[0]
    past = [_past_mask(d == 1)[0] for d in range(2)]
    ml_refs, g_refs, h_refs = (mlf_ref, mlb_ref), (gf_ref, gb_ref), (hf_ref, hb_ref)
    bd = [(bb, d) for bb in range(nbat) for d in range(2)]
    g = [g_refs[d][bb] for bb, d in bd]
    ls = [jax.nn.log_sigmoid(x) for x in g]
    bcol = [_dot_hi(past[d].astype(F32), ls[j]) for j, (bb, d) in enumerate(bd)]
    tot = [jnp.sum(x, axis=0, keepdims=True) for x in ls]
    g_t = [_transpose_hi(x) for x in g]
    b_t = [_transpose_hi(x) for x in bcol]

    chains = [(bb, d, hd) for bb in range(nbat) for d in range(2) for hd in range(ML_HEADS)]
    nc = range(len(chains))
    k0, v0 = ML_HEADS * ML_QK, 2 * ML_HEADS * ML_QK
    ones_col = (lax.broadcasted_iota(jnp.int32, (CHUNK, ML_V), 1) == 0).astype(F32)
    q, k, v, i_col, b_col, b_end, log_d = [], [], [], [], [], [], []
    for bb, d, hd in chains:
        ci = ML_GATE0 + d * 8 + hd
        cf = ci + ML_HEADS
        j = bb * 2 + d
        q.append(ml_refs[d][bb, :, hd * ML_QK:(hd + 1) * ML_QK])
        k.append(ml_refs[d][bb, :, k0 + hd * ML_QK:k0 + (hd + 1) * ML_QK] * (ML_QK ** -0.5))
        v.append(jnp.concatenate([ml_refs[d][bb, :, v0 + hd * ML_V:v0 + (hd + 1) * ML_V], ones_col], axis=-1))
        i_col.append(g[j][:, ci:ci + 1])
        b_col.append(bcol[j][:, cf:cf + 1])
        b_end.append(tot[j][:, cf:cf + 1])
        log_d.append(jnp.where(past[d], b_col[-1] - b_t[j][cf:cf + 1, :] + g_t[j][ci:ci + 1, :], -jnp.inf))
    st_idx = [(bb, d * ML_HEADS + hd) for bb, d, hd in chains]
    c_st = [c_scr[ix] for ix in st_idx]
    m_st = [m_scr[ix] for ix in st_idx]
    log_prev = [b_col[c] + m_st[c] for c in nc]
    m_t = [jnp.maximum(log_prev[c], jnp.max(log_d[c], axis=-1, keepdims=True)) for c in nc]
    qk = [_dot_nt(q[c], k[c]) for c in nc]
    qc = [_dot(q[c], c_st[c]) for c in nc]
    s = [qk[c] * jnp.exp(log_d[c] - m_t[c]) for c in nc]
    w_prev = [jnp.exp(log_prev[c] - m_t[c]) for c in nc]
    sv = [_dot(s[c], v[c]) for c in nc]
    log_s = [b_end[c] - b_col[c] + i_col[c] for c in nc]
    m_new = [jnp.maximum(b_end[c] + m_st[c], jnp.max(log_s[c], axis=0, keepdims=True)) for c in nc]
    kw = [k[c] * jnp.exp(log_s[c] - m_new[c]) for c in nc]
    w_c = [jnp.exp(b_end[c] + m_st[c] - m_new[c]) for c in nc]
    kv = [_dot_tn(kw[c], v[c]) for c in nc]
    numden = [sv[c] + w_prev[c] * qc[c] for c in nc]
    scale = [1.0 / jnp.maximum(jnp.abs(numden[c][:, ML_V:ML_V + 1]), jnp.exp(-m_t[c])) for c in nc]
    for c, (bb, d, hd) in enumerate(chains):
        h_refs[d][bb, :, hd * ML_V:(hd + 1) * ML_V] = numden[c][:, :ML_V] * scale[c]
        c_scr[st_idx[c]] = w_c[c] * c_st[c] + kv[c]
        m_scr[st_idx[c]] = m_new[c]

    @pl.when(i == pl.num_programs(1) - 1)
    def _():
        cn_ref[...] = c_scr[...]
        mn_ref[...] = m_scr[...]


def _mlstm(ml, gates, c0, m0):
    b, s, nml = ml.shape
    nc = s // CHUNK
    nbat = ML_SCAN_BATCH
    fwd = lambda n: pl.BlockSpec((nbat, CHUNK, n), lambda bi, i: (bi, i, 0))
    bwd = lambda n: pl.BlockSpec((nbat, CHUNK, n), lambda bi, i: (bi, nc - 1 - i, 0))
    st = lambda shp: pl.BlockSpec((nbat,) + shp, lambda bi, i: (bi,) + (0,) * len(shp))
    hdim = ML_HEADS * ML_V
    cshape = (N_CHAINS, ML_QK, 2 * ML_V)
    return pl.pallas_call(
        _mlstm_kernel,
        grid=(b // nbat, nc),
        in_specs=[fwd(nml), bwd(nml), fwd(GATE_LANES), bwd(GATE_LANES), st(cshape), st((N_CHAINS, 1, 1))],
        out_specs=[fwd(hdim), bwd(hdim), st(cshape), st((N_CHAINS, 1, 1))],
        out_shape=[jax.ShapeDtypeStruct((b, s, hdim), F32), jax.ShapeDtypeStruct((b, s, hdim), F32),
                   jax.ShapeDtypeStruct(c0.shape, F32), jax.ShapeDtypeStruct(m0.shape, F32)],
        scratch_shapes=[pltpu.VMEM((nbat,) + cshape, F32), pltpu.VMEM((nbat, N_CHAINS, 1, 1), F32)],
        compiler_params=_cparams("parallel", "arbitrary"),
        name="mlstm",
    )(ml, ml, gates, gates, c0, m0)


def _gdconv_kernel(has_halo, *refs):
    if has_halo:
        x_ref, prev_ref, next_ref, w_ref, o_ref, xp_ref = refs
    else:
        x_ref, w_ref, o_ref, xp_ref = refs
    rows = x_ref.shape[1]
    nch = x_ref.shape[2]
    pad = 8
    zero = jnp.zeros((pad, nch), F32)
    if has_halo:
        c = pl.program_id(1)
        xp_ref[0:pad, :] = jnp.where(c > 0, prev_ref[0], zero)
        xp_ref[pad + rows:, :] = jnp.where(c < pl.num_programs(1) - 1, next_ref[0], zero)
    else:
        xp_ref[0:pad, :] = zero
        xp_ref[pad + rows:, :] = zero
    xp_ref[pad:pad + rows, :] = x_ref[0]
    half = CONV_W // 2
    for lc in range(nch // 128):
        sl = slice(lc * 128, (lc + 1) * 128)
        acc = None
        for j in range(CONV_W):
            term = xp_ref[pad - half + j:pad - half + j + rows, sl] * w_ref[j:j + 1, sl]
            acc = term if acc is None else acc + term
        y = _silu(acc)
        if lc < 2 * GD_HEADS:
            y = y * lax.rsqrt(jnp.sum(y * y, axis=-1, keepdims=True) + EPS)
        if lc < GD_HEADS:
            y = y * (GD_QK ** -0.5)
        o_ref[0, :, sl] = y


def _gdconv_ctx(qkv, conv_w):
    b, s, nch = qkv.shape
    return pl.pallas_call(
        functools.partial(_gdconv_kernel, False),
        grid=(b,),
        in_specs=[pl.BlockSpec((1, s, nch), lambda bi: (bi, 0, 0)), pl.BlockSpec((CONV_W, nch), lambda bi: (0, 0))],
        out_specs=pl.BlockSpec((1, s, nch), lambda bi: (bi, 0, 0)),
        out_shape=jax.ShapeDtypeStruct((b, s, nch), F32),
        scratch_shapes=[pltpu.VMEM((s + 16, nch), F32)],
        compiler_params=_cparams("parallel"),
        name="gdconv_ctx",
    )(qkv, conv_w)


def _gdconv_lat(view, conv_w):
    b, rows, wn = view.shape
    nch = wn // GRID_W
    rb = rows // 8
    return pl.pallas_call(
        functools.partial(_gdconv_kernel, True),
        grid=(b, GRID_W),
        in_specs=[pl.BlockSpec((1, rows, nch), lambda bi, c: (bi, 0, c)),
                  pl.BlockSpec((1, 8, nch), lambda bi, c: (bi, rb - 1, jnp.maximum(c - 1, 0))),
                  pl.BlockSpec((1, 8, nch), lambda bi, c: (bi, 0, jnp.minimum(c + 1, GRID_W - 1))),
                  pl.BlockSpec((CONV_W, nch), lambda bi, c: (0, 0))],
        out_specs=pl.BlockSpec((1, rows, nch), lambda bi, c: (bi, 0, c)),
        out_shape=jax.ShapeDtypeStruct(view.shape, F32),
        scratch_shapes=[pltpu.VMEM((rows + 16, nch), F32)],
        compiler_params=_cparams("parallel", "arbitrary"),
        name="gdconv_lat",
    )(view, view, view, conv_w)


SOLVE_BLOCK = 16


def _hi_lo(x):
    hi = x.astype(BF16).astype(F32)
    return hi, x - hi


def _dot_split(a, b):
    a_hi, a_lo = _hi_lo(a)
    b_hi, b_lo = _hi_lo(b)
    lhs = jnp.concatenate([a_hi, a_hi, a_lo], axis=1).astype(BF16)
    rhs = jnp.concatenate([b_hi, b_lo, b_hi], axis=0).astype(BF16)
    return jnp.dot(lhs, rhs, preferred_element_type=F32)


def _unit_triangular_inverses(ns):
    c = ns[0].shape[0]
    row = lax.broadcasted_iota(jnp.int32, (c, c), 0)
    col = lax.broadcasted_iota(jnp.int32, (c, c), 1)
    eye = (row == col).astype(F32)
    in_diag_block = (row // SOLVE_BLOCK) == (col // SOLVE_BLOCK)
    mm = lambda a_list, b_list: [_dot_split(a, b) for a, b in zip(a_list, b_list)]

    n_d = [jnp.where(in_diag_block, n, 0.0) for n in ns]
    x = n_d
    d_inv = [eye - n for n in n_d]
    for _ in range(SOLVE_BLOCK.bit_length() - 2):
        x = mm(x, x)
        d_inv = [d + dx for d, dx in zip(d_inv, mm(d_inv, x))]
    m = mm(d_inv, [n - nd for n, nd in zip(ns, n_d)])
    assert c // SOLVE_BLOCK == 4
    i_minus_m = [eye - mi for mi in m]
    q = [a + b for a, b in zip(i_minus_m, mm(i_minus_m, mm(m, m)))]
    return mm(q, d_inv)


def _gdn_kernel(qf_ref, qb_ref, gf_ref, gb_ref, na_ref, s0_ref, of_ref, ob_ref, sn_ref, s_scr):
    i = pl.program_id(1)

    @pl.when(i == 0)
    def _():
        s_scr[...] = s0_ref[...]

    nbat = qf_ref.shape[0]
    nqk = GD_HEADS * GD_QK
    masks = [_past_mask(d == 1) for d in range(2)]
    x_refs, g_refs, o_refs = (qf_ref, qb_ref), (gf_ref, gb_ref), (of_ref, ob_ref)
    bd = [(bb, d) for bb in range(nbat) for d in range(2)]
    gates = [g_refs[d][bb] for bb, d in bd]
    glog = [na_ref[...] * jax.nn.softplus(g) for g in gates]
    beta_all = [jax.nn.sigmoid(g) for g in gates]
    gcum = [_dot_hi(masks[d][0].astype(F32), glog[j]) for j, (bb, d) in enumerate(bd)]
    gtot = [jnp.sum(g, axis=0, keepdims=True) for g in glog]
    gcum_t = [_transpose_hi(g) for g in gcum]

    chains = [(bb, d, hd) for bb in range(nbat) for d in range(2) for hd in range(GD_HEADS)]
    q, k, v, g_col, beta, g_end, decay = [], [], [], [], [], [], []
    for bb, d, hd in chains:
        ca = GD_GATE0 + d * 8 + hd
        j = bb * 2 + d
        q.append(x_refs[d][bb, :, hd * GD_QK:(hd + 1) * GD_QK])
        k.append(x_refs[d][bb, :, nqk + hd * GD_QK:nqk + (hd + 1) * GD_QK])
        v.append(x_refs[d][bb, :, 2 * nqk + hd * GD_V:2 * nqk + (hd + 1) * GD_V])
        g_col.append(gcum[j][:, ca:ca + 1])
        beta.append(beta_all[j][:, ca + GD_HEADS:ca + GD_HEADS + 1])
        g_end.append(gtot[j][:, ca:ca + 1])
        decay.append(jnp.exp(jnp.where(masks[d][0], g_col[-1] - gcum_t[j][ca:ca + 1, :], -jnp.inf)))
    nc = range(len(chains))
    kk = [_dot_nt(k[c], k[c]) for c in nc]
    xs = [jnp.where(masks[chains[c][1]][1], beta[c] * kk[c] * decay[c], 0.0) for c in nc]
    ps = _unit_triangular_inverses(xs)
    uw = [_dot_split(ps[c], jnp.concatenate([v[c] * beta[c], k[c] * (beta[c] * jnp.exp(g_col[c]))], axis=-1))
          for c in nc]
    qk = [_dot_nt(q[c], k[c]) * decay[c] for c in nc]
    s_st = [s_scr[bb, d * GD_HEADS + hd] for bb, d, hd in chains]
    v_new = [uw[c][:, :GD_V] - _dot(uw[c][:, GD_V:], s_st[c]) for c in nc]
    o_loc = [_dot(q[c] * jnp.exp(g_col[c]), s_st[c]) for c in nc]
    o_new = [o_loc[c] + _dot(qk[c], v_new[c]) for c in nc]
    s_new = [s_st[c] * jnp.exp(g_end[c]) + _dot_tn(k[c] * jnp.exp(g_end[c] - g_col[c]), v_new[c]) for c in nc]
    for c, (bb, d, hd) in enumerate(chains):
        o_refs[d][bb, :, hd * GD_V:(hd + 1) * GD_V] = o_new[c]
        s_scr[bb, d * GD_HEADS + hd] = s_new[c]

    @pl.when(i == pl.num_programs(1) - 1)
    def _():
        sn_ref[...] = s_scr[...]


def _gdn(qkv_view, gates_view, neg_a, s0, nc, idx_fn, out_view_shape):
    b = qkv_view.shape[0]
    nbat = GD_SCAN_BATCH
    nqkv = 2 * GD_HEADS * GD_QK + GD_HEADS * GD_V
    hdim = GD_HEADS * GD_V
    fwd = lambda n: pl.BlockSpec((nbat, CHUNK, n), lambda bi, i: idx_fn(bi, i))
    bwd = lambda n: pl.BlockSpec((nbat, CHUNK, n), lambda bi, i: idx_fn(bi, nc - 1 - i))
    st = pl.BlockSpec((nbat, N_CHAINS, GD_QK, GD_V), lambda bi, i: (bi, 0, 0, 0))
    return pl.pallas_call(
        _gdn_kernel,
        grid=(b // nbat, nc),
        in_specs=[fwd(nqkv), bwd(nqkv), fwd(GATE_LANES), bwd(GATE_LANES),
                  pl.BlockSpec((1, GATE_LANES), lambda bi, i: (0, 0)), st],
        out_specs=[fwd(hdim), bwd(hdim), st],
        out_shape=[jax.ShapeDtypeStruct(out_view_shape, F32), jax.ShapeDtypeStruct(out_view_shape, F32),
                   jax.ShapeDtypeStruct(s0.shape, F32)],
        scratch_shapes=[pltpu.VMEM((nbat, N_CHAINS, GD_QK, GD_V), F32)],
        compiler_params=_cparams("parallel", "arbitrary"),
        name="gdn",
    )(qkv_view, qkv_view, gates_view, gates_view, neg_a, s0)


def _head_rms(t, nheads, width):
    outs = []
    for hd in range(nheads):
        th = t[:, hd * width:(hd + 1) * width]
        outs.append(th * lax.rsqrt(jnp.mean(th * th, axis=-1, keepdims=True) + EPS))
    return jnp.concatenate(outs, axis=-1)


def _rms(t, w):
    return t * lax.rsqrt(jnp.mean(t * t, axis=-1, keepdims=True) + EPS) * w


def _post_kernel(x_ref, hf_ref, hb_ref, og_ref, of_ref, ob_ref, z_ref, mlw_ref, gdw_ref, wout_ref,
                 npost_ref, g2_ref, npre_ref, sh_ref, sc_ref, rwt_ref, wsg_ref, wsu_ref, wsd_ref,
                 x1_ref, hffn_ref, lt_ref, ys_ref, of_scr, ob_scr):
    ml_y = _head_rms(hf_ref[0] + hb_ref[0], ML_HEADS, ML_V) * mlw_ref[...] * jax.nn.sigmoid(og_ref[0])
    _from_grid_view(of_ref, of_scr)
    _from_grid_view(ob_ref, ob_scr)
    o_sum = jnp.concatenate([_pitched_rows(of_scr, g) + _pitched_rows(ob_scr, g) for g in range(of_scr.shape[0])],
                            axis=-1)
    gd_y = _head_rms(o_sum, GD_HEADS, GD_V) * gdw_ref[...] * _silu(z_ref[0])
    y = _dot(jnp.concatenate([ml_y, gd_y], axis=-1), wout_ref[...])
    x1 = x_ref[0] + g2_ref[0] * _rms(y, npost_ref[...])
    x1_ref[0] = x1
    hffn = _rms(x1, npre_ref[...]) * (1.0 + sc_ref[0]) + sh_ref[0]
    nct = hffn.shape[1] // LANES
    for c in range(nct):
        hffn_ref[0, pl.ds(c, hffn.shape[0], stride=nct), :] = hffn[:, c * LANES:(c + 1) * LANES]
    hb = hffn.astype(BF16)
    lt_ref[...] = lax.dot_general(rwt_ref[...], hb, (((1,), (1,)), ((), ())), preferred_element_type=F32)
    hs = _silu(jnp.dot(hb, wsg_ref[...], preferred_element_type=F32)) * jnp.dot(hb, wsu_ref[...],
                                                                                preferred_element_type=F32)
    ys_ref[0] = _dot(hs, wsd_ref[...])


def _post(x, hf, hb, ml, of, ob, gz, mlw, gdw, wout, npost, g2, npre, sh, sc, rwt, wsg, wsu, wsd, tm):
    b, s, d = x.shape
    nt = s // tm
    hw = ML_HEADS * ML_V
    og_blk = (2 * ML_HEADS * ML_QK + ML_HEADS * ML_V) // hw
    tok = lambda n: pl.BlockSpec((1, tm, n), lambda bi, i: (bi, i, 0))
    full = lambda shp: pl.BlockSpec(shp, lambda bi, i: (0,) * len(shp))
    mod = pl.BlockSpec((1, 1, d), lambda bi, i: (bi, 0, 0))
    ne = rwt.shape[0]
    ds = wsg.shape[1]
    gview = pl.BlockSpec((1, tm // GRID_W, GRID_W * hw), lambda bi, i: (bi, i, 0))
    return pl.pallas_call(
        _post_kernel,
        grid=(b, nt),
        in_specs=[tok(d), tok(hw), tok(hw), pl.BlockSpec((1, tm, hw), lambda bi, i: (bi, i, og_blk)),
                  gview, gview, tok(hw), full((1, hw)), full((1, hw)), full((d, d)),
                  full((1, d)), mod, full((1, d)), mod, mod, full((ne, d)), full((d, ds)), full((d, ds)),
                  full((ds, d))],
        out_specs=[tok(d), pl.BlockSpec((1, tm * (d // LANES), LANES), lambda bi, i: (bi, i, 0)),
                   pl.BlockSpec((ne, tm), lambda bi, i: (0, bi * nt + i)), tok(d)],
        out_shape=[jax.ShapeDtypeStruct((b, s, d), F32), jax.ShapeDtypeStruct((b, s * (d // LANES), LANES), F32),
                   jax.ShapeDtypeStruct((ne, b * s), F32), jax.ShapeDtypeStruct((b, s, d), F32)],
        scratch_shapes=[pltpu.VMEM((hw // LANES, tm // GRID_W * GRID_PITCH, LANES), F32)] * 2,
        compiler_params=_cparams("parallel", "arbitrary"),
        name="post",
    )(x, hf, hb, ml, of, ob, gz, mlw, gdw, wout, npost, g2, npre, sh, sc, rwt, wsg, wsu, wsd)


def _route_kernel(lt_ref, bias_ref, idx_ref, gate_ref):
    ne, tn = lt_ref.shape
    gsz = ne // N_GROUPS
    scores = jax.nn.sigmoid(lt_ref[...])
    sel = scores + bias_ref[...]
    neg = -jnp.inf
    sel3 = sel.reshape(N_GROUPS, gsz, tn)
    io3 = lax.broadcasted_iota(jnp.int32, sel3.shape, 1)
    top1 = jnp.max(sel3, axis=1, keepdims=True)
    first = jnp.min(jnp.where(sel3 == top1, io3, gsz), axis=1, keepdims=True)
    top2 = jnp.max(jnp.where(io3 == first, neg, sel3), axis=1, keepdims=True)
    grp = (top1 + top2).reshape(N_GROUPS, tn)
    iog = lax.broadcasted_iota(jnp.int32, grp.shape, 0)
    keep = jnp.zeros(grp.shape, jnp.bool_)
    for _ in range(TOPK_GROUPS):
        m = jnp.max(grp, axis=0, keepdims=True)
        pick = iog == jnp.min(jnp.where(grp == m, iog, N_GROUPS), axis=0, keepdims=True)
        keep = keep | pick
        grp = jnp.where(pick, neg, grp)
    cand = jnp.where(keep.reshape(N_GROUPS, 1, tn), sel3, neg).reshape(ne, tn)
    ioe = lax.broadcasted_iota(jnp.int32, cand.shape, 0)
    idxs, gates = [], []
    for _ in range(TOP_K):
        m = jnp.max(cand, axis=0, keepdims=True)
        e = jnp.min(jnp.where(cand == m, ioe, ne), axis=0, keepdims=True)
        pick = ioe == e
        idxs.append(e)
        gates.append(jnp.sum(jnp.where(pick, scores, 0.0), axis=0, keepdims=True))
        cand = jnp.where(pick, neg, cand)
    gate = jnp.concatenate(gates, axis=0)
    idx_ref[...] = jnp.concatenate(idxs, axis=0)
    gate_ref[...] = gate / jnp.sum(gate, axis=0, keepdims=True) * ROUTED_SCALE


def _route(logits_t, bias_col, tn):
    ne, t = logits_t.shape
    return pl.pallas_call(
        _route_kernel,
        grid=(t // tn,),
        in_specs=[pl.BlockSpec((ne, tn), lambda i: (0, i)), pl.BlockSpec((ne, 1), lambda i: (0, 0))],
        out_specs=[pl.BlockSpec((TOP_K, tn), lambda i: (0, i)), pl.BlockSpec((TOP_K, tn), lambda i: (0, i))],
        out_shape=[jax.ShapeDtypeStruct((TOP_K, t), jnp.int32), jax.ShapeDtypeStruct((TOP_K, t), F32)],
        compiler_params=_cparams("parallel"),
        name="route",
    )(logits_t, bias_col)


def _experts_kernel(be_ref, np_ref,
                    tok_ref, tokn_ref, w_ref, wg0_ref, wu0_ref, wd0_ref, wg1_ref, wu1_ref, wd1_ref, h_hbm,
                    o_ref, xg, wgc, wuc, wdc, gsem):
    s = pl.program_id(0)
    n_pairs = np_ref[0]
    nct = xg.shape[1] // EXPERT_BLOCK
    rows = nct * EXPERT_BLOCK
    w_refs = ((wg0_ref, wu0_ref, wd0_ref), (wg1_ref, wu1_ref, wd1_ref))

    def gather_copy(tref, p, j):
        src = h_hbm.at[pl.ds(pl.multiple_of(tref[0, p, j] * nct, nct), nct)]
        return pltpu.make_async_copy(src, xg.at[p, pl.ds(j * nct, nct)], gsem.at[p])

    def gather_wait(p):
        pltpu.make_async_copy(h_hbm.at[pl.ds(0, rows)], xg.at[p], gsem.at[p]).wait()

    @pl.when(s >= n_pairs)
    def _():
        o_ref[...] = jnp.zeros(o_ref.shape, F32)

    @pl.when(s < n_pairs)
    def _():
        @pl.when(s == 0)
        def _():
            for p in range(2):
                for j in range(EXPERT_BLOCK):
                    gather_copy(tok_ref, p, j).start(priority=j % 2)

        for p in range(2):
            blk = 2 * s + p
            wg_ref, wu_ref, wd_ref = w_refs[p]
            gather_wait(p)

            @pl.when((blk == 0) | (be_ref[blk] != be_ref[jnp.maximum(blk - 1, 0)]))
            def _():
                wgc[...] = wg_ref[0].astype(BF16)
                wuc[...] = wu_ref[0].astype(BF16)
                wdc[...] = wd_ref[0].astype(BF16)

            xb = jnp.concatenate([xg[p, pl.ds(c, EXPERT_BLOCK, stride=nct), :] for c in range(nct)],
                                 axis=-1).astype(BF16)
            for j in range(EXPERT_BLOCK):
                gather_copy(tokn_ref, p, j).start(priority=j % 2)
            hmid = _silu(jnp.dot(xb, wgc[...], preferred_element_type=F32)) * jnp.dot(xb, wuc[...],
                                                                                      preferred_element_type=F32)
            out = _dot(hmid, wdc[...])
            eye = (lax.broadcasted_iota(jnp.int32, (EXPERT_BLOCK, EXPERT_BLOCK), 0)
                   == lax.broadcasted_iota(jnp.int32, (EXPERT_BLOCK, EXPERT_BLOCK), 1))
            w_col = jnp.sum(jnp.where(eye, w_ref[0, p:p + 1, :], 0.0), axis=1, keepdims=True)
            out = out * w_col
            for c in range(nct):
                o_ref[pl.ds(p * rows + c, EXPERT_BLOCK, stride=nct), :] = out[:, c * LANES:(c + 1) * LANES]

        @pl.when(s == n_pairs - 1)
        def _():
            for p in range(2):
                gather_wait(p)


def _experts(hffn, block_e, n_pairs, row_tok, row_w, wg, wu, wd):
    d = wg.shape[1]
    nct = d // LANES
    npairs = row_tok.shape[0]
    de = wg.shape[2]
    last = npairs - 1
    smem_blk = lambda f: pl.BlockSpec((1, 2, EXPERT_BLOCK), f, memory_space=pltpu.SMEM)
    wspec = lambda shp, p: pl.BlockSpec((1,) + shp, lambda s, be, npu: (be[2 * s + p], 0, 0))
    pair_rows = 2 * EXPERT_BLOCK * nct
    grid_spec = pltpu.PrefetchScalarGridSpec(
        num_scalar_prefetch=2,
        grid=(npairs,),
        in_specs=[smem_blk(lambda s, be, npu: (s, 0, 0)),
                  smem_blk(lambda s, be, npu: (jnp.minimum(s + 1, last), 0, 0)),
                  pl.BlockSpec((1, 2, EXPERT_BLOCK), lambda s, be, npu: (s, 0, 0)),
                  wspec((d, de), 0), wspec((d, de), 0), wspec((de, d), 0),
                  wspec((d, de), 1), wspec((d, de), 1), wspec((de, d), 1),
                  pl.BlockSpec(memory_space=pl.ANY)],
        out_specs=pl.BlockSpec((pair_rows, LANES), lambda s, be, npu: (s, 0)),
        scratch_shapes=[pltpu.VMEM((2, EXPERT_BLOCK * nct, LANES), F32),
                        pltpu.VMEM((d, de), BF16), pltpu.VMEM((d, de), BF16), pltpu.VMEM((de, d), BF16),
                        pltpu.SemaphoreType.DMA((2,))],
    )
    return pl.pallas_call(
        _experts_kernel,
        grid_spec=grid_spec,
        out_shape=jax.ShapeDtypeStruct((npairs * pair_rows, LANES), F32),
        compiler_params=_cparams("arbitrary"),
        name="experts",
    )(block_e, n_pairs, row_tok, row_tok, row_w, wg, wu, wd, wg, wu, wd, hffn)


def _combine_kernel(pos_ref, posn_ref, x1_ref, ys_ref, npost_ref, g5_ref, rows_hbm, o_ref, buf, sem):
    i = pl.program_id(0)
    tm = x1_ref.shape[1]
    nct = x1_ref.shape[2] // LANES
    cur = i % 2
    nxt = 1 - cur

    def copy(pref, b, k, t):
        src = rows_hbm.at[pl.ds(pl.multiple_of(pref[k, t] * nct, nct), nct)]
        dst = buf.at[b, pl.ds(pl.multiple_of((k * tm + t) * nct, nct), nct)]
        return pltpu.make_async_copy(src, dst, sem.at[b])

    def drain(b):
        pltpu.make_async_copy(rows_hbm.at[pl.ds(0, TOP_K * tm * nct)], buf.at[b], sem.at[b]).wait()

    def issue(pref, b):
        def body(t2, carry):
            for u in range(2):
                for k in range(TOP_K):
                    copy(pref, b, k, 2 * t2 + u).start(priority=k % 2)
            return carry
        lax.fori_loop(0, tm // 2, body, 0)

    @pl.when(i == 0)
    def _():
        issue(pos_ref, cur)

    @pl.when(i + 1 < pl.num_programs(0))
    def _():
        for t in range(tm):
            for k in range(TOP_K):
                copy(posn_ref, nxt, k, t).start(priority=k % 2)

    drain(cur)
    routed = []
    for c in range(nct):
        acc = buf[cur, pl.ds(c, tm, stride=nct), :]
        for k in range(1, TOP_K):
            acc = acc + buf[cur, pl.ds(k * tm * nct + c, tm, stride=nct), :]
        routed.append(acc)
    y = ys_ref[0] + jnp.concatenate(routed, axis=-1)
    o_ref[0] = x1_ref[0] + g5_ref[0] * _rms(y, npost_ref[...])


def _combine(x1, ys, rows, pos, npost, g5, tm):
    b, s, d = x1.shape
    nt = s // tm
    nct = d // LANES
    last = b * nt - 1
    tok = pl.BlockSpec((1, tm, d), lambda i: (i // nt, i % nt, 0))
    pos_blk = lambda f: pl.BlockSpec((TOP_K, tm), f, memory_space=pltpu.SMEM)
    return pl.pallas_call(
        _combine_kernel,
        grid=(b * nt,),
        in_specs=[pos_blk(lambda i: (0, i)), pos_blk(lambda i: (0, jnp.minimum(i + 1, last))), tok, tok,
                  pl.BlockSpec((1, d), lambda i: (0, 0)), pl.BlockSpec((1, 1, d), lambda i: (i // nt, 0, 0)),
                  pl.BlockSpec(memory_space=pl.ANY)],
        out_specs=tok,
        out_shape=jax.ShapeDtypeStruct((b, s, d), F32),
        scratch_shapes=[pltpu.VMEM((2, TOP_K * tm * nct, LANES), F32), pltpu.SemaphoreType.DMA((2,))],
        compiler_params=_cparams("arbitrary"),
        name="combine",
    )(pos, pos, x1, ys, npost, g5, rows)


def _dispatch_plan(idx_t, gate_t):
    k, t = idx_t.shape
    n_asg = k * t
    nb = n_asg // EXPERT_BLOCK + N_EXPERTS
    flat_e = idx_t.reshape(-1)
    id_bits = max(1, (n_asg - 1).bit_length())
    assert (N_EXPERTS - 1).bit_length() + id_bits <= 31
    packed = jnp.sort((flat_e << id_bits) | jnp.arange(n_asg, dtype=jnp.int32))
    order = packed & ((1 << id_bits) - 1)
    counts = jnp.zeros((N_EXPERTS,), jnp.int32).at[flat_e].add(1)
    padded = (counts + EXPERT_BLOCK - 1) // EXPERT_BLOCK * EXPERT_BLOCK
    start = jnp.cumsum(counts) - counts
    pend = jnp.cumsum(padded)
    pstart = pend - padded
    blk0 = jnp.arange(nb, dtype=jnp.int32) * EXPERT_BLOCK
    block_e = jnp.minimum(jnp.sum((pend[None, :] <= blk0[:, None]).astype(jnp.int32), axis=1), N_EXPERTS - 1)
    of_block = block_e[:, None] == jnp.arange(N_EXPERTS, dtype=jnp.int32)[None, :]
    per_block = lambda v: jnp.sum(jnp.where(of_block, v[None, :], 0), axis=1)
    assert nb % 2 == 0
    n_pairs = ((pend[-1] // EXPERT_BLOCK + 1) // 2).astype(jnp.int32).reshape(1)
    pos = blk0[:, None] - per_block(pstart)[:, None] + jnp.arange(EXPERT_BLOCK, dtype=jnp.int32)[None, :]
    valid = pos < per_block(counts)[:, None]
    src = jnp.clip(per_block(start)[:, None] + pos, 0, n_asg - 1)
    asg = order[src]
    row_tok = jnp.where(valid, asg % t, 0).astype(jnp.int32)
    row_w = jnp.where(valid, gate_t.reshape(-1)[asg], 0.0).astype(F32)
    i_sorted = jnp.arange(n_asg, dtype=jnp.int32)
    pad_before = jnp.sum(jnp.where(i_sorted[:, None] >= (start + counts)[None, :], (padded - counts)[None, :], 0),
                         axis=1)
    _, pos = lax.sort((order, i_sorted + pad_before), num_keys=1)
    pos = pos.reshape(k, t)
    shp = (nb // 2, 2, EXPERT_BLOCK)
    return block_e, n_pairs, row_tok.reshape(shp), row_w.reshape(shp), pos


def _pack_in_weights(w_in, ml_i_bias, ml_f_bias, gd_dt_bias):
    d = w_in.shape[0]
    nml = 2 * ML_HEADS * ML_QK + 2 * ML_HEADS * ML_V
    ml_cols = nml + 4 * ML_HEADS
    ngq = GD_HEADS * (2 * GD_QK + GD_V)
    ngz = GD_HEADS * GD_V
    wml = w_in[:, :nml].astype(BF16)
    wgq = w_in[:, ml_cols:ml_cols + ngq].astype(BF16)
    wgz = w_in[:, ml_cols + ngq:ml_cols + ngq + ngz].astype(BF16)
    wg = jnp.zeros((d, GATE_LANES), F32)
    wg = wg.at[:, ML_GATE0:ML_GATE0 + 16].set(w_in[:, nml:ml_cols])
    wg = wg.at[:, GD_GATE0:GD_GATE0 + 16].set(w_in[:, ml_cols + ngq + ngz:])
    gb = jnp.zeros((GATE_LANES,), F32)
    gb = gb.at[ML_GATE0:ML_GATE0 + 16].set(jnp.stack([ml_i_bias, ml_f_bias], axis=1).reshape(-1))
    gb = gb.at[GD_GATE0:GD_GATE0 + 16].set(jnp.stack([gd_dt_bias, jnp.zeros_like(gd_dt_bias)], axis=1).reshape(-1))
    return wml, wgq, wgz, wg.astype(BF16), gb.reshape(1, GATE_LANES)


def _mixer(x, ctx, mod, mod_ctx, norm_pre_mix, w_in, ml_i_bias, ml_f_bias, gd_conv_w, gd_a_log, gd_dt_bias):
    b, s, d = x.shape
    sc = ctx.shape[1]
    wml, wgq, wgz, wg, gb = _pack_in_weights(w_in, ml_i_bias, ml_f_bias, gd_dt_bias)
    nw = norm_pre_mix.reshape(1, d)
    ctx_mod = lambda j: jnp.broadcast_to(mod_ctx[j].reshape(1, 1, d), (b, 1, d))
    ml_c, gq_c, _, g_c = _proj(ctx, nw, ctx_mod(0), ctx_mod(1), wml, wgq, wgz, wg, gb, tm=sc, grid_view=False)
    ml_l, gqv_l, gz_l, g_l, gv_l = _proj(x, nw, mod[0], mod[1], wml, wgq, wgz, wg, gb, tm=512, grid_view=True)

    c0 = jnp.zeros((b, N_CHAINS, ML_QK, 2 * ML_V), F32)
    m0 = jnp.zeros((b, N_CHAINS, 1, 1), F32)
    _, _, c1, m1 = _mlstm(ml_c, g_c, c0, m0)
    hf, hb, _, _ = _mlstm(ml_l, g_l, c1, m1)

    neg_a = jnp.zeros((GATE_LANES,), F32)
    neg_a = neg_a.at[GD_GATE0:GD_GATE0 + 16].set(
        jnp.stack([-jnp.exp(gd_a_log), jnp.zeros_like(gd_a_log)], axis=1).reshape(-1)).reshape(1, GATE_LANES)
    qn_c = _gdconv_ctx(gq_c, gd_conv_w)
    qnv_l = _gdconv_lat(gqv_l, gd_conv_w)
    s0 = jnp.zeros((b, N_CHAINS, GD_QK, GD_V), F32)
    hdim = GD_HEADS * GD_V
    _, _, s1 = _gdn(qn_c, g_c, neg_a, s0, sc // CHUNK, lambda bi, n: (bi, n, 0), (b, sc, hdim))
    rows = s // GRID_W
    cpc = rows // CHUNK
    col_idx = lambda bi, n: (bi, n % cpc, n // cpc)
    ofv, obv, _ = _gdn(qnv_l, gv_l, neg_a, s1, s // CHUNK, col_idx, (b, rows, GRID_W * hdim))
    return hf, hb, ml_l, ofv, obv, gz_l


def kernel(x, c, ctx, c_ctx, w_ada, b_ada, norm_pre_mix, norm_post_mix, norm_pre_ffn, norm_post_ffn, w_in,
           ml_i_bias, ml_f_bias, ml_norm_w, gd_conv_w, gd_a_log, gd_dt_bias, gd_norm_w, w_out, router_w,
           router_bias, w_gate, w_up, w_down, ws_gate, ws_up, ws_down):
    b, s, d = x.shape
    depth = w_ada.shape[0]
    assert depth == 1, "the context stream update of deeper stacks is not implemented"
    ly = 0
    cc = jnp.zeros((16, d), F32).at[:b].set(c).at[b].set(c_ctx)
    mod_all = _ada(cc, w_ada[ly], b_ada[ly])
    mod = [mod_all[:b, j * d:(j + 1) * d].reshape(b, 1, d) for j in range(6)]
    mod_ctx = [mod_all[b, j * d:(j + 1) * d] for j in range(6)]

    hf, hb, ml_l, of, ob, gz_l = _mixer(x, ctx, mod, mod_ctx, norm_pre_mix[ly], w_in[ly], ml_i_bias[ly],
                                        ml_f_bias[ly], gd_conv_w[ly], gd_a_log[ly], gd_dt_bias[ly])

    row = lambda v: v.reshape(1, -1)
    x1, hffn, logits_t, ys = _post(
        x, hf, hb, ml_l, of, ob, gz_l, row(ml_norm_w[ly]), row(jnp.tile(gd_norm_w[ly], GD_HEADS)),
        w_out[ly].astype(BF16), row(norm_post_mix[ly]), mod[2], row(norm_pre_ffn[ly]), mod[3], mod[4],
        router_w[ly].T.astype(BF16), ws_gate[ly].astype(BF16), ws_up[ly].astype(BF16), ws_down[ly].astype(BF16),
        tm=512)

    idx_t, gate_t = _route(logits_t, router_bias[ly].reshape(-1, 1), tn=512)
    block_e, n_pairs, row_tok, row_w, pos = _dispatch_plan(idx_t, gate_t)
    t = b * s
    rows = _experts(hffn.reshape(t * (d // LANES), LANES), block_e, n_pairs, row_tok, row_w,
                    w_gate[ly], w_up[ly], w_down[ly])
    return _combine(x1, ys, rows, pos, row(norm_post_ffn[ly]), mod[5], tm=256)
```

```python
import functools

import jax
import jax.numpy as jnp
from jax import lax
from jax.experimental import pallas as pl
from jax.experimental.pallas import tpu as pltpu

EPS = 1e-6
CHUNK = 64
GRID_W = 64
ML_HEADS, ML_QK, ML_V = 4, 64, 128
GD_HEADS, GD_QK, GD_V = 4, 128, 128
CONV_W = 5
N_EXPERTS, TOP_K, N_GROUPS, TOPK_GROUPS = 256, 8, 8, 4
ROUTED_SCALE = 2.5
EXPERT_BLOCK = 128
N_CHAINS = 8
ML_SCAN_BATCH, GD_SCAN_BATCH = 1, 2
LANES = 128
GATE_LANES = LANES
ML_GATE0, GD_GATE0 = 0, 16

F32 = jnp.float32
BF16 = jnp.bfloat16
HI = lax.Precision.HIGHEST
VMEM_LIMIT = 56 * 1024 * 1024
TM_PROJ, TM_POST, TM_COMBINE, TN_ROUTE, TN_ADA = 512, 512, 256, 512, 1536


def _cparams(*sem):
    return pltpu.CompilerParams(dimension_semantics=sem, vmem_limit_bytes=VMEM_LIMIT)


def _dot(a, b):
    return jnp.dot(a.astype(BF16), b.astype(BF16), preferred_element_type=F32)


def _dot_nt(a, b):
    return lax.dot_general(a.astype(BF16), b.astype(BF16), (((1,), (1,)), ((), ())), preferred_element_type=F32)


def _dot_tn(a, b):
    return lax.dot_general(a.astype(BF16), b.astype(BF16), (((0,), (0,)), ((), ())), preferred_element_type=F32)


def _dot_hi(a, b):
    return jnp.dot(a, b, precision=HI, preferred_element_type=F32)


def _dot_nt_hi(a, b):
    return lax.dot_general(a, b, (((1,), (1,)), ((), ())), precision=HI, preferred_element_type=F32)


def _transpose_hi(x):
    n = x.shape[1]
    eye = (lax.broadcasted_iota(jnp.int32, (n, n), 0) == lax.broadcasted_iota(jnp.int32, (n, n), 1)).astype(F32)
    return _dot_nt_hi(eye, x)


def _silu(x):
    return x * jax.nn.sigmoid(x)


def _past_mask(reverse):
    t = lax.broadcasted_iota(jnp.int32, (CHUNK, CHUNK), 0)
    s = lax.broadcasted_iota(jnp.int32, (CHUNK, CHUNK), 1)
    return (s >= t, s > t) if reverse else (s <= t, s < t)


def _ada_kernel(c_ref, w_ref, b_ref, o_ref):
    o_ref[...] = _dot(_silu(c_ref[...]), w_ref[...]) + b_ref[...]


def _ada(cc, w_ada, b_ada):
    rows, d = cc.shape
    n = w_ada.shape[1]
    tn = TN_ADA
    return pl.pallas_call(
        _ada_kernel,
        grid=(n // tn,),
        in_specs=[pl.BlockSpec((rows, d), lambda j: (0, 0)),
                  pl.BlockSpec((d, tn), lambda j: (0, j)),
                  pl.BlockSpec((1, tn), lambda j: (0, j))],
        out_specs=pl.BlockSpec((rows, tn), lambda j: (0, j)),
        out_shape=jax.ShapeDtypeStruct((rows, n), F32),
        compiler_params=_cparams("arbitrary"),
        name="ada",
    )(cc, w_ada, b_ada.reshape(1, n))


GRID_PITCH = GRID_W + 8


def _to_grid_view(src_ref, dst_ref):
    ng = src_ref.shape[0]
    r = src_ref.shape[1] // GRID_PITCH
    for c in range(GRID_W):
        for g in range(ng):
            lo = (c * ng + g) * LANES
            dst_ref[0, :, lo:lo + LANES] = src_ref[g, pl.ds(c, r, stride=GRID_PITCH), :]


def _from_grid_view(src_ref, dst_ref):
    ng = dst_ref.shape[0]
    r = dst_ref.shape[1] // GRID_PITCH
    for c in range(GRID_W):
        for g in range(ng):
            lo = (c * ng + g) * LANES
            dst_ref[g, pl.ds(c, r, stride=GRID_PITCH), :] = src_ref[0, :, lo:lo + LANES]


def _pitched_rows(ref, g):
    r = ref.shape[1] // GRID_PITCH
    return jnp.concatenate([ref[g, i * GRID_PITCH:i * GRID_PITCH + GRID_W, :] for i in range(r)], axis=0)


def _proj_kernel(grid_view, x_ref, nw_ref, sh_ref, sc_ref, wml_ref, wgq_ref, wgz_ref, wg_ref, gb_ref, *refs):
    if grid_view:
        ml_ref, gqv_ref, gz_ref, g_ref, gv_ref, gq_scr, g_scr = refs
    else:
        ml_ref, gq_ref, gz_ref, g_ref = refs
    x = x_ref[0]
    xn = x * lax.rsqrt(jnp.mean(x * x, axis=-1, keepdims=True) + EPS) * nw_ref[...]
    h = (xn * (1.0 + sc_ref[0]) + sh_ref[0]).astype(BF16)
    ml_ref[0] = jnp.dot(h, wml_ref[...], preferred_element_type=F32)
    gz_ref[0] = jnp.dot(h, wgz_ref[...], preferred_element_type=F32)
    gates = jnp.dot(h, wg_ref[...], preferred_element_type=F32) + gb_ref[...]
    g_ref[0] = gates
    gq = jnp.dot(h, wgq_ref[...], preferred_element_type=F32)
    if grid_view:
        for r in range(x.shape[0] // GRID_W):
            rows = slice(r * GRID_W, (r + 1) * GRID_W)
            prow = slice(r * GRID_PITCH, r * GRID_PITCH + GRID_W)
            g_scr[0, prow, :] = gates[rows]
            for g in range(gq_scr.shape[0]):
                gq_scr[g, prow, :] = gq[rows, g * LANES:(g + 1) * LANES]
        _to_grid_view(gq_scr, gqv_ref)
        _to_grid_view(g_scr, gv_ref)
    else:
        gq_ref[0] = gq


def _proj(x, norm_w, shift, scale, wml, wgq, wgz, wg, gbias, tm, grid_view):
    b, s, d = x.shape
    nml, ngq, ngz = wml.shape[1], wgq.shape[1], wgz.shape[1]
    full = lambda shp: pl.BlockSpec(shp, lambda bi, i: (0,) * len(shp))
    tok = lambda n: pl.BlockSpec((1, tm, n), lambda bi, i: (bi, i, 0))
    mod = pl.BlockSpec((1, 1, d), lambda bi, i: (bi, 0, 0))
    if grid_view:
        rt = tm // GRID_W
        view = lambda n: pl.BlockSpec((1, rt, GRID_W * n), lambda bi, i: (bi, i, 0))
        vshape = lambda n: jax.ShapeDtypeStruct((b, s // GRID_W, GRID_W * n), F32)
        out_specs = [tok(nml), view(ngq), tok(ngz), tok(GATE_LANES), view(GATE_LANES)]
        out_shape = [jax.ShapeDtypeStruct((b, s, nml), F32), vshape(ngq), jax.ShapeDtypeStruct((b, s, ngz), F32),
                     jax.ShapeDtypeStruct((b, s, GATE_LANES), F32), vshape(GATE_LANES)]
        scratch = [pltpu.VMEM((ngq // LANES, rt * GRID_PITCH, LANES), F32), pltpu.VMEM((1, rt * GRID_PITCH, LANES), F32)]
    else:
        out_specs = [tok(nml), tok(ngq), tok(ngz), tok(GATE_LANES)]
        out_shape = [jax.ShapeDtypeStruct((b, s, n), F32) for n in (nml, ngq, ngz, GATE_LANES)]
        scratch = []
    return pl.pallas_call(
        functools.partial(_proj_kernel, grid_view),
        grid=(b, s // tm),
        in_specs=[tok(d), full((1, d)), mod, mod, full((d, nml)), full((d, ngq)), full((d, ngz)),
                  full((d, GATE_LANES)), full((1, GATE_LANES))],
        out_specs=out_specs,
        out_shape=out_shape,
        scratch_shapes=scratch,
        compiler_params=_cparams("parallel", "arbitrary"),
        name="proj",
    )(x, norm_w, shift, scale, wml, wgq, wgz, wg, gbias)


def _mlstm_kernel(mlf_ref, mlb_ref, gf_ref, gb_ref, c0_ref, m0_ref, hf_ref, hb_ref, cn_ref, mn_ref, c_scr, m_scr):
    i = pl.program_id(1)

    @pl.when(i == 0)
    def _():
        c_scr[...] = c0_ref[...]
        m_scr[...] = m0_ref[...]

    nbat = mlf_ref.shape[0]
    past = [_past_mask(d == 1)[0] for d in range(2)]
    ml_refs, g_refs, h_refs = (mlf_ref, mlb_ref), (gf_ref, gb_ref), (hf_ref, hb_ref)
    bd = [(bb, d) for bb in range(nbat) for d in range(2)]
    g = [g_refs[d][bb] for bb, d in bd]
    ls = [jax.nn.log_sigmoid(x) for x in g]
    bcol = [_dot_hi(past[d].astype(F32), ls[j]) for j, (bb, d) in enumerate(bd)]
    tot = [jnp.sum(x, axis=0, keepdims=True) for x in ls]
    g_t = [_transpose_hi(x) for x in g]
    b_t = [_transpose_hi(x) for x in bcol]

    chains = [(bb, d, hd) for bb in range(nbat) for d in range(2) for hd in range(ML_HEADS)]
    nc = range(len(chains))
    k0, v0 = ML_HEADS * ML_QK, 2 * ML_HEADS * ML_QK
    ones_col = (lax.broadcasted_iota(jnp.int32, (CHUNK, ML_V), 1) == 0).astype(F32)
    q, k, v, i_col, b_col, b_end, log_d = [], [], [], [], [], [], []
    for bb, d, hd in chains:
        ci = ML_GATE0 + d * 8 + hd
        cf = ci + ML_HEADS
        j = bb * 2 + d
        q.append(ml_refs[d][bb, :, hd * ML_QK:(hd + 1) * ML_QK])
        k.append(ml_refs[d][bb, :, k0 + hd * ML_QK:k0 + (hd + 1) * ML_QK] * (ML_QK ** -0.5))
        v.append(jnp.concatenate([ml_refs[d][bb, :, v0 + hd * ML_V:v0 + (hd + 1) * ML_V], ones_col], axis=-1))
        i_col.append(g[j][:, ci:ci + 1])
        b_col.append(bcol[j][:, cf:cf + 1])
        b_end.append(tot[j][:, cf:cf + 1])
        log_d.append(jnp.where(past[d], b_col[-1] - b_t[j][cf:cf + 1, :] + g_t[j][ci:ci + 1, :], -jnp.inf))
    st_idx = [(bb, d * ML_HEADS + hd) for bb, d, hd in chains]
    c_st = [c_scr[ix] for ix in st_idx]
    m_st = [m_scr[ix] for ix in st_idx]
    log_prev = [b_col[c] + m_st[c] for c in nc]
    m_t = [jnp.maximum(log_prev[c], jnp.max(log_d[c], axis=-1, keepdims=True)) for c in nc]
    qk = [_dot_nt(q[c], k[c]) for c in nc]
    qc = [_dot(q[c], c_st[c]) for c in nc]
    s = [qk[c] * jnp.exp(log_d[c] - m_t[c]) for c in nc]
    w_prev = [jnp.exp(log_prev[c] - m_t[c]) for c in nc]
    sv = [_dot(s[c], v[c]) for c in nc]
    log_s = [b_end[c] - b_col[c] + i_col[c] for c in nc]
    m_new = [jnp.maximum(b_end[c] + m_st[c], jnp.max(log_s[c], axis=0, keepdims=True)) for c in nc]
    kw = [k[c] * jnp.exp(log_s[c] - m_new[c]) for c in nc]
    w_c = [jnp.exp(b_end[c] + m_st[c] - m_new[c]) for c in nc]
    kv = [_dot_tn(kw[c], v[c]) for c in nc]
    numden = [sv[c] + w_prev[c] * qc[c] for c in nc]
    scale = [1.0 / jnp.maximum(jnp.abs(numden[c][:, ML_V:ML_V + 1]), jnp.exp(-m_t[c])) for c in nc]
    for c, (bb, d, hd) in enumerate(chains):
        h_refs[d][bb, :, hd * ML_V:(hd + 1) * ML_V] = numden[c][:, :ML_V] * scale[c]
        c_scr[st_idx[c]] = w_c[c] * c_st[c] + kv[c]
        m_scr[st_idx[c]] = m_new[c]

    @pl.when(i == pl.num_programs(1) - 1)
    def _():
        cn_ref[...] = c_scr[...]
        mn_ref[...] = m_scr[...]


def _mlstm(ml, gates, c0, m0):
    b, s, nml = ml.shape
    nc = s // CHUNK
    nbat = ML_SCAN_BATCH
    fwd = lambda n: pl.BlockSpec((nbat, CHUNK, n), lambda bi, i: (bi, i, 0))
    bwd = lambda n: pl.BlockSpec((nbat, CHUNK, n), lambda bi, i: (bi, nc - 1 - i, 0))
    st = lambda shp: pl.BlockSpec((nbat,) + shp, lambda bi, i: (bi,) + (0,) * len(shp))
    hdim = ML_HEADS * ML_V
    cshape = (N_CHAINS, ML_QK, 2 * ML_V)
    return pl.pallas_call(
        _mlstm_kernel,
        grid=(b // nbat, nc),
        in_specs=[fwd(nml), bwd(nml), fwd(GATE_LANES), bwd(GATE_LANES), st(cshape), st((N_CHAINS, 1, 1))],
        out_specs=[fwd(hdim), bwd(hdim), st(cshape), st((N_CHAINS, 1, 1))],
        out_shape=[jax.ShapeDtypeStruct((b, s, hdim), F32), jax.ShapeDtypeStruct((b, s, hdim), F32),
                   jax.ShapeDtypeStruct(c0.shape, F32), jax.ShapeDtypeStruct(m0.shape, F32)],
        scratch_shapes=[pltpu.VMEM((nbat,) + cshape, F32), pltpu.VMEM((nbat, N_CHAINS, 1, 1), F32)],
        compiler_params=_cparams("parallel", "arbitrary"),
        name="mlstm",
    )(ml, ml, gates, gates, c0, m0)


def _gdconv_kernel(has_halo, *refs):
    if has_halo:
        x_ref, prev_ref, next_ref, w_ref, o_ref, xp_ref = refs
    else:
        x_ref, w_ref, o_ref, xp_ref = refs
    rows = x_ref.shape[1]
    nch = x_ref.shape[2]
    pad = 8
    zero = jnp.zeros((pad, nch), F32)
    if has_halo:
        c = pl.program_id(1)
        xp_ref[0:pad, :] = jnp.where(c > 0, prev_ref[0], zero)
        xp_ref[pad + rows:, :] = jnp.where(c < pl.num_programs(1) - 1, next_ref[0], zero)
    else:
        xp_ref[0:pad, :] = zero
        xp_ref[pad + rows:, :] = zero
    xp_ref[pad:pad + rows, :] = x_ref[0]
    half = CONV_W // 2
    for lc in range(nch // 128):
        sl = slice(lc * 128, (lc + 1) * 128)
        acc = None
        for j in range(CONV_W):
            term = xp_ref[pad - half + j:pad - half + j + rows, sl] * w_ref[j:j + 1, sl]
            acc = term if acc is None else acc + term
        y = _silu(acc)
        if lc < 2 * GD_HEADS:
            y = y * lax.rsqrt(jnp.sum(y * y, axis=-1, keepdims=True) + EPS)
        if lc < GD_HEADS:
            y = y * (GD_QK ** -0.5)
        o_ref[0, :, sl] = y


def _gdconv_ctx(qkv, conv_w):
    b, s, nch = qkv.shape
    return pl.pallas_call(
        functools.partial(_gdconv_kernel, False),
        grid=(b,),
        in_specs=[pl.BlockSpec((1, s, nch), lambda bi: (bi, 0, 0)), pl.BlockSpec((CONV_W, nch), lambda bi: (0, 0))],
        out_specs=pl.BlockSpec((1, s, nch), lambda bi: (bi, 0, 0)),
        out_shape=jax.ShapeDtypeStruct((b, s, nch), F32),
        scratch_shapes=[pltpu.VMEM((s + 16, nch), F32)],
        compiler_params=_cparams("parallel"),
        name="gdconv_ctx",
    )(qkv, conv_w)


def _gdconv_lat(view, conv_w):
    b, rows, wn = view.shape
    nch = wn // GRID_W
    rb = rows // 8
    return pl.pallas_call(
        functools.partial(_gdconv_kernel, True),
        grid=(b, GRID_W),
        in_specs=[pl.BlockSpec((1, rows, nch), lambda bi, c: (bi, 0, c)),
                  pl.BlockSpec((1, 8, nch), lambda bi, c: (bi, rb - 1, jnp.maximum(c - 1, 0))),
                  pl.BlockSpec((1, 8, nch), lambda bi, c: (bi, 0, jnp.minimum(c + 1, GRID_W - 1))),
                  pl.BlockSpec((CONV_W, nch), lambda bi, c: (0, 0))],
        out_specs=pl.BlockSpec((1, rows, nch), lambda bi, c: (bi, 0, c)),
        out_shape=jax.ShapeDtypeStruct(view.shape, F32),
        scratch_shapes=[pltpu.VMEM((rows + 16, nch), F32)],
        compiler_params=_cparams("parallel", "arbitrary"),
        name="gdconv_lat",
    )(view, view, view, conv_w)


SOLVE_BLOCK = 16


def _hi_lo(x):
    hi = x.astype(BF16).astype(F32)
    return hi, x - hi


def _dot_split(a, b):
    a_hi, a_lo = _hi_lo(a)
    b_hi, b_lo = _hi_lo(b)
    lhs = jnp.concatenate([a_hi, a_hi, a_lo], axis=1).astype(BF16)
    rhs = jnp.concatenate([b_hi, b_lo, b_hi], axis=0).astype(BF16)
    return jnp.dot(lhs, rhs, preferred_element_type=F32)


def _unit_triangular_inverses(ns):
    c = ns[0].shape[0]
    row = lax.broadcasted_iota(jnp.int32, (c, c), 0)
    col = lax.broadcasted_iota(jnp.int32, (c, c), 1)
    eye = (row == col).astype(F32)
    in_diag_block = (row // SOLVE_BLOCK) == (col // SOLVE_BLOCK)
    mm = lambda a_list, b_list: [_dot_split(a, b) for a, b in zip(a_list, b_list)]

    n_d = [jnp.where(in_diag_block, n, 0.0) for n in ns]
    x = n_d
    d_inv = [eye - n for n in n_d]
    for _ in range(SOLVE_BLOCK.bit_length() - 2):
        x = mm(x, x)
        d_inv = [d + dx for d, dx in zip(d_inv, mm(d_inv, x))]
    m = mm(d_inv, [n - nd for n, nd in zip(ns, n_d)])
    assert c // SOLVE_BLOCK == 4
    i_minus_m = [eye - mi for mi in m]
    q = [a + b for a, b in zip(i_minus_m, mm(i_minus_m, mm(m, m)))]
    return mm(q, d_inv)


def _gdn_kernel(qf_ref, qb_ref, gf_ref, gb_ref, na_ref, s0_ref, of_ref, ob_ref, sn_ref, s_scr):
    i = pl.program_id(1)

    @pl.when(i == 0)
    def _():
        s_scr[...] = s0_ref[...]

    nbat = qf_ref.shape[0]
    nqk = GD_HEADS * GD_QK
    masks = [_past_mask(d == 1) for d in range(2)]
    x_refs, g_refs, o_refs = (qf_ref, qb_ref), (gf_ref, gb_ref), (of_ref, ob_ref)
    bd = [(bb, d) for bb in range(nbat) for d in range(2)]
    gates = [g_refs[d][bb] for bb, d in bd]
    glog = [na_ref[...] * jax.nn.softplus(g) for g in gates]
    beta_all = [jax.nn.sigmoid(g) for g in gates]
    gcum = [_dot_hi(masks[d][0].astype(F32), glog[j]) for j, (bb, d) in enumerate(bd)]
    gtot = [jnp.sum(g, axis=0, keepdims=True) for g in glog]
    gcum_t = [_transpose_hi(g) for g in gcum]

    chains = [(bb, d, hd) for bb in range(nbat) for d in range(2) for hd in range(GD_HEADS)]
    q, k, v, g_col, beta, g_end, decay = [], [], [], [], [], [], []
    for bb, d, hd in chains:
        ca = GD_GATE0 + d * 8 + hd
        j = bb * 2 + d
        q.append(x_refs[d][bb, :, hd * GD_QK:(hd + 1) * GD_QK])
        k.append(x_refs[d][bb, :, nqk + hd * GD_QK:nqk + (hd + 1) * GD_QK])
        v.append(x_refs[d][bb, :, 2 * nqk + hd * GD_V:2 * nqk + (hd + 1) * GD_V])
        g_col.append(gcum[j][:, ca:ca + 1])
        beta.append(beta_all[j][:, ca + GD_HEADS:ca + GD_HEADS + 1])
        g_end.append(gtot[j][:, ca:ca + 1])
        decay.append(jnp.exp(jnp.where(masks[d][0], g_col[-1] - gcum_t[j][ca:ca + 1, :], -jnp.inf)))
    nc = range(len(chains))
    kk = [_dot_nt(k[c], k[c]) for c in nc]
    xs = [jnp.where(masks[chains[c][1]][1], beta[c] * kk[c] * decay[c], 0.0) for c in nc]
    ps = _unit_triangular_inverses(xs)
    uw = [_dot_split(ps[c], jnp.concatenate([v[c] * beta[c], k[c] * (beta[c] * jnp.exp(g_col[c]))], axis=-1))
          for c in nc]
    qk = [_dot_nt(q[c], k[c]) * decay[c] for c in nc]
    s_st = [s_scr[bb, d * GD_HEADS + hd] for bb, d, hd in chains]
    v_new = [uw[c][:, :GD_V] - _dot(uw[c][:, GD_V:], s_st[c]) for c in nc]
    o_loc = [_dot(q[c] * jnp.exp(g_col[c]), s_st[c]) for c in nc]
    o_new = [o_loc[c] + _dot(qk[c], v_new[c]) for c in nc]
    s_new = [s_st[c] * jnp.exp(g_end[c]) + _dot_tn(k[c] * jnp.exp(g_end[c] - g_col[c]), v_new[c]) for c in nc]
    for c, (bb, d, hd) in enumerate(chains):
        o_refs[d][bb, :, hd * GD_V:(hd + 1) * GD_V] = o_new[c]
        s_scr[bb, d * GD_HEADS + hd] = s_new[c]

    @pl.when(i == pl.num_programs(1) - 1)
    def _():
        sn_ref[...] = s_scr[...]


def _gdn(qkv_view, gates_view, neg_a, s0, nc, idx_fn, out_view_shape):
    b = qkv_view.shape[0]
    nbat = GD_SCAN_BATCH
    nqkv = 2 * GD_HEADS * GD_QK + GD_HEADS * GD_V
    hdim = GD_HEADS * GD_V
    fwd = lambda n: pl.BlockSpec((nbat, CHUNK, n), lambda bi, i: idx_fn(bi, i))
    bwd = lambda n: pl.BlockSpec((nbat, CHUNK, n), lambda bi, i: idx_fn(bi, nc - 1 - i))
    st = pl.BlockSpec((nbat, N_CHAINS, GD_QK, GD_V), lambda bi, i: (bi, 0, 0, 0))
    return pl.pallas_call(
        _gdn_kernel,
        grid=(b // nbat, nc),
        in_specs=[fwd(nqkv), bwd(nqkv), fwd(GATE_LANES), bwd(GATE_LANES),
                  pl.BlockSpec((1, GATE_LANES), lambda bi, i: (0, 0)), st],
        out_specs=[fwd(hdim), bwd(hdim), st],
        out_shape=[jax.ShapeDtypeStruct(out_view_shape, F32), jax.ShapeDtypeStruct(out_view_shape, F32),
                   jax.ShapeDtypeStruct(s0.shape, F32)],
        scratch_shapes=[pltpu.VMEM((nbat, N_CHAINS, GD_QK, GD_V), F32)],
        compiler_params=_cparams("parallel", "arbitrary"),
        name="gdn",
    )(qkv_view, qkv_view, gates_view, gates_view, neg_a, s0)


def _head_rms(t, nheads, width):
    outs = []
    for hd in range(nheads):
        th = t[:, hd * width:(hd + 1) * width]
        outs.append(th * lax.rsqrt(jnp.mean(th * th, axis=-1, keepdims=True) + EPS))
    return jnp.concatenate(outs, axis=-1)


def _rms(t, w):
    return t * lax.rsqrt(jnp.mean(t * t, axis=-1, keepdims=True) + EPS) * w


def _post_kernel(x_ref, hf_ref, hb_ref, og_ref, of_ref, ob_ref, z_ref, mlw_ref, gdw_ref, wout_ref,
                 npost_ref, g2_ref, npre_ref, sh_ref, sc_ref, rwt_ref, wsg_ref, wsu_ref, wsd_ref,
                 x1_ref, hffn_ref, lt_ref, ys_ref, of_scr, ob_scr):
    ml_y = _head_rms(hf_ref[0] + hb_ref[0], ML_HEADS, ML_V) * mlw_ref[...] * jax.nn.sigmoid(og_ref[0])
    _from_grid_view(of_ref, of_scr)
    _from_grid_view(ob_ref, ob_scr)
    o_sum = jnp.concatenate([_pitched_rows(of_scr, g) + _pitched_rows(ob_scr, g) for g in range(of_scr.shape[0])],
                            axis=-1)
    gd_y = _head_rms(o_sum, GD_HEADS, GD_V) * gdw_ref[...] * _silu(z_ref[0])
    y = _dot(jnp.concatenate([ml_y, gd_y], axis=-1), wout_ref[...])
    x1 = x_ref[0] + g2_ref[0] * _rms(y, npost_ref[...])
    x1_ref[0] = x1
    hffn = _rms(x1, npre_ref[...]) * (1.0 + sc_ref[0]) + sh_ref[0]
    nct = hffn.shape[1] // LANES
    for c in range(nct):
        hffn_ref[0, pl.ds(c, hffn.shape[0], stride=nct), :] = hffn[:, c * LANES:(c + 1) * LANES]
    hb = hffn.astype(BF16)
    lt_ref[...] = lax.dot_general(rwt_ref[...], hb, (((1,), (1,)), ((), ())), preferred_element_type=F32)
    hs = _silu(jnp.dot(hb, wsg_ref[...], preferred_element_type=F32)) * jnp.dot(hb, wsu_ref[...],
                                                                                preferred_element_type=F32)
    ys_ref[0] = _dot(hs, wsd_ref[...])


def _post(x, hf, hb, ml, of, ob, gz, mlw, gdw, wout, npost, g2, npre, sh, sc, rwt, wsg, wsu, wsd, tm):
    b, s, d = x.shape
    nt = s // tm
    hw = ML_HEADS * ML_V
    og_blk = (2 * ML_HEADS * ML_QK + ML_HEADS * ML_V) // hw
    tok = lambda n: pl.BlockSpec((1, tm, n), lambda bi, i: (bi, i, 0))
    full = lambda shp: pl.BlockSpec(shp, lambda bi, i: (0,) * len(shp))
    mod = pl.BlockSpec((1, 1, d), lambda bi, i: (bi, 0, 0))
    ne = rwt.shape[0]
    ds = wsg.shape[1]
    gview = pl.BlockSpec((1, tm // GRID_W, GRID_W * hw), lambda bi, i: (bi, i, 0))
    return pl.pallas_call(
        _post_kernel,
        grid=(b, nt),
        in_specs=[tok(d), tok(hw), tok(hw), pl.BlockSpec((1, tm, hw), lambda bi, i: (bi, i, og_blk)),
                  gview, gview, tok(hw), full((1, hw)), full((1, hw)), full((d, d)),
                  full((1, d)), mod, full((1, d)), mod, mod, full((ne, d)), full((d, ds)), full((d, ds)),
                  full((ds, d))],
        out_specs=[tok(d), pl.BlockSpec((1, tm * (d // LANES), LANES), lambda bi, i: (bi, i, 0)),
                   pl.BlockSpec((ne, tm), lambda bi, i: (0, bi * nt + i)), tok(d)],
        out_shape=[jax.ShapeDtypeStruct((b, s, d), F32), jax.ShapeDtypeStruct((b, s * (d // LANES), LANES), F32),
                   jax.ShapeDtypeStruct((ne, b * s), F32), jax.ShapeDtypeStruct((b, s, d), F32)],
        scratch_shapes=[pltpu.VMEM((hw // LANES, tm // GRID_W * GRID_PITCH, LANES), F32)] * 2,
        compiler_params=_cparams("parallel", "arbitrary"),
        name="post",
    )(x, hf, hb, ml, of, ob, gz, mlw, gdw, wout, npost, g2, npre, sh, sc, rwt, wsg, wsu, wsd)


def _route_kernel(lt_ref, bias_ref, idx_ref, gate_ref):
    ne, tn = lt_ref.shape
    gsz = ne // N_GROUPS
    scores = jax.nn.sigmoid(lt_ref[...])
    sel = scores + bias_ref[...]
    neg = -jnp.inf
    sel3 = sel.reshape(N_GROUPS, gsz, tn)
    io3 = lax.broadcasted_iota(jnp.int32, sel3.shape, 1)
    top1 = jnp.max(sel3, axis=1, keepdims=True)
    first = jnp.min(jnp.where(sel3 == top1, io3, gsz), axis=1, keepdims=True)
    top2 = jnp.max(jnp.where(io3 == first, neg, sel3), axis=1, keepdims=True)
    grp = (top1 + top2).reshape(N_GROUPS, tn)
    iog = lax.broadcasted_iota(jnp.int32, grp.shape, 0)
    keep = jnp.zeros(grp.shape, jnp.bool_)
    for _ in range(TOPK_GROUPS):
        m = jnp.max(grp, axis=0, keepdims=True)
        pick = iog == jnp.min(jnp.where(grp == m, iog, N_GROUPS), axis=0, keepdims=True)
        keep = keep | pick
        grp = jnp.where(pick, neg, grp)
    cand = jnp.where(keep.reshape(N_GROUPS, 1, tn), sel3, neg).reshape(ne, tn)
    ioe = lax.broadcasted_iota(jnp.int32, cand.shape, 0)
    idxs, gates = [], []
    for _ in range(TOP_K):
        m = jnp.max(cand, axis=0, keepdims=True)
        e = jnp.min(jnp.where(cand == m, ioe, ne), axis=0, keepdims=True)
        pick = ioe == e
        idxs.append(e)
        gates.append(jnp.sum(jnp.where(pick, scores, 0.0), axis=0, keepdims=True))
        cand = jnp.where(pick, neg, cand)
    gate = jnp.concatenate(gates, axis=0)
    idx_ref[...] = jnp.concatenate(idxs, axis=0)
    gate_ref[...] = gate / jnp.sum(gate, axis=0, keepdims=True) * ROUTED_SCALE


def _route(logits_t, bias_col, tn):
    ne, t = logits_t.shape
    return pl.pallas_call(
        _route_kernel,
        grid=(t // tn,),
        in_specs=[pl.BlockSpec((ne, tn), lambda i: (0, i)), pl.BlockSpec((ne, 1), lambda i: (0, 0))],
        out_specs=[pl.BlockSpec((TOP_K, tn), lambda i: (0, i)), pl.BlockSpec((TOP_K, tn), lambda i: (0, i))],
        out_shape=[jax.ShapeDtypeStruct((TOP_K, t), jnp.int32), jax.ShapeDtypeStruct((TOP_K, t), F32)],
        compiler_params=_cparams("parallel"),
        name="route",
    )(logits_t, bias_col)


def _experts_kernel(be_ref, np_ref,
                    tok_ref, tokn_ref, w_ref, wg0_ref, wu0_ref, wd0_ref, wg1_ref, wu1_ref, wd1_ref, h_hbm,
                    o_ref, xg, wgc, wuc, wdc, gsem):
    s = pl.program_id(0)
    n_pairs = np_ref[0]
    nct = xg.shape[1] // EXPERT_BLOCK
    rows = nct * EXPERT_BLOCK
    w_refs = ((wg0_ref, wu0_ref, wd0_ref), (wg1_ref, wu1_ref, wd1_ref))

    def gather_copy(tref, p, j):
        src = h_hbm.at[pl.ds(pl.multiple_of(tref[0, p, j] * nct, nct), nct)]
        return pltpu.make_async_copy(src, xg.at[p, pl.ds(j * nct, nct)], gsem.at[p])

    def gather_wait(p):
        pltpu.make_async_copy(h_hbm.at[pl.ds(0, rows)], xg.at[p], gsem.at[p]).wait()

    @pl.when(s >= n_pairs)
    def _():
        o_ref[...] = jnp.zeros(o_ref.shape, F32)

    @pl.when(s < n_pairs)
    def _():
        @pl.when(s == 0)
        def _():
            for p in range(2):
                for j in range(EXPERT_BLOCK):
                    gather_copy(tok_ref, p, j).start(priority=j % 2)

        for p in range(2):
            blk = 2 * s + p
            wg_ref, wu_ref, wd_ref = w_refs[p]
            gather_wait(p)

            @pl.when((blk == 0) | (be_ref[blk] != be_ref[jnp.maximum(blk - 1, 0)]))
            def _():
                wgc[...] = wg_ref[0].astype(BF16)
                wuc[...] = wu_ref[0].astype(BF16)
                wdc[...] = wd_ref[0].astype(BF16)

            xb = jnp.concatenate([xg[p, pl.ds(c, EXPERT_BLOCK, stride=nct), :] for c in range(nct)],
                                 axis=-1).astype(BF16)
            for j in range(EXPERT_BLOCK):
                gather_copy(tokn_ref, p, j).start(priority=j % 2)
            hmid = _silu(jnp.dot(xb, wgc[...], preferred_element_type=F32)) * jnp.dot(xb, wuc[...],
                                                                                      preferred_element_type=F32)
            out = _dot(hmid, wdc[...])
            eye = (lax.broadcasted_iota(jnp.int32, (EXPERT_BLOCK, EXPERT_BLOCK), 0)
                   == lax.broadcasted_iota(jnp.int32, (EXPERT_BLOCK, EXPERT_BLOCK), 1))
            w_col = jnp.sum(jnp.where(eye, w_ref[0, p:p + 1, :], 0.0), axis=1, keepdims=True)
            out = out * w_col
            for c in range(nct):
                o_ref[pl.ds(p * rows + c, EXPERT_BLOCK, stride=nct), :] = out[:, c * LANES:(c + 1) * LANES]

        @pl.when(s == n_pairs - 1)
        def _():
            for p in range(2):
                gather_wait(p)


def _experts(hffn, block_e, n_pairs, row_tok, row_w, wg, wu, wd):
    d = wg.shape[1]
    nct = d // LANES
    npairs = row_tok.shape[0]
    de = wg.shape[2]
    last = npairs - 1
    smem_blk = lambda f: pl.BlockSpec((1, 2, EXPERT_BLOCK), f, memory_space=pltpu.SMEM)
    wspec = lambda shp, p: pl.BlockSpec((1,) + shp, lambda s, be, npu: (be[2 * s + p], 0, 0))
    pair_rows = 2 * EXPERT_BLOCK * nct
    grid_spec = pltpu.PrefetchScalarGridSpec(
        num_scalar_prefetch=2,
        grid=(npairs,),
        in_specs=[smem_blk(lambda s, be, npu: (s, 0, 0)),
                  smem_blk(lambda s, be, npu: (jnp.minimum(s + 1, last), 0, 0)),
                  pl.BlockSpec((1, 2, EXPERT_BLOCK), lambda s, be, npu: (s, 0, 0)),
                  wspec((d, de), 0), wspec((d, de), 0), wspec((de, d), 0),
                  wspec((d, de), 1), wspec((d, de), 1), wspec((de, d), 1),
                  pl.BlockSpec(memory_space=pl.ANY)],
        out_specs=pl.BlockSpec((pair_rows, LANES), lambda s, be, npu: (s, 0)),
        scratch_shapes=[pltpu.VMEM((2, EXPERT_BLOCK * nct, LANES), F32),
                        pltpu.VMEM((d, de), BF16), pltpu.VMEM((d, de), BF16), pltpu.VMEM((de, d), BF16),
                        pltpu.SemaphoreType.DMA((2,))],
    )
    return pl.pallas_call(
        _experts_kernel,
        grid_spec=grid_spec,
        out_shape=jax.ShapeDtypeStruct((npairs * pair_rows, LANES), F32),
        compiler_params=_cparams("arbitrary"),
        name="experts",
    )(block_e, n_pairs, row_tok, row_tok, row_w, wg, wu, wd, wg, wu, wd, hffn)


def _combine_kernel(pos_ref, posn_ref, x1_ref, ys_ref, npost_ref, g5_ref, rows_hbm, o_ref, buf, sem):
    i = pl.program_id(0)
    tm = x1_ref.shape[1]
    nct = x1_ref.shape[2] // LANES
    cur = i % 2
    nxt = 1 - cur

    def copy(pref, b, k, t):
        src = rows_hbm.at[pl.ds(pl.multiple_of(pref[k, t] * nct, nct), nct)]
        dst = buf.at[b, pl.ds(pl.multiple_of((k * tm + t) * nct, nct), nct)]
        return pltpu.make_async_copy(src, dst, sem.at[b])

    def drain(b):
        pltpu.make_async_copy(rows_hbm.at[pl.ds(0, TOP_K * tm * nct)], buf.at[b], sem.at[b]).wait()

    def issue(pref, b):
        def body(t2, carry):
            for u in range(2):
                for k in range(TOP_K):
                    copy(pref, b, k, 2 * t2 + u).start(priority=k % 2)
            return carry
        lax.fori_loop(0, tm // 2, body, 0)

    @pl.when(i == 0)
    def _():
        issue(pos_ref, cur)

    @pl.when(i + 1 < pl.num_programs(0))
    def _():
        for t in range(tm):
            for k in range(TOP_K):
                copy(posn_ref, nxt, k, t).start(priority=k % 2)

    drain(cur)
    routed = []
    for c in range(nct):
        acc = buf[cur, pl.ds(c, tm, stride=nct), :]
        for k in range(1, TOP_K):
            acc = acc + buf[cur, pl.ds(k * tm * nct + c, tm, stride=nct), :]
        routed.append(acc)
    y = ys_ref[0] + jnp.concatenate(routed, axis=-1)
    o_ref[0] = x1_ref[0] + g5_ref[0] * _rms(y, npost_ref[...])


def _combine(x1, ys, rows, pos, npost, g5, tm):
    b, s, d = x1.shape
    nt = s // tm
    nct = d // LANES
    last = b * nt - 1
    tok = pl.BlockSpec((1, tm, d), lambda i: (i // nt, i % nt, 0))
    pos_blk = lambda f: pl.BlockSpec((TOP_K, tm), f, memory_space=pltpu.SMEM)
    return pl.pallas_call(
        _combine_kernel,
        grid=(b * nt,),
        in_specs=[pos_blk(lambda i: (0, i)), pos_blk(lambda i: (0, jnp.minimum(i + 1, last))), tok, tok,
                  pl.BlockSpec((1, d), lambda i: (0, 0)), pl.BlockSpec((1, 1, d), lambda i: (i // nt, 0, 0)),
                  pl.BlockSpec(memory_space=pl.ANY)],
        out_specs=tok,
        out_shape=jax.ShapeDtypeStruct((b, s, d), F32),
        scratch_shapes=[pltpu.VMEM((2, TOP_K * tm * nct, LANES), F32), pltpu.SemaphoreType.DMA((2,))],
        compiler_params=_cparams("arbitrary"),
        name="combine",
    )(pos, pos, x1, ys, npost, g5, rows)


def _dispatch_plan(idx_t, gate_t):
    k, t = idx_t.shape
    n_asg = k * t
    nb = n_asg // EXPERT_BLOCK + N_EXPERTS
    flat_e = idx_t.reshape(-1)
    id_bits = max(1, (n_asg - 1).bit_length())
    assert (N_EXPERTS - 1).bit_length() + id_bits <= 31
    packed = jnp.sort((flat_e << id_bits) | jnp.arange(n_asg, dtype=jnp.int32))
    order = packed & ((1 << id_bits) - 1)
    counts = jnp.zeros((N_EXPERTS,), jnp.int32).at[flat_e].add(1)
    padded = (counts + EXPERT_BLOCK - 1) // EXPERT_BLOCK * EXPERT_BLOCK
    start = jnp.cumsum(counts) - counts
    pend = jnp.cumsum(padded)
    pstart = pend - padded
    blk0 = jnp.arange(nb, dtype=jnp.int32) * EXPERT_BLOCK
    block_e = jnp.minimum(jnp.sum((pend[None, :] <= blk0[:, None]).astype(jnp.int32), axis=1), N_EXPERTS - 1)
    of_block = block_e[:, None] == jnp.arange(N_EXPERTS, dtype=jnp.int32)[None, :]
    per_block = lambda v: jnp.sum(jnp.where(of_block, v[None, :], 0), axis=1)
    assert nb % 2 == 0
    n_pairs = ((pend[-1] // EXPERT_BLOCK + 1) // 2).astype(jnp.int32).reshape(1)
    pos = blk0[:, None] - per_block(pstart)[:, None] + jnp.arange(EXPERT_BLOCK, dtype=jnp.int32)[None, :]
    valid = pos < per_block(counts)[:, None]
    src = jnp.clip(per_block(start)[:, None] + pos, 0, n_asg - 1)
    asg = order[src]
    row_tok = jnp.where(valid, asg % t, 0).astype(jnp.int32)
    row_w = jnp.where(valid, gate_t.reshape(-1)[asg], 0.0).astype(F32)
    i_sorted = jnp.arange(n_asg, dtype=jnp.int32)
    pad_before = jnp.sum(jnp.where(i_sorted[:, None] >= (start + counts)[None, :], (padded - counts)[None, :], 0),
                         axis=1)
    _, pos = lax.sort((order, i_sorted + pad_before), num_keys=1)
    pos = pos.reshape(k, t)
    shp = (nb // 2, 2, EXPERT_BLOCK)
    return block_e, n_pairs, row_tok.reshape(shp), row_w.reshape(shp), pos


def _pack_in_weights(w_in, ml_i_bias, ml_f_bias, gd_dt_bias):
    d = w_in.shape[0]
    nml = 2 * ML_HEADS * ML_QK + 2 * ML_HEADS * ML_V
    ml_cols = nml + 4 * ML_HEADS
    ngq = GD_HEADS * (2 * GD_QK + GD_V)
    ngz = GD_HEADS * GD_V
    wml = w_in[:, :nml].astype(BF16)
    wgq = w_in[:, ml_cols:ml_cols + ngq].astype(BF16)
    wgz = w_in[:, ml_cols + ngq:ml_cols + ngq + ngz].astype(BF16)
    wg = jnp.zeros((d, GATE_LANES), F32)
    wg = wg.at[:, ML_GATE0:ML_GATE0 + 16].set(w_in[:, nml:ml_cols])
    wg = wg.at[:, GD_GATE0:GD_GATE0 + 16].set(w_in[:, ml_cols + ngq + ngz:])
    gb = jnp.zeros((GATE_LANES,), F32)
    gb = gb.at[ML_GATE0:ML_GATE0 + 16].set(jnp.stack([ml_i_bias, ml_f_bias], axis=1).reshape(-1))
    gb = gb.at[GD_GATE0:GD_GATE0 + 16].set(jnp.stack([gd_dt_bias, jnp.zeros_like(gd_dt_bias)], axis=1).reshape(-1))
    return wml, wgq, wgz, wg.astype(BF16), gb.reshape(1, GATE_LANES)


def _mixer(x, ctx, mod, mod_ctx, norm_pre_mix, w_in, ml_i_bias, ml_f_bias, gd_conv_w, gd_a_log, gd_dt_bias):
    b, s, d = x.shape
    sc = ctx.shape[1]
    wml, wgq, wgz, wg, gb = _pack_in_weights(w_in, ml_i_bias, ml_f_bias, gd_dt_bias)
    nw = norm_pre_mix.reshape(1, d)
    ctx_mod = lambda j: jnp.broadcast_to(mod_ctx[j].reshape(1, 1, d), (b, 1, d))
    ml_c, gq_c, _, g_c = _proj(ctx, nw, ctx_mod(0), ctx_mod(1), wml, wgq, wgz, wg, gb, tm=sc, grid_view=False)
    ml_l, gqv_l, gz_l, g_l, gv_l = _proj(x, nw, mod[0], mod[1], wml, wgq, wgz, wg, gb, tm=TM_PROJ, grid_view=True)

    c0 = jnp.zeros((b, N_CHAINS, ML_QK, 2 * ML_V), F32)
    m0 = jnp.zeros((b, N_CHAINS, 1, 1), F32)
    _, _, c1, m1 = _mlstm(ml_c, g_c, c0, m0)
    hf, hb, _, _ = _mlstm(ml_l, g_l, c1, m1)

    neg_a = jnp.zeros((GATE_LANES,), F32)
    neg_a = neg_a.at[GD_GATE0:GD_GATE0 + 16].set(
        jnp.stack([-jnp.exp(gd_a_log), jnp.zeros_like(gd_a_log)], axis=1).reshape(-1)).reshape(1, GATE_LANES)
    qn_c = _gdconv_ctx(gq_c, gd_conv_w)
    qnv_l = _gdconv_lat(gqv_l, gd_conv_w)
    s0 = jnp.zeros((b, N_CHAINS, GD_QK, GD_V), F32)
    hdim = GD_HEADS * GD_V
    _, _, s1 = _gdn(qn_c, g_c, neg_a, s0, sc // CHUNK, lambda bi, n: (bi, n, 0), (b, sc, hdim))
    rows = s // GRID_W
    cpc = rows // CHUNK
    col_idx = lambda bi, n: (bi, n % cpc, n // cpc)
    ofv, obv, _ = _gdn(qnv_l, gv_l, neg_a, s1, s // CHUNK, col_idx, (b, rows, GRID_W * hdim))
    return hf, hb, ml_l, ofv, obv, gz_l


def kernel(x, c, ctx, c_ctx, w_ada, b_ada, norm_pre_mix, norm_post_mix, norm_pre_ffn, norm_post_ffn, w_in,
           ml_i_bias, ml_f_bias, ml_norm_w, gd_conv_w, gd_a_log, gd_dt_bias, gd_norm_w, w_out, router_w,
           router_bias, w_gate, w_up, w_down, ws_gate, ws_up, ws_down):
    b, s, d = x.shape
    depth = w_ada.shape[0]
    assert depth == 1, "the context stream update of deeper stacks is not implemented"
    ly = 0
    cc = jnp.zeros((16, d), F32).at[:b].set(c).at[b].set(c_ctx)
    mod_all = _ada(cc, w_ada[ly], b_ada[ly])
    mod = [mod_all[:b, j * d:(j + 1) * d].reshape(b, 1, d) for j in range(6)]
    mod_ctx = [mod_all[b, j * d:(j + 1) * d] for j in range(6)]

    hf, hb, ml_l, of, ob, gz_l = _mixer(x, ctx, mod, mod_ctx, norm_pre_mix[ly], w_in[ly], ml_i_bias[ly],
                                        ml_f_bias[ly], gd_conv_w[ly], gd_a_log[ly], gd_dt_bias[ly])

    row = lambda v: v.reshape(1, -1)
    x1, hffn, logits_t, ys = _post(
        x, hf, hb, ml_l, of, ob, gz_l, row(ml_norm_w[ly]), row(jnp.tile(gd_norm_w[ly], GD_HEADS)),
        w_out[ly].astype(BF16), row(norm_post_mix[ly]), mod[2], row(norm_pre_ffn[ly]), mod[3], mod[4],
        router_w[ly].T.astype(BF16), ws_gate[ly].astype(BF16), ws_up[ly].astype(BF16), ws_down[ly].astype(BF16),
        tm=TM_POST)

    idx_t, gate_t = _route(logits_t, router_bias[ly].reshape(-1, 1), tn=TN_ROUTE)
    block_e, n_pairs, row_tok, row_w, pos = _dispatch_plan(idx_t, gate_t)
    t = b * s
    rows = _experts(hffn.reshape(t * (d // LANES), LANES), block_e, n_pairs, row_tok, row_w,
                    w_gate[ly], w_up[ly], w_down[ly])
    return _combine(x1, ys, rows, pos, row(norm_post_ffn[ly]), mod[5], tm=TM_COMBINE)
```

```python
import functools

import jax
import jax.numpy as jnp
from jax import lax
from jax.experimental import pallas as pl
from jax.experimental.pallas import tpu as pltpu

EPS = 1e-6
CHUNK = 64
GRID_W = 64
ML_HEADS, ML_QK, ML_V = 4, 64, 128
GD_HEADS, GD_QK, GD_V = 4, 128, 128
CONV_W = 5
N_EXPERTS, TOP_K, N_GROUPS, TOPK_GROUPS = 256, 8, 8, 4
ROUTED_SCALE = 2.5
EXPERT_BLOCK = 128
N_CHAINS = 8
ML_SCAN_BATCH, GD_SCAN_BATCH = 1, 4
LANES = 128
GATE_LANES = LANES
ML_GATE0, GD_GATE0 = 0, 16

F32 = jnp.float32
BF16 = jnp.bfloat16
HI = lax.Precision.HIGHEST
VMEM_LIMIT = 56 * 1024 * 1024
TM_PROJ, TM_POST, TM_COMBINE, TN_ROUTE, TN_ADA = 512, 512, 256, 512, 1536


def _cparams(*sem):
    return pltpu.CompilerParams(dimension_semantics=sem, vmem_limit_bytes=VMEM_LIMIT)


def _dot(a, b):
    return jnp.dot(a.astype(BF16), b.astype(BF16), preferred_element_type=F32)


def _dot_nt(a, b):
    return lax.dot_general(a.astype(BF16), b.astype(BF16), (((1,), (1,)), ((), ())), preferred_element_type=F32)


def _dot_tn(a, b):
    return lax.dot_general(a.astype(BF16), b.astype(BF16), (((0,), (0,)), ((), ())), preferred_element_type=F32)


def _dot_hi(a, b):
    return jnp.dot(a, b, precision=HI, preferred_element_type=F32)


def _dot_nt_hi(a, b):
    return lax.dot_general(a, b, (((1,), (1,)), ((), ())), precision=HI, preferred_element_type=F32)


def _transpose_hi(x):
    n = x.shape[1]
    eye = (lax.broadcasted_iota(jnp.int32, (n, n), 0) == lax.broadcasted_iota(jnp.int32, (n, n), 1)).astype(F32)
    return _dot_nt_hi(eye, x)


def _silu(x):
    return x * jax.nn.sigmoid(x)


def _past_mask(reverse):
    t = lax.broadcasted_iota(jnp.int32, (CHUNK, CHUNK), 0)
    s = lax.broadcasted_iota(jnp.int32, (CHUNK, CHUNK), 1)
    return (s >= t, s > t) if reverse else (s <= t, s < t)


def _ada_kernel(c_ref, w_ref, b_ref, o_ref):
    o_ref[...] = _dot(_silu(c_ref[...]), w_ref[...]) + b_ref[...]


def _ada(cc, w_ada, b_ada):
    rows, d = cc.shape
    n = w_ada.shape[1]
    tn = TN_ADA
    return pl.pallas_call(
        _ada_kernel,
        grid=(n // tn,),
        in_specs=[pl.BlockSpec((rows, d), lambda j: (0, 0)),
                  pl.BlockSpec((d, tn), lambda j: (0, j)),
                  pl.BlockSpec((1, tn), lambda j: (0, j))],
        out_specs=pl.BlockSpec((rows, tn), lambda j: (0, j)),
        out_shape=jax.ShapeDtypeStruct((rows, n), F32),
        compiler_params=_cparams("arbitrary"),
        name="ada",
    )(cc, w_ada, b_ada.reshape(1, n))


GRID_PITCH = GRID_W + 8


def _to_grid_view(src_ref, dst_ref):
    ng = src_ref.shape[0]
    r = src_ref.shape[1] // GRID_PITCH
    for c in range(GRID_W):
        for g in range(ng):
            lo = (c * ng + g) * LANES
            dst_ref[0, :, lo:lo + LANES] = src_ref[g, pl.ds(c, r, stride=GRID_PITCH), :]


def _from_grid_view(src_ref, dst_ref):
    ng = dst_ref.shape[0]
    r = dst_ref.shape[1] // GRID_PITCH
    for c in range(GRID_W):
        for g in range(ng):
            lo = (c * ng + g) * LANES
            dst_ref[g, pl.ds(c, r, stride=GRID_PITCH), :] = src_ref[0, :, lo:lo + LANES]


def _pitched_rows(ref, g):
    r = ref.shape[1] // GRID_PITCH
    return jnp.concatenate([ref[g, i * GRID_PITCH:i * GRID_PITCH + GRID_W, :] for i in range(r)], axis=0)


def _proj_kernel(grid_view, x_ref, nw_ref, sh_ref, sc_ref, wml_ref, wgq_ref, wgz_ref, wg_ref, gb_ref, *refs):
    if grid_view:
        ml_ref, gqv_ref, gz_ref, g_ref, gv_ref, gq_scr, g_scr = refs
    else:
        ml_ref, gq_ref, gz_ref, g_ref = refs
    x = x_ref[0]
    xn = x * lax.rsqrt(jnp.mean(x * x, axis=-1, keepdims=True) + EPS) * nw_ref[...]
    h = (xn * (1.0 + sc_ref[0]) + sh_ref[0]).astype(BF16)
    ml_ref[0] = jnp.dot(h, wml_ref[...], preferred_element_type=F32)
    gz_ref[0] = jnp.dot(h, wgz_ref[...], preferred_element_type=F32)
    gates = jnp.dot(h, wg_ref[...], preferred_element_type=F32) + gb_ref[...]
    g_ref[0] = gates
    gq = jnp.dot(h, wgq_ref[...], preferred_element_type=F32)
    if grid_view:
        for r in range(x.shape[0] // GRID_W):
            rows = slice(r * GRID_W, (r + 1) * GRID_W)
            prow = slice(r * GRID_PITCH, r * GRID_PITCH + GRID_W)
            g_scr[0, prow, :] = gates[rows]
            for g in range(gq_scr.shape[0]):
                gq_scr[g, prow, :] = gq[rows, g * LANES:(g + 1) * LANES]
        _to_grid_view(gq_scr, gqv_ref)
        _to_grid_view(g_scr, gv_ref)
    else:
        gq_ref[0] = gq


def _proj(x, norm_w, shift, scale, wml, wgq, wgz, wg, gbias, tm, grid_view):
    b, s, d = x.shape
    nml, ngq, ngz = wml.shape[1], wgq.shape[1], wgz.shape[1]
    full = lambda shp: pl.BlockSpec(shp, lambda bi, i: (0,) * len(shp))
    tok = lambda n: pl.BlockSpec((1, tm, n), lambda bi, i: (bi, i, 0))
    mod = pl.BlockSpec((1, 1, d), lambda bi, i: (bi, 0, 0))
    if grid_view:
        rt = tm // GRID_W
        view = lambda n: pl.BlockSpec((1, rt, GRID_W * n), lambda bi, i: (bi, i, 0))
        vshape = lambda n: jax.ShapeDtypeStruct((b, s // GRID_W, GRID_W * n), F32)
        out_specs = [tok(nml), view(ngq), tok(ngz), tok(GATE_LANES), view(GATE_LANES)]
        out_shape = [jax.ShapeDtypeStruct((b, s, nml), F32), vshape(ngq), jax.ShapeDtypeStruct((b, s, ngz), F32),
                     jax.ShapeDtypeStruct((b, s, GATE_LANES), F32), vshape(GATE_LANES)]
        scratch = [pltpu.VMEM((ngq // LANES, rt * GRID_PITCH, LANES), F32), pltpu.VMEM((1, rt * GRID_PITCH, LANES), F32)]
    else:
        out_specs = [tok(nml), tok(ngq), tok(ngz), tok(GATE_LANES)]
        out_shape = [jax.ShapeDtypeStruct((b, s, n), F32) for n in (nml, ngq, ngz, GATE_LANES)]
        scratch = []
    return pl.pallas_call(
        functools.partial(_proj_kernel, grid_view),
        grid=(b, s // tm),
        in_specs=[tok(d), full((1, d)), mod, mod, full((d, nml)), full((d, ngq)), full((d, ngz)),
                  full((d, GATE_LANES)), full((1, GATE_LANES))],
        out_specs=out_specs,
        out_shape=out_shape,
        scratch_shapes=scratch,
        compiler_params=_cparams("parallel", "arbitrary"),
        name="proj",
    )(x, norm_w, shift, scale, wml, wgq, wgz, wg, gbias)


def _mlstm_kernel(mlf_ref, mlb_ref, gf_ref, gb_ref, c0_ref, m0_ref, hf_ref, hb_ref, cn_ref, mn_ref, c_scr, m_scr):
    i = pl.program_id(1)

    @pl.when(i == 0)
    def _():
        c_scr[...] = c0_ref[...]
        m_scr[...] = m0_ref[...]

    nbat = mlf_ref.shape[0]
    past = [_past_mask(d == 1)[0] for d in range(2)]
    ml_refs, g_refs, h_refs = (mlf_ref, mlb_ref), (gf_ref, gb_ref), (hf_ref, hb_ref)
    bd = [(bb, d) for bb in range(nbat) for d in range(2)]
    g = [g_refs[d][bb] for bb, d in bd]
    ls = [jax.nn.log_sigmoid(x) for x in g]
    bcol = [_dot_hi(past[d].astype(F32), ls[j]) for j, (bb, d) in enumerate(bd)]
    tot = [jnp.sum(x, axis=0, keepdims=True) for x in ls]
    g_t = [_transpose_hi(x) for x in g]
    b_t = [_transpose_hi(x) for x in bcol]

    chains = [(bb, d, hd) for bb in range(nbat) for d in range(2) for hd in range(ML_HEADS)]
    nc = range(len(chains))
    k0, v0 = ML_HEADS * ML_QK, 2 * ML_HEADS * ML_QK
    ones_col = (lax.broadcasted_iota(jnp.int32, (CHUNK, ML_V), 1) == 0).astype(F32)
    q, k, v, i_col, b_col, b_end, log_d = [], [], [], [], [], [], []
    for bb, d, hd in chains:
        ci = ML_GATE0 + d * 8 + hd
        cf = ci + ML_HEADS
        j = bb * 2 + d
        q.append(ml_refs[d][bb, :, hd * ML_QK:(hd + 1) * ML_QK])
        k.append(ml_refs[d][bb, :, k0 + hd * ML_QK:k0 + (hd + 1) * ML_QK] * (ML_QK ** -0.5))
        v.append(jnp.concatenate([ml_refs[d][bb, :, v0 + hd * ML_V:v0 + (hd + 1) * ML_V], ones_col], axis=-1))
        i_col.append(g[j][:, ci:ci + 1])
        b_col.append(bcol[j][:, cf:cf + 1])
        b_end.append(tot[j][:, cf:cf + 1])
        log_d.append(jnp.where(past[d], b_col[-1] - b_t[j][cf:cf + 1, :] + g_t[j][ci:ci + 1, :], -jnp.inf))
    st_idx = [(bb, d * ML_HEADS + hd) for bb, d, hd in chains]
    c_st = [c_scr[ix] for ix in st_idx]
    m_st = [m_scr[ix] for ix in st_idx]
    log_prev = [b_col[c] + m_st[c] for c in nc]
    m_t = [jnp.maximum(log_prev[c], jnp.max(log_d[c], axis=-1, keepdims=True)) for c in nc]
    qk = [_dot_nt(q[c], k[c]) for c in nc]
    qc = [_dot(q[c], c_st[c]) for c in nc]
    s = [qk[c] * jnp.exp(log_d[c] - m_t[c]) for c in nc]
    w_prev = [jnp.exp(log_prev[c] - m_t[c]) for c in nc]
    sv = [_dot(s[c], v[c]) for c in nc]
    log_s = [b_end[c] - b_col[c] + i_col[c] for c in nc]
    m_new = [jnp.maximum(b_end[c] + m_st[c], jnp.max(log_s[c], axis=0, keepdims=True)) for c in nc]
    kw = [k[c] * jnp.exp(log_s[c] - m_new[c]) for c in nc]
    w_c = [jnp.exp(b_end[c] + m_st[c] - m_new[c]) for c in nc]
    kv = [_dot_tn(kw[c], v[c]) for c in nc]
    numden = [sv[c] + w_prev[c] * qc[c] for c in nc]
    scale = [1.0 / jnp.maximum(jnp.abs(numden[c][:, ML_V:ML_V + 1]), jnp.exp(-m_t[c])) for c in nc]
    for c, (bb, d, hd) in enumerate(chains):
        h_refs[d][bb, :, hd * ML_V:(hd + 1) * ML_V] = numden[c][:, :ML_V] * scale[c]
        c_scr[st_idx[c]] = w_c[c] * c_st[c] + kv[c]
        m_scr[st_idx[c]] = m_new[c]

    @pl.when(i == pl.num_programs(1) - 1)
    def _():
        cn_ref[...] = c_scr[...]
        mn_ref[...] = m_scr[...]


def _mlstm(ml, gates, c0, m0):
    b, s, nml = ml.shape
    nc = s // CHUNK
    nbat = ML_SCAN_BATCH
    fwd = lambda n: pl.BlockSpec((nbat, CHUNK, n), lambda bi, i: (bi, i, 0))
    bwd = lambda n: pl.BlockSpec((nbat, CHUNK, n), lambda bi, i: (bi, nc - 1 - i, 0))
    st = lambda shp: pl.BlockSpec((nbat,) + shp, lambda bi, i: (bi,) + (0,) * len(shp))
    hdim = ML_HEADS * ML_V
    cshape = (N_CHAINS, ML_QK, 2 * ML_V)
    return pl.pallas_call(
        _mlstm_kernel,
        grid=(b // nbat, nc),
        in_specs=[fwd(nml), bwd(nml), fwd(GATE_LANES), bwd(GATE_LANES), st(cshape), st((N_CHAINS, 1, 1))],
        out_specs=[fwd(hdim), bwd(hdim), st(cshape), st((N_CHAINS, 1, 1))],
        out_shape=[jax.ShapeDtypeStruct((b, s, hdim), F32), jax.ShapeDtypeStruct((b, s, hdim), F32),
                   jax.ShapeDtypeStruct(c0.shape, F32), jax.ShapeDtypeStruct(m0.shape, F32)],
        scratch_shapes=[pltpu.VMEM((nbat,) + cshape, F32), pltpu.VMEM((nbat, N_CHAINS, 1, 1), F32)],
        compiler_params=_cparams("parallel", "arbitrary"),
        name="mlstm",
    )(ml, ml, gates, gates, c0, m0)


def _gdconv_kernel(has_halo, *refs):
    if has_halo:
        x_ref, prev_ref, next_ref, w_ref, o_ref, xp_ref = refs
    else:
        x_ref, w_ref, o_ref, xp_ref = refs
    rows = x_ref.shape[1]
    nch = x_ref.shape[2]
    pad = 8
    zero = jnp.zeros((pad, nch), F32)
    if has_halo:
        c = pl.program_id(1)
        xp_ref[0:pad, :] = jnp.where(c > 0, prev_ref[0], zero)
        xp_ref[pad + rows:, :] = jnp.where(c < pl.num_programs(1) - 1, next_ref[0], zero)
    else:
        xp_ref[0:pad, :] = zero
        xp_ref[pad + rows:, :] = zero
    xp_ref[pad:pad + rows, :] = x_ref[0]
    half = CONV_W // 2
    for lc in range(nch // 128):
        sl = slice(lc * 128, (lc + 1) * 128)
        acc = None
        for j in range(CONV_W):
            term = xp_ref[pad - half + j:pad - half + j + rows, sl] * w_ref[j:j + 1, sl]
            acc = term if acc is None else acc + term
        y = _silu(acc)
        if lc < 2 * GD_HEADS:
            y = y * lax.rsqrt(jnp.sum(y * y, axis=-1, keepdims=True) + EPS)
        if lc < GD_HEADS:
            y = y * (GD_QK ** -0.5)
        o_ref[0, :, sl] = y


def _gdconv_ctx(qkv, conv_w):
    b, s, nch = qkv.shape
    return pl.pallas_call(
        functools.partial(_gdconv_kernel, False),
        grid=(b,),
        in_specs=[pl.BlockSpec((1, s, nch), lambda bi: (bi, 0, 0)), pl.BlockSpec((CONV_W, nch), lambda bi: (0, 0))],
        out_specs=pl.BlockSpec((1, s, nch), lambda bi: (bi, 0, 0)),
        out_shape=jax.ShapeDtypeStruct((b, s, nch), F32),
        scratch_shapes=[pltpu.VMEM((s + 16, nch), F32)],
        compiler_params=_cparams("parallel"),
        name="gdconv_ctx",
    )(qkv, conv_w)


def _gdconv_lat(view, conv_w):
    b, rows, wn = view.shape
    nch = wn // GRID_W
    rb = rows // 8
    return pl.pallas_call(
        functools.partial(_gdconv_kernel, True),
        grid=(b, GRID_W),
        in_specs=[pl.BlockSpec((1, rows, nch), lambda bi, c: (bi, 0, c)),
                  pl.BlockSpec((1, 8, nch), lambda bi, c: (bi, rb - 1, jnp.maximum(c - 1, 0))),
                  pl.BlockSpec((1, 8, nch), lambda bi, c: (bi, 0, jnp.minimum(c + 1, GRID_W - 1))),
                  pl.BlockSpec((CONV_W, nch), lambda bi, c: (0, 0))],
        out_specs=pl.BlockSpec((1, rows, nch), lambda bi, c: (bi, 0, c)),
        out_shape=jax.ShapeDtypeStruct(view.shape, F32),
        scratch_shapes=[pltpu.VMEM((rows + 16, nch), F32)],
        compiler_params=_cparams("parallel", "arbitrary"),
        name="gdconv_lat",
    )(view, view, view, conv_w)


SOLVE_BLOCK = 16


def _hi_lo(x):
    hi = x.astype(BF16).astype(F32)
    return hi, x - hi


def _dot_split(a, b):
    a_hi, a_lo = _hi_lo(a)
    b_hi, b_lo = _hi_lo(b)
    lhs = jnp.concatenate([a_hi, a_hi, a_lo], axis=1).astype(BF16)
    rhs = jnp.concatenate([b_hi, b_lo, b_hi], axis=0).astype(BF16)
    return jnp.dot(lhs, rhs, preferred_element_type=F32)


def _unit_triangular_inverses(ns):
    c = ns[0].shape[0]
    row = lax.broadcasted_iota(jnp.int32, (c, c), 0)
    col = lax.broadcasted_iota(jnp.int32, (c, c), 1)
    eye = (row == col).astype(F32)
    in_diag_block = (row // SOLVE_BLOCK) == (col // SOLVE_BLOCK)
    mm = lambda a_list, b_list: [_dot_split(a, b) for a, b in zip(a_list, b_list)]

    n_d = [jnp.where(in_diag_block, n, 0.0) for n in ns]
    x = n_d
    d_inv = [eye - n for n in n_d]
    for _ in range(SOLVE_BLOCK.bit_length() - 2):
        x = mm(x, x)
        d_inv = [d + dx for d, dx in zip(d_inv, mm(d_inv, x))]
    m = mm(d_inv, [n - nd for n, nd in zip(ns, n_d)])
    assert c // SOLVE_BLOCK == 4
    i_minus_m = [eye - mi for mi in m]
    q = [a + b for a, b in zip(i_minus_m, mm(i_minus_m, mm(m, m)))]
    return mm(q, d_inv)


def _gdn_kernel(qf_ref, qb_ref, gf_ref, gb_ref, na_ref, s0_ref, of_ref, ob_ref, sn_ref, s_scr):
    i = pl.program_id(1)

    @pl.when(i == 0)
    def _():
        s_scr[...] = s0_ref[...]

    nbat = qf_ref.shape[0]
    nqk = GD_HEADS * GD_QK
    masks = [_past_mask(d == 1) for d in range(2)]
    x_refs, g_refs, o_refs = (qf_ref, qb_ref), (gf_ref, gb_ref), (of_ref, ob_ref)
    bd = [(bb, d) for bb in range(nbat) for d in range(2)]
    gates = [g_refs[d][bb] for bb, d in bd]
    glog = [na_ref[...] * jax.nn.softplus(g) for g in gates]
    beta_all = [jax.nn.sigmoid(g) for g in gates]
    gcum = [_dot_hi(masks[d][0].astype(F32), glog[j]) for j, (bb, d) in enumerate(bd)]
    gtot = [jnp.sum(g, axis=0, keepdims=True) for g in glog]
    gcum_t = [_transpose_hi(g) for g in gcum]

    chains = [(bb, d, hd) for bb in range(nbat) for d in range(2) for hd in range(GD_HEADS)]
    q, k, v, g_col, beta, g_end, decay = [], [], [], [], [], [], []
    for bb, d, hd in chains:
        ca = GD_GATE0 + d * 8 + hd
        j = bb * 2 + d
        q.append(x_refs[d][bb, :, hd * GD_QK:(hd + 1) * GD_QK])
        k.append(x_refs[d][bb, :, nqk + hd * GD_QK:nqk + (hd + 1) * GD_QK])
        v.append(x_refs[d][bb, :, 2 * nqk + hd * GD_V:2 * nqk + (hd + 1) * GD_V])
        g_col.append(gcum[j][:, ca:ca + 1])
        beta.append(beta_all[j][:, ca + GD_HEADS:ca + GD_HEADS + 1])
        g_end.append(gtot[j][:, ca:ca + 1])
        decay.append(jnp.exp(jnp.where(masks[d][0], g_col[-1] - gcum_t[j][ca:ca + 1, :], -jnp.inf)))
    nc = range(len(chains))
    kk = [_dot_nt(k[c], k[c]) for c in nc]
    xs = [jnp.where(masks[chains[c][1]][1], beta[c] * kk[c] * decay[c], 0.0) for c in nc]
    ps = _unit_triangular_inverses(xs)
    uw = [_dot_split(ps[c], jnp.concatenate([v[c] * beta[c], k[c] * (beta[c] * jnp.exp(g_col[c]))], axis=-1))
          for c in nc]
    qk = [_dot_nt(q[c], k[c]) * decay[c] for c in nc]
    s_st = [s_scr[bb, d * GD_HEADS + hd] for bb, d, hd in chains]
    v_new = [uw[c][:, :GD_V] - _dot(uw[c][:, GD_V:], s_st[c]) for c in nc]
    o_loc = [_dot(q[c] * jnp.exp(g_col[c]), s_st[c]) for c in nc]
    o_new = [o_loc[c] + _dot(qk[c], v_new[c]) for c in nc]
    s_new = [s_st[c] * jnp.exp(g_end[c]) + _dot_tn(k[c] * jnp.exp(g_end[c] - g_col[c]), v_new[c]) for c in nc]
    for c, (bb, d, hd) in enumerate(chains):
        o_refs[d][bb, :, hd * GD_V:(hd + 1) * GD_V] = o_new[c]
        s_scr[bb, d * GD_HEADS + hd] = s_new[c]

    @pl.when(i == pl.num_programs(1) - 1)
    def _():
        sn_ref[...] = s_scr[...]


def _gdn(qkv_view, gates_view, neg_a, s0, nc, idx_fn, out_view_shape):
    b = qkv_view.shape[0]
    nbat = GD_SCAN_BATCH
    nqkv = 2 * GD_HEADS * GD_QK + GD_HEADS * GD_V
    hdim = GD_HEADS * GD_V
    fwd = lambda n: pl.BlockSpec((nbat, CHUNK, n), lambda bi, i: idx_fn(bi, i))
    bwd = lambda n: pl.BlockSpec((nbat, CHUNK, n), lambda bi, i: idx_fn(bi, nc - 1 - i))
    st = pl.BlockSpec((nbat, N_CHAINS, GD_QK, GD_V), lambda bi, i: (bi, 0, 0, 0))
    return pl.pallas_call(
        _gdn_kernel,
        grid=(b // nbat, nc),
        in_specs=[fwd(nqkv), bwd(nqkv), fwd(GATE_LANES), bwd(GATE_LANES),
                  pl.BlockSpec((1, GATE_LANES), lambda bi, i: (0, 0)), st],
        out_specs=[fwd(hdim), bwd(hdim), st],
        out_shape=[jax.ShapeDtypeStruct(out_view_shape, F32), jax.ShapeDtypeStruct(out_view_shape, F32),
                   jax.ShapeDtypeStruct(s0.shape, F32)],
        scratch_shapes=[pltpu.VMEM((nbat, N_CHAINS, GD_QK, GD_V), F32)],
        compiler_params=_cparams("parallel", "arbitrary"),
        name="gdn",
    )(qkv_view, qkv_view, gates_view, gates_view, neg_a, s0)


def _head_rms(t, nheads, width):
    outs = []
    for hd in range(nheads):
        th = t[:, hd * width:(hd + 1) * width]
        outs.append(th * lax.rsqrt(jnp.mean(th * th, axis=-1, keepdims=True) + EPS))
    return jnp.concatenate(outs, axis=-1)


def _rms(t, w):
    return t * lax.rsqrt(jnp.mean(t * t, axis=-1, keepdims=True) + EPS) * w


def _post_kernel(x_ref, hf_ref, hb_ref, og_ref, of_ref, ob_ref, z_ref, mlw_ref, gdw_ref, wout_ref,
                 npost_ref, g2_ref, npre_ref, sh_ref, sc_ref, rwt_ref, wsg_ref, wsu_ref, wsd_ref,
                 x1_ref, hffn_ref, lt_ref, ys_ref, of_scr, ob_scr):
    ml_y = _head_rms(hf_ref[0] + hb_ref[0], ML_HEADS, ML_V) * mlw_ref[...] * jax.nn.sigmoid(og_ref[0])
    _from_grid_view(of_ref, of_scr)
    _from_grid_view(ob_ref, ob_scr)
    o_sum = jnp.concatenate([_pitched_rows(of_scr, g) + _pitched_rows(ob_scr, g) for g in range(of_scr.shape[0])],
                            axis=-1)
    gd_y = _head_rms(o_sum, GD_HEADS, GD_V) * gdw_ref[...] * _silu(z_ref[0])
    y = _dot(jnp.concatenate([ml_y, gd_y], axis=-1), wout_ref[...])
    x1 = x_ref[0] + g2_ref[0] * _rms(y, npost_ref[...])
    x1_ref[0] = x1
    hffn = _rms(x1, npre_ref[...]) * (1.0 + sc_ref[0]) + sh_ref[0]
    nct = hffn.shape[1] // LANES
    for c in range(nct):
        hffn_ref[0, pl.ds(c, hffn.shape[0], stride=nct), :] = hffn[:, c * LANES:(c + 1) * LANES]
    hb = hffn.astype(BF16)
    lt_ref[...] = lax.dot_general(rwt_ref[...], hb, (((1,), (1,)), ((), ())), preferred_element_type=F32)
    hs = _silu(jnp.dot(hb, wsg_ref[...], preferred_element_type=F32)) * jnp.dot(hb, wsu_ref[...],
                                                                                preferred_element_type=F32)
    ys_ref[0] = _dot(hs, wsd_ref[...])


def _post(x, hf, hb, ml, of, ob, gz, mlw, gdw, wout, npost, g2, npre, sh, sc, rwt, wsg, wsu, wsd, tm):
    b, s, d = x.shape
    nt = s // tm
    hw = ML_HEADS * ML_V
    og_blk = (2 * ML_HEADS * ML_QK + ML_HEADS * ML_V) // hw
    tok = lambda n: pl.BlockSpec((1, tm, n), lambda bi, i: (bi, i, 0))
    full = lambda shp: pl.BlockSpec(shp, lambda bi, i: (0,) * len(shp))
    mod = pl.BlockSpec((1, 1, d), lambda bi, i: (bi, 0, 0))
    ne = rwt.shape[0]
    ds = wsg.shape[1]
    gview = pl.BlockSpec((1, tm // GRID_W, GRID_W * hw), lambda bi, i: (bi, i, 0))
    return pl.pallas_call(
        _post_kernel,
        grid=(b, nt),
        in_specs=[tok(d), tok(hw), tok(hw), pl.BlockSpec((1, tm, hw), lambda bi, i: (bi, i, og_blk)),
                  gview, gview, tok(hw), full((1, hw)), full((1, hw)), full((d, d)),
                  full((1, d)), mod, full((1, d)), mod, mod, full((ne, d)), full((d, ds)), full((d, ds)),
                  full((ds, d))],
        out_specs=[tok(d), pl.BlockSpec((1, tm * (d // LANES), LANES), lambda bi, i: (bi, i, 0)),
                   pl.BlockSpec((ne, tm), lambda bi, i: (0, bi * nt + i)), tok(d)],
        out_shape=[jax.ShapeDtypeStruct((b, s, d), F32), jax.ShapeDtypeStruct((b, s * (d // LANES), LANES), F32),
                   jax.ShapeDtypeStruct((ne, b * s), F32), jax.ShapeDtypeStruct((b, s, d), F32)],
        scratch_shapes=[pltpu.VMEM((hw // LANES, tm // GRID_W * GRID_PITCH, LANES), F32)] * 2,
        compiler_params=_cparams("parallel", "arbitrary"),
        name="post",
    )(x, hf, hb, ml, of, ob, gz, mlw, gdw, wout, npost, g2, npre, sh, sc, rwt, wsg, wsu, wsd)


def _route_kernel(lt_ref, bias_ref, idx_ref, gate_ref):
    ne, tn = lt_ref.shape
    gsz = ne // N_GROUPS
    scores = jax.nn.sigmoid(lt_ref[...])
    sel = scores + bias_ref[...]
    neg = -jnp.inf
    sel3 = sel.reshape(N_GROUPS, gsz, tn)
    io3 = lax.broadcasted_iota(jnp.int32, sel3.shape, 1)
    top1 = jnp.max(sel3, axis=1, keepdims=True)
    first = jnp.min(jnp.where(sel3 == top1, io3, gsz), axis=1, keepdims=True)
    top2 = jnp.max(jnp.where(io3 == first, neg, sel3), axis=1, keepdims=True)
    grp = (top1 + top2).reshape(N_GROUPS, tn)
    iog = lax.broadcasted_iota(jnp.int32, grp.shape, 0)
    keep = jnp.zeros(grp.shape, jnp.bool_)
    for _ in range(TOPK_GROUPS):
        m = jnp.max(grp, axis=0, keepdims=True)
        pick = iog == jnp.min(jnp.where(grp == m, iog, N_GROUPS), axis=0, keepdims=True)
        keep = keep | pick
        grp = jnp.where(pick, neg, grp)
    cand = jnp.where(keep.reshape(N_GROUPS, 1, tn), sel3, neg).reshape(ne, tn)
    ioe = lax.broadcasted_iota(jnp.int32, cand.shape, 0)
    idxs, gates = [], []
    for _ in range(TOP_K):
        m = jnp.max(cand, axis=0, keepdims=True)
        e = jnp.min(jnp.where(cand == m, ioe, ne), axis=0, keepdims=True)
        pick = ioe == e
        idxs.append(e)
        gates.append(jnp.sum(jnp.where(pick, scores, 0.0), axis=0, keepdims=True))
        cand = jnp.where(pick, neg, cand)
    gate = jnp.concatenate(gates, axis=0)
    idx_ref[...] = jnp.concatenate(idxs, axis=0)
    gate_ref[...] = gate / jnp.sum(gate, axis=0, keepdims=True) * ROUTED_SCALE


def _route(logits_t, bias_col, tn):
    ne, t = logits_t.shape
    return pl.pallas_call(
        _route_kernel,
        grid=(t // tn,),
        in_specs=[pl.BlockSpec((ne, tn), lambda i: (0, i)), pl.BlockSpec((ne, 1), lambda i: (0, 0))],
        out_specs=[pl.BlockSpec((TOP_K, tn), lambda i: (0, i)), pl.BlockSpec((TOP_K, tn), lambda i: (0, i))],
        out_shape=[jax.ShapeDtypeStruct((TOP_K, t), jnp.int32), jax.ShapeDtypeStruct((TOP_K, t), F32)],
        compiler_params=_cparams("parallel"),
        name="route",
    )(logits_t, bias_col)


def _experts_kernel(be_ref, np_ref,
                    tok_ref, tokn_ref, w_ref, wg0_ref, wu0_ref, wd0_ref, wg1_ref, wu1_ref, wd1_ref, h_hbm,
                    o_ref, xg, wgc, wuc, wdc, gsem):
    s = pl.program_id(0)
    n_pairs = np_ref[0]
    nct = xg.shape[1] // EXPERT_BLOCK
    rows = nct * EXPERT_BLOCK
    w_refs = ((wg0_ref, wu0_ref, wd0_ref), (wg1_ref, wu1_ref, wd1_ref))

    def gather_copy(tref, p, j):
        src = h_hbm.at[pl.ds(pl.multiple_of(tref[0, p, j] * nct, nct), nct)]
        return pltpu.make_async_copy(src, xg.at[p, pl.ds(j * nct, nct)], gsem.at[p])

    def gather_wait(p):
        pltpu.make_async_copy(h_hbm.at[pl.ds(0, rows)], xg.at[p], gsem.at[p]).wait()

    @pl.when(s >= n_pairs)
    def _():
        o_ref[...] = jnp.zeros(o_ref.shape, F32)

    @pl.when(s < n_pairs)
    def _():
        @pl.when(s == 0)
        def _():
            for p in range(2):
                for j in range(EXPERT_BLOCK):
                    gather_copy(tok_ref, p, j).start(priority=j % 2)

        for p in range(2):
            blk = 2 * s + p
            wg_ref, wu_ref, wd_ref = w_refs[p]
            gather_wait(p)

            @pl.when((blk == 0) | (be_ref[blk] != be_ref[jnp.maximum(blk - 1, 0)]))
            def _():
                wgc[...] = wg_ref[0].astype(BF16)
                wuc[...] = wu_ref[0].astype(BF16)
                wdc[...] = wd_ref[0].astype(BF16)

            xb = jnp.concatenate([xg[p, pl.ds(c, EXPERT_BLOCK, stride=nct), :] for c in range(nct)],
                                 axis=-1).astype(BF16)
            for j in range(EXPERT_BLOCK):
                gather_copy(tokn_ref, p, j).start(priority=j % 2)
            hmid = _silu(jnp.dot(xb, wgc[...], preferred_element_type=F32)) * jnp.dot(xb, wuc[...],
                                                                                      preferred_element_type=F32)
            out = _dot(hmid, wdc[...])
            eye = (lax.broadcasted_iota(jnp.int32, (EXPERT_BLOCK, EXPERT_BLOCK), 0)
                   == lax.broadcasted_iota(jnp.int32, (EXPERT_BLOCK, EXPERT_BLOCK), 1))
            w_col = jnp.sum(jnp.where(eye, w_ref[0, p:p + 1, :], 0.0), axis=1, keepdims=True)
            out = out * w_col
            for c in range(nct):
                o_ref[pl.ds(p * rows + c, EXPERT_BLOCK, stride=nct), :] = out[:, c * LANES:(c + 1) * LANES]

        @pl.when(s == n_pairs - 1)
        def _():
            for p in range(2):
                gather_wait(p)


def _experts(hffn, block_e, n_pairs, row_tok, row_w, wg, wu, wd):
    d = wg.shape[1]
    nct = d // LANES
    npairs = row_tok.shape[0]
    de = wg.shape[2]
    last = npairs - 1
    smem_blk = lambda f: pl.BlockSpec((1, 2, EXPERT_BLOCK), f, memory_space=pltpu.SMEM)
    wspec = lambda shp, p: pl.BlockSpec((1,) + shp, lambda s, be, npu: (be[2 * s + p], 0, 0))
    pair_rows = 2 * EXPERT_BLOCK * nct
    grid_spec = pltpu.PrefetchScalarGridSpec(
        num_scalar_prefetch=2,
        grid=(npairs,),
        in_specs=[smem_blk(lambda s, be, npu: (s, 0, 0)),
                  smem_blk(lambda s, be, npu: (jnp.minimum(s + 1, last), 0, 0)),
                  pl.BlockSpec((1, 2, EXPERT_BLOCK), lambda s, be, npu: (s, 0, 0)),
                  wspec((d, de), 0), wspec((d, de), 0), wspec((de, d), 0),
                  wspec((d, de), 1), wspec((d, de), 1), wspec((de, d), 1),
                  pl.BlockSpec(memory_space=pl.ANY)],
        out_specs=pl.BlockSpec((pair_rows, LANES), lambda s, be, npu: (s, 0)),
        scratch_shapes=[pltpu.VMEM((2, EXPERT_BLOCK * nct, LANES), F32),
                        pltpu.VMEM((d, de), BF16), pltpu.VMEM((d, de), BF16), pltpu.VMEM((de, d), BF16),
                        pltpu.SemaphoreType.DMA((2,))],
    )
    return pl.pallas_call(
        _experts_kernel,
        grid_spec=grid_spec,
        out_shape=jax.ShapeDtypeStruct((npairs * pair_rows, LANES), F32),
        compiler_params=_cparams("arbitrary"),
        name="experts",
    )(block_e, n_pairs, row_tok, row_tok, row_w, wg, wu, wd, wg, wu, wd, hffn)


def _combine_kernel(pos_ref, posn_ref, x1_ref, ys_ref, npost_ref, g5_ref, rows_hbm, o_ref, buf, sem):
    i = pl.program_id(0)
    tm = x1_ref.shape[1]
    nct = x1_ref.shape[2] // LANES
    cur = i % 2
    nxt = 1 - cur

    def copy(pref, b, k, t):
        src = rows_hbm.at[pl.ds(pl.multiple_of(pref[k, t] * nct, nct), nct)]
        dst = buf.at[b, pl.ds(pl.multiple_of((k * tm + t) * nct, nct), nct)]
        return pltpu.make_async_copy(src, dst, sem.at[b])

    def drain(b):
        pltpu.make_async_copy(rows_hbm.at[pl.ds(0, TOP_K * tm * nct)], buf.at[b], sem.at[b]).wait()

    def issue(pref, b):
        def body(t2, carry):
            for u in range(2):
                for k in range(TOP_K):
                    copy(pref, b, k, 2 * t2 + u).start(priority=k % 2)
            return carry
        lax.fori_loop(0, tm // 2, body, 0)

    @pl.when(i == 0)
    def _():
        issue(pos_ref, cur)

    @pl.when(i + 1 < pl.num_programs(0))
    def _():
        for t in range(tm):
            for k in range(TOP_K):
                copy(posn_ref, nxt, k, t).start(priority=k % 2)

    drain(cur)
    routed = []
    for c in range(nct):
        acc = buf[cur, pl.ds(c, tm, stride=nct), :]
        for k in range(1, TOP_K):
            acc = acc + buf[cur, pl.ds(k * tm * nct + c, tm, stride=nct), :]
        routed.append(acc)
    y = ys_ref[0] + jnp.concatenate(routed, axis=-1)
    o_ref[0] = x1_ref[0] + g5_ref[0] * _rms(y, npost_ref[...])


def _combine(x1, ys, rows, pos, npost, g5, tm):
    b, s, d = x1.shape
    nt = s // tm
    nct = d // LANES
    last = b * nt - 1
    tok = pl.BlockSpec((1, tm, d), lambda i: (i // nt, i % nt, 0))
    pos_blk = lambda f: pl.BlockSpec((TOP_K, tm), f, memory_space=pltpu.SMEM)
    return pl.pallas_call(
        _combine_kernel,
        grid=(b * nt,),
        in_specs=[pos_blk(lambda i: (0, i)), pos_blk(lambda i: (0, jnp.minimum(i + 1, last))), tok, tok,
                  pl.BlockSpec((1, d), lambda i: (0, 0)), pl.BlockSpec((1, 1, d), lambda i: (i // nt, 0, 0)),
                  pl.BlockSpec(memory_space=pl.ANY)],
        out_specs=tok,
        out_shape=jax.ShapeDtypeStruct((b, s, d), F32),
        scratch_shapes=[pltpu.VMEM((2, TOP_K * tm * nct, LANES), F32), pltpu.SemaphoreType.DMA((2,))],
        compiler_params=_cparams("arbitrary"),
        name="combine",
    )(pos, pos, x1, ys, npost, g5, rows)


def _dispatch_plan(idx_t, gate_t):
    k, t = idx_t.shape
    n_asg = k * t
    nb = n_asg // EXPERT_BLOCK + N_EXPERTS
    flat_e = idx_t.reshape(-1)
    id_bits = max(1, (n_asg - 1).bit_length())
    assert (N_EXPERTS - 1).bit_length() + id_bits <= 31
    packed = jnp.sort((flat_e << id_bits) | jnp.arange(n_asg, dtype=jnp.int32))
    order = packed & ((1 << id_bits) - 1)
    counts = jnp.zeros((N_EXPERTS,), jnp.int32).at[flat_e].add(1)
    padded = (counts + EXPERT_BLOCK - 1) // EXPERT_BLOCK * EXPERT_BLOCK
    start = jnp.cumsum(counts) - counts
    pend = jnp.cumsum(padded)
    pstart = pend - padded
    blk0 = jnp.arange(nb, dtype=jnp.int32) * EXPERT_BLOCK
    block_e = jnp.minimum(jnp.sum((pend[None, :] <= blk0[:, None]).astype(jnp.int32), axis=1), N_EXPERTS - 1)
    of_block = block_e[:, None] == jnp.arange(N_EXPERTS, dtype=jnp.int32)[None, :]
    per_block = lambda v: jnp.sum(jnp.where(of_block, v[None, :], 0), axis=1)
    assert nb % 2 == 0
    n_pairs = ((pend[-1] // EXPERT_BLOCK + 1) // 2).astype(jnp.int32).reshape(1)
    pos = blk0[:, None] - per_block(pstart)[:, None] + jnp.arange(EXPERT_BLOCK, dtype=jnp.int32)[None, :]
    valid = pos < per_block(counts)[:, None]
    src = jnp.clip(per_block(start)[:, None] + pos, 0, n_asg - 1)
    asg = order[src]
    row_tok = jnp.where(valid, asg % t, 0).astype(jnp.int32)
    row_w = jnp.where(valid, gate_t.reshape(-1)[asg], 0.0).astype(F32)
    i_sorted = jnp.arange(n_asg, dtype=jnp.int32)
    pad_before = jnp.sum(jnp.where(i_sorted[:, None] >= (start + counts)[None, :], (padded - counts)[None, :], 0),
                         axis=1)
    _, pos = lax.sort((order, i_sorted + pad_before), num_keys=1)
    pos = pos.reshape(k, t)
    shp = (nb // 2, 2, EXPERT_BLOCK)
    return block_e, n_pairs, row_tok.reshape(shp), row_w.reshape(shp), pos


def _pack_in_weights(w_in, ml_i_bias, ml_f_bias, gd_dt_bias):
    d = w_in.shape[0]
    nml = 2 * ML_HEADS * ML_QK + 2 * ML_HEADS * ML_V
    ml_cols = nml + 4 * ML_HEADS
    ngq = GD_HEADS * (2 * GD_QK + GD_V)
    ngz = GD_HEADS * GD_V
    wml = w_in[:, :nml].astype(BF16)
    wgq = w_in[:, ml_cols:ml_cols + ngq].astype(BF16)
    wgz = w_in[:, ml_cols + ngq:ml_cols + ngq + ngz].astype(BF16)
    wg = jnp.zeros((d, GATE_LANES), F32)
    wg = wg.at[:, ML_GATE0:ML_GATE0 + 16].set(w_in[:, nml:ml_cols])
    wg = wg.at[:, GD_GATE0:GD_GATE0 + 16].set(w_in[:, ml_cols + ngq + ngz:])
    gb = jnp.zeros((GATE_LANES,), F32)
    gb = gb.at[ML_GATE0:ML_GATE0 + 16].set(jnp.stack([ml_i_bias, ml_f_bias], axis=1).reshape(-1))
    gb = gb.at[GD_GATE0:GD_GATE0 + 16].set(jnp.stack([gd_dt_bias, jnp.zeros_like(gd_dt_bias)], axis=1).reshape(-1))
    return wml, wgq, wgz, wg.astype(BF16), gb.reshape(1, GATE_LANES)


def _mixer(x, ctx, mod, mod_ctx, norm_pre_mix, w_in, ml_i_bias, ml_f_bias, gd_conv_w, gd_a_log, gd_dt_bias):
    b, s, d = x.shape
    sc = ctx.shape[1]
    wml, wgq, wgz, wg, gb = _pack_in_weights(w_in, ml_i_bias, ml_f_bias, gd_dt_bias)
    nw = norm_pre_mix.reshape(1, d)
    ctx_mod = lambda j: jnp.broadcast_to(mod_ctx[j].reshape(1, 1, d), (b, 1, d))
    ml_c, gq_c, _, g_c = _proj(ctx, nw, ctx_mod(0), ctx_mod(1), wml, wgq, wgz, wg, gb, tm=sc, grid_view=False)
    ml_l, gqv_l, gz_l, g_l, gv_l = _proj(x, nw, mod[0], mod[1], wml, wgq, wgz, wg, gb, tm=TM_PROJ, grid_view=True)

    c0 = jnp.zeros((b, N_CHAINS, ML_QK, 2 * ML_V), F32)
    m0 = jnp.zeros((b, N_CHAINS, 1, 1), F32)
    _, _, c1, m1 = _mlstm(ml_c, g_c, c0, m0)
    hf, hb, _, _ = _mlstm(ml_l, g_l, c1, m1)

    neg_a = jnp.zeros((GATE_LANES,), F32)
    neg_a = neg_a.at[GD_GATE0:GD_GATE0 + 16].set(
        jnp.stack([-jnp.exp(gd_a_log), jnp.zeros_like(gd_a_log)], axis=1).reshape(-1)).reshape(1, GATE_LANES)
    qn_c = _gdconv_ctx(gq_c, gd_conv_w)
    qnv_l = _gdconv_lat(gqv_l, gd_conv_w)
    s0 = jnp.zeros((b, N_CHAINS, GD_QK, GD_V), F32)
    hdim = GD_HEADS * GD_V
    _, _, s1 = _gdn(qn_c, g_c, neg_a, s0, sc // CHUNK, lambda bi, n: (bi, n, 0), (b, sc, hdim))
    rows = s // GRID_W
    cpc = rows // CHUNK
    col_idx = lambda bi, n: (bi, n % cpc, n // cpc)
    ofv, obv, _ = _gdn(qnv_l, gv_l, neg_a, s1, s // CHUNK, col_idx, (b, rows, GRID_W * hdim))
    return hf, hb, ml_l, ofv, obv, gz_l


def kernel(x, c, ctx, c_ctx, w_ada, b_ada, norm_pre_mix, norm_post_mix, norm_pre_ffn, norm_post_ffn, w_in,
           ml_i_bias, ml_f_bias, ml_norm_w, gd_conv_w, gd_a_log, gd_dt_bias, gd_norm_w, w_out, router_w,
           router_bias, w_gate, w_up, w_down, ws_gate, ws_up, ws_down):
    b, s, d = x.shape
    depth = w_ada.shape[0]
    assert depth == 1, "the context stream update of deeper stacks is not implemented"
    ly = 0
    cc = jnp.zeros((16, d), F32).at[:b].set(c).at[b].set(c_ctx)
    mod_all = _ada(cc, w_ada[ly], b_ada[ly])
    mod = [mod_all[:b, j * d:(j + 1) * d].reshape(b, 1, d) for j in range(6)]
    mod_ctx = [mod_all[b, j * d:(j + 1) * d] for j in range(6)]

    hf, hb, ml_l, of, ob, gz_l = _mixer(x, ctx, mod, mod_ctx, norm_pre_mix[ly], w_in[ly], ml_i_bias[ly],
                                        ml_f_bias[ly], gd_conv_w[ly], gd_a_log[ly], gd_dt_bias[ly])

    row = lambda v: v.reshape(1, -1)
    x1, hffn, logits_t, ys = _post(
        x, hf, hb, ml_l, of, ob, gz_l, row(ml_norm_w[ly]), row(jnp.tile(gd_norm_w[ly], GD_HEADS)),
        w_out[ly].astype(BF16), row(norm_post_mix[ly]), mod[2], row(norm_pre_ffn[ly]), mod[3], mod[4],
        router_w[ly].T.astype(BF16), ws_gate[ly].astype(BF16), ws_up[ly].astype(BF16), ws_down[ly].astype(BF16),
        tm=TM_POST)

    idx_t, gate_t = _route(logits_t, router_bias[ly].reshape(-1, 1), tn=TN_ROUTE)
    block_e, n_pairs, row_tok, row_w, pos = _dispatch_plan(idx_t, gate_t)
    t = b * s
    rows = _experts(hffn.reshape(t * (d // LANES), LANES), block_e, n_pairs, row_tok, row_w,
                    w_gate[ly], w_up[ly], w_down[ly])
    return _combine(x1, ys, rows, pos, row(norm_post_ffn[ly]), mod[5], tm=TM_COMBINE)
```

```python
import functools

import jax
import jax.numpy as jnp
from jax import lax
from jax.experimental import pallas as pl
from jax.experimental.pallas import tpu as pltpu

EPS = 1e-6
CHUNK = 64
GRID_W = 64
ML_HEADS, ML_QK, ML_V = 4, 64, 128
GD_HEADS, GD_QK, GD_V = 4, 128, 128
CONV_W = 5
N_EXPERTS, TOP_K, N_GROUPS, TOPK_GROUPS = 256, 8, 8, 4
ROUTED_SCALE = 2.5
EXPERT_BLOCK = 128
N_CHAINS = 8
ML_SCAN_BATCH, GD_SCAN_BATCH = 1, 4
LANES = 128
GATE_LANES = LANES
ML_GATE0, GD_GATE0 = 0, 16

F32 = jnp.float32
BF16 = jnp.bfloat16
HI = lax.Precision.HIGHEST
VMEM_LIMIT = 56 * 1024 * 1024
TM_PROJ, TM_POST, TM_COMBINE, TN_ROUTE, TN_ADA = 512, 512, 256, 512, 1536


def _cparams(*sem):
    return pltpu.CompilerParams(dimension_semantics=sem, vmem_limit_bytes=VMEM_LIMIT)


def _dot(a, b):
    return jnp.dot(a.astype(BF16), b.astype(BF16), preferred_element_type=F32)


def _dot_nt(a, b):
    return lax.dot_general(a.astype(BF16), b.astype(BF16), (((1,), (1,)), ((), ())), preferred_element_type=F32)


def _dot_tn(a, b):
    return lax.dot_general(a.astype(BF16), b.astype(BF16), (((0,), (0,)), ((), ())), preferred_element_type=F32)


def _dot_hi(a, b):
    return jnp.dot(a, b, precision=HI, preferred_element_type=F32)


def _dot_nt_hi(a, b):
    return lax.dot_general(a, b, (((1,), (1,)), ((), ())), precision=HI, preferred_element_type=F32)


def _transpose_hi(x):
    n = x.shape[1]
    eye = (lax.broadcasted_iota(jnp.int32, (n, n), 0) == lax.broadcasted_iota(jnp.int32, (n, n), 1)).astype(F32)
    return _dot_nt_hi(eye, x)


def _silu(x):
    return x * jax.nn.sigmoid(x)


def _past_mask(reverse):
    t = lax.broadcasted_iota(jnp.int32, (CHUNK, CHUNK), 0)
    s = lax.broadcasted_iota(jnp.int32, (CHUNK, CHUNK), 1)
    return (s >= t, s > t) if reverse else (s <= t, s < t)


def _ada_kernel(c_ref, w_ref, b_ref, o_ref):
    o_ref[...] = _dot(_silu(c_ref[...]), w_ref[...]) + b_ref[...]


def _ada(cc, w_ada, b_ada):
    rows, d = cc.shape
    n = w_ada.shape[1]
    tn = TN_ADA
    return pl.pallas_call(
        _ada_kernel,
        grid=(n // tn,),
        in_specs=[pl.BlockSpec((rows, d), lambda j: (0, 0)),
                  pl.BlockSpec((d, tn), lambda j: (0, j)),
                  pl.BlockSpec((1, tn), lambda j: (0, j))],
        out_specs=pl.BlockSpec((rows, tn), lambda j: (0, j)),
        out_shape=jax.ShapeDtypeStruct((rows, n), F32),
        compiler_params=_cparams("arbitrary"),
        name="ada",
    )(cc, w_ada, b_ada.reshape(1, n))


GRID_PITCH = GRID_W + 8


def _to_grid_view(src_ref, dst_ref):
    ng = src_ref.shape[0]
    r = src_ref.shape[1] // GRID_PITCH
    for c in range(GRID_W):
        for g in range(ng):
            lo = (c * ng + g) * LANES
            dst_ref[0, :, lo:lo + LANES] = src_ref[g, pl.ds(c, r, stride=GRID_PITCH), :]


def _from_grid_view(src_ref, dst_ref):
    ng = dst_ref.shape[0]
    r = dst_ref.shape[1] // GRID_PITCH
    for c in range(GRID_W):
        for g in range(ng):
            lo = (c * ng + g) * LANES
            dst_ref[g, pl.ds(c, r, stride=GRID_PITCH), :] = src_ref[0, :, lo:lo + LANES]


def _pitched_rows(ref, g):
    r = ref.shape[1] // GRID_PITCH
    return jnp.concatenate([ref[g, i * GRID_PITCH:i * GRID_PITCH + GRID_W, :] for i in range(r)], axis=0)


def _proj_kernel(grid_view, x_ref, nw_ref, sh_ref, sc_ref, wml_ref, wgq_ref, wgz_ref, wg_ref, gb_ref, *refs):
    if grid_view:
        ml_ref, gqv_ref, gz_ref, g_ref, gv_ref, gq_scr, g_scr = refs
    else:
        ml_ref, gq_ref, gz_ref, g_ref = refs
    x = x_ref[0]
    xn = x * lax.rsqrt(jnp.mean(x * x, axis=-1, keepdims=True) + EPS) * nw_ref[...]
    h = (xn * (1.0 + sc_ref[0]) + sh_ref[0]).astype(BF16)
    ml_ref[0] = jnp.dot(h, wml_ref[...], preferred_element_type=F32)
    gz_ref[0] = jnp.dot(h, wgz_ref[...], preferred_element_type=F32)
    gates = jnp.dot(h, wg_ref[...], preferred_element_type=F32) + gb_ref[...]
    g_ref[0] = gates
    gq = jnp.dot(h, wgq_ref[...], preferred_element_type=F32)
    if grid_view:
        for r in range(x.shape[0] // GRID_W):
            rows = slice(r * GRID_W, (r + 1) * GRID_W)
            prow = slice(r * GRID_PITCH, r * GRID_PITCH + GRID_W)
            g_scr[0, prow, :] = gates[rows]
            for g in range(gq_scr.shape[0]):
                gq_scr[g, prow, :] = gq[rows, g * LANES:(g + 1) * LANES]
        _to_grid_view(gq_scr, gqv_ref)
        _to_grid_view(g_scr, gv_ref)
    else:
        gq_ref[0] = gq


def _proj(x, norm_w, shift, scale, wml, wgq, wgz, wg, gbias, tm, grid_view):
    b, s, d = x.shape
    nml, ngq, ngz = wml.shape[1], wgq.shape[1], wgz.shape[1]
    full = lambda shp: pl.BlockSpec(shp, lambda bi, i: (0,) * len(shp))
    tok = lambda n: pl.BlockSpec((1, tm, n), lambda bi, i: (bi, i, 0))
    mod = pl.BlockSpec((1, 1, d), lambda bi, i: (bi, 0, 0))
    if grid_view:
        rt = tm // GRID_W
        view = lambda n: pl.BlockSpec((1, rt, GRID_W * n), lambda bi, i: (bi, i, 0))
        vshape = lambda n: jax.ShapeDtypeStruct((b, s // GRID_W, GRID_W * n), F32)
        out_specs = [tok(nml), view(ngq), tok(ngz), tok(GATE_LANES), view(GATE_LANES)]
        out_shape = [jax.ShapeDtypeStruct((b, s, nml), F32), vshape(ngq), jax.ShapeDtypeStruct((b, s, ngz), F32),
                     jax.ShapeDtypeStruct((b, s, GATE_LANES), F32), vshape(GATE_LANES)]
        scratch = [pltpu.VMEM((ngq // LANES, rt * GRID_PITCH, LANES), F32), pltpu.VMEM((1, rt * GRID_PITCH, LANES), F32)]
    else:
        out_specs = [tok(nml), tok(ngq), tok(ngz), tok(GATE_LANES)]
        out_shape = [jax.ShapeDtypeStruct((b, s, n), F32) for n in (nml, ngq, ngz, GATE_LANES)]
        scratch = []
    return pl.pallas_call(
        functools.partial(_proj_kernel, grid_view),
        grid=(b, s // tm),
        in_specs=[tok(d), full((1, d)), mod, mod, full((d, nml)), full((d, ngq)), full((d, ngz)),
                  full((d, GATE_LANES)), full((1, GATE_LANES))],
        out_specs=out_specs,
        out_shape=out_shape,
        scratch_shapes=scratch,
        compiler_params=_cparams("parallel", "arbitrary"),
        name="proj",
    )(x, norm_w, shift, scale, wml, wgq, wgz, wg, gbias)


def _mlstm_kernel(mlf_ref, mlb_ref, gf_ref, gb_ref, c0_ref, m0_ref, hf_ref, hb_ref, cn_ref, mn_ref, c_scr, m_scr):
    i = pl.program_id(1)

    @pl.when(i == 0)
    def _():
        c_scr[...] = c0_ref[...]
        m_scr[...] = m0_ref[...]

    nbat = mlf_ref.shape[0]
    past = [_past_mask(d == 1)[0] for d in range(2)]
    ml_refs, g_refs, h_refs = (mlf_ref, mlb_ref), (gf_ref, gb_ref), (hf_ref, hb_ref)
    bd = [(bb, d) for bb in range(nbat) for d in range(2)]
    g = [g_refs[d][bb] for bb, d in bd]
    ls = [jax.nn.log_sigmoid(x) for x in g]
    bcol = [_dot_hi(past[d].astype(F32), ls[j]) for j, (bb, d) in enumerate(bd)]
    tot = [jnp.sum(x, axis=0, keepdims=True) for x in ls]
    g_t = [_transpose_hi(x) for x in g]
    b_t = [_transpose_hi(x) for x in bcol]

    chains = [(bb, d, hd) for bb in range(nbat) for d in range(2) for hd in range(ML_HEADS)]
    nc = range(len(chains))
    k0, v0 = ML_HEADS * ML_QK, 2 * ML_HEADS * ML_QK
    ones_col = (lax.broadcasted_iota(jnp.int32, (CHUNK, ML_V), 1) == 0).astype(F32)
    q, k, v, i_col, b_col, b_end, log_d = [], [], [], [], [], [], []
    for bb, d, hd in chains:
        ci = ML_GATE0 + d * 8 + hd
        cf = ci + ML_HEADS
        j = bb * 2 + d
        q.append(ml_refs[d][bb, :, hd * ML_QK:(hd + 1) * ML_QK])
        k.append(ml_refs[d][bb, :, k0 + hd * ML_QK:k0 + (hd + 1) * ML_QK] * (ML_QK ** -0.5))
        v.append(jnp.concatenate([ml_refs[d][bb, :, v0 + hd * ML_V:v0 + (hd + 1) * ML_V], ones_col], axis=-1))
        i_col.append(g[j][:, ci:ci + 1])
        b_col.append(bcol[j][:, cf:cf + 1])
        b_end.append(tot[j][:, cf:cf + 1])
        log_d.append(jnp.where(past[d], b_col[-1] - b_t[j][cf:cf + 1, :] + g_t[j][ci:ci + 1, :], -jnp.inf))
    st_idx = [(bb, d * ML_HEADS + hd) for bb, d, hd in chains]
    c_st = [c_scr[ix] for ix in st_idx]
    m_st = [m_scr[ix] for ix in st_idx]
    log_prev = [b_col[c] + m_st[c] for c in nc]
    m_t = [jnp.maximum(log_prev[c], jnp.max(log_d[c], axis=-1, keepdims=True)) for c in nc]
    qk = [_dot_nt(q[c], k[c]) for c in nc]
    qc = [_dot(q[c], c_st[c]) for c in nc]
    s = [qk[c] * jnp.exp(log_d[c] - m_t[c]) for c in nc]
    w_prev = [jnp.exp(log_prev[c] - m_t[c]) for c in nc]
    sv = [_dot(s[c], v[c]) for c in nc]
    log_s = [b_end[c] - b_col[c] + i_col[c] for c in nc]
    m_new = [jnp.maximum(b_end[c] + m_st[c], jnp.max(log_s[c], axis=0, keepdims=True)) for c in nc]
    kw = [k[c] * jnp.exp(log_s[c] - m_new[c]) for c in nc]
    w_c = [jnp.exp(b_end[c] + m_st[c] - m_new[c]) for c in nc]
    kv = [_dot_tn(kw[c], v[c]) for c in nc]
    numden = [sv[c] + w_prev[c] * qc[c] for c in nc]
    scale = [1.0 / jnp.maximum(jnp.abs(numden[c][:, ML_V:ML_V + 1]), jnp.exp(-m_t[c])) for c in nc]
    for c, (bb, d, hd) in enumerate(chains):
        h_refs[d][bb, :, hd * ML_V:(hd + 1) * ML_V] = numden[c][:, :ML_V] * scale[c]
        c_scr[st_idx[c]] = w_c[c] * c_st[c] + kv[c]
        m_scr[st_idx[c]] = m_new[c]

    @pl.when(i == pl.num_programs(1) - 1)
    def _():
        cn_ref[...] = c_scr[...]
        mn_ref[...] = m_scr[...]


def _mlstm(ml, gates, c0, m0):
    b, s, nml = ml.shape
    nc = s // CHUNK
    nbat = ML_SCAN_BATCH
    fwd = lambda n: pl.BlockSpec((nbat, CHUNK, n), lambda bi, i: (bi, i, 0))
    bwd = lambda n: pl.BlockSpec((nbat, CHUNK, n), lambda bi, i: (bi, nc - 1 - i, 0))
    st = lambda shp: pl.BlockSpec((nbat,) + shp, lambda bi, i: (bi,) + (0,) * len(shp))
    hdim = ML_HEADS * ML_V
    cshape = (N_CHAINS, ML_QK, 2 * ML_V)
    return pl.pallas_call(
        _mlstm_kernel,
        grid=(b // nbat, nc),
        in_specs=[fwd(nml), bwd(nml), fwd(GATE_LANES), bwd(GATE_LANES), st(cshape), st((N_CHAINS, 1, 1))],
        out_specs=[fwd(hdim), bwd(hdim), st(cshape), st((N_CHAINS, 1, 1))],
        out_shape=[jax.ShapeDtypeStruct((b, s, hdim), F32), jax.ShapeDtypeStruct((b, s, hdim), F32),
                   jax.ShapeDtypeStruct(c0.shape, F32), jax.ShapeDtypeStruct(m0.shape, F32)],
        scratch_shapes=[pltpu.VMEM((nbat,) + cshape, F32), pltpu.VMEM((nbat, N_CHAINS, 1, 1), F32)],
        compiler_params=_cparams("parallel", "arbitrary"),
        name="mlstm",
    )(ml, ml, gates, gates, c0, m0)


def _gdconv_kernel(has_halo, *refs):
    if has_halo:
        x_ref, prev_ref, next_ref, w_ref, o_ref, xp_ref = refs
    else:
        x_ref, w_ref, o_ref, xp_ref = refs
    rows = x_ref.shape[1]
    nch = x_ref.shape[2]
    pad = 8
    zero = jnp.zeros((pad, nch), F32)
    if has_halo:
        c = pl.program_id(1)
        xp_ref[0:pad, :] = jnp.where(c > 0, prev_ref[0], zero)
        xp_ref[pad + rows:, :] = jnp.where(c < pl.num_programs(1) - 1, next_ref[0], zero)
    else:
        xp_ref[0:pad, :] = zero
        xp_ref[pad + rows:, :] = zero
    xp_ref[pad:pad + rows, :] = x_ref[0]
    half = CONV_W // 2
    for lc in range(nch // 128):
        sl = slice(lc * 128, (lc + 1) * 128)
        acc = None
        for j in range(CONV_W):
            term = xp_ref[pad - half + j:pad - half + j + rows, sl] * w_ref[j:j + 1, sl]
            acc = term if acc is None else acc + term
        y = _silu(acc)
        if lc < 2 * GD_HEADS:
            y = y * lax.rsqrt(jnp.sum(y * y, axis=-1, keepdims=True) + EPS)
        if lc < GD_HEADS:
            y = y * (GD_QK ** -0.5)
        o_ref[0, :, sl] = y


def _gdconv_ctx(qkv, conv_w):
    b, s, nch = qkv.shape
    return pl.pallas_call(
        functools.partial(_gdconv_kernel, False),
        grid=(b,),
        in_specs=[pl.BlockSpec((1, s, nch), lambda bi: (bi, 0, 0)), pl.BlockSpec((CONV_W, nch), lambda bi: (0, 0))],
        out_specs=pl.BlockSpec((1, s, nch), lambda bi: (bi, 0, 0)),
        out_shape=jax.ShapeDtypeStruct((b, s, nch), F32),
        scratch_shapes=[pltpu.VMEM((s + 16, nch), F32)],
        compiler_params=_cparams("parallel"),
        name="gdconv_ctx",
    )(qkv, conv_w)


def _gdconv_lat(view, conv_w):
    b, rows, wn = view.shape
    nch = wn // GRID_W
    rb = rows // 8
    return pl.pallas_call(
        functools.partial(_gdconv_kernel, True),
        grid=(b, GRID_W),
        in_specs=[pl.BlockSpec((1, rows, nch), lambda bi, c: (bi, 0, c)),
                  pl.BlockSpec((1, 8, nch), lambda bi, c: (bi, rb - 1, jnp.maximum(c - 1, 0))),
                  pl.BlockSpec((1, 8, nch), lambda bi, c: (bi, 0, jnp.minimum(c + 1, GRID_W - 1))),
                  pl.BlockSpec((CONV_W, nch), lambda bi, c: (0, 0))],
        out_specs=pl.BlockSpec((1, rows, nch), lambda bi, c: (bi, 0, c)),
        out_shape=jax.ShapeDtypeStruct(view.shape, F32),
        scratch_shapes=[pltpu.VMEM((rows + 16, nch), F32)],
        compiler_params=_cparams("parallel", "arbitrary"),
        name="gdconv_lat",
    )(view, view, view, conv_w)


SOLVE_BLOCK = 16


def _hi_lo(x):
    hi = x.astype(BF16).astype(F32)
    return hi, x - hi


def _dot_split(a, b):
    a_hi, a_lo = _hi_lo(a)
    b_hi, b_lo = _hi_lo(b)
    lhs = jnp.concatenate([a_hi, a_hi, a_lo], axis=1).astype(BF16)
    rhs = jnp.concatenate([b_hi, b_lo, b_hi], axis=0).astype(BF16)
    return jnp.dot(lhs, rhs, preferred_element_type=F32)


def _unit_triangular_inverses(ns):
    c = ns[0].shape[0]
    row = lax.broadcasted_iota(jnp.int32, (c, c), 0)
    col = lax.broadcasted_iota(jnp.int32, (c, c), 1)
    eye = (row == col).astype(F32)
    in_diag_block = (row // SOLVE_BLOCK) == (col // SOLVE_BLOCK)
    mm = lambda a_list, b_list: [_dot_split(a, b) for a, b in zip(a_list, b_list)]

    n_d = [jnp.where(in_diag_block, n, 0.0) for n in ns]
    x = n_d
    d_inv = [eye - n for n in n_d]
    for _ in range(SOLVE_BLOCK.bit_length() - 2):
        x = mm(x, x)
        d_inv = [d + dx for d, dx in zip(d_inv, mm(d_inv, x))]
    m = mm(d_inv, [n - nd for n, nd in zip(ns, n_d)])
    assert c // SOLVE_BLOCK == 4
    i_minus_m = [eye - mi for mi in m]
    q = [a + b for a, b in zip(i_minus_m, mm(i_minus_m, mm(m, m)))]
    return mm(q, d_inv)


def _gdn_kernel(qf_ref, qb_ref, gf_ref, gb_ref, na_ref, s0_ref, of_ref, ob_ref, sn_ref, s_scr):
    i = pl.program_id(1)

    @pl.when(i == 0)
    def _():
        s_scr[...] = s0_ref[...]

    nbat = qf_ref.shape[0]
    nqk = GD_HEADS * GD_QK
    masks = [_past_mask(d == 1) for d in range(2)]
    x_refs, g_refs, o_refs = (qf_ref, qb_ref), (gf_ref, gb_ref), (of_ref, ob_ref)
    bd = [(bb, d) for bb in range(nbat) for d in range(2)]
    gates = [g_refs[d][bb] for bb, d in bd]
    glog = [na_ref[...] * jax.nn.softplus(g) for g in gates]
    beta_all = [jax.nn.sigmoid(g) for g in gates]
    gcum = [_dot_hi(masks[d][0].astype(F32), glog[j]) for j, (bb, d) in enumerate(bd)]
    gtot = [jnp.sum(g, axis=0, keepdims=True) for g in glog]
    gcum_t = [_transpose_hi(g) for g in gcum]

    chains = [(bb, d, hd) for bb in range(nbat) for d in range(2) for hd in range(GD_HEADS)]
    q, k, v, g_col, beta, g_end, decay = [], [], [], [], [], [], []
    for bb, d, hd in chains:
        ca = GD_GATE0 + d * 8 + hd
        j = bb * 2 + d
        q.append(x_refs[d][bb, :, hd * GD_QK:(hd + 1) * GD_QK])
        k.append(x_refs[d][bb, :, nqk + hd * GD_QK:nqk + (hd + 1) * GD_QK])
        v.append(x_refs[d][bb, :, 2 * nqk + hd * GD_V:2 * nqk + (hd + 1) * GD_V])
        g_col.append(gcum[j][:, ca:ca + 1])
        beta.append(beta_all[j][:, ca + GD_HEADS:ca + GD_HEADS + 1])
        g_end.append(gtot[j][:, ca:ca + 1])
        decay.append(jnp.exp(jnp.where(masks[d][0], g_col[-1] - gcum_t[j][ca:ca + 1, :], -jnp.inf)))
    nc = range(len(chains))
    kk = [_dot_nt(k[c], k[c]) for c in nc]
    xs = [jnp.where(masks[chains[c][1]][1], beta[c] * kk[c] * decay[c], 0.0) for c in nc]
    ps = _unit_triangular_inverses(xs)
    uw = [_dot_split(ps[c], jnp.concatenate([v[c] * beta[c], k[c] * (beta[c] * jnp.exp(g_col[c]))], axis=-1))
          for c in nc]
    qk = [_dot_nt(q[c], k[c]) * decay[c] for c in nc]
    s_st = [s_scr[bb, d * GD_HEADS + hd] for bb, d, hd in chains]
    v_new = [uw[c][:, :GD_V] - _dot(uw[c][:, GD_V:], s_st[c]) for c in nc]
    o_loc = [_dot(q[c] * jnp.exp(g_col[c]), s_st[c]) for c in nc]
    o_new = [o_loc[c] + _dot(qk[c], v_new[c]) for c in nc]
    s_new = [s_st[c] * jnp.exp(g_end[c]) + _dot_tn(k[c] * jnp.exp(g_end[c] - g_col[c]), v_new[c]) for c in nc]
    for c, (bb, d, hd) in enumerate(chains):
        o_refs[d][bb, :, hd * GD_V:(hd + 1) * GD_V] = o_new[c]
        s_scr[bb, d * GD_HEADS + hd] = s_new[c]

    @pl.when(i == pl.num_programs(1) - 1)
    def _():
        sn_ref[...] = s_scr[...]


def _gdn(qkv_view, gates_view, neg_a, s0, nc, idx_fn, out_view_shape):
    b = qkv_view.shape[0]
    nbat = GD_SCAN_BATCH
    nqkv = 2 * GD_HEADS * GD_QK + GD_HEADS * GD_V
    hdim = GD_HEADS * GD_V
    fwd = lambda n: pl.BlockSpec((nbat, CHUNK, n), lambda bi, i: idx_fn(bi, i))
    bwd = lambda n: pl.BlockSpec((nbat, CHUNK, n), lambda bi, i: idx_fn(bi, nc - 1 - i))
    st = pl.BlockSpec((nbat, N_CHAINS, GD_QK, GD_V), lambda bi, i: (bi, 0, 0, 0))
    return pl.pallas_call(
        _gdn_kernel,
        grid=(b // nbat, nc),
        in_specs=[fwd(nqkv), bwd(nqkv), fwd(GATE_LANES), bwd(GATE_LANES),
                  pl.BlockSpec((1, GATE_LANES), lambda bi, i: (0, 0)), st],
        out_specs=[fwd(hdim), bwd(hdim), st],
        out_shape=[jax.ShapeDtypeStruct(out_view_shape, F32), jax.ShapeDtypeStruct(out_view_shape, F32),
                   jax.ShapeDtypeStruct(s0.shape, F32)],
        scratch_shapes=[pltpu.VMEM((nbat, N_CHAINS, GD_QK, GD_V), F32)],
        compiler_params=_cparams("parallel", "arbitrary"),
        name="gdn",
    )(qkv_view, qkv_view, gates_view, gates_view, neg_a, s0)


def _head_rms(t, nheads, width):
    outs = []
    for hd in range(nheads):
        th = t[:, hd * width:(hd + 1) * width]
        outs.append(th * lax.rsqrt(jnp.mean(th * th, axis=-1, keepdims=True) + EPS))
    return jnp.concatenate(outs, axis=-1)


def _rms(t, w):
    return t * lax.rsqrt(jnp.mean(t * t, axis=-1, keepdims=True) + EPS) * w


def _post_kernel(x_ref, hf_ref, hb_ref, og_ref, of_ref, ob_ref, z_ref, mlw_ref, gdw_ref, wout_ref,
                 npost_ref, g2_ref, npre_ref, sh_ref, sc_ref, rwt_ref, wsg_ref, wsu_ref, wsd_ref,
                 x1_ref, hffn_ref, lt_ref, ys_ref, of_scr, ob_scr):
    ml_y = _head_rms(hf_ref[0] + hb_ref[0], ML_HEADS, ML_V) * mlw_ref[...] * jax.nn.sigmoid(og_ref[0])
    _from_grid_view(of_ref, of_scr)
    _from_grid_view(ob_ref, ob_scr)
    o_sum = jnp.concatenate([_pitched_rows(of_scr, g) + _pitched_rows(ob_scr, g) for g in range(of_scr.shape[0])],
                            axis=-1)
    gd_y = _head_rms(o_sum, GD_HEADS, GD_V) * gdw_ref[...] * _silu(z_ref[0])
    y = _dot(jnp.concatenate([ml_y, gd_y], axis=-1), wout_ref[...])
    x1 = x_ref[0] + g2_ref[0] * _rms(y, npost_ref[...])
    x1_ref[0] = x1
    hffn = _rms(x1, npre_ref[...]) * (1.0 + sc_ref[0]) + sh_ref[0]
    hb = hffn.astype(BF16)
    bits = pltpu.bitcast(hb.astype(F32), jnp.uint32)
    half = hffn.shape[1] // 2
    packed = (bits[:, half:] & jnp.uint32(0xFFFF0000)) | (bits[:, :half] >> 16)
    for c in range(half // LANES):
        hffn_ref[0, :, c, :] = packed[:, c * LANES:(c + 1) * LANES]
    lt_ref[...] = lax.dot_general(rwt_ref[...], hb, (((1,), (1,)), ((), ())), preferred_element_type=F32)
    hs = _silu(jnp.dot(hb, wsg_ref[...], preferred_element_type=F32)) * jnp.dot(hb, wsu_ref[...],
                                                                                preferred_element_type=F32)
    ys_ref[0] = _dot(hs, wsd_ref[...])


def _post(x, hf, hb, ml, of, ob, gz, mlw, gdw, wout, npost, g2, npre, sh, sc, rwt, wsg, wsu, wsd, tm):
    b, s, d = x.shape
    nt = s // tm
    hw = ML_HEADS * ML_V
    og_blk = (2 * ML_HEADS * ML_QK + ML_HEADS * ML_V) // hw
    tok = lambda n: pl.BlockSpec((1, tm, n), lambda bi, i: (bi, i, 0))
    full = lambda shp: pl.BlockSpec(shp, lambda bi, i: (0,) * len(shp))
    mod = pl.BlockSpec((1, 1, d), lambda bi, i: (bi, 0, 0))
    ne = rwt.shape[0]
    ds = wsg.shape[1]
    gview = pl.BlockSpec((1, tm // GRID_W, GRID_W * hw), lambda bi, i: (bi, i, 0))
    return pl.pallas_call(
        _post_kernel,
        grid=(b, nt),
        in_specs=[tok(d), tok(hw), tok(hw), pl.BlockSpec((1, tm, hw), lambda bi, i: (bi, i, og_blk)),
                  gview, gview, tok(hw), full((1, hw)), full((1, hw)), full((d, d)),
                  full((1, d)), mod, full((1, d)), mod, mod, full((ne, d)), full((d, ds)), full((d, ds)),
                  full((ds, d))],
        out_specs=[tok(d), pl.BlockSpec((1, tm, d // (2 * LANES), LANES), lambda bi, i: (bi, i, 0, 0)),
                   pl.BlockSpec((ne, tm), lambda bi, i: (0, bi * nt + i)), tok(d)],
        out_shape=[jax.ShapeDtypeStruct((b, s, d), F32), jax.ShapeDtypeStruct((b, s, d // (2 * LANES), LANES), jnp.uint32),
                   jax.ShapeDtypeStruct((ne, b * s), F32), jax.ShapeDtypeStruct((b, s, d), F32)],
        scratch_shapes=[pltpu.VMEM((hw // LANES, tm // GRID_W * GRID_PITCH, LANES), F32)] * 2,
        compiler_params=_cparams("parallel", "arbitrary"),
        name="post",
    )(x, hf, hb, ml, of, ob, gz, mlw, gdw, wout, npost, g2, npre, sh, sc, rwt, wsg, wsu, wsd)


def _route_kernel(lt_ref, bias_ref, idx_ref, gate_ref):
    ne, tn = lt_ref.shape
    gsz = ne // N_GROUPS
    scores = jax.nn.sigmoid(lt_ref[...])
    sel = scores + bias_ref[...]
    neg = -jnp.inf
    sel3 = sel.reshape(N_GROUPS, gsz, tn)
    io3 = lax.broadcasted_iota(jnp.int32, sel3.shape, 1)
    top1 = jnp.max(sel3, axis=1, keepdims=True)
    first = jnp.min(jnp.where(sel3 == top1, io3, gsz), axis=1, keepdims=True)
    top2 = jnp.max(jnp.where(io3 == first, neg, sel3), axis=1, keepdims=True)
    grp = (top1 + top2).reshape(N_GROUPS, tn)
    iog = lax.broadcasted_iota(jnp.int32, grp.shape, 0)
    keep = jnp.zeros(grp.shape, jnp.bool_)
    for _ in range(TOPK_GROUPS):
        m = jnp.max(grp, axis=0, keepdims=True)
        pick = iog == jnp.min(jnp.where(grp == m, iog, N_GROUPS), axis=0, keepdims=True)
        keep = keep | pick
        grp = jnp.where(pick, neg, grp)
    cand = jnp.where(keep.reshape(N_GROUPS, 1, tn), sel3, neg).reshape(ne, tn)
    ioe = lax.broadcasted_iota(jnp.int32, cand.shape, 0)
    idxs, gates = [], []
    for _ in range(TOP_K):
        m = jnp.max(cand, axis=0, keepdims=True)
        e = jnp.min(jnp.where(cand == m, ioe, ne), axis=0, keepdims=True)
        pick = ioe == e
        idxs.append(e)
        gates.append(jnp.sum(jnp.where(pick, scores, 0.0), axis=0, keepdims=True))
        cand = jnp.where(pick, neg, cand)
    gate = jnp.concatenate(gates, axis=0)
    idx_ref[...] = jnp.concatenate(idxs, axis=0)
    gate_ref[...] = gate / jnp.sum(gate, axis=0, keepdims=True) * ROUTED_SCALE


def _route(logits_t, bias_col, tn):
    ne, t = logits_t.shape
    return pl.pallas_call(
        _route_kernel,
        grid=(t // tn,),
        in_specs=[pl.BlockSpec((ne, tn), lambda i: (0, i)), pl.BlockSpec((ne, 1), lambda i: (0, 0))],
        out_specs=[pl.BlockSpec((TOP_K, tn), lambda i: (0, i)), pl.BlockSpec((TOP_K, tn), lambda i: (0, i))],
        out_shape=[jax.ShapeDtypeStruct((TOP_K, t), jnp.int32), jax.ShapeDtypeStruct((TOP_K, t), F32)],
        compiler_params=_cparams("parallel"),
        name="route",
    )(logits_t, bias_col)


def _experts_kernel(be_ref, np_ref,
                    tok_ref, tokn_ref, w_ref, wg0_ref, wu0_ref, wd0_ref, wg1_ref, wu1_ref, wd1_ref, h_hbm,
                    o_ref, xg, wgc, wuc, wdc, gsem):
    s = pl.program_id(0)
    n_pairs = np_ref[0]
    nct = o_ref.shape[0] // (2 * EXPERT_BLOCK)
    rows = nct * EXPERT_BLOCK
    w_refs = ((wg0_ref, wu0_ref, wd0_ref), (wg1_ref, wu1_ref, wd1_ref))

    def gather_copy(tref, p, j):
        return pltpu.make_async_copy(h_hbm.at[tref[0, p, j]], xg.at[p, j], gsem.at[p])

    def gather_wait(p):
        pltpu.make_async_copy(h_hbm.at[pl.ds(0, EXPERT_BLOCK)], xg.at[p], gsem.at[p]).wait()

    @pl.when(s >= n_pairs)
    def _():
        o_ref[...] = jnp.zeros(o_ref.shape, F32)

    @pl.when(s < n_pairs)
    def _():
        @pl.when(s == 0)
        def _():
            for p in range(2):
                for j in range(EXPERT_BLOCK):
                    gather_copy(tok_ref, p, j).start(priority=j % 2)

        for p in range(2):
            blk = 2 * s + p
            wg_ref, wu_ref, wd_ref = w_refs[p]
            gather_wait(p)

            @pl.when((blk == 0) | (be_ref[blk] != be_ref[jnp.maximum(blk - 1, 0)]))
            def _():
                wgc[...] = wg_ref[0].astype(BF16)
                wuc[...] = wu_ref[0].astype(BF16)
                wdc[...] = wd_ref[0].astype(BF16)

            words = jnp.concatenate([xg[p, :, c, :] for c in range(xg.shape[2])], axis=-1)
            xb = jnp.concatenate([pltpu.bitcast(words << 16, F32),
                                  pltpu.bitcast(words & jnp.uint32(0xFFFF0000), F32)], axis=-1).astype(BF16)
            for j in range(EXPERT_BLOCK):
                gather_copy(tokn_ref, p, j).start(priority=j % 2)
            hmid = _silu(jnp.dot(xb, wgc[...], preferred_element_type=F32)) * jnp.dot(xb, wuc[...],
                                                                                      preferred_element_type=F32)
            out = _dot(hmid, wdc[...])
            eye = (lax.broadcasted_iota(jnp.int32, (EXPERT_BLOCK, EXPERT_BLOCK), 0)
                   == lax.broadcasted_iota(jnp.int32, (EXPERT_BLOCK, EXPERT_BLOCK), 1))
            w_col = jnp.sum(jnp.where(eye, w_ref[0, p:p + 1, :], 0.0), axis=1, keepdims=True)
            out = out * w_col
            for c in range(nct):
                o_ref[pl.ds(p * rows + c, EXPERT_BLOCK, stride=nct), :] = out[:, c * LANES:(c + 1) * LANES]

        @pl.when(s == n_pairs - 1)
        def _():
            for p in range(2):
                gather_wait(p)


def _experts(hffn, block_e, n_pairs, row_tok, row_w, wg, wu, wd):
    d = wg.shape[1]
    nct = d // LANES
    npairs = row_tok.shape[0]
    de = wg.shape[2]
    last = npairs - 1
    smem_blk = lambda f: pl.BlockSpec((1, 2, EXPERT_BLOCK), f, memory_space=pltpu.SMEM)
    wspec = lambda shp, p: pl.BlockSpec((1,) + shp, lambda s, be, npu: (be[2 * s + p], 0, 0))
    pair_rows = 2 * EXPERT_BLOCK * nct
    grid_spec = pltpu.PrefetchScalarGridSpec(
        num_scalar_prefetch=2,
        grid=(npairs,),
        in_specs=[smem_blk(lambda s, be, npu: (s, 0, 0)),
                  smem_blk(lambda s, be, npu: (jnp.minimum(s + 1, last), 0, 0)),
                  pl.BlockSpec((1, 2, EXPERT_BLOCK), lambda s, be, npu: (s, 0, 0)),
                  wspec((d, de), 0), wspec((d, de), 0), wspec((de, d), 0),
                  wspec((d, de), 1), wspec((d, de), 1), wspec((de, d), 1),
                  pl.BlockSpec(memory_space=pl.ANY)],
        out_specs=pl.BlockSpec((pair_rows, LANES), lambda s, be, npu: (s, 0)),
        scratch_shapes=[pltpu.VMEM((2, EXPERT_BLOCK) + hffn.shape[1:], jnp.uint32),
                        pltpu.VMEM((d, de), BF16), pltpu.VMEM((d, de), BF16), pltpu.VMEM((de, d), BF16),
                        pltpu.SemaphoreType.DMA((2,))],
    )
    return pl.pallas_call(
        _experts_kernel,
        grid_spec=grid_spec,
        out_shape=jax.ShapeDtypeStruct((npairs * pair_rows, LANES), F32),
        compiler_params=_cparams("arbitrary"),
        name="experts",
    )(block_e, n_pairs, row_tok, row_tok, row_w, wg, wu, wd, wg, wu, wd, hffn)


def _combine_kernel(pos_ref, posn_ref, x1_ref, ys_ref, npost_ref, g5_ref, rows_hbm, o_ref, buf, sem):
    i = pl.program_id(0)
    tm = x1_ref.shape[1]
    nct = x1_ref.shape[2] // LANES
    cur = i % 2
    nxt = 1 - cur

    def copy(pref, b, k, t):
        src = rows_hbm.at[pl.ds(pl.multiple_of(pref[k, t] * nct, nct), nct)]
        dst = buf.at[b, pl.ds(pl.multiple_of((k * tm + t) * nct, nct), nct)]
        return pltpu.make_async_copy(src, dst, sem.at[b])

    def drain(b):
        pltpu.make_async_copy(rows_hbm.at[pl.ds(0, TOP_K * tm * nct)], buf.at[b], sem.at[b]).wait()

    def issue(pref, b):
        def body(t2, carry):
            for u in range(2):
                for k in range(TOP_K):
                    copy(pref, b, k, 2 * t2 + u).start(priority=k % 2)
            return carry
        lax.fori_loop(0, tm // 2, body, 0)

    @pl.when(i == 0)
    def _():
        issue(pos_ref, cur)

    @pl.when(i + 1 < pl.num_programs(0))
    def _():
        for t in range(tm):
            for k in range(TOP_K):
                copy(posn_ref, nxt, k, t).start(priority=k % 2)

    drain(cur)
    routed = []
    for c in range(nct):
        acc = buf[cur, pl.ds(c, tm, stride=nct), :]
        for k in range(1, TOP_K):
            acc = acc + buf[cur, pl.ds(k * tm * nct + c, tm, stride=nct), :]
        routed.append(acc)
    y = ys_ref[0] + jnp.concatenate(routed, axis=-1)
    o_ref[0] = x1_ref[0] + g5_ref[0] * _rms(y, npost_ref[...])


def _combine(x1, ys, rows, pos, npost, g5, tm):
    b, s, d = x1.shape
    nt = s // tm
    nct = d // LANES
    last = b * nt - 1
    tok = pl.BlockSpec((1, tm, d), lambda i: (i // nt, i % nt, 0))
    pos_blk = lambda f: pl.BlockSpec((TOP_K, tm), f, memory_space=pltpu.SMEM)
    return pl.pallas_call(
        _combine_kernel,
        grid=(b * nt,),
        in_specs=[pos_blk(lambda i: (0, i)), pos_blk(lambda i: (0, jnp.minimum(i + 1, last))), tok, tok,
                  pl.BlockSpec((1, d), lambda i: (0, 0)), pl.BlockSpec((1, 1, d), lambda i: (i // nt, 0, 0)),
                  pl.BlockSpec(memory_space=pl.ANY)],
        out_specs=tok,
        out_shape=jax.ShapeDtypeStruct((b, s, d), F32),
        scratch_shapes=[pltpu.VMEM((2, TOP_K * tm * nct, LANES), F32), pltpu.SemaphoreType.DMA((2,))],
        compiler_params=_cparams("arbitrary"),
        name="combine",
    )(pos, pos, x1, ys, npost, g5, rows)


def _dispatch_plan(idx_t, gate_t):
    k, t = idx_t.shape
    n_asg = k * t
    nb = n_asg // EXPERT_BLOCK + N_EXPERTS
    flat_e = idx_t.reshape(-1)
    id_bits = max(1, (n_asg - 1).bit_length())
    assert (N_EXPERTS - 1).bit_length() + id_bits <= 31
    packed = jnp.sort((flat_e << id_bits) | jnp.arange(n_asg, dtype=jnp.int32))
    order = packed & ((1 << id_bits) - 1)
    counts = jnp.zeros((N_EXPERTS,), jnp.int32).at[flat_e].add(1)
    padded = (counts + EXPERT_BLOCK - 1) // EXPERT_BLOCK * EXPERT_BLOCK
    start = jnp.cumsum(counts) - counts
    pend = jnp.cumsum(padded)
    pstart = pend - padded
    blk0 = jnp.arange(nb, dtype=jnp.int32) * EXPERT_BLOCK
    block_e = jnp.minimum(jnp.sum((pend[None, :] <= blk0[:, None]).astype(jnp.int32), axis=1), N_EXPERTS - 1)
    of_block = block_e[:, None] == jnp.arange(N_EXPERTS, dtype=jnp.int32)[None, :]
    per_block = lambda v: jnp.sum(jnp.where(of_block, v[None, :], 0), axis=1)
    assert nb % 2 == 0
    n_pairs = ((pend[-1] // EXPERT_BLOCK + 1) // 2).astype(jnp.int32).reshape(1)
    pos = blk0[:, None] - per_block(pstart)[:, None] + jnp.arange(EXPERT_BLOCK, dtype=jnp.int32)[None, :]
    valid = pos < per_block(counts)[:, None]
    src = jnp.clip(per_block(start)[:, None] + pos, 0, n_asg - 1)
    asg = order[src]
    row_tok = jnp.where(valid, asg % t, 0).astype(jnp.int32)
    row_w = jnp.where(valid, gate_t.reshape(-1)[asg], 0.0).astype(F32)
    i_sorted = jnp.arange(n_asg, dtype=jnp.int32)
    pad_before = jnp.sum(jnp.where(i_sorted[:, None] >= (start + counts)[None, :], (padded - counts)[None, :], 0),
                         axis=1)
    _, pos = lax.sort((order, i_sorted + pad_before), num_keys=1)
    pos = pos.reshape(k, t)
    shp = (nb // 2, 2, EXPERT_BLOCK)
    return block_e, n_pairs, row_tok.reshape(shp), row_w.reshape(shp), pos


def _pack_in_weights(w_in, ml_i_bias, ml_f_bias, gd_dt_bias):
    d = w_in.shape[0]
    nml = 2 * ML_HEADS * ML_QK + 2 * ML_HEADS * ML_V
    ml_cols = nml + 4 * ML_HEADS
    ngq = GD_HEADS * (2 * GD_QK + GD_V)
    ngz = GD_HEADS * GD_V
    wml = w_in[:, :nml].astype(BF16)
    wgq = w_in[:, ml_cols:ml_cols + ngq].astype(BF16)
    wgz = w_in[:, ml_cols + ngq:ml_cols + ngq + ngz].astype(BF16)
    wg = jnp.zeros((d, GATE_LANES), F32)
    wg = wg.at[:, ML_GATE0:ML_GATE0 + 16].set(w_in[:, nml:ml_cols])
    wg = wg.at[:, GD_GATE0:GD_GATE0 + 16].set(w_in[:, ml_cols + ngq + ngz:])
    gb = jnp.zeros((GATE_LANES,), F32)
    gb = gb.at[ML_GATE0:ML_GATE0 + 16].set(jnp.stack([ml_i_bias, ml_f_bias], axis=1).reshape(-1))
    gb = gb.at[GD_GATE0:GD_GATE0 + 16].set(jnp.stack([gd_dt_bias, jnp.zeros_like(gd_dt_bias)], axis=1).reshape(-1))
    return wml, wgq, wgz, wg.astype(BF16), gb.reshape(1, GATE_LANES)


def _mixer(x, ctx, mod, mod_ctx, norm_pre_mix, w_in, ml_i_bias, ml_f_bias, gd_conv_w, gd_a_log, gd_dt_bias):
    b, s, d = x.shape
    sc = ctx.shape[1]
    wml, wgq, wgz, wg, gb = _pack_in_weights(w_in, ml_i_bias, ml_f_bias, gd_dt_bias)
    nw = norm_pre_mix.reshape(1, d)
    ctx_mod = lambda j: jnp.broadcast_to(mod_ctx[j].reshape(1, 1, d), (b, 1, d))
    ml_c, gq_c, _, g_c = _proj(ctx, nw, ctx_mod(0), ctx_mod(1), wml, wgq, wgz, wg, gb, tm=sc, grid_view=False)
    ml_l, gqv_l, gz_l, g_l, gv_l = _proj(x, nw, mod[0], mod[1], wml, wgq, wgz, wg, gb, tm=TM_PROJ, grid_view=True)

    c0 = jnp.zeros((b, N_CHAINS, ML_QK, 2 * ML_V), F32)
    m0 = jnp.zeros((b, N_CHAINS, 1, 1), F32)
    _, _, c1, m1 = _mlstm(ml_c, g_c, c0, m0)
    hf, hb, _, _ = _mlstm(ml_l, g_l, c1, m1)

    neg_a = jnp.zeros((GATE_LANES,), F32)
    neg_a = neg_a.at[GD_GATE0:GD_GATE0 + 16].set(
        jnp.stack([-jnp.exp(gd_a_log), jnp.zeros_like(gd_a_log)], axis=1).reshape(-1)).reshape(1, GATE_LANES)
    qn_c = _gdconv_ctx(gq_c, gd_conv_w)
    qnv_l = _gdconv_lat(gqv_l, gd_conv_w)
    s0 = jnp.zeros((b, N_CHAINS, GD_QK, GD_V), F32)
    hdim = GD_HEADS * GD_V
    _, _, s1 = _gdn(qn_c, g_c, neg_a, s0, sc // CHUNK, lambda bi, n: (bi, n, 0), (b, sc, hdim))
    rows = s // GRID_W
    cpc = rows // CHUNK
    col_idx = lambda bi, n: (bi, n % cpc, n // cpc)
    ofv, obv, _ = _gdn(qnv_l, gv_l, neg_a, s1, s // CHUNK, col_idx, (b, rows, GRID_W * hdim))
    return hf, hb, ml_l, ofv, obv, gz_l


def kernel(x, c, ctx, c_ctx, w_ada, b_ada, norm_pre_mix, norm_post_mix, norm_pre_ffn, norm_post_ffn, w_in,
           ml_i_bias, ml_f_bias, ml_norm_w, gd_conv_w, gd_a_log, gd_dt_bias, gd_norm_w, w_out, router_w,
           router_bias, w_gate, w_up, w_down, ws_gate, ws_up, ws_down):
    b, s, d = x.shape
    depth = w_ada.shape[0]
    assert depth == 1, "the context stream update of deeper stacks is not implemented"
    ly = 0
    cc = jnp.zeros((16, d), F32).at[:b].set(c).at[b].set(c_ctx)
    mod_all = _ada(cc, w_ada[ly], b_ada[ly])
    mod = [mod_all[:b, j * d:(j + 1) * d].reshape(b, 1, d) for j in range(6)]
    mod_ctx = [mod_all[b, j * d:(j + 1) * d] for j in range(6)]

    hf, hb, ml_l, of, ob, gz_l = _mixer(x, ctx, mod, mod_ctx, norm_pre_mix[ly], w_in[ly], ml_i_bias[ly],
                                        ml_f_bias[ly], gd_conv_w[ly], gd_a_log[ly], gd_dt_bias[ly])

    row = lambda v: v.reshape(1, -1)
    x1, hffn, logits_t, ys = _post(
        x, hf, hb, ml_l, of, ob, gz_l, row(ml_norm_w[ly]), row(jnp.tile(gd_norm_w[ly], GD_HEADS)),
        w_out[ly].astype(BF16), row(norm_post_mix[ly]), mod[2], row(norm_pre_ffn[ly]), mod[3], mod[4],
        router_w[ly].T.astype(BF16), ws_gate[ly].astype(BF16), ws_up[ly].astype(BF16), ws_down[ly].astype(BF16),
        tm=TM_POST)

    idx_t, gate_t = _route(logits_t, router_bias[ly].reshape(-1, 1), tn=TN_ROUTE)
    block_e, n_pairs, row_tok, row_w, pos = _dispatch_plan(idx_t, gate_t)
    t = b * s
    rows = _experts(hffn.reshape(t, d // (2 * LANES), LANES), block_e, n_pairs, row_tok, row_w,
                    w_gate[ly], w_up[ly], w_down[ly])
    return _combine(x1, ys, rows, pos, row(norm_post_ffn[ly]), mod[5], tm=TM_COMBINE)
```

```python
import functools

import jax
import jax.numpy as jnp
from jax import lax
from jax.experimental import pallas as pl
from jax.experimental.pallas import tpu as pltpu

EPS = 1e-6
CHUNK = 64
GRID_W = 64
ML_HEADS, ML_QK, ML_V = 4, 64, 128
GD_HEADS, GD_QK, GD_V = 4, 128, 128
CONV_W = 5
N_EXPERTS, TOP_K, N_GROUPS, TOPK_GROUPS = 256, 8, 8, 4
ROUTED_SCALE = 2.5
EXPERT_BLOCK = 128
N_CHAINS = 8
ML_SCAN_BATCH, GD_SCAN_BATCH = 1, 4
LANES = 128
GATE_LANES = LANES
ML_GATE0, GD_GATE0 = 0, 16

F32 = jnp.float32
BF16 = jnp.bfloat16
HI = lax.Precision.HIGHEST
VMEM_LIMIT = 56 * 1024 * 1024
TM_PROJ, TM_POST, TM_COMBINE, TN_ROUTE, TN_ADA = 512, 512, 256, 512, 1536


def _cparams(*sem):
    return pltpu.CompilerParams(dimension_semantics=sem, vmem_limit_bytes=VMEM_LIMIT)


def _dot(a, b):
    return jnp.dot(a.astype(BF16), b.astype(BF16), preferred_element_type=F32)


def _dot_nt(a, b):
    return lax.dot_general(a.astype(BF16), b.astype(BF16), (((1,), (1,)), ((), ())), preferred_element_type=F32)


def _dot_tn(a, b):
    return lax.dot_general(a.astype(BF16), b.astype(BF16), (((0,), (0,)), ((), ())), preferred_element_type=F32)


def _dot_hi(a, b):
    return jnp.dot(a, b, precision=HI, preferred_element_type=F32)


def _dot_nt_hi(a, b):
    return lax.dot_general(a, b, (((1,), (1,)), ((), ())), precision=HI, preferred_element_type=F32)


def _transpose_hi(x):
    n = x.shape[1]
    eye = (lax.broadcasted_iota(jnp.int32, (n, n), 0) == lax.broadcasted_iota(jnp.int32, (n, n), 1)).astype(F32)
    return _dot_nt_hi(eye, x)


def _silu(x):
    return x * jax.nn.sigmoid(x)


def _past_mask(reverse):
    t = lax.broadcasted_iota(jnp.int32, (CHUNK, CHUNK), 0)
    s = lax.broadcasted_iota(jnp.int32, (CHUNK, CHUNK), 1)
    return (s >= t, s > t) if reverse else (s <= t, s < t)


def _ada_kernel(c_ref, w_ref, b_ref, o_ref):
    o_ref[...] = _dot(_silu(c_ref[...]), w_ref[...]) + b_ref[...]


def _ada(cc, w_ada, b_ada):
    rows, d = cc.shape
    n = w_ada.shape[1]
    tn = TN_ADA
    return pl.pallas_call(
        _ada_kernel,
        grid=(n // tn,),
        in_specs=[pl.BlockSpec((rows, d), lambda j: (0, 0)),
                  pl.BlockSpec((d, tn), lambda j: (0, j)),
                  pl.BlockSpec((1, tn), lambda j: (0, j))],
        out_specs=pl.BlockSpec((rows, tn), lambda j: (0, j)),
        out_shape=jax.ShapeDtypeStruct((rows, n), F32),
        compiler_params=_cparams("arbitrary"),
        name="ada",
    )(cc, w_ada, b_ada.reshape(1, n))


GRID_PITCH = GRID_W + 8


def _to_grid_view(src_ref, dst_ref):
    ng = src_ref.shape[0]
    r = src_ref.shape[1] // GRID_PITCH
    for c in range(GRID_W):
        for g in range(ng):
            lo = (c * ng + g) * LANES
            dst_ref[0, :, lo:lo + LANES] = src_ref[g, pl.ds(c, r, stride=GRID_PITCH), :]


def _from_grid_view(src_ref, dst_ref):
    ng = dst_ref.shape[0]
    r = dst_ref.shape[1] // GRID_PITCH
    for c in range(GRID_W):
        for g in range(ng):
            lo = (c * ng + g) * LANES
            dst_ref[g, pl.ds(c, r, stride=GRID_PITCH), :] = src_ref[0, :, lo:lo + LANES]


def _pitched_rows(ref, g):
    r = ref.shape[1] // GRID_PITCH
    return jnp.concatenate([ref[g, i * GRID_PITCH:i * GRID_PITCH + GRID_W, :] for i in range(r)], axis=0)


def _proj_kernel(grid_view, x_ref, nw_ref, sh_ref, sc_ref, wml_ref, wgq_ref, wgz_ref, wg_ref, gb_ref, *refs):
    if grid_view:
        ml_ref, gqv_ref, gz_ref, g_ref, gv_ref, gq_scr, g_scr = refs
    else:
        ml_ref, gq_ref, gz_ref, g_ref = refs
    x = x_ref[0]
    xn = x * lax.rsqrt(jnp.mean(x * x, axis=-1, keepdims=True) + EPS) * nw_ref[...]
    h = (xn * (1.0 + sc_ref[0]) + sh_ref[0]).astype(BF16)
    ml_ref[0] = jnp.dot(h, wml_ref[...], preferred_element_type=F32)
    gz_ref[0] = jnp.dot(h, wgz_ref[...], preferred_element_type=F32)
    gates = jnp.dot(h, wg_ref[...], preferred_element_type=F32) + gb_ref[...]
    g_ref[0] = gates
    gq = jnp.dot(h, wgq_ref[...], preferred_element_type=F32)
    if grid_view:
        for r in range(x.shape[0] // GRID_W):
            rows = slice(r * GRID_W, (r + 1) * GRID_W)
            prow = slice(r * GRID_PITCH, r * GRID_PITCH + GRID_W)
            g_scr[0, prow, :] = gates[rows]
            for g in range(gq_scr.shape[0]):
                gq_scr[g, prow, :] = gq[rows, g * LANES:(g + 1) * LANES]
        _to_grid_view(gq_scr, gqv_ref)
        _to_grid_view(g_scr, gv_ref)
    else:
        gq_ref[0] = gq


def _proj(x, norm_w, shift, scale, wml, wgq, wgz, wg, gbias, tm, grid_view):
    b, s, d = x.shape
    nml, ngq, ngz = wml.shape[1], wgq.shape[1], wgz.shape[1]
    full = lambda shp: pl.BlockSpec(shp, lambda bi, i: (0,) * len(shp))
    tok = lambda n: pl.BlockSpec((1, tm, n), lambda bi, i: (bi, i, 0))
    mod = pl.BlockSpec((1, 1, d), lambda bi, i: (bi, 0, 0))
    if grid_view:
        rt = tm // GRID_W
        view = lambda n: pl.BlockSpec((1, rt, GRID_W * n), lambda bi, i: (bi, i, 0))
        vshape = lambda n: jax.ShapeDtypeStruct((b, s // GRID_W, GRID_W * n), F32)
        out_specs = [tok(nml), view(ngq), tok(ngz), tok(GATE_LANES), view(GATE_LANES)]
        out_shape = [jax.ShapeDtypeStruct((b, s, nml), F32), vshape(ngq), jax.ShapeDtypeStruct((b, s, ngz), F32),
                     jax.ShapeDtypeStruct((b, s, GATE_LANES), F32), vshape(GATE_LANES)]
        scratch = [pltpu.VMEM((ngq // LANES, rt * GRID_PITCH, LANES), F32), pltpu.VMEM((1, rt * GRID_PITCH, LANES), F32)]
    else:
        out_specs = [tok(nml), tok(ngq), tok(ngz), tok(GATE_LANES)]
        out_shape = [jax.ShapeDtypeStruct((b, s, n), F32) for n in (nml, ngq, ngz, GATE_LANES)]
        scratch = []
    return pl.pallas_call(
        functools.partial(_proj_kernel, grid_view),
        grid=(b, s // tm),
        in_specs=[tok(d), full((1, d)), mod, mod, full((d, nml)), full((d, ngq)), full((d, ngz)),
                  full((d, GATE_LANES)), full((1, GATE_LANES))],
        out_specs=out_specs,
        out_shape=out_shape,
        scratch_shapes=scratch,
        compiler_params=_cparams("parallel", "arbitrary"),
        name="proj",
    )(x, norm_w, shift, scale, wml, wgq, wgz, wg, gbias)


def _mlstm_kernel(mlf_ref, mlb_ref, gf_ref, gb_ref, c0_ref, m0_ref, hf_ref, hb_ref, cn_ref, mn_ref, c_scr, m_scr):
    i = pl.program_id(1)

    @pl.when(i == 0)
    def _():
        c_scr[...] = c0_ref[...]
        m_scr[...] = m0_ref[...]

    nbat = mlf_ref.shape[0]
    past = [_past_mask(d == 1)[0] for d in range(2)]
    ml_refs, g_refs, h_refs = (mlf_ref, mlb_ref), (gf_ref, gb_ref), (hf_ref, hb_ref)
    bd = [(bb, d) for bb in range(nbat) for d in range(2)]
    g = [g_refs[d][bb] for bb, d in bd]
    ls = [jax.nn.log_sigmoid(x) for x in g]
    bcol = [_dot_hi(past[d].astype(F32), ls[j]) for j, (bb, d) in enumerate(bd)]
    tot = [jnp.sum(x, axis=0, keepdims=True) for x in ls]
    g_t = [_transpose_hi(x) for x in g]
    b_t = [_transpose_hi(x) for x in bcol]

    chains = [(bb, d, hd) for bb in range(nbat) for d in range(2) for hd in range(ML_HEADS)]
    nc = range(len(chains))
    k0, v0 = ML_HEADS * ML_QK, 2 * ML_HEADS * ML_QK
    ones_col = (lax.broadcasted_iota(jnp.int32, (CHUNK, ML_V), 1) == 0).astype(F32)
    q, k, v, i_col, b_col, b_end, log_d = [], [], [], [], [], [], []
    for bb, d, hd in chains:
        ci = ML_GATE0 + d * 8 + hd
        cf = ci + ML_HEADS
        j = bb * 2 + d
        q.append(ml_refs[d][bb, :, hd * ML_QK:(hd + 1) * ML_QK])
        k.append(ml_refs[d][bb, :, k0 + hd * ML_QK:k0 + (hd + 1) * ML_QK] * (ML_QK ** -0.5))
        v.append(jnp.concatenate([ml_refs[d][bb, :, v0 + hd * ML_V:v0 + (hd + 1) * ML_V], ones_col], axis=-1))
        i_col.append(g[j][:, ci:ci + 1])
        b_col.append(bcol[j][:, cf:cf + 1])
        b_end.append(tot[j][:, cf:cf + 1])
        log_d.append(jnp.where(past[d], b_col[-1] - b_t[j][cf:cf + 1, :] + g_t[j][ci:ci + 1, :], -jnp.inf))
    st_idx = [(bb, d * ML_HEADS + hd) for bb, d, hd in chains]
    c_st = [c_scr[ix] for ix in st_idx]
    m_st = [m_scr[ix] for ix in st_idx]
    log_prev = [b_col[c] + m_st[c] for c in nc]
    m_t = [jnp.maximum(log_prev[c], jnp.max(log_d[c], axis=-1, keepdims=True)) for c in nc]
    qk = [_dot_nt(q[c], k[c]) for c in nc]
    qc = [_dot(q[c], c_st[c]) for c in nc]
    s = [qk[c] * jnp.exp(log_d[c] - m_t[c]) for c in nc]
    w_prev = [jnp.exp(log_prev[c] - m_t[c]) for c in nc]
    sv = [_dot(s[c], v[c]) for c in nc]
    log_s = [b_end[c] - b_col[c] + i_col[c] for c in nc]
    m_new = [jnp.maximum(b_end[c] + m_st[c], jnp.max(log_s[c], axis=0, keepdims=True)) for c in nc]
    kw = [k[c] * jnp.exp(log_s[c] - m_new[c]) for c in nc]
    w_c = [jnp.exp(b_end[c] + m_st[c] - m_new[c]) for c in nc]
    kv = [_dot_tn(kw[c], v[c]) for c in nc]
    numden = [sv[c] + w_prev[c] * qc[c] for c in nc]
    scale = [1.0 / jnp.maximum(jnp.abs(numden[c][:, ML_V:ML_V + 1]), jnp.exp(-m_t[c])) for c in nc]
    for c, (bb, d, hd) in enumerate(chains):
        h_refs[d][bb, :, hd * ML_V:(hd + 1) * ML_V] = numden[c][:, :ML_V] * scale[c]
        c_scr[st_idx[c]] = w_c[c] * c_st[c] + kv[c]
        m_scr[st_idx[c]] = m_new[c]

    @pl.when(i == pl.num_programs(1) - 1)
    def _():
        cn_ref[...] = c_scr[...]
        mn_ref[...] = m_scr[...]


def _mlstm(ml, gates, c0, m0):
    b, s, nml = ml.shape
    nc = s // CHUNK
    nbat = ML_SCAN_BATCH
    fwd = lambda n: pl.BlockSpec((nbat, CHUNK, n), lambda bi, i: (bi, i, 0))
    bwd = lambda n: pl.BlockSpec((nbat, CHUNK, n), lambda bi, i: (bi, nc - 1 - i, 0))
    st = lambda shp: pl.BlockSpec((nbat,) + shp, lambda bi, i: (bi,) + (0,) * len(shp))
    hdim = ML_HEADS * ML_V
    cshape = (N_CHAINS, ML_QK, 2 * ML_V)
    return pl.pallas_call(
        _mlstm_kernel,
        grid=(b // nbat, nc),
        in_specs=[fwd(nml), bwd(nml), fwd(GATE_LANES), bwd(GATE_LANES), st(cshape), st((N_CHAINS, 1, 1))],
        out_specs=[fwd(hdim), bwd(hdim), st(cshape), st((N_CHAINS, 1, 1))],
        out_shape=[jax.ShapeDtypeStruct((b, s, hdim), F32), jax.ShapeDtypeStruct((b, s, hdim), F32),
                   jax.ShapeDtypeStruct(c0.shape, F32), jax.ShapeDtypeStruct(m0.shape, F32)],
        scratch_shapes=[pltpu.VMEM((nbat,) + cshape, F32), pltpu.VMEM((nbat, N_CHAINS, 1, 1), F32)],
        compiler_params=_cparams("parallel", "arbitrary"),
        name="mlstm",
    )(ml, ml, gates, gates, c0, m0)


def _gdconv_kernel(has_halo, *refs):
    if has_halo:
        x_ref, prev_ref, next_ref, w_ref, o_ref, xp_ref = refs
    else:
        x_ref, w_ref, o_ref, xp_ref = refs
    rows = x_ref.shape[1]
    nch = x_ref.shape[2]
    pad = 8
    zero = jnp.zeros((pad, nch), F32)
    if has_halo:
        c = pl.program_id(1)
        xp_ref[0:pad, :] = jnp.where(c > 0, prev_ref[0], zero)
        xp_ref[pad + rows:, :] = jnp.where(c < pl.num_programs(1) - 1, next_ref[0], zero)
    else:
        xp_ref[0:pad, :] = zero
        xp_ref[pad + rows:, :] = zero
    xp_ref[pad:pad + rows, :] = x_ref[0]
    half = CONV_W // 2
    for lc in range(nch // 128):
        sl = slice(lc * 128, (lc + 1) * 128)
        acc = None
        for j in range(CONV_W):
            term = xp_ref[pad - half + j:pad - half + j + rows, sl] * w_ref[j:j + 1, sl]
            acc = term if acc is None else acc + term
        y = _silu(acc)
        if lc < 2 * GD_HEADS:
            y = y * lax.rsqrt(jnp.sum(y * y, axis=-1, keepdims=True) + EPS)
        if lc < GD_HEADS:
            y = y * (GD_QK ** -0.5)
        o_ref[0, :, sl] = y


def _gdconv_ctx(qkv, conv_w):
    b, s, nch = qkv.shape
    return pl.pallas_call(
        functools.partial(_gdconv_kernel, False),
        grid=(b,),
        in_specs=[pl.BlockSpec((1, s, nch), lambda bi: (bi, 0, 0)), pl.BlockSpec((CONV_W, nch), lambda bi: (0, 0))],
        out_specs=pl.BlockSpec((1, s, nch), lambda bi: (bi, 0, 0)),
        out_shape=jax.ShapeDtypeStruct((b, s, nch), F32),
        scratch_shapes=[pltpu.VMEM((s + 16, nch), F32)],
        compiler_params=_cparams("parallel"),
        name="gdconv_ctx",
    )(qkv, conv_w)


def _gdconv_lat(view, conv_w):
    b, rows, wn = view.shape
    nch = wn // GRID_W
    rb = rows // 8
    return pl.pallas_call(
        functools.partial(_gdconv_kernel, True),
        grid=(b, GRID_W),
        in_specs=[pl.BlockSpec((1, rows, nch), lambda bi, c: (bi, 0, c)),
                  pl.BlockSpec((1, 8, nch), lambda bi, c: (bi, rb - 1, jnp.maximum(c - 1, 0))),
                  pl.BlockSpec((1, 8, nch), lambda bi, c: (bi, 0, jnp.minimum(c + 1, GRID_W - 1))),
                  pl.BlockSpec((CONV_W, nch), lambda bi, c: (0, 0))],
        out_specs=pl.BlockSpec((1, rows, nch), lambda bi, c: (bi, 0, c)),
        out_shape=jax.ShapeDtypeStruct(view.shape, F32),
        scratch_shapes=[pltpu.VMEM((rows + 16, nch), F32)],
        compiler_params=_cparams("parallel", "arbitrary"),
        name="gdconv_lat",
    )(view, view, view, conv_w)


SOLVE_BLOCK = 16


def _hi_lo(x):
    hi = x.astype(BF16).astype(F32)
    return hi, x - hi


def _dot_split(a, b):
    a_hi, a_lo = _hi_lo(a)
    b_hi, b_lo = _hi_lo(b)
    lhs = jnp.concatenate([a_hi, a_hi, a_lo], axis=1).astype(BF16)
    rhs = jnp.concatenate([b_hi, b_lo, b_hi], axis=0).astype(BF16)
    return jnp.dot(lhs, rhs, preferred_element_type=F32)


def _unit_triangular_inverses(ns):
    c = ns[0].shape[0]
    row = lax.broadcasted_iota(jnp.int32, (c, c), 0)
    col = lax.broadcasted_iota(jnp.int32, (c, c), 1)
    eye = (row == col).astype(F32)
    in_diag_block = (row // SOLVE_BLOCK) == (col // SOLVE_BLOCK)
    mm = lambda a_list, b_list: [_dot_split(a, b) for a, b in zip(a_list, b_list)]

    n_d = [jnp.where(in_diag_block, n, 0.0) for n in ns]
    x = n_d
    d_inv = [eye - n for n in n_d]
    for _ in range(SOLVE_BLOCK.bit_length() - 2):
        x = mm(x, x)
        d_inv = [d + dx for d, dx in zip(d_inv, mm(d_inv, x))]
    m = mm(d_inv, [n - nd for n, nd in zip(ns, n_d)])
    assert c // SOLVE_BLOCK == 4
    i_minus_m = [eye - mi for mi in m]
    q = [a + b for a, b in zip(i_minus_m, mm(i_minus_m, mm(m, m)))]
    return mm(q, d_inv)


def _gdn_kernel(qf_ref, qb_ref, gf_ref, gb_ref, na_ref, s0_ref, of_ref, ob_ref, sn_ref, s_scr):
    i = pl.program_id(1)

    @pl.when(i == 0)
    def _():
        s_scr[...] = s0_ref[...]

    nbat = qf_ref.shape[0]
    nqk = GD_HEADS * GD_QK
    masks = [_past_mask(d == 1) for d in range(2)]
    x_refs, g_refs, o_refs = (qf_ref, qb_ref), (gf_ref, gb_ref), (of_ref, ob_ref)
    bd = [(bb, d) for bb in range(nbat) for d in range(2)]
    gates = [g_refs[d][bb] for bb, d in bd]
    glog = [na_ref[...] * jax.nn.softplus(g) for g in gates]
    beta_all = [jax.nn.sigmoid(g) for g in gates]
    gcum = [_dot_hi(masks[d][0].astype(F32), glog[j]) for j, (bb, d) in enumerate(bd)]
    gtot = [jnp.sum(g, axis=0, keepdims=True) for g in glog]
    gcum_t = [_transpose_hi(g) for g in gcum]

    chains = [(bb, d, hd) for bb in range(nbat) for d in range(2) for hd in range(GD_HEADS)]
    q, k, v, g_col, beta, g_end, decay = [], [], [], [], [], [], []
    for bb, d, hd in chains:
        ca = GD_GATE0 + d * 8 + hd
        j = bb * 2 + d
        q.append(x_refs[d][bb, :, hd * GD_QK:(hd + 1) * GD_QK])
        k.append(x_refs[d][bb, :, nqk + hd * GD_QK:nqk + (hd + 1) * GD_QK])
        v.append(x_refs[d][bb, :, 2 * nqk + hd * GD_V:2 * nqk + (hd + 1) * GD_V])
        g_col.append(gcum[j][:, ca:ca + 1])
        beta.append(beta_all[j][:, ca + GD_HEADS:ca + GD_HEADS + 1])
        g_end.append(gtot[j][:, ca:ca + 1])
        decay.append(jnp.exp(jnp.where(masks[d][0], g_col[-1] - gcum_t[j][ca:ca + 1, :], -jnp.inf)))
    nc = range(len(chains))
    kk = [_dot_nt(k[c], k[c]) for c in nc]
    xs = [jnp.where(masks[chains[c][1]][1], beta[c] * kk[c] * decay[c], 0.0) for c in nc]
    ps = _unit_triangular_inverses(xs)
    uw = [_dot_split(ps[c], jnp.concatenate([v[c] * beta[c], k[c] * (beta[c] * jnp.exp(g_col[c]))], axis=-1))
          for c in nc]
    qk = [_dot_nt(q[c], k[c]) * decay[c] for c in nc]
    s_st = [s_scr[bb, d * GD_HEADS + hd] for bb, d, hd in chains]
    v_new = [uw[c][:, :GD_V] - _dot(uw[c][:, GD_V:], s_st[c]) for c in nc]
    o_loc = [_dot(q[c] * jnp.exp(g_col[c]), s_st[c]) for c in nc]
    o_new = [o_loc[c] + _dot(qk[c], v_new[c]) for c in nc]
    s_new = [s_st[c] * jnp.exp(g_end[c]) + _dot_tn(k[c] * jnp.exp(g_end[c] - g_col[c]), v_new[c]) for c in nc]
    for c, (bb, d, hd) in enumerate(chains):
        o_refs[d][bb, :, hd * GD_V:(hd + 1) * GD_V] = o_new[c]
        s_scr[bb, d * GD_HEADS + hd] = s_new[c]

    @pl.when(i == pl.num_programs(1) - 1)
    def _():
        sn_ref[...] = s_scr[...]


def _gdn(qkv_view, gates_view, neg_a, s0, nc, idx_fn, out_view_shape):
    b = qkv_view.shape[0]
    nbat = GD_SCAN_BATCH
    nqkv = 2 * GD_HEADS * GD_QK + GD_HEADS * GD_V
    hdim = GD_HEADS * GD_V
    fwd = lambda n: pl.BlockSpec((nbat, CHUNK, n), lambda bi, i: idx_fn(bi, i))
    bwd = lambda n: pl.BlockSpec((nbat, CHUNK, n), lambda bi, i: idx_fn(bi, nc - 1 - i))
    st = pl.BlockSpec((nbat, N_CHAINS, GD_QK, GD_V), lambda bi, i: (bi, 0, 0, 0))
    return pl.pallas_call(
        _gdn_kernel,
        grid=(b // nbat, nc),
        in_specs=[fwd(nqkv), bwd(nqkv), fwd(GATE_LANES), bwd(GATE_LANES),
                  pl.BlockSpec((1, GATE_LANES), lambda bi, i: (0, 0)), st],
        out_specs=[fwd(hdim), bwd(hdim), st],
        out_shape=[jax.ShapeDtypeStruct(out_view_shape, F32), jax.ShapeDtypeStruct(out_view_shape, F32),
                   jax.ShapeDtypeStruct(s0.shape, F32)],
        scratch_shapes=[pltpu.VMEM((nbat, N_CHAINS, GD_QK, GD_V), F32)],
        compiler_params=_cparams("parallel", "arbitrary"),
        name="gdn",
    )(qkv_view, qkv_view, gates_view, gates_view, neg_a, s0)


def _head_rms(t, nheads, width):
    outs = []
    for hd in range(nheads):
        th = t[:, hd * width:(hd + 1) * width]
        outs.append(th * lax.rsqrt(jnp.mean(th * th, axis=-1, keepdims=True) + EPS))
    return jnp.concatenate(outs, axis=-1)


def _rms(t, w):
    return t * lax.rsqrt(jnp.mean(t * t, axis=-1, keepdims=True) + EPS) * w


def _post_kernel(x_ref, hf_ref, hb_ref, og_ref, of_ref, ob_ref, z_ref, mlw_ref, gdw_ref, wout_ref,
                 npost_ref, g2_ref, npre_ref, sh_ref, sc_ref, rwt_ref, wsg_ref, wsu_ref, wsd_ref,
                 x1_ref, hffn_ref, lt_ref, ys_ref, of_scr, ob_scr):
    ml_y = _head_rms(hf_ref[0] + hb_ref[0], ML_HEADS, ML_V) * mlw_ref[...] * jax.nn.sigmoid(og_ref[0])
    _from_grid_view(of_ref, of_scr)
    _from_grid_view(ob_ref, ob_scr)
    o_sum = jnp.concatenate([_pitched_rows(of_scr, g) + _pitched_rows(ob_scr, g) for g in range(of_scr.shape[0])],
                            axis=-1)
    gd_y = _head_rms(o_sum, GD_HEADS, GD_V) * gdw_ref[...] * _silu(z_ref[0])
    y = _dot(jnp.concatenate([ml_y, gd_y], axis=-1), wout_ref[...])
    x1 = x_ref[0] + g2_ref[0] * _rms(y, npost_ref[...])
    x1_ref[0] = x1
    hffn = _rms(x1, npre_ref[...]) * (1.0 + sc_ref[0]) + sh_ref[0]
    hb = hffn.astype(BF16)
    bits = pltpu.bitcast(hb.astype(F32), jnp.uint32)
    half = hffn.shape[1] // 2
    packed = (bits[:, half:] & jnp.uint32(0xFFFF0000)) | (bits[:, :half] >> 16)
    for c in range(half // LANES):
        hffn_ref[0, :, c, :] = packed[:, c * LANES:(c + 1) * LANES]
    lt_ref[...] = lax.dot_general(rwt_ref[...], hb, (((1,), (1,)), ((), ())), preferred_element_type=F32)
    hs = _silu(jnp.dot(hb, wsg_ref[...], preferred_element_type=F32)) * jnp.dot(hb, wsu_ref[...],
                                                                                preferred_element_type=F32)
    ys_ref[0] = _dot(hs, wsd_ref[...])


def _post(x, hf, hb, ml, of, ob, gz, mlw, gdw, wout, npost, g2, npre, sh, sc, rwt, wsg, wsu, wsd, tm):
    b, s, d = x.shape
    nt = s // tm
    hw = ML_HEADS * ML_V
    og_blk = (2 * ML_HEADS * ML_QK + ML_HEADS * ML_V) // hw
    tok = lambda n: pl.BlockSpec((1, tm, n), lambda bi, i: (bi, i, 0))
    full = lambda shp: pl.BlockSpec(shp, lambda bi, i: (0,) * len(shp))
    mod = pl.BlockSpec((1, 1, d), lambda bi, i: (bi, 0, 0))
    ne = rwt.shape[0]
    ds = wsg.shape[1]
    gview = pl.BlockSpec((1, tm // GRID_W, GRID_W * hw), lambda bi, i: (bi, i, 0))
    return pl.pallas_call(
        _post_kernel,
        grid=(b, nt),
        in_specs=[tok(d), tok(hw), tok(hw), pl.BlockSpec((1, tm, hw), lambda bi, i: (bi, i, og_blk)),
                  gview, gview, tok(hw), full((1, hw)), full((1, hw)), full((d, d)),
                  full((1, d)), mod, full((1, d)), mod, mod, full((ne, d)), full((d, ds)), full((d, ds)),
                  full((ds, d))],
        out_specs=[tok(d), pl.BlockSpec((1, tm, d // (2 * LANES), LANES), lambda bi, i: (bi, i, 0, 0)),
                   pl.BlockSpec((ne, tm), lambda bi, i: (0, bi * nt + i)), tok(d)],
        out_shape=[jax.ShapeDtypeStruct((b, s, d), F32), jax.ShapeDtypeStruct((b, s, d // (2 * LANES), LANES), jnp.uint32),
                   jax.ShapeDtypeStruct((ne, b * s), F32), jax.ShapeDtypeStruct((b, s, d), F32)],
        scratch_shapes=[pltpu.VMEM((hw // LANES, tm // GRID_W * GRID_PITCH, LANES), F32)] * 2,
        compiler_params=_cparams("parallel", "arbitrary"),
        name="post",
    )(x, hf, hb, ml, of, ob, gz, mlw, gdw, wout, npost, g2, npre, sh, sc, rwt, wsg, wsu, wsd)


def _route_kernel(lt_ref, bias_ref, idx_ref, gate_ref):
    ne, tn = lt_ref.shape
    gsz = ne // N_GROUPS
    scores = jax.nn.sigmoid(lt_ref[...])
    sel = scores + bias_ref[...]
    neg = -jnp.inf
    sel3 = sel.reshape(N_GROUPS, gsz, tn)
    io3 = lax.broadcasted_iota(jnp.int32, sel3.shape, 1)
    top1 = jnp.max(sel3, axis=1, keepdims=True)
    first = jnp.min(jnp.where(sel3 == top1, io3, gsz), axis=1, keepdims=True)
    top2 = jnp.max(jnp.where(io3 == first, neg, sel3), axis=1, keepdims=True)
    grp = (top1 + top2).reshape(N_GROUPS, tn)
    iog = lax.broadcasted_iota(jnp.int32, grp.shape, 0)
    keep = jnp.zeros(grp.shape, jnp.bool_)
    for _ in range(TOPK_GROUPS):
        m = jnp.max(grp, axis=0, keepdims=True)
        pick = iog == jnp.min(jnp.where(grp == m, iog, N_GROUPS), axis=0, keepdims=True)
        keep = keep | pick
        grp = jnp.where(pick, neg, grp)
    cand = jnp.where(keep.reshape(N_GROUPS, 1, tn), sel3, neg).reshape(ne, tn)
    ioe = lax.broadcasted_iota(jnp.int32, cand.shape, 0)
    idxs, gates = [], []
    for _ in range(TOP_K):
        m = jnp.max(cand, axis=0, keepdims=True)
        e = jnp.min(jnp.where(cand == m, ioe, ne), axis=0, keepdims=True)
        pick = ioe == e
        idxs.append(e)
        gates.append(jnp.sum(jnp.where(pick, scores, 0.0), axis=0, keepdims=True))
        cand = jnp.where(pick, neg, cand)
    gate = jnp.concatenate(gates, axis=0)
    idx_ref[...] = jnp.concatenate(idxs, axis=0)
    gate_ref[...] = gate / jnp.sum(gate, axis=0, keepdims=True) * ROUTED_SCALE


def _route(logits_t, bias_col, tn):
    ne, t = logits_t.shape
    return pl.pallas_call(
        _route_kernel,
        grid=(t // tn,),
        in_specs=[pl.BlockSpec((ne, tn), lambda i: (0, i)), pl.BlockSpec((ne, 1), lambda i: (0, 0))],
        out_specs=[pl.BlockSpec((TOP_K, tn), lambda i: (0, i)), pl.BlockSpec((TOP_K, tn), lambda i: (0, i))],
        out_shape=[jax.ShapeDtypeStruct((TOP_K, t), jnp.int32), jax.ShapeDtypeStruct((TOP_K, t), F32)],
        compiler_params=_cparams("parallel"),
        name="route",
    )(logits_t, bias_col)


def _experts_kernel(be_ref, np_ref,
                    tok_ref, tokn_ref, w_ref, wg_ref, wu_ref, wd_ref, h_hbm,
                    o_ref, xg, wgc, wuc, wdc, gsem):
    s = pl.program_id(0)
    n_pairs = np_ref[0]
    nct = o_ref.shape[0] // (2 * EXPERT_BLOCK)

    def gather_copy(tref, p, j):
        return pltpu.make_async_copy(h_hbm.at[tref[0, p, j]], xg.at[p, j], gsem.at[p])

    def gather_wait(p):
        pltpu.make_async_copy(h_hbm.at[pl.ds(0, EXPERT_BLOCK)], xg.at[p], gsem.at[p]).wait()

    @pl.when(s >= n_pairs)
    def _():
        o_ref[...] = jnp.zeros(o_ref.shape, F32)

    @pl.when(s < n_pairs)
    def _():
        @pl.when(s == 0)
        def _():
            for p in range(2):
                for j in range(EXPERT_BLOCK):
                    gather_copy(tok_ref, p, j).start(priority=j % 2)

        @pl.when((s == 0) | (be_ref[s] != be_ref[jnp.maximum(s - 1, 0)]))
        def _():
            wgc[...] = wg_ref[0].astype(BF16)
            wuc[...] = wu_ref[0].astype(BF16)
            wdc[...] = wd_ref[0].astype(BF16)

        eye = (lax.broadcasted_iota(jnp.int32, (EXPERT_BLOCK, EXPERT_BLOCK), 0)
               == lax.broadcasted_iota(jnp.int32, (EXPERT_BLOCK, EXPERT_BLOCK), 1))
        xbs, w_cols = [], []
        for p in range(2):
            gather_wait(p)
            words = jnp.concatenate([xg[p, :, c, :] for c in range(xg.shape[2])], axis=-1)
            xbs.append(jnp.concatenate([pltpu.bitcast(words << 16, F32),
                                        pltpu.bitcast(words & jnp.uint32(0xFFFF0000), F32)], axis=-1).astype(BF16))
            w_cols.append(jnp.sum(jnp.where(eye, w_ref[0, p:p + 1, :], 0.0), axis=1, keepdims=True))
            for j in range(EXPERT_BLOCK):
                gather_copy(tokn_ref, p, j).start(priority=j % 2)
        xb = jnp.concatenate(xbs, axis=0)
        hmid = _silu(jnp.dot(xb, wgc[...], preferred_element_type=F32)) * jnp.dot(xb, wuc[...],
                                                                                  preferred_element_type=F32)
        out = _dot(hmid, wdc[...]) * jnp.concatenate(w_cols, axis=0)
        for c in range(nct):
            o_ref[pl.ds(c, 2 * EXPERT_BLOCK, stride=nct), :] = out[:, c * LANES:(c + 1) * LANES]

        @pl.when(s == n_pairs - 1)
        def _():
            for p in range(2):
                gather_wait(p)


def _experts(hffn, block_e, n_pairs, row_tok, row_w, wg, wu, wd):
    d = wg.shape[1]
    nct = d // LANES
    npairs = row_tok.shape[0]
    de = wg.shape[2]
    last = npairs - 1
    smem_blk = lambda f: pl.BlockSpec((1, 2, EXPERT_BLOCK), f, memory_space=pltpu.SMEM)
    wspec = lambda shp: pl.BlockSpec((1,) + shp, lambda s, be, npu: (be[s], 0, 0))
    pair_rows = 2 * EXPERT_BLOCK * nct
    grid_spec = pltpu.PrefetchScalarGridSpec(
        num_scalar_prefetch=2,
        grid=(npairs,),
        in_specs=[smem_blk(lambda s, be, npu: (s, 0, 0)),
                  smem_blk(lambda s, be, npu: (jnp.minimum(s + 1, last), 0, 0)),
                  pl.BlockSpec((1, 2, EXPERT_BLOCK), lambda s, be, npu: (s, 0, 0)),
                  wspec((d, de)), wspec((d, de)), wspec((de, d)),
                  pl.BlockSpec(memory_space=pl.ANY)],
        out_specs=pl.BlockSpec((pair_rows, LANES), lambda s, be, npu: (s, 0)),
        scratch_shapes=[pltpu.VMEM((2, EXPERT_BLOCK) + hffn.shape[1:], jnp.uint32),
                        pltpu.VMEM((d, de), BF16), pltpu.VMEM((d, de), BF16), pltpu.VMEM((de, d), BF16),
                        pltpu.SemaphoreType.DMA((2,))],
    )
    return pl.pallas_call(
        _experts_kernel,
        grid_spec=grid_spec,
        out_shape=jax.ShapeDtypeStruct((npairs * pair_rows, LANES), F32),
        compiler_params=_cparams("arbitrary"),
        name="experts",
    )(block_e, n_pairs, row_tok, row_tok, row_w, wg, wu, wd, hffn)


def _combine_kernel(pos_ref, posn_ref, x1_ref, ys_ref, npost_ref, g5_ref, rows_hbm, o_ref, buf, sem):
    i = pl.program_id(0)
    tm = x1_ref.shape[1]
    nct = x1_ref.shape[2] // LANES
    cur = i % 2
    nxt = 1 - cur

    def copy(pref, b, k, t):
        src = rows_hbm.at[pl.ds(pl.multiple_of(pref[k, t] * nct, nct), nct)]
        dst = buf.at[b, pl.ds(pl.multiple_of((k * tm + t) * nct, nct), nct)]
        return pltpu.make_async_copy(src, dst, sem.at[b])

    def drain(b):
        pltpu.make_async_copy(rows_hbm.at[pl.ds(0, TOP_K * tm * nct)], buf.at[b], sem.at[b]).wait()

    def issue(pref, b):
        def body(t2, carry):
            for u in range(2):
                for k in range(TOP_K):
                    copy(pref, b, k, 2 * t2 + u).start(priority=k % 2)
            return carry
        lax.fori_loop(0, tm // 2, body, 0)

    @pl.when(i == 0)
    def _():
        issue(pos_ref, cur)

    @pl.when(i + 1 < pl.num_programs(0))
    def _():
        for t in range(tm):
            for k in range(TOP_K):
                copy(posn_ref, nxt, k, t).start(priority=k % 2)

    drain(cur)
    routed = []
    for c in range(nct):
        acc = buf[cur, pl.ds(c, tm, stride=nct), :]
        for k in range(1, TOP_K):
            acc = acc + buf[cur, pl.ds(k * tm * nct + c, tm, stride=nct), :]
        routed.append(acc)
    y = ys_ref[0] + jnp.concatenate(routed, axis=-1)
    o_ref[0] = x1_ref[0] + g5_ref[0] * _rms(y, npost_ref[...])


def _combine(x1, ys, rows, pos, npost, g5, tm):
    b, s, d = x1.shape
    nt = s // tm
    nct = d // LANES
    last = b * nt - 1
    tok = pl.BlockSpec((1, tm, d), lambda i: (i // nt, i % nt, 0))
    pos_blk = lambda f: pl.BlockSpec((TOP_K, tm), f, memory_space=pltpu.SMEM)
    return pl.pallas_call(
        _combine_kernel,
        grid=(b * nt,),
        in_specs=[pos_blk(lambda i: (0, i)), pos_blk(lambda i: (0, jnp.minimum(i + 1, last))), tok, tok,
                  pl.BlockSpec((1, d), lambda i: (0, 0)), pl.BlockSpec((1, 1, d), lambda i: (i // nt, 0, 0)),
                  pl.BlockSpec(memory_space=pl.ANY)],
        out_specs=tok,
        out_shape=jax.ShapeDtypeStruct((b, s, d), F32),
        scratch_shapes=[pltpu.VMEM((2, TOP_K * tm * nct, LANES), F32), pltpu.SemaphoreType.DMA((2,))],
        compiler_params=_cparams("arbitrary"),
        name="combine",
    )(pos, pos, x1, ys, npost, g5, rows)


def _dispatch_plan(idx_t, gate_t):
    k, t = idx_t.shape
    n_asg = k * t
    pair = 2 * EXPERT_BLOCK
    nb = 2 * (n_asg // pair + N_EXPERTS)
    flat_e = idx_t.reshape(-1)
    id_bits = max(1, (n_asg - 1).bit_length())
    assert (N_EXPERTS - 1).bit_length() + id_bits <= 31
    packed = jnp.sort((flat_e << id_bits) | jnp.arange(n_asg, dtype=jnp.int32))
    order = packed & ((1 << id_bits) - 1)
    counts = jnp.zeros((N_EXPERTS,), jnp.int32).at[flat_e].add(1)
    padded = (counts + pair - 1) // pair * pair
    start = jnp.cumsum(counts) - counts
    pend = jnp.cumsum(padded)
    pstart = pend - padded
    blk0 = jnp.arange(nb, dtype=jnp.int32) * EXPERT_BLOCK
    block_e = jnp.minimum(jnp.sum((pend[None, :] <= blk0[:, None]).astype(jnp.int32), axis=1), N_EXPERTS - 1)
    of_block = block_e[:, None] == jnp.arange(N_EXPERTS, dtype=jnp.int32)[None, :]
    per_block = lambda v: jnp.sum(jnp.where(of_block, v[None, :], 0), axis=1)
    n_pairs = (pend[-1] // pair).astype(jnp.int32).reshape(1)
    pos = blk0[:, None] - per_block(pstart)[:, None] + jnp.arange(EXPERT_BLOCK, dtype=jnp.int32)[None, :]
    valid = pos < per_block(counts)[:, None]
    src = jnp.clip(per_block(start)[:, None] + pos, 0, n_asg - 1)
    asg = order[src]
    row_tok = jnp.where(valid, asg % t, 0).astype(jnp.int32)
    row_w = jnp.where(valid, gate_t.reshape(-1)[asg], 0.0).astype(F32)
    i_sorted = jnp.arange(n_asg, dtype=jnp.int32)
    pad_before = jnp.sum(jnp.where(i_sorted[:, None] >= (start + counts)[None, :], (padded - counts)[None, :], 0),
                         axis=1)
    _, pos = lax.sort((order, i_sorted + pad_before), num_keys=1)
    pos = pos.reshape(k, t)
    shp = (nb // 2, 2, EXPERT_BLOCK)
    return block_e[::2], n_pairs, row_tok.reshape(shp), row_w.reshape(shp), pos


def _pack_in_weights(w_in, ml_i_bias, ml_f_bias, gd_dt_bias):
    d = w_in.shape[0]
    nml = 2 * ML_HEADS * ML_QK + 2 * ML_HEADS * ML_V
    ml_cols = nml + 4 * ML_HEADS
    ngq = GD_HEADS * (2 * GD_QK + GD_V)
    ngz = GD_HEADS * GD_V
    wml = w_in[:, :nml].astype(BF16)
    wgq = w_in[:, ml_cols:ml_cols + ngq].astype(BF16)
    wgz = w_in[:, ml_cols + ngq:ml_cols + ngq + ngz].astype(BF16)
    wg = jnp.zeros((d, GATE_LANES), F32)
    wg = wg.at[:, ML_GATE0:ML_GATE0 + 16].set(w_in[:, nml:ml_cols])
    wg = wg.at[:, GD_GATE0:GD_GATE0 + 16].set(w_in[:, ml_cols + ngq + ngz:])
    gb = jnp.zeros((GATE_LANES,), F32)
    gb = gb.at[ML_GATE0:ML_GATE0 + 16].set(jnp.stack([ml_i_bias, ml_f_bias], axis=1).reshape(-1))
    gb = gb.at[GD_GATE0:GD_GATE0 + 16].set(jnp.stack([gd_dt_bias, jnp.zeros_like(gd_dt_bias)], axis=1).reshape(-1))
    return wml, wgq, wgz, wg.astype(BF16), gb.reshape(1, GATE_LANES)


def _mixer(x, ctx, mod, mod_ctx, norm_pre_mix, w_in, ml_i_bias, ml_f_bias, gd_conv_w, gd_a_log, gd_dt_bias):
    b, s, d = x.shape
    sc = ctx.shape[1]
    wml, wgq, wgz, wg, gb = _pack_in_weights(w_in, ml_i_bias, ml_f_bias, gd_dt_bias)
    nw = norm_pre_mix.reshape(1, d)
    ctx_mod = lambda j: jnp.broadcast_to(mod_ctx[j].reshape(1, 1, d), (b, 1, d))
    ml_c, gq_c, _, g_c = _proj(ctx, nw, ctx_mod(0), ctx_mod(1), wml, wgq, wgz, wg, gb, tm=sc, grid_view=False)
    ml_l, gqv_l, gz_l, g_l, gv_l = _proj(x, nw, mod[0], mod[1], wml, wgq, wgz, wg, gb, tm=TM_PROJ, grid_view=True)

    c0 = jnp.zeros((b, N_CHAINS, ML_QK, 2 * ML_V), F32)
    m0 = jnp.zeros((b, N_CHAINS, 1, 1), F32)
    _, _, c1, m1 = _mlstm(ml_c, g_c, c0, m0)
    hf, hb, _, _ = _mlstm(ml_l, g_l, c1, m1)

    neg_a = jnp.zeros((GATE_LANES,), F32)
    neg_a = neg_a.at[GD_GATE0:GD_GATE0 + 16].set(
        jnp.stack([-jnp.exp(gd_a_log), jnp.zeros_like(gd_a_log)], axis=1).reshape(-1)).reshape(1, GATE_LANES)
    qn_c = _gdconv_ctx(gq_c, gd_conv_w)
    qnv_l = _gdconv_lat(gqv_l, gd_conv_w)
    s0 = jnp.zeros((b, N_CHAINS, GD_QK, GD_V), F32)
    hdim = GD_HEADS * GD_V
    _, _, s1 = _gdn(qn_c, g_c, neg_a, s0, sc // CHUNK, lambda bi, n: (bi, n, 0), (b, sc, hdim))
    rows = s // GRID_W
    cpc = rows // CHUNK
    col_idx = lambda bi, n: (bi, n % cpc, n // cpc)
    ofv, obv, _ = _gdn(qnv_l, gv_l, neg_a, s1, s // CHUNK, col_idx, (b, rows, GRID_W * hdim))
    return hf, hb, ml_l, ofv, obv, gz_l


def kernel(x, c, ctx, c_ctx, w_ada, b_ada, norm_pre_mix, norm_post_mix, norm_pre_ffn, norm_post_ffn, w_in,
           ml_i_bias, ml_f_bias, ml_norm_w, gd_conv_w, gd_a_log, gd_dt_bias, gd_norm_w, w_out, router_w,
           router_bias, w_gate, w_up, w_down, ws_gate, ws_up, ws_down):
    b, s, d = x.shape
    depth = w_ada.shape[0]
    assert depth == 1, "the context stream update of deeper stacks is not implemented"
    ly = 0
    cc = jnp.zeros((16, d), F32).at[:b].set(c).at[b].set(c_ctx)
    mod_all = _ada(cc, w_ada[ly], b_ada[ly])
    mod = [mod_all[:b, j * d:(j + 1) * d].reshape(b, 1, d) for j in range(6)]
    mod_ctx = [mod_all[b, j * d:(j + 1) * d] for j in range(6)]

    hf, hb, ml_l, of, ob, gz_l = _mixer(x, ctx, mod, mod_ctx, norm_pre_mix[ly], w_in[ly], ml_i_bias[ly],
                                        ml_f_bias[ly], gd_conv_w[ly], gd_a_log[ly], gd_dt_bias[ly])

    row = lambda v: v.reshape(1, -1)
    x1, hffn, logits_t, ys = _post(
        x, hf, hb, ml_l, of, ob, gz_l, row(ml_norm_w[ly]), row(jnp.tile(gd_norm_w[ly], GD_HEADS)),
        w_out[ly].astype(BF16), row(norm_post_mix[ly]), mod[2], row(norm_pre_ffn[ly]), mod[3], mod[4],
        router_w[ly].T.astype(BF16), ws_gate[ly].astype(BF16), ws_up[ly].astype(BF16), ws_down[ly].astype(BF16),
        tm=TM_POST)

    idx_t, gate_t = _route(logits_t, router_bias[ly].reshape(-1, 1), tn=TN_ROUTE)
    block_e, n_pairs, row_tok, row_w, pos = _dispatch_plan(idx_t, gate_t)
    t = b * s
    rows = _experts(hffn.reshape(t, d // (2 * LANES), LANES), block_e, n_pairs, row_tok, row_w,
                    w_gate[ly], w_up[ly], w_down[ly])
    return _combine(x1, ys, rows, pos, row(norm_post_ffn[ly]), mod[5], tm=TM_COMBINE)
```

```python
import functools

import jax
import jax.numpy as jnp
from jax import lax
from jax.experimental import pallas as pl
from jax.experimental.pallas import tpu as pltpu

EPS = 1e-6
CHUNK = 64
GRID_W = 64
ML_HEADS, ML_QK, ML_V = 4, 64, 128
GD_HEADS, GD_QK, GD_V = 4, 128, 128
CONV_W = 5
N_EXPERTS, TOP_K, N_GROUPS, TOPK_GROUPS = 256, 8, 8, 4
ROUTED_SCALE = 2.5
EXPERT_BLOCK = 128
N_CHAINS = 8
ML_SCAN_BATCH, GD_SCAN_BATCH = 1, 4
LANES = 128
GATE_LANES = LANES
ML_GATE0, GD_GATE0 = 0, 16

F32 = jnp.float32
BF16 = jnp.bfloat16
HI = lax.Precision.HIGHEST
VMEM_LIMIT = 56 * 1024 * 1024
TM_PROJ, TM_POST, TM_COMBINE, TN_ROUTE, TN_ADA = 512, 512, 256, 512, 1536


def _cparams(*sem):
    return pltpu.CompilerParams(dimension_semantics=sem, vmem_limit_bytes=VMEM_LIMIT)


def _dot(a, b):
    return jnp.dot(a.astype(BF16), b.astype(BF16), preferred_element_type=F32)


def _dot_nt(a, b):
    return lax.dot_general(a.astype(BF16), b.astype(BF16), (((1,), (1,)), ((), ())), preferred_element_type=F32)


def _dot_tn(a, b):
    return lax.dot_general(a.astype(BF16), b.astype(BF16), (((0,), (0,)), ((), ())), preferred_element_type=F32)


def _dot_hi(a, b):
    return jnp.dot(a, b, precision=HI, preferred_element_type=F32)


def _dot_nt_hi(a, b):
    return lax.dot_general(a, b, (((1,), (1,)), ((), ())), precision=HI, preferred_element_type=F32)


def _transpose_hi(x):
    n = x.shape[1]
    eye = (lax.broadcasted_iota(jnp.int32, (n, n), 0) == lax.broadcasted_iota(jnp.int32, (n, n), 1)).astype(F32)
    return _dot_nt_hi(eye, x)


def _silu(x):
    return x * jax.nn.sigmoid(x)


def _past_mask(reverse):
    t = lax.broadcasted_iota(jnp.int32, (CHUNK, CHUNK), 0)
    s = lax.broadcasted_iota(jnp.int32, (CHUNK, CHUNK), 1)
    return (s >= t, s > t) if reverse else (s <= t, s < t)


def _ada_kernel(c_ref, w_ref, b_ref, o_ref):
    o_ref[...] = _dot(_silu(c_ref[...]), w_ref[...]) + b_ref[...]


def _ada(cc, w_ada, b_ada):
    rows, d = cc.shape
    n = w_ada.shape[1]
    tn = TN_ADA
    return pl.pallas_call(
        _ada_kernel,
        grid=(n // tn,),
        in_specs=[pl.BlockSpec((rows, d), lambda j: (0, 0)),
                  pl.BlockSpec((d, tn), lambda j: (0, j)),
                  pl.BlockSpec((1, tn), lambda j: (0, j))],
        out_specs=pl.BlockSpec((rows, tn), lambda j: (0, j)),
        out_shape=jax.ShapeDtypeStruct((rows, n), F32),
        compiler_params=_cparams("arbitrary"),
        name="ada",
    )(cc, w_ada, b_ada.reshape(1, n))


GRID_PITCH = GRID_W + 8


def _to_grid_view(src_ref, dst_ref):
    ng = src_ref.shape[0]
    r = src_ref.shape[1] // GRID_PITCH
    for c in range(GRID_W):
        for g in range(ng):
            lo = (c * ng + g) * LANES
            dst_ref[0, :, lo:lo + LANES] = src_ref[g, pl.ds(c, r, stride=GRID_PITCH), :]


def _from_grid_view(src_ref, dst_ref):
    ng = dst_ref.shape[0]
    r = dst_ref.shape[1] // GRID_PITCH
    for c in range(GRID_W):
        for g in range(ng):
            lo = (c * ng + g) * LANES
            dst_ref[g, pl.ds(c, r, stride=GRID_PITCH), :] = src_ref[0, :, lo:lo + LANES]


def _pitched_rows(ref, g):
    r = ref.shape[1] // GRID_PITCH
    return jnp.concatenate([ref[g, i * GRID_PITCH:i * GRID_PITCH + GRID_W, :] for i in range(r)], axis=0)


def _proj_kernel(grid_view, x_ref, nw_ref, sh_ref, sc_ref, wml_ref, wgq_ref, wgz_ref, wg_ref, gb_ref, *refs):
    if grid_view:
        ml_ref, gqv_ref, gz_ref, g_ref, gv_ref, gq_scr, g_scr = refs
    else:
        ml_ref, gq_ref, gz_ref, g_ref = refs
    x = x_ref[0]
    xn = x * lax.rsqrt(jnp.mean(x * x, axis=-1, keepdims=True) + EPS) * nw_ref[...]
    h = (xn * (1.0 + sc_ref[0]) + sh_ref[0]).astype(BF16)
    ml_ref[0] = jnp.dot(h, wml_ref[...], preferred_element_type=F32)
    gz_ref[0] = jnp.dot(h, wgz_ref[...], preferred_element_type=F32)
    gates = jnp.dot(h, wg_ref[...], preferred_element_type=F32) + gb_ref[...]
    g_ref[0] = gates
    gq = jnp.dot(h, wgq_ref[...], preferred_element_type=F32)
    if grid_view:
        for r in range(x.shape[0] // GRID_W):
            rows = slice(r * GRID_W, (r + 1) * GRID_W)
            prow = slice(r * GRID_PITCH, r * GRID_PITCH + GRID_W)
            g_scr[0, prow, :] = gates[rows]
            for g in range(gq_scr.shape[0]):
                gq_scr[g, prow, :] = gq[rows, g * LANES:(g + 1) * LANES]
        _to_grid_view(gq_scr, gqv_ref)
        _to_grid_view(g_scr, gv_ref)
    else:
        gq_ref[0] = gq


def _proj(x, norm_w, shift, scale, wml, wgq, wgz, wg, gbias, tm, grid_view):
    b, s, d = x.shape
    nml, ngq, ngz = wml.shape[1], wgq.shape[1], wgz.shape[1]
    full = lambda shp: pl.BlockSpec(shp, lambda bi, i: (0,) * len(shp))
    tok = lambda n: pl.BlockSpec((1, tm, n), lambda bi, i: (bi, i, 0))
    mod = pl.BlockSpec((1, 1, d), lambda bi, i: (bi, 0, 0))
    if grid_view:
        rt = tm // GRID_W
        view = lambda n: pl.BlockSpec((1, rt, GRID_W * n), lambda bi, i: (bi, i, 0))
        vshape = lambda n: jax.ShapeDtypeStruct((b, s // GRID_W, GRID_W * n), F32)
        out_specs = [tok(nml), view(ngq), tok(ngz), tok(GATE_LANES), view(GATE_LANES)]
        out_shape = [jax.ShapeDtypeStruct((b, s, nml), F32), vshape(ngq), jax.ShapeDtypeStruct((b, s, ngz), F32),
                     jax.ShapeDtypeStruct((b, s, GATE_LANES), F32), vshape(GATE_LANES)]
        scratch = [pltpu.VMEM((ngq // LANES, rt * GRID_PITCH, LANES), F32), pltpu.VMEM((1, rt * GRID_PITCH, LANES), F32)]
    else:
        out_specs = [tok(nml), tok(ngq), tok(ngz), tok(GATE_LANES)]
        out_shape = [jax.ShapeDtypeStruct((b, s, n), F32) for n in (nml, ngq, ngz, GATE_LANES)]
        scratch = []
    return pl.pallas_call(
        functools.partial(_proj_kernel, grid_view),
        grid=(b, s // tm),
        in_specs=[tok(d), full((1, d)), mod, mod, full((d, nml)), full((d, ngq)), full((d, ngz)),
                  full((d, GATE_LANES)), full((1, GATE_LANES))],
        out_specs=out_specs,
        out_shape=out_shape,
        scratch_shapes=scratch,
        compiler_params=_cparams("parallel", "arbitrary"),
        name="proj",
    )(x, norm_w, shift, scale, wml, wgq, wgz, wg, gbias)


def _mlstm_kernel(mlf_ref, mlb_ref, gf_ref, gb_ref, c0_ref, m0_ref, hf_ref, hb_ref, cn_ref, mn_ref, c_scr, m_scr):
    i = pl.program_id(1)

    @pl.when(i == 0)
    def _():
        c_scr[...] = c0_ref[...]
        m_scr[...] = m0_ref[...]

    nbat = mlf_ref.shape[0]
    past = [_past_mask(d == 1)[0] for d in range(2)]
    ml_refs, g_refs, h_refs = (mlf_ref, mlb_ref), (gf_ref, gb_ref), (hf_ref, hb_ref)
    bd = [(bb, d) for bb in range(nbat) for d in range(2)]
    g = [g_refs[d][bb] for bb, d in bd]
    ls = [jax.nn.log_sigmoid(x) for x in g]
    bcol = [_dot_hi(past[d].astype(F32), ls[j]) for j, (bb, d) in enumerate(bd)]
    tot = [jnp.sum(x, axis=0, keepdims=True) for x in ls]
    g_t = [_transpose_hi(x) for x in g]
    b_t = [_transpose_hi(x) for x in bcol]

    chains = [(bb, d, hd) for bb in range(nbat) for d in range(2) for hd in range(ML_HEADS)]
    nc = range(len(chains))
    k0, v0 = ML_HEADS * ML_QK, 2 * ML_HEADS * ML_QK
    ones_col = (lax.broadcasted_iota(jnp.int32, (CHUNK, ML_V), 1) == 0).astype(F32)
    q, k, v, i_col, b_col, b_end, log_d = [], [], [], [], [], [], []
    for bb, d, hd in chains:
        ci = ML_GATE0 + d * 8 + hd
        cf = ci + ML_HEADS
        j = bb * 2 + d
        q.append(ml_refs[d][bb, :, hd * ML_QK:(hd + 1) * ML_QK])
        k.append(ml_refs[d][bb, :, k0 + hd * ML_QK:k0 + (hd + 1) * ML_QK] * (ML_QK ** -0.5))
        v.append(jnp.concatenate([ml_refs[d][bb, :, v0 + hd * ML_V:v0 + (hd + 1) * ML_V], ones_col], axis=-1))
        i_col.append(g[j][:, ci:ci + 1])
        b_col.append(bcol[j][:, cf:cf + 1])
        b_end.append(tot[j][:, cf:cf + 1])
        log_d.append(jnp.where(past[d], b_col[-1] - b_t[j][cf:cf + 1, :] + g_t[j][ci:ci + 1, :], -jnp.inf))
    st_idx = [(bb, d * ML_HEADS + hd) for bb, d, hd in chains]
    c_st = [c_scr[ix] for ix in st_idx]
    m_st = [m_scr[ix] for ix in st_idx]
    log_prev = [b_col[c] + m_st[c] for c in nc]
    m_t = [jnp.maximum(log_prev[c], jnp.max(log_d[c], axis=-1, keepdims=True)) for c in nc]
    qk = [_dot_nt(q[c], k[c]) for c in nc]
    qc = [_dot(q[c], c_st[c]) for c in nc]
    s = [qk[c] * jnp.exp(log_d[c] - m_t[c]) for c in nc]
    w_prev = [jnp.exp(log_prev[c] - m_t[c]) for c in nc]
    sv = [_dot(s[c], v[c]) for c in nc]
    log_s = [b_end[c] - b_col[c] + i_col[c] for c in nc]
    m_new = [jnp.maximum(b_end[c] + m_st[c], jnp.max(log_s[c], axis=0, keepdims=True)) for c in nc]
    kw = [k[c] * jnp.exp(log_s[c] - m_new[c]) for c in nc]
    w_c = [jnp.exp(b_end[c] + m_st[c] - m_new[c]) for c in nc]
    kv = [_dot_tn(kw[c], v[c]) for c in nc]
    numden = [sv[c] + w_prev[c] * qc[c] for c in nc]
    scale = [1.0 / jnp.maximum(jnp.abs(numden[c][:, ML_V:ML_V + 1]), jnp.exp(-m_t[c])) for c in nc]
    for c, (bb, d, hd) in enumerate(chains):
        h_refs[d][bb, :, hd * ML_V:(hd + 1) * ML_V] = numden[c][:, :ML_V] * scale[c]
        c_scr[st_idx[c]] = w_c[c] * c_st[c] + kv[c]
        m_scr[st_idx[c]] = m_new[c]

    @pl.when(i == pl.num_programs(1) - 1)
    def _():
        cn_ref[...] = c_scr[...]
        mn_ref[...] = m_scr[...]


def _mlstm(ml, gates, c0, m0):
    b, s, nml = ml.shape
    nc = s // CHUNK
    nbat = ML_SCAN_BATCH
    fwd = lambda n: pl.BlockSpec((nbat, CHUNK, n), lambda bi, i: (bi, i, 0))
    bwd = lambda n: pl.BlockSpec((nbat, CHUNK, n), lambda bi, i: (bi, nc - 1 - i, 0))
    st = lambda shp: pl.BlockSpec((nbat,) + shp, lambda bi, i: (bi,) + (0,) * len(shp))
    hdim = ML_HEADS * ML_V
    cshape = (N_CHAINS, ML_QK, 2 * ML_V)
    return pl.pallas_call(
        _mlstm_kernel,
        grid=(b // nbat, nc),
        in_specs=[fwd(nml), bwd(nml), fwd(GATE_LANES), bwd(GATE_LANES), st(cshape), st((N_CHAINS, 1, 1))],
        out_specs=[fwd(hdim), bwd(hdim), st(cshape), st((N_CHAINS, 1, 1))],
        out_shape=[jax.ShapeDtypeStruct((b, s, hdim), F32), jax.ShapeDtypeStruct((b, s, hdim), F32),
                   jax.ShapeDtypeStruct(c0.shape, F32), jax.ShapeDtypeStruct(m0.shape, F32)],
        scratch_shapes=[pltpu.VMEM((nbat,) + cshape, F32), pltpu.VMEM((nbat, N_CHAINS, 1, 1), F32)],
        compiler_params=_cparams("parallel", "arbitrary"),
        name="mlstm",
    )(ml, ml, gates, gates, c0, m0)


def _gdconv_kernel(has_halo, *refs):
    if has_halo:
        x_ref, prev_ref, next_ref, w_ref, o_ref, xp_ref = refs
    else:
        x_ref, w_ref, o_ref, xp_ref = refs
    rows = x_ref.shape[1]
    nch = x_ref.shape[2]
    pad = 8
    zero = jnp.zeros((pad, nch), F32)
    if has_halo:
        c = pl.program_id(1)
        xp_ref[0:pad, :] = jnp.where(c > 0, prev_ref[0], zero)
        xp_ref[pad + rows:, :] = jnp.where(c < pl.num_programs(1) - 1, next_ref[0], zero)
    else:
        xp_ref[0:pad, :] = zero
        xp_ref[pad + rows:, :] = zero
    xp_ref[pad:pad + rows, :] = x_ref[0]
    half = CONV_W // 2
    for lc in range(nch // 128):
        sl = slice(lc * 128, (lc + 1) * 128)
        acc = None
        for j in range(CONV_W):
            term = xp_ref[pad - half + j:pad - half + j + rows, sl] * w_ref[j:j + 1, sl]
            acc = term if acc is None else acc + term
        y = _silu(acc)
        if lc < 2 * GD_HEADS:
            y = y * lax.rsqrt(jnp.sum(y * y, axis=-1, keepdims=True) + EPS)
        if lc < GD_HEADS:
            y = y * (GD_QK ** -0.5)
        o_ref[0, :, sl] = y


def _gdconv_ctx(qkv, conv_w):
    b, s, nch = qkv.shape
    return pl.pallas_call(
        functools.partial(_gdconv_kernel, False),
        grid=(b,),
        in_specs=[pl.BlockSpec((1, s, nch), lambda bi: (bi, 0, 0)), pl.BlockSpec((CONV_W, nch), lambda bi: (0, 0))],
        out_specs=pl.BlockSpec((1, s, nch), lambda bi: (bi, 0, 0)),
        out_shape=jax.ShapeDtypeStruct((b, s, nch), F32),
        scratch_shapes=[pltpu.VMEM((s + 16, nch), F32)],
        compiler_params=_cparams("parallel"),
        name="gdconv_ctx",
    )(qkv, conv_w)


def _gdconv_lat(view, conv_w):
    b, rows, wn = view.shape
    nch = wn // GRID_W
    rb = rows // 8
    return pl.pallas_call(
        functools.partial(_gdconv_kernel, True),
        grid=(b, GRID_W),
        in_specs=[pl.BlockSpec((1, rows, nch), lambda bi, c: (bi, 0, c)),
                  pl.BlockSpec((1, 8, nch), lambda bi, c: (bi, rb - 1, jnp.maximum(c - 1, 0))),
                  pl.BlockSpec((1, 8, nch), lambda bi, c: (bi, 0, jnp.minimum(c + 1, GRID_W - 1))),
                  pl.BlockSpec((CONV_W, nch), lambda bi, c: (0, 0))],
        out_specs=pl.BlockSpec((1, rows, nch), lambda bi, c: (bi, 0, c)),
        out_shape=jax.ShapeDtypeStruct(view.shape, F32),
        scratch_shapes=[pltpu.VMEM((rows + 16, nch), F32)],
        compiler_params=_cparams("parallel", "arbitrary"),
        name="gdconv_lat",
    )(view, view, view, conv_w)


SOLVE_BLOCK = 16


def _hi_lo(x):
    hi = x.astype(BF16).astype(F32)
    return hi, x - hi


def _dot_split(a, b):
    a_hi, a_lo = _hi_lo(a)
    b_hi, b_lo = _hi_lo(b)
    lhs = jnp.concatenate([a_hi, a_hi, a_lo], axis=1).astype(BF16)
    rhs = jnp.concatenate([b_hi, b_lo, b_hi], axis=0).astype(BF16)
    return jnp.dot(lhs, rhs, preferred_element_type=F32)


def _unit_triangular_inverses(ns):
    c = ns[0].shape[0]
    row = lax.broadcasted_iota(jnp.int32, (c, c), 0)
    col = lax.broadcasted_iota(jnp.int32, (c, c), 1)
    eye = (row == col).astype(F32)
    in_diag_block = (row // SOLVE_BLOCK) == (col // SOLVE_BLOCK)
    mm = lambda a_list, b_list: [_dot_split(a, b) for a, b in zip(a_list, b_list)]

    n_d = [jnp.where(in_diag_block, n, 0.0) for n in ns]
    x = n_d
    d_inv = [eye - n for n in n_d]
    for _ in range(SOLVE_BLOCK.bit_length() - 2):
        x = mm(x, x)
        d_inv = [d + dx for d, dx in zip(d_inv, mm(d_inv, x))]
    m = mm(d_inv, [n - nd for n, nd in zip(ns, n_d)])
    assert c // SOLVE_BLOCK == 4
    i_minus_m = [eye - mi for mi in m]
    q = [a + b for a, b in zip(i_minus_m, mm(i_minus_m, mm(m, m)))]
    return mm(q, d_inv)


def _gdn_kernel(qf_ref, qb_ref, gf_ref, gb_ref, na_ref, s0_ref, of_ref, ob_ref, sn_ref, s_scr):
    i = pl.program_id(1)

    @pl.when(i == 0)
    def _():
        s_scr[...] = s0_ref[...]

    nbat = qf_ref.shape[0]
    nqk = GD_HEADS * GD_QK
    masks = [_past_mask(d == 1) for d in range(2)]
    x_refs, g_refs, o_refs = (qf_ref, qb_ref), (gf_ref, gb_ref), (of_ref, ob_ref)
    bd = [(bb, d) for bb in range(nbat) for d in range(2)]
    gates = [g_refs[d][bb] for bb, d in bd]
    glog = [na_ref[...] * jax.nn.softplus(g) for g in gates]
    beta_all = [jax.nn.sigmoid(g) for g in gates]
    gcum = [_dot_hi(masks[d][0].astype(F32), glog[j]) for j, (bb, d) in enumerate(bd)]
    gtot = [jnp.sum(g, axis=0, keepdims=True) for g in glog]
    gcum_t = [_transpose_hi(g) for g in gcum]

    chains = [(bb, d, hd) for bb in range(nbat) for d in range(2) for hd in range(GD_HEADS)]
    q, k, v, g_col, beta, g_end, decay = [], [], [], [], [], [], []
    for bb, d, hd in chains:
        ca = GD_GATE0 + d * 8 + hd
        j = bb * 2 + d
        q.append(x_refs[d][bb, :, hd * GD_QK:(hd + 1) * GD_QK])
        k.append(x_refs[d][bb, :, nqk + hd * GD_QK:nqk + (hd + 1) * GD_QK])
        v.append(x_refs[d][bb, :, 2 * nqk + hd * GD_V:2 * nqk + (hd + 1) * GD_V])
        g_col.append(gcum[j][:, ca:ca + 1])
        beta.append(beta_all[j][:, ca + GD_HEADS:ca + GD_HEADS + 1])
        g_end.append(gtot[j][:, ca:ca + 1])
        decay.append(jnp.exp(jnp.where(masks[d][0], g_col[-1] - gcum_t[j][ca:ca + 1, :], -jnp.inf)))
    nc = range(len(chains))
    kk = [_dot_nt(k[c], k[c]) for c in nc]
    xs = [jnp.where(masks[chains[c][1]][1], beta[c] * kk[c] * decay[c], 0.0) for c in nc]
    ps = _unit_triangular_inverses(xs)
    uw = [_dot_split(ps[c], jnp.concatenate([v[c] * beta[c], k[c] * (beta[c] * jnp.exp(g_col[c]))], axis=-1))
          for c in nc]
    qk = [_dot_nt(q[c], k[c]) * decay[c] for c in nc]
    s_st = [s_scr[bb, d * GD_HEADS + hd] for bb, d, hd in chains]
    v_new = [uw[c][:, :GD_V] - _dot(uw[c][:, GD_V:], s_st[c]) for c in nc]
    o_loc = [_dot(q[c] * jnp.exp(g_col[c]), s_st[c]) for c in nc]
    o_new = [o_loc[c] + _dot(qk[c], v_new[c]) for c in nc]
    s_new = [s_st[c] * jnp.exp(g_end[c]) + _dot_tn(k[c] * jnp.exp(g_end[c] - g_col[c]), v_new[c]) for c in nc]
    for c, (bb, d, hd) in enumerate(chains):
        o_refs[d][bb, :, hd * GD_V:(hd + 1) * GD_V] = o_new[c]
        s_scr[bb, d * GD_HEADS + hd] = s_new[c]

    @pl.when(i == pl.num_programs(1) - 1)
    def _():
        sn_ref[...] = s_scr[...]


def _gdn(qkv_view, gates_view, neg_a, s0, nc, idx_fn, out_view_shape):
    b = qkv_view.shape[0]
    nbat = GD_SCAN_BATCH
    nqkv = 2 * GD_HEADS * GD_QK + GD_HEADS * GD_V
    hdim = GD_HEADS * GD_V
    fwd = lambda n: pl.BlockSpec((nbat, CHUNK, n), lambda bi, i: idx_fn(bi, i))
    bwd = lambda n: pl.BlockSpec((nbat, CHUNK, n), lambda bi, i: idx_fn(bi, nc - 1 - i))
    st = pl.BlockSpec((nbat, N_CHAINS, GD_QK, GD_V), lambda bi, i: (bi, 0, 0, 0))
    return pl.pallas_call(
        _gdn_kernel,
        grid=(b // nbat, nc),
        in_specs=[fwd(nqkv), bwd(nqkv), fwd(GATE_LANES), bwd(GATE_LANES),
                  pl.BlockSpec((1, GATE_LANES), lambda bi, i: (0, 0)), st],
        out_specs=[fwd(hdim), bwd(hdim), st],
        out_shape=[jax.ShapeDtypeStruct(out_view_shape, F32), jax.ShapeDtypeStruct(out_view_shape, F32),
                   jax.ShapeDtypeStruct(s0.shape, F32)],
        scratch_shapes=[pltpu.VMEM((nbat, N_CHAINS, GD_QK, GD_V), F32)],
        compiler_params=_cparams("parallel", "arbitrary"),
        name="gdn",
    )(qkv_view, qkv_view, gates_view, gates_view, neg_a, s0)


def _head_rms(t, nheads, width):
    outs = []
    for hd in range(nheads):
        th = t[:, hd * width:(hd + 1) * width]
        outs.append(th * lax.rsqrt(jnp.mean(th * th, axis=-1, keepdims=True) + EPS))
    return jnp.concatenate(outs, axis=-1)


def _rms(t, w):
    return t * lax.rsqrt(jnp.mean(t * t, axis=-1, keepdims=True) + EPS) * w


def _post_kernel(x_ref, hf_ref, hb_ref, og_ref, of_ref, ob_ref, z_ref, mlw_ref, gdw_ref, wout_ref,
                 npost_ref, g2_ref, npre_ref, sh_ref, sc_ref, rwt_ref, wsg_ref, wsu_ref, wsd_ref,
                 x1_ref, hffn_ref, lt_ref, ys_ref, of_scr, ob_scr):
    ml_y = _head_rms(hf_ref[0] + hb_ref[0], ML_HEADS, ML_V) * mlw_ref[...] * jax.nn.sigmoid(og_ref[0])
    _from_grid_view(of_ref, of_scr)
    _from_grid_view(ob_ref, ob_scr)
    o_sum = jnp.concatenate([_pitched_rows(of_scr, g) + _pitched_rows(ob_scr, g) for g in range(of_scr.shape[0])],
                            axis=-1)
    gd_y = _head_rms(o_sum, GD_HEADS, GD_V) * gdw_ref[...] * _silu(z_ref[0])
    y = _dot(jnp.concatenate([ml_y, gd_y], axis=-1), wout_ref[...])
    x1 = x_ref[0] + g2_ref[0] * _rms(y, npost_ref[...])
    x1_ref[0] = x1
    hffn = _rms(x1, npre_ref[...]) * (1.0 + sc_ref[0]) + sh_ref[0]
    nct = hffn.shape[1] // LANES
    for c in range(nct):
        hffn_ref[0, pl.ds(c, hffn.shape[0], stride=nct), :] = hffn[:, c * LANES:(c + 1) * LANES]
    hb = hffn.astype(BF16)
    lt_ref[...] = lax.dot_general(rwt_ref[...], hb, (((1,), (1,)), ((), ())), preferred_element_type=F32)
    hs = _silu(jnp.dot(hb, wsg_ref[...], preferred_element_type=F32)) * jnp.dot(hb, wsu_ref[...],
                                                                                preferred_element_type=F32)
    ys_ref[0] = _dot(hs, wsd_ref[...])


def _post(x, hf, hb, ml, of, ob, gz, mlw, gdw, wout, npost, g2, npre, sh, sc, rwt, wsg, wsu, wsd, tm):
    b, s, d = x.shape
    nt = s // tm
    hw = ML_HEADS * ML_V
    og_blk = (2 * ML_HEADS * ML_QK + ML_HEADS * ML_V) // hw
    tok = lambda n: pl.BlockSpec((1, tm, n), lambda bi, i: (bi, i, 0))
    full = lambda shp: pl.BlockSpec(shp, lambda bi, i: (0,) * len(shp))
    mod = pl.BlockSpec((1, 1, d), lambda bi, i: (bi, 0, 0))
    ne = rwt.shape[0]
    ds = wsg.shape[1]
    gview = pl.BlockSpec((1, tm // GRID_W, GRID_W * hw), lambda bi, i: (bi, i, 0))
    return pl.pallas_call(
        _post_kernel,
        grid=(b, nt),
        in_specs=[tok(d), tok(hw), tok(hw), pl.BlockSpec((1, tm, hw), lambda bi, i: (bi, i, og_blk)),
                  gview, gview, tok(hw), full((1, hw)), full((1, hw)), full((d, d)),
                  full((1, d)), mod, full((1, d)), mod, mod, full((ne, d)), full((d, ds)), full((d, ds)),
                  full((ds, d))],
        out_specs=[tok(d), pl.BlockSpec((1, tm * (d // LANES), LANES), lambda bi, i: (bi, i, 0)),
                   pl.BlockSpec((ne, tm), lambda bi, i: (0, bi * nt + i)), tok(d)],
        out_shape=[jax.ShapeDtypeStruct((b, s, d), F32), jax.ShapeDtypeStruct((b, s * (d // LANES), LANES), F32),
                   jax.ShapeDtypeStruct((ne, b * s), F32), jax.ShapeDtypeStruct((b, s, d), F32)],
        scratch_shapes=[pltpu.VMEM((hw // LANES, tm // GRID_W * GRID_PITCH, LANES), F32)] * 2,
        compiler_params=_cparams("parallel", "arbitrary"),
        name="post",
    )(x, hf, hb, ml, of, ob, gz, mlw, gdw, wout, npost, g2, npre, sh, sc, rwt, wsg, wsu, wsd)


def _route_kernel(lt_ref, bias_ref, idx_ref, gate_ref):
    ne, tn = lt_ref.shape
    gsz = ne // N_GROUPS
    scores = jax.nn.sigmoid(lt_ref[...])
    sel = scores + bias_ref[...]
    neg = -jnp.inf
    sel3 = sel.reshape(N_GROUPS, gsz, tn)
    io3 = lax.broadcasted_iota(jnp.int32, sel3.shape, 1)
    top1 = jnp.max(sel3, axis=1, keepdims=True)
    first = jnp.min(jnp.where(sel3 == top1, io3, gsz), axis=1, keepdims=True)
    top2 = jnp.max(jnp.where(io3 == first, neg, sel3), axis=1, keepdims=True)
    grp = (top1 + top2).reshape(N_GROUPS, tn)
    iog = lax.broadcasted_iota(jnp.int32, grp.shape, 0)
    keep = jnp.zeros(grp.shape, jnp.bool_)
    for _ in range(TOPK_GROUPS):
        m = jnp.max(grp, axis=0, keepdims=True)
        pick = iog == jnp.min(jnp.where(grp == m, iog, N_GROUPS), axis=0, keepdims=True)
        keep = keep | pick
        grp = jnp.where(pick, neg, grp)
    cand = jnp.where(keep.reshape(N_GROUPS, 1, tn), sel3, neg).reshape(ne, tn)
    ioe = lax.broadcasted_iota(jnp.int32, cand.shape, 0)
    idxs, gates = [], []
    for _ in range(TOP_K):
        m = jnp.max(cand, axis=0, keepdims=True)
        e = jnp.min(jnp.where(cand == m, ioe, ne), axis=0, keepdims=True)
        pick = ioe == e
        idxs.append(e)
        gates.append(jnp.sum(jnp.where(pick, scores, 0.0), axis=0, keepdims=True))
        cand = jnp.where(pick, neg, cand)
    gate = jnp.concatenate(gates, axis=0)
    idx_ref[...] = jnp.concatenate(idxs, axis=0)
    gate_ref[...] = gate / jnp.sum(gate, axis=0, keepdims=True) * ROUTED_SCALE


def _route(logits_t, bias_col, tn):
    ne, t = logits_t.shape
    return pl.pallas_call(
        _route_kernel,
        grid=(t // tn,),
        in_specs=[pl.BlockSpec((ne, tn), lambda i: (0, i)), pl.BlockSpec((ne, 1), lambda i: (0, 0))],
        out_specs=[pl.BlockSpec((TOP_K, tn), lambda i: (0, i)), pl.BlockSpec((TOP_K, tn), lambda i: (0, i))],
        out_shape=[jax.ShapeDtypeStruct((TOP_K, t), jnp.int32), jax.ShapeDtypeStruct((TOP_K, t), F32)],
        compiler_params=_cparams("parallel"),
        name="route",
    )(logits_t, bias_col)


def _experts_kernel(be_ref, np_ref,
                    tok_ref, tokn_ref, w_ref, wg0_ref, wu0_ref, wd0_ref, wg1_ref, wu1_ref, wd1_ref, h_hbm,
                    o_ref, xg, wgc, wuc, wdc, gsem):
    s = pl.program_id(0)
    n_pairs = np_ref[0]
    nct = xg.shape[1] // EXPERT_BLOCK
    rows = nct * EXPERT_BLOCK
    w_refs = ((wg0_ref, wu0_ref, wd0_ref), (wg1_ref, wu1_ref, wd1_ref))

    def gather_copy(tref, p, j):
        src = h_hbm.at[pl.ds(pl.multiple_of(tref[0, p, j] * nct, nct), nct)]
        return pltpu.make_async_copy(src, xg.at[p, pl.ds(j * nct, nct)], gsem.at[p])

    def gather_wait(p):
        pltpu.make_async_copy(h_hbm.at[pl.ds(0, rows)], xg.at[p], gsem.at[p]).wait()

    @pl.when(s >= n_pairs)
    def _():
        o_ref[...] = jnp.zeros(o_ref.shape, F32)

    @pl.when(s < n_pairs)
    def _():
        @pl.when(s == 0)
        def _():
            for p in range(2):
                for j in range(EXPERT_BLOCK):
                    gather_copy(tok_ref, p, j).start(priority=j % 2)

        for p in range(2):
            blk = 2 * s + p
            wg_ref, wu_ref, wd_ref = w_refs[p]
            gather_wait(p)

            @pl.when((blk == 0) | (be_ref[blk] != be_ref[jnp.maximum(blk - 1, 0)]))
            def _():
                wgc[...] = wg_ref[0].astype(BF16)
                wuc[...] = wu_ref[0].astype(BF16)
                wdc[...] = wd_ref[0].astype(BF16)

            xb = jnp.concatenate([xg[p, pl.ds(c, EXPERT_BLOCK, stride=nct), :] for c in range(nct)],
                                 axis=-1).astype(BF16)
            for j in range(EXPERT_BLOCK):
                gather_copy(tokn_ref, p, j).start(priority=j % 2)
            hmid = _silu(jnp.dot(xb, wgc[...], preferred_element_type=F32)) * jnp.dot(xb, wuc[...],
                                                                                      preferred_element_type=F32)
            out = _dot(hmid, wdc[...])
            eye = (lax.broadcasted_iota(jnp.int32, (EXPERT_BLOCK, EXPERT_BLOCK), 0)
                   == lax.broadcasted_iota(jnp.int32, (EXPERT_BLOCK, EXPERT_BLOCK), 1))
            w_col = jnp.sum(jnp.where(eye, w_ref[0, p:p + 1, :], 0.0), axis=1, keepdims=True)
            out = out * w_col
            for c in range(nct):
                o_ref[pl.ds(p * rows + c, EXPERT_BLOCK, stride=nct), :] = out[:, c * LANES:(c + 1) * LANES]

        @pl.when(s == n_pairs - 1)
        def _():
            for p in range(2):
                gather_wait(p)


def _experts(hffn, block_e, n_pairs, row_tok, row_w, wg, wu, wd):
    d = wg.shape[1]
    nct = d // LANES
    npairs = row_tok.shape[0]
    de = wg.shape[2]
    last = npairs - 1
    smem_blk = lambda f: pl.BlockSpec((1, 2, EXPERT_BLOCK), f, memory_space=pltpu.SMEM)
    wspec = lambda shp, p: pl.BlockSpec((1,) + shp, lambda s, be, npu: (be[2 * s + p], 0, 0))
    pair_rows = 2 * EXPERT_BLOCK * nct
    grid_spec = pltpu.PrefetchScalarGridSpec(
        num_scalar_prefetch=2,
        grid=(npairs,),
        in_specs=[smem_blk(lambda s, be, npu: (s, 0, 0)),
                  smem_blk(lambda s, be, npu: (jnp.minimum(s + 1, last), 0, 0)),
                  pl.BlockSpec((1, 2, EXPERT_BLOCK), lambda s, be, npu: (s, 0, 0)),
                  wspec((d, de), 0), wspec((d, de), 0), wspec((de, d), 0),
                  wspec((d, de), 1), wspec((d, de), 1), wspec((de, d), 1),
                  pl.BlockSpec(memory_space=pl.ANY)],
        out_specs=pl.BlockSpec((pair_rows, LANES), lambda s, be, npu: (s, 0)),
        scratch_shapes=[pltpu.VMEM((2, EXPERT_BLOCK * nct, LANES), F32),
                        pltpu.VMEM((d, de), BF16), pltpu.VMEM((d, de), BF16), pltpu.VMEM((de, d), BF16),
                        pltpu.SemaphoreType.DMA((2,))],
    )
    return pl.pallas_call(
        _experts_kernel,
        grid_spec=grid_spec,
        out_shape=jax.ShapeDtypeStruct((npairs * pair_rows, LANES), F32),
        compiler_params=_cparams("arbitrary"),
        name="experts",
    )(block_e, n_pairs, row_tok, row_tok, row_w, wg, wu, wd, wg, wu, wd, hffn)


def _combine_kernel(pos_ref, posn_ref, x1_ref, ys_ref, npost_ref, g5_ref, rows_hbm, o_ref, buf, sem):
    i = pl.program_id(0)
    tm = x1_ref.shape[1]
    nct = x1_ref.shape[2] // LANES
    cur = i % 2
    nxt = 1 - cur

    def copy(pref, b, k, t):
        src = rows_hbm.at[pl.ds(pl.multiple_of(pref[k, t] * nct, nct), nct)]
        dst = buf.at[b, pl.ds(pl.multiple_of((k * tm + t) * nct, nct), nct)]
        return pltpu.make_async_copy(src, dst, sem.at[b])

    def drain(b):
        pltpu.make_async_copy(rows_hbm.at[pl.ds(0, TOP_K * tm * nct)], buf.at[b], sem.at[b]).wait()

    def issue(pref, b):
        def body(t2, carry):
            for u in range(2):
                for k in range(TOP_K):
                    copy(pref, b, k, 2 * t2 + u).start(priority=k % 2)
            return carry
        lax.fori_loop(0, tm // 2, body, 0)

    @pl.when(i == 0)
    def _():
        issue(pos_ref, cur)

    @pl.when(i + 1 < pl.num_programs(0))
    def _():
        for t in range(tm):
            for k in range(TOP_K):
                copy(posn_ref, nxt, k, t).start(priority=k % 2)

    drain(cur)
    routed = []
    for c in range(nct):
        acc = buf[cur, pl.ds(c, tm, stride=nct), :]
        for k in range(1, TOP_K):
            acc = acc + buf[cur, pl.ds(k * tm * nct + c, tm, stride=nct), :]
        routed.append(acc)
    y = ys_ref[0] + jnp.concatenate(routed, axis=-1)
    o_ref[0] = x1_ref[0] + g5_ref[0] * _rms(y, npost_ref[...])


def _combine(x1, ys, rows, pos, npost, g5, tm):
    b, s, d = x1.shape
    nt = s // tm
    nct = d // LANES
    last = b * nt - 1
    tok = pl.BlockSpec((1, tm, d), lambda i: (i // nt, i % nt, 0))
    pos_blk = lambda f: pl.BlockSpec((TOP_K, tm), f, memory_space=pltpu.SMEM)
    return pl.pallas_call(
        _combine_kernel,
        grid=(b * nt,),
        in_specs=[pos_blk(lambda i: (0, i)), pos_blk(lambda i: (0, jnp.minimum(i + 1, last))), tok, tok,
                  pl.BlockSpec((1, d), lambda i: (0, 0)), pl.BlockSpec((1, 1, d), lambda i: (i // nt, 0, 0)),
                  pl.BlockSpec(memory_space=pl.ANY)],
        out_specs=tok,
        out_shape=jax.ShapeDtypeStruct((b, s, d), F32),
        scratch_shapes=[pltpu.VMEM((2, TOP_K * tm * nct, LANES), F32), pltpu.SemaphoreType.DMA((2,))],
        compiler_params=_cparams("arbitrary"),
        name="combine",
    )(pos, pos, x1, ys, npost, g5, rows)


def _dispatch_plan(idx_t, gate_t):
    k, t = idx_t.shape
    n_asg = k * t
    nb = n_asg // EXPERT_BLOCK + N_EXPERTS
    flat_e = idx_t.reshape(-1)
    id_bits = max(1, (n_asg - 1).bit_length())
    assert (N_EXPERTS - 1).bit_length() + id_bits <= 31
    packed = jnp.sort((flat_e << id_bits) | jnp.arange(n_asg, dtype=jnp.int32))
    order = packed & ((1 << id_bits) - 1)
    counts = jnp.zeros((N_EXPERTS,), jnp.int32).at[flat_e].add(1)
    padded = (counts + EXPERT_BLOCK - 1) // EXPERT_BLOCK * EXPERT_BLOCK
    start = jnp.cumsum(counts) - counts
    pend = jnp.cumsum(padded)
    pstart = pend - padded
    blk0 = jnp.arange(nb, dtype=jnp.int32) * EXPERT_BLOCK
    block_e = jnp.minimum(jnp.sum((pend[None, :] <= blk0[:, None]).astype(jnp.int32), axis=1), N_EXPERTS - 1)
    of_block = block_e[:, None] == jnp.arange(N_EXPERTS, dtype=jnp.int32)[None, :]
    per_block = lambda v: jnp.sum(jnp.where(of_block, v[None, :], 0), axis=1)
    assert nb % 2 == 0
    n_pairs = ((pend[-1] // EXPERT_BLOCK + 1) // 2).astype(jnp.int32).reshape(1)
    pos = blk0[:, None] - per_block(pstart)[:, None] + jnp.arange(EXPERT_BLOCK, dtype=jnp.int32)[None, :]
    valid = pos < per_block(counts)[:, None]
    src = jnp.clip(per_block(start)[:, None] + pos, 0, n_asg - 1)
    asg = order[src]
    row_tok = jnp.where(valid, asg % t, 0).astype(jnp.int32)
    row_w = jnp.where(valid, gate_t.reshape(-1)[asg], 0.0).astype(F32)
    i_sorted = jnp.arange(n_asg, dtype=jnp.int32)
    pad_before = jnp.sum(jnp.where(i_sorted[:, None] >= (start + counts)[None, :], (padded - counts)[None, :], 0),
                         axis=1)
    _, pos = lax.sort((order, i_sorted + pad_before), num_keys=1)
    pos = pos.reshape(k, t)
    shp = (nb // 2, 2, EXPERT_BLOCK)
    return block_e, n_pairs, row_tok.reshape(shp), row_w.reshape(shp), pos


def _pack_in_weights(w_in, ml_i_bias, ml_f_bias, gd_dt_bias):
    d = w_in.shape[0]
    nml = 2 * ML_HEADS * ML_QK + 2 * ML_HEADS * ML_V
    ml_cols = nml + 4 * ML_HEADS
    ngq = GD_HEADS * (2 * GD_QK + GD_V)
    ngz = GD_HEADS * GD_V
    wml = w_in[:, :nml].astype(BF16)
    wgq = w_in[:, ml_cols:ml_cols + ngq].astype(BF16)
    wgz = w_in[:, ml_cols + ngq:ml_cols + ngq + ngz].astype(BF16)
    wg = jnp.zeros((d, GATE_LANES), F32)
    wg = wg.at[:, ML_GATE0:ML_GATE0 + 16].set(w_in[:, nml:ml_cols])
    wg = wg.at[:, GD_GATE0:GD_GATE0 + 16].set(w_in[:, ml_cols + ngq + ngz:])
    gb = jnp.zeros((GATE_LANES,), F32)
    gb = gb.at[ML_GATE0:ML_GATE0 + 16].set(jnp.stack([ml_i_bias, ml_f_bias], axis=1).reshape(-1))
    gb = gb.at[GD_GATE0:GD_GATE0 + 16].set(jnp.stack([gd_dt_bias, jnp.zeros_like(gd_dt_bias)], axis=1).reshape(-1))
    return wml, wgq, wgz, wg.astype(BF16), gb.reshape(1, GATE_LANES)


def _mixer(x, ctx, mod, mod_ctx, norm_pre_mix, w_in, ml_i_bias, ml_f_bias, gd_conv_w, gd_a_log, gd_dt_bias):
    b, s, d = x.shape
    sc = ctx.shape[1]
    wml, wgq, wgz, wg, gb = _pack_in_weights(w_in, ml_i_bias, ml_f_bias, gd_dt_bias)
    nw = norm_pre_mix.reshape(1, d)
    ctx_mod = lambda j: jnp.broadcast_to(mod_ctx[j].reshape(1, 1, d), (b, 1, d))
    ml_c, gq_c, _, g_c = _proj(ctx, nw, ctx_mod(0), ctx_mod(1), wml, wgq, wgz, wg, gb, tm=sc, grid_view=False)
    ml_l, gqv_l, gz_l, g_l, gv_l = _proj(x, nw, mod[0], mod[1], wml, wgq, wgz, wg, gb, tm=TM_PROJ, grid_view=True)

    c0 = jnp.zeros((b, N_CHAINS, ML_QK, 2 * ML_V), F32)
    m0 = jnp.zeros((b, N_CHAINS, 1, 1), F32)
    _, _, c1, m1 = _mlstm(ml_c, g_c, c0, m0)
    hf, hb, _, _ = _mlstm(ml_l, g_l, c1, m1)

    neg_a = jnp.zeros((GATE_LANES,), F32)
    neg_a = neg_a.at[GD_GATE0:GD_GATE0 + 16].set(
        jnp.stack([-jnp.exp(gd_a_log), jnp.zeros_like(gd_a_log)], axis=1).reshape(-1)).reshape(1, GATE_LANES)
    qn_c = _gdconv_ctx(gq_c, gd_conv_w)
    qnv_l = _gdconv_lat(gqv_l, gd_conv_w)
    s0 = jnp.zeros((b, N_CHAINS, GD_QK, GD_V), F32)
    hdim = GD_HEADS * GD_V
    _, _, s1 = _gdn(qn_c, g_c, neg_a, s0, sc // CHUNK, lambda bi, n: (bi, n, 0), (b, sc, hdim))
    rows = s // GRID_W
    cpc = rows // CHUNK
    col_idx = lambda bi, n: (bi, n % cpc, n // cpc)
    ofv, obv, _ = _gdn(qnv_l, gv_l, neg_a, s1, s // CHUNK, col_idx, (b, rows, GRID_W * hdim))
    return hf, hb, ml_l, ofv, obv, gz_l


def kernel(x, c, ctx, c_ctx, w_ada, b_ada, norm_pre_mix, norm_post_mix, norm_pre_ffn, norm_post_ffn, w_in,
           ml_i_bias, ml_f_bias, ml_norm_w, gd_conv_w, gd_a_log, gd_dt_bias, gd_norm_w, w_out, router_w,
           router_bias, w_gate, w_up, w_down, ws_gate, ws_up, ws_down):
    b, s, d = x.shape
    depth = w_ada.shape[0]
    assert depth == 1, "the context stream update of deeper stacks is not implemented"
    ly = 0
    cc = jnp.zeros((16, d), F32).at[:b].set(c).at[b].set(c_ctx)
    mod_all = _ada(cc, w_ada[ly], b_ada[ly])
    mod = [mod_all[:b, j * d:(j + 1) * d].reshape(b, 1, d) for j in range(6)]
    mod_ctx = [mod_all[b, j * d:(j + 1) * d] for j in range(6)]

    hf, hb, ml_l, of, ob, gz_l = _mixer(x, ctx, mod, mod_ctx, norm_pre_mix[ly], w_in[ly], ml_i_bias[ly],
                                        ml_f_bias[ly], gd_conv_w[ly], gd_a_log[ly], gd_dt_bias[ly])

    row = lambda v: v.reshape(1, -1)
    x1, hffn, logits_t, ys = _post(
        x, hf, hb, ml_l, of, ob, gz_l, row(ml_norm_w[ly]), row(jnp.tile(gd_norm_w[ly], GD_HEADS)),
        w_out[ly].astype(BF16), row(norm_post_mix[ly]), mod[2], row(norm_pre_ffn[ly]), mod[3], mod[4],
        router_w[ly].T.astype(BF16), ws_gate[ly].astype(BF16), ws_up[ly].astype(BF16), ws_down[ly].astype(BF16),
        tm=TM_POST)

    idx_t, gate_t = _route(logits_t, router_bias[ly].reshape(-1, 1), tn=TN_ROUTE)
    block_e, n_pairs, row_tok, row_w, pos = _dispatch_plan(idx_t, gate_t)
    t = b * s
    rows = _experts(hffn.reshape(t * (d // LANES), LANES), block_e, n_pairs, row_tok, row_w,
                    w_gate[ly].astype(BF16), w_up[ly].astype(BF16), w_down[ly].astype(BF16))
    return _combine(x1, ys, rows, pos, row(norm_post_ffn[ly]), mod[5], tm=TM_COMBINE)
```
